```python
import jax, jax.numpy as jnp
from jax import lax
import numpy as np

D_MODEL = 1024
BATCH = 8
SEQ = 4096
DEPTH = 1

CONV_WIDTH = D_MODEL
CONV_KERNEL = 31
HEAD_DIM = 64
N_Q_HEADS = 16
N_KV_HEADS = 4
GROUP = N_Q_HEADS // N_KV_HEADS
ATTN_WIDTH = N_Q_HEADS * HEAD_DIM
KV_WIDTH = N_KV_HEADS * HEAD_DIM
WINDOW = 128
BLOCK = 128
ROPE_THETA = 10000.0
RMS_EPS = 1e-5
LN_EPS = 1e-5
N_BRANCHES = 2

SECTION_WIDTHS = (
    2 * CONV_WIDTH,
    CONV_WIDTH,
    ATTN_WIDTH,
    KV_WIDTH,
    KV_WIDTH,
    ATTN_WIDTH,
    N_BRANCHES * D_MODEL,
)
SPLIT_POINTS = tuple(int(v) for v in np.cumsum(SECTION_WIDTHS)[:-1])
IN_WIDTH = sum(SECTION_WIDTHS)

kernel_name = "hybrid_conformer_conv_swa_sink_gated_block"


def rmsnorm(x, g):
    xf = x.astype(jnp.float32)
    y = xf * lax.rsqrt(jnp.mean(xf * xf, axis=-1, keepdims=True) + RMS_EPS)
    return (y * g.astype(jnp.float32)).astype(x.dtype)


def layernorm(x, g, b):
    xf = x.astype(jnp.float32)
    mu = jnp.mean(xf, axis=-1, keepdims=True)
    var = jnp.mean(jnp.square(xf - mu), axis=-1, keepdims=True)
    y = (xf - mu) * lax.rsqrt(var + LN_EPS)
    return (y * g.astype(jnp.float32) + b.astype(jnp.float32)).astype(x.dtype)


def rope(t, pos):
    inv_freq = ROPE_THETA ** (-jnp.arange(0, HEAD_DIM, 2, dtype=jnp.float32) / HEAD_DIM)
    ang = pos.astype(jnp.float32)[:, None] * inv_freq[None, :]
    cos = jnp.cos(ang)[None, :, None, :]
    sin = jnp.sin(ang)[None, :, None, :]
    tf = t.astype(jnp.float32)
    t1, t2 = jnp.split(tf, 2, axis=-1)
    out = jnp.concatenate([t1 * cos - t2 * sin, t2 * cos + t1 * sin], axis=-1)
    return out.astype(t.dtype)


def conformer_conv(glu_in, w_dw, b_dw, ln_g, ln_b):
    a, b = jnp.split(glu_in, 2, axis=-1)
    h = a * jax.nn.sigmoid(b)
    h = lax.conv_general_dilated(
        h, w_dw[:, None, :].astype(h.dtype), window_strides=(1,),
        padding=[(CONV_KERNEL - 1, 0)],
        dimension_numbers=("NWC", "WIO", "NWC"),
        feature_group_count=CONV_WIDTH) + b_dw
    h = layernorm(h, ln_g, ln_b)
    return jax.nn.silu(h)


def sliding_window_attention(q, k, v, sinks):
    B, S = q.shape[0], q.shape[1]
    nb = S // BLOCK
    qb = q.reshape(B, nb, BLOCK, N_KV_HEADS, GROUP, HEAD_DIM)

    def band(t):
        tb = t.reshape(B, nb, BLOCK, N_KV_HEADS, HEAD_DIM)
        prev = jnp.pad(tb, ((0, 0), (1, 0), (0, 0), (0, 0), (0, 0)))[:, :-1]
        return jnp.concatenate([prev, tb], axis=2)

    kb, vb = band(k), band(v)
    scores = jnp.einsum("bnqhgd,bnkhd->bnhgqk", qb, kb).astype(jnp.float32) * (HEAD_DIM ** -0.5)
    qi = jnp.arange(BLOCK)[:, None]
    kj = jnp.arange(2 * BLOCK)[None, :]
    rel = qi + BLOCK - kj
    in_window = (rel >= 0) & (rel < WINDOW)
    key_pos = jnp.arange(nb)[:, None, None] * BLOCK - BLOCK + kj[None]
    mask = in_window[None] & (key_pos >= 0)
    scores = jnp.where(mask[None, :, None, None], scores, jnp.float32(-1e30))
    sink = sinks.astype(jnp.float32).reshape(N_KV_HEADS, GROUP)[None, None, :, :, None, None]
    m = jnp.maximum(jnp.max(scores, axis=-1, keepdims=True), sink)
    p = jnp.exp(scores - m)
    denom = jnp.sum(p, axis=-1, keepdims=True) + jnp.exp(sink - m)
    probs = (p / denom).astype(v.dtype)
    out = jnp.einsum("bnhgqk,bnkhd->bnqhgd", probs, vb)
    return out.reshape(B, S, ATTN_WIDTH)


def _fwd_setup_inputs(seed: int = 0) -> dict:
    key = jax.random.key(seed)
    ks = jax.random.split(key, 13)
    f32 = jnp.float32
    x = jax.random.normal(ks[0], (BATCH, SEQ, D_MODEL), f32)
    norm_g = 1.0 + 0.02 * jax.random.normal(ks[1], (DEPTH, D_MODEL), f32)
    w_in = jax.random.normal(ks[2], (DEPTH, D_MODEL, IN_WIDTH), f32) * D_MODEL ** -0.5
    conv_dw_w = jax.random.normal(ks[3], (DEPTH, CONV_KERNEL, CONV_WIDTH), f32) * CONV_KERNEL ** -0.5
    conv_dw_b = 0.02 * jax.random.normal(ks[4], (DEPTH, CONV_WIDTH), f32)
    conv_ln_g = 1.0 + 0.02 * jax.random.normal(ks[5], (DEPTH, CONV_WIDTH), f32)
    conv_ln_b = 0.02 * jax.random.normal(ks[6], (DEPTH, CONV_WIDTH), f32)
    w_conv_out = jax.random.normal(ks[7], (DEPTH, CONV_WIDTH, D_MODEL), f32) * CONV_WIDTH ** -0.5
    attn_sinks = 0.5 * jax.random.normal(ks[8], (DEPTH, N_Q_HEADS), f32)
    w_attn_out = jax.random.normal(ks[9], (DEPTH, ATTN_WIDTH, D_MODEL), f32) * ATTN_WIDTH ** -0.5
    w_out = jax.random.normal(ks[10], (DEPTH, D_MODEL, D_MODEL), f32) * D_MODEL ** -0.5
    final_norm_g = 1.0 + 0.02 * jax.random.normal(ks[11], (D_MODEL,), f32)
    return {"x": x, "norm_g": norm_g, "w_in": w_in, "conv_dw_w": conv_dw_w,
            "conv_dw_b": conv_dw_b, "conv_ln_g": conv_ln_g, "conv_ln_b": conv_ln_b,
            "w_conv_out": w_conv_out, "attn_sinks": attn_sinks, "w_attn_out": w_attn_out,
            "w_out": w_out, "final_norm_g": final_norm_g}


def _fwd_reference(x, norm_g, w_in, conv_dw_w, conv_dw_b, conv_ln_g, conv_ln_b,
              w_conv_out, attn_sinks, w_attn_out, w_out, final_norm_g):
    B, S = x.shape[0], x.shape[1]
    pos = jnp.arange(S, dtype=jnp.int32)
    for l in range(DEPTH):
        h = rmsnorm(x, norm_g[l])
        proj = jnp.einsum("bsd,de->bse", h, w_in[l])
        glu_in, conv_gate, q, k, v, attn_gate, merge_logits = jnp.split(proj, SPLIT_POINTS, axis=-1)

        c = conformer_conv(glu_in, conv_dw_w[l], conv_dw_b[l], conv_ln_g[l], conv_ln_b[l])
        y_conv = jnp.einsum("bsc,cd->bsd", c * jax.nn.silu(conv_gate), w_conv_out[l])

        q = rope(q.reshape(B, S, N_Q_HEADS, HEAD_DIM), pos)
        k = rope(k.reshape(B, S, N_KV_HEADS, HEAD_DIM), pos)
        v = v.reshape(B, S, N_KV_HEADS, HEAD_DIM)
        a = sliding_window_attention(q, k, v, attn_sinks[l])
        y_attn = jnp.einsum("bsa,ad->bsd", a * jax.nn.silu(attn_gate), w_attn_out[l])

        gates = jax.nn.sigmoid(merge_logits)
        g_conv, g_attn = jnp.split(gates, 2, axis=-1)
        merged = g_conv * y_conv + g_attn * y_attn
        x = x + jnp.einsum("bsd,de->bse", merged, w_out[l])
    return rmsnorm(x, final_norm_g)


import jax as _jax
import jax.numpy as _jnp

TWIN_FORMAT = 'train_step'
FWD_PARAMS = ['x', 'norm_g', 'w_in', 'conv_dw_w', 'conv_dw_b', 'conv_ln_g', 'conv_ln_b', 'w_conv_out', 'attn_sinks', 'w_attn_out', 'w_out', 'final_norm_g']
TWIN_WEIGHTS = ['norm_g', 'w_in', 'conv_dw_w', 'conv_dw_b', 'conv_ln_g', 'conv_ln_b', 'w_conv_out', 'attn_sinks', 'w_attn_out', 'w_out', 'final_norm_g']
TWIN_DIFF_INPUT = 'x'
TWIN_INPUTS = ['x', 'norm_g', 'w_in', 'conv_dw_w', 'conv_dw_b', 'conv_ln_g', 'conv_ln_b', 'w_conv_out', 'attn_sinks', 'w_attn_out', 'w_out', 'final_norm_g', 'loss_target', 'm_norm_g', 'm_w_in', 'm_conv_dw_w', 'm_conv_dw_b', 'm_conv_ln_g', 'm_conv_ln_b', 'm_w_conv_out', 'm_attn_sinks', 'm_w_attn_out', 'm_w_out', 'm_final_norm_g', 'v_norm_g', 'v_w_in', 'v_conv_dw_w', 'v_conv_dw_b', 'v_conv_ln_g', 'v_conv_ln_b', 'v_w_conv_out', 'v_attn_sinks', 'v_w_attn_out', 'v_w_out', 'v_final_norm_g']
TWIN_OUTPUTS = ['loss', 'grad_x', 'grad_norm_g', 'grad_w_in', 'grad_conv_dw_w', 'grad_conv_dw_b', 'grad_conv_ln_g', 'grad_conv_ln_b', 'grad_w_conv_out', 'grad_attn_sinks', 'grad_w_attn_out', 'grad_w_out', 'grad_final_norm_g', 'delta_norm_g', 'delta_w_in', 'delta_conv_dw_w', 'delta_conv_dw_b', 'delta_conv_ln_g', 'delta_conv_ln_b', 'delta_w_conv_out', 'delta_attn_sinks', 'delta_w_attn_out', 'delta_w_out', 'delta_final_norm_g', 'new_m_norm_g', 'new_m_w_in', 'new_m_conv_dw_w', 'new_m_conv_dw_b', 'new_m_conv_ln_g', 'new_m_conv_ln_b', 'new_m_w_conv_out', 'new_m_attn_sinks', 'new_m_w_attn_out', 'new_m_w_out', 'new_m_final_norm_g', 'new_v_norm_g', 'new_v_w_in', 'new_v_conv_dw_w', 'new_v_conv_dw_b', 'new_v_conv_ln_g', 'new_v_conv_ln_b', 'new_v_w_conv_out', 'new_v_attn_sinks', 'new_v_w_attn_out', 'new_v_w_out', 'new_v_final_norm_g']
TWIN_LEAF_KINDS = {'loss': 'loss', 'grad_x': 'grad_x', 'grad_norm_g': 'grad_w', 'grad_w_in': 'grad_w', 'grad_conv_dw_w': 'grad_w', 'grad_conv_dw_b': 'grad_w', 'grad_conv_ln_g': 'grad_w', 'grad_conv_ln_b': 'grad_w', 'grad_w_conv_out': 'grad_w', 'grad_attn_sinks': 'grad_w', 'grad_w_attn_out': 'grad_w', 'grad_w_out': 'grad_w', 'grad_final_norm_g': 'grad_w', 'delta_norm_g': 'delta_w', 'delta_w_in': 'delta_w', 'delta_conv_dw_w': 'delta_w', 'delta_conv_dw_b': 'delta_w', 'delta_conv_ln_g': 'delta_w', 'delta_conv_ln_b': 'delta_w', 'delta_w_conv_out': 'delta_w', 'delta_attn_sinks': 'delta_w', 'delta_w_attn_out': 'delta_w', 'delta_w_out': 'delta_w', 'delta_final_norm_g': 'delta_w', 'new_m_norm_g': 'new_m', 'new_m_w_in': 'new_m', 'new_m_conv_dw_w': 'new_m', 'new_m_conv_dw_b': 'new_m', 'new_m_conv_ln_g': 'new_m', 'new_m_conv_ln_b': 'new_m', 'new_m_w_conv_out': 'new_m', 'new_m_attn_sinks': 'new_m', 'new_m_w_attn_out': 'new_m', 'new_m_w_out': 'new_m', 'new_m_final_norm_g': 'new_m', 'new_v_norm_g': 'new_v', 'new_v_w_in': 'new_v', 'new_v_conv_dw_w': 'new_v', 'new_v_conv_dw_b': 'new_v', 'new_v_conv_ln_g': 'new_v', 'new_v_conv_ln_b': 'new_v', 'new_v_w_conv_out': 'new_v', 'new_v_attn_sinks': 'new_v', 'new_v_w_attn_out': 'new_v', 'new_v_w_out': 'new_v', 'new_v_final_norm_g': 'new_v'}


def _forward(args):
    return _fwd_reference(*[args[k] for k in FWD_PARAMS])


def _output_shape():
    out = _jax.eval_shape(lambda: _forward(_fwd_setup_inputs(0)))
    return out.shape, out.dtype

N_MICROBATCH = 1
ADAM_LR = 0.001
ADAM_B1 = 0.9
ADAM_B2 = 0.999
ADAM_EPS = 1e-08
ADAM_WD = 0.01
ADAM_STEP = 10
PER_EXAMPLE_BATCH_AXIS = {'x': 0, 'loss_target': 0}
SHARED_INPUTS = []
_WEIGHT_DTYPES = {'norm_g': _jnp.float32, 'w_in': _jnp.float32, 'conv_dw_w': _jnp.float32, 'conv_dw_b': _jnp.float32, 'conv_ln_g': _jnp.float32, 'conv_ln_b': _jnp.float32, 'w_conv_out': _jnp.float32, 'attn_sinks': _jnp.float32, 'w_attn_out': _jnp.float32, 'w_out': _jnp.float32, 'final_norm_g': _jnp.float32}
MOMENT_SCALE = {'norm_g': 5.771893e-02, 'w_in': 2.095887e-02, 'conv_dw_w': 3.500692e-02, 'conv_dw_b': 7.472652e-02, 'conv_ln_g': 4.292083e-02, 'conv_ln_b': 3.505147e-02, 'w_conv_out': 3.374632e-02, 'attn_sinks': 1.076449e-02, 'w_attn_out': 1.154009e-02, 'w_out': 3.565046e-02, 'final_norm_g': 3.198270e+01}


def _to_microbatches(a, axis):
    t = _jnp.moveaxis(a, axis, 0)
    t = t.reshape((N_MICROBATCH, t.shape[0] // N_MICROBATCH) + t.shape[1:])
    return _jnp.moveaxis(t, 1, axis + 1)


def setup_inputs(seed: int = 0) -> dict:
    inp = _fwd_setup_inputs(seed)
    key = _jax.random.fold_in(_jax.random.key(seed), 7919)
    shape, _ = _output_shape()
    out = dict(inp)
    out["loss_target"] = _jax.random.normal(_jax.random.fold_in(key, 0), shape, _jnp.float32)
    for i, name in enumerate(TWIN_WEIGHTS):
        w = inp[name].astype(_jnp.float32)
        if MOMENT_SCALE is None:
            s = _jnp.sqrt(_jnp.mean(_jnp.square(w)) + 1e-30)
        else:
            s = MOMENT_SCALE[name]
        km, kv = _jax.random.split(_jax.random.fold_in(key, i + 1))
        out[name] = w
        out["m_" + name] = s * _jax.random.normal(km, w.shape, _jnp.float32)
        out["v_" + name] = (s * s) * _jax.random.uniform(kv, w.shape, _jnp.float32, 0.5, 1.5)
    if N_MICROBATCH > 1:
        for name, axis in PER_EXAMPLE_BATCH_AXIS.items():
            out[name] = _to_microbatches(out[name], axis)
    return {'x': out['x'], 'norm_g': out['norm_g'], 'w_in': out['w_in'], 'conv_dw_w': out['conv_dw_w'], 'conv_dw_b': out['conv_dw_b'], 'conv_ln_g': out['conv_ln_g'], 'conv_ln_b': out['conv_ln_b'], 'w_conv_out': out['w_conv_out'], 'attn_sinks': out['attn_sinks'], 'w_attn_out': out['w_attn_out'], 'w_out': out['w_out'], 'final_norm_g': out['final_norm_g'], 'loss_target': out['loss_target'], 'm_norm_g': out['m_norm_g'], 'm_w_in': out['m_w_in'], 'm_conv_dw_w': out['m_conv_dw_w'], 'm_conv_dw_b': out['m_conv_dw_b'], 'm_conv_ln_g': out['m_conv_ln_g'], 'm_conv_ln_b': out['m_conv_ln_b'], 'm_w_conv_out': out['m_w_conv_out'], 'm_attn_sinks': out['m_attn_sinks'], 'm_w_attn_out': out['m_w_attn_out'], 'm_w_out': out['m_w_out'], 'm_final_norm_g': out['m_final_norm_g'], 'v_norm_g': out['v_norm_g'], 'v_w_in': out['v_w_in'], 'v_conv_dw_w': out['v_conv_dw_w'], 'v_conv_dw_b': out['v_conv_dw_b'], 'v_conv_ln_g': out['v_conv_ln_g'], 'v_conv_ln_b': out['v_conv_ln_b'], 'v_w_conv_out': out['v_w_conv_out'], 'v_attn_sinks': out['v_attn_sinks'], 'v_w_attn_out': out['v_w_attn_out'], 'v_w_out': out['v_w_out'], 'v_final_norm_g': out['v_final_norm_g']}


def _loss(weights, diff, rest, loss_target):
    with _jax.named_scope("forward"):
        args = {**rest, TWIN_DIFF_INPUT: diff, **{k: w.astype(_WEIGHT_DTYPES[k]) for k, w in weights.items()}}
        y = _forward(args)
    with _jax.named_scope("loss_head"):
        err = _jnp.square(y.astype(_jnp.float32) - loss_target)
        return 0.5 * _jnp.sum(_jnp.mean(err, axis=-1)) if err.ndim else 0.5 * err


def _adamw(w, g, m, v):
    m = ADAM_B1 * m + (1.0 - ADAM_B1) * g
    v = ADAM_B2 * v + (1.0 - ADAM_B2) * _jnp.square(g)
    m_hat = m / (1.0 - ADAM_B1 ** ADAM_STEP)
    v_hat = v / (1.0 - ADAM_B2 ** ADAM_STEP)
    delta = -ADAM_LR * (m_hat / (_jnp.sqrt(v_hat) + ADAM_EPS) + ADAM_WD * w)
    return delta, m, v


def reference(x, norm_g, w_in, conv_dw_w, conv_dw_b, conv_ln_g, conv_ln_b, w_conv_out, attn_sinks, w_attn_out, w_out, final_norm_g, loss_target, m_norm_g, m_w_in, m_conv_dw_w, m_conv_dw_b, m_conv_ln_g, m_conv_ln_b, m_w_conv_out, m_attn_sinks, m_w_attn_out, m_w_out, m_final_norm_g, v_norm_g, v_w_in, v_conv_dw_w, v_conv_dw_b, v_conv_ln_g, v_conv_ln_b, v_w_conv_out, v_attn_sinks, v_w_attn_out, v_w_out, v_final_norm_g):
    given = dict(x=x, norm_g=norm_g, w_in=w_in, conv_dw_w=conv_dw_w, conv_dw_b=conv_dw_b, conv_ln_g=conv_ln_g, conv_ln_b=conv_ln_b, w_conv_out=w_conv_out, attn_sinks=attn_sinks, w_attn_out=w_attn_out, w_out=w_out, final_norm_g=final_norm_g, loss_target=loss_target, m_norm_g=m_norm_g, m_w_in=m_w_in, m_conv_dw_w=m_conv_dw_w, m_conv_dw_b=m_conv_dw_b, m_conv_ln_g=m_conv_ln_g, m_conv_ln_b=m_conv_ln_b, m_w_conv_out=m_w_conv_out, m_attn_sinks=m_attn_sinks, m_w_attn_out=m_w_attn_out, m_w_out=m_w_out, m_final_norm_g=m_final_norm_g, v_norm_g=v_norm_g, v_w_in=v_w_in, v_conv_dw_w=v_conv_dw_w, v_conv_dw_b=v_conv_dw_b, v_conv_ln_g=v_conv_ln_g, v_conv_ln_b=v_conv_ln_b, v_w_conv_out=v_w_conv_out, v_attn_sinks=v_attn_sinks, v_w_attn_out=v_w_attn_out, v_w_out=v_w_out, v_final_norm_g=v_final_norm_g)
    weights = {n: given[n] for n in TWIN_WEIGHTS}
    shared = {n: given[n] for n in SHARED_INPUTS}
    per_example = {n: given[n] for n in ['x']}
    grad_fn = _jax.value_and_grad(_loss, argnums=(0, 1))

    def one_microbatch(ex, loss_target):
        ex = dict(ex)
        diff = ex.pop(TWIN_DIFF_INPUT)
        return grad_fn(weights, diff, {**shared, **ex}, loss_target)

    if N_MICROBATCH == 1:
        loss, (grad_w, grad_x) = one_microbatch(per_example, given["loss_target"])
    else:
        def body(carry, xs):
            loss_sum, grad_sum = carry
            l_k, (gw_k, gx_k) = one_microbatch(xs[0], xs[1])
            with _jax.named_scope("update"):
                return (loss_sum + l_k, _jax.tree.map(_jnp.add, grad_sum, gw_k)), gx_k

        init = (_jnp.zeros((), _jnp.float32), _jax.tree.map(_jnp.zeros_like, weights))
        (loss, grad_w), grad_x = _jax.lax.scan(body, init, (per_example, given["loss_target"]))
    with _jax.named_scope("update"):
        delta_w, new_m, new_v = {}, {}, {}
        for n in TWIN_WEIGHTS:
            delta_w[n], new_m[n], new_v[n] = _adamw(weights[n], grad_w[n], given["m_" + n], given["v_" + n])
    return (loss, grad_x, *[grad_w[n] for n in TWIN_WEIGHTS], *[delta_w[n] for n in TWIN_WEIGHTS],
            *[new_m[n] for n in TWIN_WEIGHTS], *[new_v[n] for n in TWIN_WEIGHTS])
```

```python
import jax
import jax.numpy as jnp
from jax import lax
from jax.experimental import pallas as pl
from jax.experimental.pallas import tpu as pltpu

F32 = jnp.float32
BF16 = jnp.bfloat16
MESH = pl.DeviceIdType.MESH

D_MODEL = 1024
IN_WIDTH = 7680
N_DEV = 8
SHARD_IN = IN_WIDTH // N_DEV
SHARD_SQ = D_MODEL // N_DEV
CONV_KERNEL = 31
CONV_PAD = 32
HEAD_DIM = 64
N_Q_HEADS = 16
N_KV_HEADS = 4
BLOCK = 128
LANES = 128
ROPE_THETA = 10000.0
RMS_EPS = 1e-5
LN_EPS = 1e-5
NEG = -1e30
ADAM_LR = 0.001
ADAM_B1 = 0.9
ADAM_B2 = 0.999
ADAM_EPS = 1e-08
ADAM_WD = 0.01
ADAM_STEP = 10

OFF_A, OFF_B, OFF_CG, OFF_Q, OFF_KV, OFF_AG, OFF_MLC, OFF_MLA = 0, 1024, 2048, 3072, 4096, 4608, 5632, 6656
COL_A, COL_B, COL_CG, COL_Q, COL_AG, COL_MLC, COL_MLA, COL_KV512 = 0, 1, 2, 3, 4, 5, 6, 14

VMEM_LIMIT = 56 * 1024 * 1024


def _cparams(sem=None, vmem=None):
    return pltpu.CompilerParams(dimension_semantics=sem, vmem_limit_bytes=vmem)


def _sig(v):
    return 1.0 / (1.0 + jnp.exp(-v))


def _dot(a, b):
    return jnp.dot(a, b, preferred_element_type=F32)


def _dot_nt(a, b):
    return lax.dot_general(a, b, (((1,), (1,)), ((), ())), preferred_element_type=F32)


def _dot_tn(a, b):
    return lax.dot_general(a, b, (((0,), (0,)), ((), ())), preferred_element_type=F32)


def _const_spec(shape):
    nd = len(shape)
    return pl.BlockSpec(shape, lambda *_: (0,) * nd)


def _mesh_pos():
    x, y, c = lax.axis_index("x"), lax.axis_index("y"), lax.axis_index("c")
    return x, y, c, 4 * x + 2 * y + c


def _peer(x, y, c, k):
    px = 1 - x if (k >> 2) & 1 else x
    py = 1 - y if (k >> 1) & 1 else y
    pc = 1 - c if k & 1 else c
    return (px, py, pc), 4 * px + 2 * py + pc


def _all_gather_weights(w_in_t, w_co, w_ao, w_out, conv_w):
    shards = (w_in_t, w_co, w_ao, w_out, conv_w)
    n_arr = len(shards)

    def body(*refs):
        srcs, outs = refs[:n_arr], refs[n_arr:2 * n_arr]
        send_sems, recv_sems, local_sems = refs[2 * n_arr:]
        x, y, c, me = _mesh_pos()

        def block(a, idx):
            if a == 4:
                return outs[a].at[idx]
            rows = srcs[a].shape[0]
            return outs[a].at[pl.ds(pl.multiple_of(idx * rows, 64), rows)]

        own = [pltpu.make_async_copy(srcs[a], block(a, me), local_sems.at[a]) for a in range(n_arr)]
        for cp in own:
            cp.start()
        sent = []
        for k in range(1, N_DEV):
            peer, _ = _peer(x, y, c, k)
            for a in range(n_arr):
                cp = pltpu.make_async_remote_copy(
                    src_ref=srcs[a], dst_ref=block(a, me), send_sem=send_sems.at[a, k - 1],
                    recv_sem=recv_sems.at[a, k - 1], device_id=peer, device_id_type=MESH)
                cp.start()
                sent.append(cp)
        for k in range(1, N_DEV):
            peer, peer_idx = _peer(x, y, c, k)
            for a in range(n_arr):
                pltpu.make_async_remote_copy(
                    src_ref=srcs[a], dst_ref=block(a, peer_idx), send_sem=send_sems.at[a, k - 1],
                    recv_sem=recv_sems.at[a, k - 1], device_id=peer, device_id_type=MESH).wait_recv()
        for cp in sent:
            cp.wait_send()
        for cp in own:
            cp.wait()

    hbm = pl.BlockSpec(memory_space=pltpu.HBM)
    out_shape = (
        jax.ShapeDtypeStruct((IN_WIDTH, D_MODEL), BF16),
        jax.ShapeDtypeStruct((D_MODEL, D_MODEL), BF16),
        jax.ShapeDtypeStruct((D_MODEL, D_MODEL), BF16),
        jax.ShapeDtypeStruct((D_MODEL, D_MODEL), BF16),
        jax.ShapeDtypeStruct((N_DEV, CONV_PAD, LANES), F32),
    )
    return pl.pallas_call(
        body, name="gather_weights", out_shape=out_shape,
        in_specs=[hbm] * n_arr, out_specs=(hbm,) * n_arr,
        scratch_shapes=[pltpu.SemaphoreType.DMA((n_arr, N_DEV - 1)),
                        pltpu.SemaphoreType.DMA((n_arr, N_DEV - 1)),
                        pltpu.SemaphoreType.DMA((n_arr,))],
    )(*shards)


def _exchange_grads(g_in_t, g_co, g_ao, g_out, g_conv, small):
    parts = (g_in_t, g_co, g_ao, g_out, g_conv, small)
    n_arr = len(parts)
    rows = (SHARD_IN, SHARD_SQ, SHARD_SQ, SHARD_SQ)

    def body(*refs):
        srcs, outs = refs[:n_arr], refs[n_arr:2 * n_arr]
        send_sems, recv_sems, local_sems = refs[2 * n_arr:]
        x, y, c, me = _mesh_pos()

        def src_block(a, idx):
            if a < 4:
                return srcs[a].at[pl.ds(pl.multiple_of(idx * rows[a], 64), rows[a])]
            if a == 4:
                return srcs[a].at[idx]
            return srcs[a]

        own = [pltpu.make_async_copy(src_block(a, me), outs[a].at[me], local_sems.at[a]) for a in range(n_arr)]
        for cp in own:
            cp.start()
        sent = []
        for k in range(1, N_DEV):
            peer, peer_idx = _peer(x, y, c, k)
            for a in range(n_arr):
                cp = pltpu.make_async_remote_copy(
                    src_ref=src_block(a, peer_idx), dst_ref=outs[a].at[me], send_sem=send_sems.at[a, k - 1],
                    recv_sem=recv_sems.at[a, k - 1], device_id=peer, device_id_type=MESH)
                cp.start()
                sent.append(cp)
        for k in range(1, N_DEV):
            peer, peer_idx = _peer(x, y, c, k)
            for a in range(n_arr):
                pltpu.make_async_remote_copy(
                    src_ref=src_block(a, me), dst_ref=outs[a].at[peer_idx], send_sem=send_sems.at[a, k - 1],
                    recv_sem=recv_sems.at[a, k - 1], device_id=peer, device_id_type=MESH).wait_recv()
        for cp in sent:
            cp.wait_send()
        for cp in own:
            cp.wait()

    hbm = pl.BlockSpec(memory_space=pltpu.HBM)
    out_shape = (
        jax.ShapeDtypeStruct((N_DEV, SHARD_IN, D_MODEL), BF16),
        jax.ShapeDtypeStruct((N_DEV, SHARD_SQ, D_MODEL), BF16),
        jax.ShapeDtypeStruct((N_DEV, SHARD_SQ, D_MODEL), BF16),
        jax.ShapeDtypeStruct((N_DEV, SHARD_SQ, D_MODEL), BF16),
        jax.ShapeDtypeStruct((N_DEV, CONV_PAD, LANES), F32),
        jax.ShapeDtypeStruct((N_DEV, 8, D_MODEL), F32),
    )
    return pl.pallas_call(
        body, name="exchange_grads", out_shape=out_shape,
        in_specs=[hbm] * n_arr, out_specs=(hbm,) * n_arr,
        scratch_shapes=[pltpu.SemaphoreType.DMA((n_arr, N_DEV - 1)),
                        pltpu.SemaphoreType.DMA((n_arr, N_DEV - 1)),
                        pltpu.SemaphoreType.DMA((n_arr,))],
    )(*parts)


def _in_projection(x, norm_g, w_in_t):
    tokens = x.shape[0]
    tm, tn = min(1024, tokens), 512
    n_col = IN_WIDTH // tn

    def body(x_ref, g_ref, w_ref, proj_ref, h_ref, h_scr):
        @pl.when(pl.program_id(1) == 0)
        def _():
            def chunk(r0):
                xv = x_ref[pl.ds(r0, 128), :]
                r = lax.rsqrt(jnp.mean(xv * xv, axis=-1, keepdims=True) + RMS_EPS)
                h = (xv * r * g_ref[...]).astype(BF16)
                h_scr[pl.ds(r0, 128), :] = h
                h_ref[pl.ds(r0, 128), :] = h
            _row_chunks(tm, 128, chunk)
        proj_ref[...] = _dot_nt(h_scr[...], w_ref[...]).astype(BF16)

    def w_map(i, j):
        return (jnp.where(j < 8, j, jnp.where(j < 14, j + 1, 8)), 0)

    return pl.pallas_call(
        body, name="in_projection",
        grid=(tokens // tm, n_col),
        in_specs=[pl.BlockSpec((tm, D_MODEL), lambda i, j: (i, 0)),
                  pl.BlockSpec((1, D_MODEL), lambda i, j: (0, 0)),
                  pl.BlockSpec((tn, D_MODEL), w_map)],
        out_specs=(pl.BlockSpec((tm, tn), lambda i, j: (i, j)),
                   pl.BlockSpec((tm, D_MODEL), lambda i, j: (i, 0))),
        out_shape=(jax.ShapeDtypeStruct((tokens, IN_WIDTH), BF16),
                   jax.ShapeDtypeStruct((tokens, D_MODEL), BF16)),
        scratch_shapes=[pltpu.VMEM((tm, D_MODEL), BF16)],
        compiler_params=_cparams(("parallel", "arbitrary"), VMEM_LIMIT),
    )(x, norm_g, w_in_t)


def _row_chunks(total, size, fn):
    n = total // size
    if n == 1:
        fn(0)
        return

    def step(i, carry):
        fn(pl.multiple_of(i * size, size))
        return carry
    lax.fori_loop(0, n, step, 0)


def _rope_tables(tokens):
    inv_freq = ROPE_THETA ** (-jnp.arange(0, HEAD_DIM, 2, dtype=F32) / HEAD_DIM)
    ang = jnp.arange(tokens, dtype=jnp.int32).astype(F32)[:, None] * inv_freq[None, :]
    cos, sin = jnp.cos(ang), jnp.sin(ang)
    zero = jnp.zeros_like(sin)
    cos_t = jnp.tile(jnp.concatenate([cos, cos], axis=1), (1, LANES // HEAD_DIM))
    sin_up = jnp.tile(jnp.concatenate([-sin, zero], axis=1), (1, LANES // HEAD_DIM))
    sin_dn = jnp.tile(jnp.concatenate([zero, sin], axis=1), (1, LANES // HEAD_DIM))
    return cos_t, sin_up, sin_dn


def _rope(t, cos_t, sin_up, sin_dn):
    return t * cos_t + pltpu.roll(t, LANES - 32, 1) * sin_up + pltpu.roll(t, 32, 1) * sin_dn


def _rope_transposed(g, cos_t, sin_up, sin_dn):
    return g * cos_t + pltpu.roll(g * sin_up, 32, 1) + pltpu.roll(g * sin_dn, LANES - 32, 1)


def _lane_halves():
    lane = lax.broadcasted_iota(jnp.int32, (BLOCK, LANES), 1)
    return lane < HEAD_DIM


def _rope_qkv(proj, cos_t, sin_up, sin_dn):
    tokens = proj.shape[0]
    tm = min(512, tokens)
    scale = HEAD_DIM ** -0.5

    def body(q_ref, kv_ref, cos_ref, up_ref, dn_ref, qr_ref, kd_ref, vd_ref):
        lo = _lane_halves()

        def chunk(r0):
            rows = pl.ds(r0, BLOCK)
            cs, up, dn = cos_ref[rows, :], up_ref[rows, :], dn_ref[rows, :]
            for p in range(D_MODEL // LANES):
                sl = slice(LANES * p, LANES * (p + 1))
                qt = q_ref[rows, sl].astype(F32)
                qr_ref[rows, sl] = (_rope(qt, cs, up, dn) * scale).astype(BF16)
            for p in range(2):
                sl = slice(LANES * p, LANES * (p + 1))
                kt = _rope(kv_ref[rows, sl].astype(F32), cs, up, dn)
                vt = kv_ref[rows, slice(256 + LANES * p, 256 + LANES * (p + 1))].astype(F32)
                for src, dst in ((kt, kd_ref), (vt, vd_ref)):
                    first = jnp.where(lo, src, 0.0)
                    second = src - first
                    dst[rows, slice(LANES * 2 * p, LANES * (2 * p + 1))] = (first + pltpu.roll(first, HEAD_DIM, 1)).astype(BF16)
                    dst[rows, slice(LANES * (2 * p + 1), LANES * (2 * p + 2))] = (second + pltpu.roll(second, HEAD_DIM, 1)).astype(BF16)
        _row_chunks(tm, BLOCK, chunk)

    tab = pl.BlockSpec((tm, LANES), lambda i: (i, 0))
    return pl.pallas_call(
        body, name="rope_qkv", grid=(tokens // tm,),
        in_specs=[pl.BlockSpec((tm, D_MODEL), lambda i: (i, COL_Q)),
                  pl.BlockSpec((tm, 512), lambda i: (i, COL_KV512)), tab, tab, tab],
        out_specs=(pl.BlockSpec((tm, D_MODEL), lambda i: (i, 0)),
                   pl.BlockSpec((tm, 512), lambda i: (i, 0)),
                   pl.BlockSpec((tm, 512), lambda i: (i, 0))),
        out_shape=(jax.ShapeDtypeStruct((tokens, D_MODEL), BF16),
                   jax.ShapeDtypeStruct((tokens, 512), BF16),
                   jax.ShapeDtypeStruct((tokens, 512), BF16)),
        compiler_params=_cparams(("parallel",)),
    )(proj, proj, cos_t, sin_up, sin_dn)


CONV_TM = 256
N_LANE_CHUNKS = D_MODEL // LANES


def _fill_u_ext(u_ext, a_ref, b_ref, ah_ref, bh_ref, first_tile):
    for lc in range(N_LANE_CHUNKS):
        sl = slice(LANES * lc, LANES * (lc + 1))
        uh = ah_ref[:, sl].astype(F32) * _sig(bh_ref[:, sl].astype(F32))
        u_ext[lc, 0:CONV_PAD, :] = jnp.where(first_tile, 0.0, uh)
        u_ext[lc, CONV_PAD:CONV_PAD + CONV_TM, :] = a_ref[:, sl].astype(F32) * _sig(b_ref[:, sl].astype(F32))


def _conv_forward(proj, conv_w, dw_b, ln_g, ln_b, w_co):
    tokens = proj.shape[0]
    tm = CONV_TM
    halo_blocks = tm // CONV_PAD

    def body(a_ref, b_ref, ah_ref, bh_ref, cg_ref, cw_ref, dwb_ref, lng_ref, lnb_ref, wco_ref,
             cv_ref, yc_ref, u_ext, cv_scr):
        _fill_u_ext(u_ext, a_ref, b_ref, ah_ref, bh_ref, pl.program_id(0) == 0)

        def lane_chunk(lc, carry):
            for rc in range(tm // 64):
                acc = jnp.zeros((64, LANES), F32)
                for j in range(CONV_KERNEL):
                    acc = acc + cw_ref[lc, pl.ds(j, 1), :] * u_ext[lc, pl.ds(64 * rc + 2 + j, 64), :]
                cv_scr[lc, pl.ds(64 * rc, 64), :] = acc
            return carry
        lax.fori_loop(0, N_LANE_CHUNKS, lane_chunk, 0)

        cv = jnp.concatenate([cv_scr[lc] for lc in range(N_LANE_CHUNKS)], axis=1) + dwb_ref[...]
        cv_ref[...] = cv
        mu = jnp.mean(cv, axis=-1, keepdims=True)
        zc = cv - mu
        rstd = lax.rsqrt(jnp.mean(zc * zc, axis=-1, keepdims=True) + LN_EPS)
        ln = zc * rstd * lng_ref[...] + lnb_ref[...]
        cg = cg_ref[...].astype(F32)
        pc = (ln * _sig(ln)) * (cg * _sig(cg))
        yc_ref[...] = _dot(pc.astype(BF16), wco_ref[...]).astype(BF16)

    def halo_map(i):
        return (jnp.maximum(i * halo_blocks - 1, 0), 0)

    tile = lambda col: pl.BlockSpec((tm, D_MODEL), lambda i: (i, col))
    return pl.pallas_call(
        body, name="conv_forward", grid=(tokens // tm,),
        in_specs=[tile(COL_A), tile(COL_B),
                  pl.BlockSpec((CONV_PAD, D_MODEL), lambda i: (halo_map(i)[0], COL_A)),
                  pl.BlockSpec((CONV_PAD, D_MODEL), lambda i: (halo_map(i)[0], COL_B)),
                  tile(COL_CG), _const_spec((N_DEV, CONV_PAD, LANES)),
                  _const_spec((1, D_MODEL)), _const_spec((1, D_MODEL)), _const_spec((1, D_MODEL)),
                  _const_spec((D_MODEL, D_MODEL))],
        out_specs=(pl.BlockSpec((tm, D_MODEL), lambda i: (i, 0)),
                   pl.BlockSpec((tm, D_MODEL), lambda i: (i, 0))),
        out_shape=(jax.ShapeDtypeStruct((tokens, D_MODEL), F32),
                   jax.ShapeDtypeStruct((tokens, D_MODEL), BF16)),
        scratch_shapes=[pltpu.VMEM((N_LANE_CHUNKS, CONV_PAD + tm, LANES), F32),
                        pltpu.VMEM((N_LANE_CHUNKS, tm, LANES), F32)],
        compiler_params=_cparams(("parallel",), VMEM_LIMIT),
    )(proj, proj, proj, proj, proj, conv_w, dw_b, ln_g, ln_b, w_co)


def _band_masks(n):
    row = lax.broadcasted_iota(jnp.int32, (4 * BLOCK, BLOCK), 0) & (BLOCK - 1)
    col = lax.broadcasted_iota(jnp.int32, (4 * BLOCK, BLOCK), 1)
    return col <= row, jnp.logical_and(col > row, n > 0)


def _stack_heads(tile_a, tile_b, lo):
    zero = jnp.zeros_like(tile_a)
    return jnp.concatenate([jnp.where(lo, tile_a, zero), jnp.where(lo, zero, tile_a),
                            jnp.where(lo, tile_b, zero), jnp.where(lo, zero, tile_b)], axis=0)


def _unstack_heads(stacked, lo):
    s = [stacked[BLOCK * g:BLOCK * (g + 1)] for g in range(4)]
    return (jnp.where(lo, s[0], 0.0) + jnp.where(lo, 0.0, s[1]),
            jnp.where(lo, s[2], 0.0) + jnp.where(lo, 0.0, s[3]))


def _sink_column(sinks_ref, kvh):
    return jnp.concatenate([jnp.full((BLOCK, 1), sinks_ref[0, 4 * kvh + g], F32) for g in range(4)], axis=0)


def _softmax_band(q_stack, k_cur, k_prev, sink, cur_ok, prev_ok):
    s_c = jnp.where(cur_ok, _dot_nt(q_stack, k_cur), NEG)
    s_p = jnp.where(prev_ok, _dot_nt(q_stack, k_prev), NEG)
    m = jnp.maximum(jnp.maximum(jnp.max(s_c, axis=1, keepdims=True), jnp.max(s_p, axis=1, keepdims=True)), sink)
    e_c, e_p, e_s = jnp.exp(s_c - m), jnp.exp(s_p - m), jnp.exp(sink - m)
    inv = 1.0 / (jnp.sum(e_c, axis=1, keepdims=True) + jnp.sum(e_p, axis=1, keepdims=True) + e_s)
    return e_c * inv, e_p * inv, e_s * inv


def _attention_forward(qr, kd, vd, proj, sinks, w_ao):
    tokens = qr.shape[0]
    nb = tokens // BLOCK

    def body(q_ref, kc_ref, kp_ref, vc_ref, vp_ref, ag_ref, sinks_ref, wao_ref, o_ref, ya_ref, o_scr):
        n = pl.program_id(0)
        lo = _lane_halves()
        cur_ok, prev_ok = _band_masks(n)
        for kvh in range(N_KV_HEADS):
            ta, tb = slice(LANES * 2 * kvh, LANES * (2 * kvh + 1)), slice(LANES * (2 * kvh + 1), LANES * (2 * kvh + 2))
            ks = slice(LANES * kvh, LANES * (kvh + 1))
            q_stack = _stack_heads(q_ref[:, ta], q_ref[:, tb], lo)
            p_c, p_p, _ = _softmax_band(q_stack, kc_ref[:, ks], kp_ref[:, ks], _sink_column(sinks_ref, kvh), cur_ok, prev_ok)
            o_stack = _dot(p_c.astype(BF16), vc_ref[:, ks]) + _dot(p_p.astype(BF16), vp_ref[:, ks])
            o_scr[:, ta], o_scr[:, tb] = _unstack_heads(o_stack, lo)
        o = o_scr[...]
        o_ref[...] = o.astype(BF16)
        ag = ag_ref[...].astype(F32)
        ya_ref[...] = _dot((o * (ag * _sig(ag))).astype(BF16), wao_ref[...]).astype(BF16)

    cur = lambda w, col=0: pl.BlockSpec((BLOCK, w), lambda n: (n, col))
    prev = lambda w: pl.BlockSpec((BLOCK, w), lambda n: (jnp.maximum(n - 1, 0), 0))
    return pl.pallas_call(
        body, name="attention_forward", grid=(nb,),
        in_specs=[cur(D_MODEL), cur(512), prev(512), cur(512), prev(512), cur(D_MODEL, COL_AG),
                  pl.BlockSpec(memory_space=pltpu.SMEM), _const_spec((D_MODEL, D_MODEL))],
        out_specs=(cur(D_MODEL), cur(D_MODEL)),
        out_shape=(jax.ShapeDtypeStruct((tokens, D_MODEL), BF16),
                   jax.ShapeDtypeStruct((tokens, D_MODEL), BF16)),
        scratch_shapes=[pltpu.VMEM((BLOCK, D_MODEL), F32)],
        compiler_params=_cparams(("parallel",), VMEM_LIMIT),
    )(qr, kd, kd, vd, vd, proj, sinks, w_ao)


def _merge_and_head(yc, ya, proj, x, target, w_out, final_g):
    tokens = x.shape[0]
    tm = 256
    last = tokens // tm - 1

    def body(yc_ref, ya_ref, mlc_ref, mla_ref, x_ref, t_ref, wout_ref, fg_ref,
             dx2_ref, dyc_ref, dya_ref, dmlc_ref, dmla_ref, gwout_ref, part_ref, gacc):
        i = pl.program_id(0)

        @pl.when(i == 0)
        def _():
            gacc[...] = jnp.zeros_like(gacc)
            part_ref[...] = jnp.zeros_like(part_ref)

        yc, ya = yc_ref[...].astype(F32), ya_ref[...].astype(F32)
        gc, ga = _sig(mlc_ref[...].astype(F32)), _sig(mla_ref[...].astype(F32))
        merged = (gc * yc + ga * ya).astype(BF16)
        x2 = x_ref[...] + _dot(merged, wout_ref[...])
        r2 = lax.rsqrt(jnp.mean(x2 * x2, axis=-1, keepdims=True) + RMS_EPS)
        x2n = x2 * r2
        fg = fg_ref[...]
        err = x2n * fg - t_ref[...]
        dy = err * (1.0 / D_MODEL)
        part_ref[0:1, :] += jnp.sum(dy * x2n, axis=0, keepdims=True)
        part_ref[1:2, :] += jnp.sum(err * err, axis=0, keepdims=True) * (0.5 / D_MODEL)
        dx2n = dy * fg
        dx2 = r2 * (dx2n - x2n * jnp.mean(dx2n * x2n, axis=-1, keepdims=True))
        dx2_ref[...] = dx2
        dx2b = dx2.astype(BF16)
        gacc[...] += _dot_tn(merged, dx2b)
        dm = _dot_nt(dx2b, wout_ref[...])
        dyc_ref[...] = (dm * gc).astype(BF16)
        dya_ref[...] = (dm * ga).astype(BF16)
        dmlc_ref[...] = (dm * yc * (gc * (1.0 - gc))).astype(BF16)
        dmla_ref[...] = (dm * ya * (ga * (1.0 - ga))).astype(BF16)

        @pl.when(i == last)
        def _():
            gwout_ref[...] = gacc[...].astype(BF16)

    tile = lambda col=0: pl.BlockSpec((tm, D_MODEL), lambda i: (i, col))
    return pl.pallas_call(
        body, name="merge_and_head", grid=(tokens // tm,),
        in_specs=[tile(), tile(), tile(COL_MLC), tile(COL_MLA), tile(), tile(),
                  _const_spec((D_MODEL, D_MODEL)), _const_spec((1, D_MODEL))],
        out_specs=(tile(), tile(), tile(), tile(), tile(),
                   _const_spec((D_MODEL, D_MODEL)), _const_spec((8, D_MODEL))),
        out_shape=(jax.ShapeDtypeStruct((tokens, D_MODEL), F32),
                   jax.ShapeDtypeStruct((tokens, D_MODEL), BF16),
                   jax.ShapeDtypeStruct((tokens, D_MODEL), BF16),
                   jax.ShapeDtypeStruct((tokens, D_MODEL), BF16),
                   jax.ShapeDtypeStruct((tokens, D_MODEL), BF16),
                   jax.ShapeDtypeStruct((D_MODEL, D_MODEL), BF16),
                   jax.ShapeDtypeStruct((8, D_MODEL), F32)),
        scratch_shapes=[pltpu.VMEM((D_MODEL, D_MODEL), F32)],
        compiler_params=_cparams(("arbitrary",), VMEM_LIMIT),
    )(yc, ya, proj, proj, x, target, w_out, final_g)


def _conv_backward_pointwise(dyc, cv, proj, w_co, ln_g, ln_b):
    tokens = cv.shape[0]
    tm = 256
    last = tokens // tm - 1

    def body(dyc_ref, cv_ref, cg_ref, wco_ref, lng_ref, lnb_ref, dcv_ref, dcg_ref, gwco_ref, part_ref, gacc):
        i = pl.program_id(0)

        @pl.when(i == 0)
        def _():
            gacc[...] = jnp.zeros_like(gacc)
            part_ref[...] = jnp.zeros_like(part_ref)

        cv = cv_ref[...]
        mu = jnp.mean(cv, axis=-1, keepdims=True)
        zc = cv - mu
        rstd = lax.rsqrt(jnp.mean(zc * zc, axis=-1, keepdims=True) + LN_EPS)
        z = zc * rstd
        lng = lng_ref[...]
        ln = z * lng + lnb_ref[...]
        sl = _sig(ln)
        c = ln * sl
        cg = cg_ref[...].astype(F32)
        scg = _sig(cg)
        gate = cg * scg
        dyc = dyc_ref[...]
        gacc[...] += _dot_tn((c * gate).astype(BF16), dyc)
        dpc = _dot_nt(dyc, wco_ref[...])
        dcg_ref[...] = (dpc * c * (scg * (1.0 + cg * (1.0 - scg)))).astype(BF16)
        dln = dpc * gate * (sl * (1.0 + ln * (1.0 - sl)))
        part_ref[0:1, :] += jnp.sum(dln * z, axis=0, keepdims=True)
        part_ref[1:2, :] += jnp.sum(dln, axis=0, keepdims=True)
        dz = dln * lng
        dcv = rstd * (dz - jnp.mean(dz, axis=-1, keepdims=True) - z * jnp.mean(dz * z, axis=-1, keepdims=True))
        part_ref[2:3, :] += jnp.sum(dcv, axis=0, keepdims=True)
        dcv_ref[...] = dcv

        @pl.when(i == last)
        def _():
            gwco_ref[...] = gacc[...].astype(BF16)

    tile = lambda col=0: pl.BlockSpec((tm, D_MODEL), lambda i: (i, col))
    return pl.pallas_call(
        body, name="conv_backward_pointwise", grid=(tokens // tm,),
        in_specs=[tile(), tile(), tile(COL_CG), _const_spec((D_MODEL, D_MODEL)),
                  _const_spec((1, D_MODEL)), _const_spec((1, D_MODEL))],
        out_specs=(tile(), tile(), _const_spec((D_MODEL, D_MODEL)), _const_spec((8, D_MODEL))),
        out_shape=(jax.ShapeDtypeStruct((tokens, D_MODEL), F32),
                   jax.ShapeDtypeStruct((tokens, D_MODEL), BF16),
                   jax.ShapeDtypeStruct((D_MODEL, D_MODEL), BF16),
                   jax.ShapeDtypeStruct((8, D_MODEL), F32)),
        scratch_shapes=[pltpu.VMEM((D_MODEL, D_MODEL), F32)],
        compiler_params=_cparams(("arbitrary",), VMEM_LIMIT),
    )(dyc, cv, proj, w_co, ln_g, ln_b)


def _conv_backward_taps(dcv, proj, conv_w):
    tokens = dcv.shape[0]
    tm = CONV_TM
    nt = tokens // tm
    halo_blocks = tm // CONV_PAD

    def body(d_ref, dn_ref, a_ref, b_ref, ah_ref, bh_ref, cw_ref, da_ref, db_ref, gw_ref, u_ext, d_ext, du_scr, gw_acc):
        i = pl.program_id(0)

        @pl.when(i == 0)
        def _():
            gw_acc[...] = jnp.zeros_like(gw_acc)

        _fill_u_ext(u_ext, a_ref, b_ref, ah_ref, bh_ref, i == 0)
        for lc in range(N_LANE_CHUNKS):
            sl = slice(LANES * lc, LANES * (lc + 1))
            d_ext[lc, 0:tm, :] = d_ref[:, sl]
            d_ext[lc, tm:tm + CONV_PAD, :] = jnp.where(i == nt - 1, 0.0, dn_ref[:, sl])

        def lane_chunk(lc, carry):
            n_rc = tm // 64
            du = [jnp.zeros((64, LANES), F32) for _ in range(n_rc)]
            for j in range(CONV_KERNEL):
                w = cw_ref[lc, pl.ds(j, 1), :]
                gsum = jnp.zeros((8, LANES), F32)
                for rc in range(n_rc):
                    du[rc] = du[rc] + w * d_ext[lc, pl.ds(64 * rc + 30 - j, 64), :]
                    prod = d_ext[lc, pl.ds(64 * rc, 64), :] * u_ext[lc, pl.ds(64 * rc + 2 + j, 64), :]
                    gsum = gsum + jnp.sum(prod.reshape(8, 8, LANES), axis=0)
                gw_acc[lc, j] += gsum
            for rc in range(n_rc):
                du_scr[lc, pl.ds(64 * rc, 64), :] = du[rc]
            return carry
        lax.fori_loop(0, N_LANE_CHUNKS, lane_chunk, 0)

        du = jnp.concatenate([du_scr[lc] for lc in range(N_LANE_CHUNKS)], axis=1)
        a, b = a_ref[...].astype(F32), b_ref[...].astype(F32)
        sb = _sig(b)
        da_ref[...] = (du * sb).astype(BF16)
        db_ref[...] = (du * a * (sb * (1.0 - sb))).astype(BF16)

        @pl.when(i == nt - 1)
        def _():
            gw_ref[...] = jnp.sum(gw_acc[...], axis=2)

    def prev_halo(i):
        return jnp.maximum(i * halo_blocks - 1, 0)

    def next_halo(i):
        return jnp.minimum((i + 1) * halo_blocks, tokens // CONV_PAD - 1)

    tile = lambda col=0: pl.BlockSpec((tm, D_MODEL), lambda i: (i, col))
    return pl.pallas_call(
        body, name="conv_backward_taps", grid=(nt,),
        in_specs=[tile(), pl.BlockSpec((CONV_PAD, D_MODEL), lambda i: (next_halo(i), 0)),
                  tile(COL_A), tile(COL_B),
                  pl.BlockSpec((CONV_PAD, D_MODEL), lambda i: (prev_halo(i), COL_A)),
                  pl.BlockSpec((CONV_PAD, D_MODEL), lambda i: (prev_halo(i), COL_B)),
                  _const_spec((N_DEV, CONV_PAD, LANES))],
        out_specs=(tile(), tile(), _const_spec((N_DEV, CONV_PAD, LANES))),
        out_shape=(jax.ShapeDtypeStruct((tokens, D_MODEL), BF16),
                   jax.ShapeDtypeStruct((tokens, D_MODEL), BF16),
                   jax.ShapeDtypeStruct((N_DEV, CONV_PAD, LANES), F32)),
        scratch_shapes=[pltpu.VMEM((N_LANE_CHUNKS, CONV_PAD + tm, LANES), F32),
                        pltpu.VMEM((N_LANE_CHUNKS, tm + CONV_PAD, LANES), F32),
                        pltpu.VMEM((N_LANE_CHUNKS, tm, LANES), F32),
                        pltpu.VMEM((N_LANE_CHUNKS, CONV_PAD, 8, LANES), F32)],
        compiler_params=_cparams(("arbitrary",), VMEM_LIMIT),
    )(dcv, dcv, proj, proj, proj, proj, conv_w)


def _fold_kv_head(dup, lo, second_half):
    both = dup + pltpu.roll(dup, HEAD_DIM, 1)
    return jnp.where(lo, 0.0, both) if second_half else jnp.where(lo, both, 0.0)


def _attention_backward(dya, o, qr, kd, vd, proj, sinks, w_ao, cos_t, sin_up, sin_dn):
    tokens = qr.shape[0]
    nb = tokens // BLOCK
    scale = HEAD_DIM ** -0.5

    def body(dya_ref, o_ref, ag_ref, q_ref, kc_ref, kp_ref, vc_ref, vp_ref, sinks_ref, wao_ref,
             cos_c, up_c, dn_c, cos_p, up_p, dn_p,
             dq_ref, dkv_ref, dag_ref, gwao_ref, gsink_ref, gacc, dk_carry, dv_carry, dq_scr):
        n = pl.program_id(0)
        lo = _lane_halves()

        @pl.when(n == 0)
        def _():
            gacc[...] = jnp.zeros_like(gacc)
            gsink_ref[...] = jnp.zeros_like(gsink_ref)
            dk_carry[...] = jnp.zeros_like(dk_carry)
            dv_carry[...] = jnp.zeros_like(dv_carry)

        def emit_prev(dk_prev, dv_prev):
            for p in range(2):
                sl = slice(LANES * p, LANES * (p + 1))
                dk = _rope_transposed(dk_carry[:, sl] + dk_prev[p], cos_p[...], up_p[...], dn_p[...])
                dkv_ref[:, sl] = dk.astype(BF16)
                dkv_ref[:, slice(256 + LANES * p, 256 + LANES * (p + 1))] = (dv_carry[:, sl] + dv_prev[p]).astype(BF16)

        @pl.when(n < nb)
        def _():
            dya = dya_ref[...]
            dpa = _dot_nt(dya, wao_ref[...])
            o = o_ref[...].astype(F32)
            ag = ag_ref[...].astype(F32)
            sg = _sig(ag)
            gate = ag * sg
            gacc[...] += _dot_tn((o * gate).astype(BF16), dya)
            dag_ref[...] = (dpa * o * (sg * (1.0 + ag * (1.0 - sg)))).astype(BF16)
            do = (dpa * gate).astype(BF16)

            cur_ok, prev_ok = _band_masks(n)
            head_lane = lax.broadcasted_iota(jnp.int32, (1, LANES), 1)
            gsink = jnp.zeros((1, LANES), F32)
            zero_tile = jnp.zeros((BLOCK, LANES), F32)
            dk_cur, dk_prev = [zero_tile, zero_tile], [zero_tile, zero_tile]
            dv_cur, dv_prev = [zero_tile, zero_tile], [zero_tile, zero_tile]
            for kvh in range(N_KV_HEADS):
                ta, tb = slice(LANES * 2 * kvh, LANES * (2 * kvh + 1)), slice(LANES * (2 * kvh + 1), LANES * (2 * kvh + 2))
                ks = slice(LANES * kvh, LANES * (kvh + 1))
                q_stack = _stack_heads(q_ref[:, ta], q_ref[:, tb], lo)
                do_stack = _stack_heads(do[:, ta], do[:, tb], lo)
                k_c, k_p, v_c, v_p = kc_ref[:, ks], kp_ref[:, ks], vc_ref[:, ks], vp_ref[:, ks]
                p_c, p_p, p_s = _softmax_band(q_stack, k_c, k_p, _sink_column(sinks_ref, kvh), cur_ok, prev_ok)
                dp_c, dp_p = _dot_nt(do_stack, v_c), _dot_nt(do_stack, v_p)
                delta = jnp.sum(p_c * dp_c, axis=1, keepdims=True) + jnp.sum(p_p * dp_p, axis=1, keepdims=True)
                ds_c = (p_c * (dp_c - delta)).astype(BF16)
                ds_p = (p_p * (dp_p - delta)).astype(BF16)
                sink_terms = p_s * delta
                for g in range(4):
                    total = jnp.sum(sink_terms[BLOCK * g:BLOCK * (g + 1)], axis=0, keepdims=True)
                    gsink = gsink - jnp.where(head_lane == 4 * kvh + g, total, 0.0)
                dq_stack = _dot(ds_c, k_c) + _dot(ds_p, k_p)
                dq_scr[:, ta], dq_scr[:, tb] = _unstack_heads(dq_stack, lo)
                tile, second = kvh // 2, kvh % 2 == 1
                dk_cur[tile] = dk_cur[tile] + _fold_kv_head(_dot_tn(ds_c, q_stack), lo, second)
                dk_prev[tile] = dk_prev[tile] + _fold_kv_head(_dot_tn(ds_p, q_stack), lo, second)
                dv_cur[tile] = dv_cur[tile] + _fold_kv_head(_dot_tn(p_c.astype(BF16), do_stack), lo, second)
                dv_prev[tile] = dv_prev[tile] + _fold_kv_head(_dot_tn(p_p.astype(BF16), do_stack), lo, second)
            gsink_ref[0:1, :] += gsink
            for p in range(D_MODEL // LANES):
                sl = slice(LANES * p, LANES * (p + 1))
                dq_ref[:, sl] = (_rope_transposed(dq_scr[:, sl], cos_c[...], up_c[...], dn_c[...]) * scale).astype(BF16)
            emit_prev(dk_prev, dv_prev)
            for p in range(2):
                sl = slice(LANES * p, LANES * (p + 1))
                dk_carry[:, sl] = dk_cur[p]
                dv_carry[:, sl] = dv_cur[p]

        @pl.when(n == nb)
        def _():
            zero_tile = jnp.zeros((BLOCK, LANES), F32)
            emit_prev([zero_tile, zero_tile], [zero_tile, zero_tile])
            gwao_ref[...] = gacc[...].astype(BF16)

    def cur_idx(n):
        return jnp.minimum(n, nb - 1)

    def prev_idx(n):
        return jnp.clip(n - 1, 0, nb - 1)

    cur = lambda w, col=0: pl.BlockSpec((BLOCK, w), lambda n: (cur_idx(n), col))
    prev = lambda w: pl.BlockSpec((BLOCK, w), lambda n: (prev_idx(n), 0))
    return pl.pallas_call(
        body, name="attention_backward", grid=(nb + 1,),
        in_specs=[cur(D_MODEL), cur(D_MODEL), cur(D_MODEL, COL_AG), cur(D_MODEL),
                  cur(512), prev(512), cur(512), prev(512),
                  pl.BlockSpec(memory_space=pltpu.SMEM), _const_spec((D_MODEL, D_MODEL)),
                  cur(LANES), cur(LANES), cur(LANES), prev(LANES), prev(LANES), prev(LANES)],
        out_specs=(cur(D_MODEL), prev(512), cur(D_MODEL),
                   _const_spec((D_MODEL, D_MODEL)), _const_spec((8, LANES))),
        out_shape=(jax.ShapeDtypeStruct((tokens, D_MODEL), BF16),
                   jax.ShapeDtypeStruct((tokens, 512), BF16),
                   jax.ShapeDtypeStruct((tokens, D_MODEL), BF16),
                   jax.ShapeDtypeStruct((D_MODEL, D_MODEL), BF16),
                   jax.ShapeDtypeStruct((8, LANES), F32)),
        scratch_shapes=[pltpu.VMEM((D_MODEL, D_MODEL), F32),
                        pltpu.VMEM((BLOCK, 256), F32), pltpu.VMEM((BLOCK, 256), F32),
                        pltpu.VMEM((BLOCK, D_MODEL), F32)],
        compiler_params=_cparams(("arbitrary",), VMEM_LIMIT),
    )(dya, o, proj, qr, kd, kd, vd, vd, sinks, w_ao, cos_t, sin_up, sin_dn, cos_t, sin_up, sin_dn)


_SECTION_ROWS = (OFF_A, OFF_B, OFF_CG, OFF_Q, OFF_KV, OFF_AG, OFF_MLC, OFF_MLA)
_SECTION_WIDTH = (1024, 1024, 1024, 1024, 512, 1024, 1024, 1024)


def _input_backward(sections, w_in_t, x, dx2, norm_g):
    tokens = x.shape[0]
    tm = 256

    def body(*refs):
        sec = refs[:8]
        w_ref, x_ref, dx2_ref, g_ref, gx_ref, part_ref = refs[8:]

        @pl.when(pl.program_id(0) == 0)
        def _():
            part_ref[...] = jnp.zeros_like(part_ref)

        dh = jnp.zeros((tm, D_MODEL), F32)
        for s in range(8):
            dh = dh + _dot(sec[s][...], w_ref[_SECTION_ROWS[s]:_SECTION_ROWS[s] + _SECTION_WIDTH[s], :])
        xv = x_ref[...]
        r = lax.rsqrt(jnp.mean(xv * xv, axis=-1, keepdims=True) + RMS_EPS)
        xn = xv * r
        part_ref[0:1, :] += jnp.sum(dh * xn, axis=0, keepdims=True)
        dxn = dh * g_ref[...]
        gx_ref[...] = dx2_ref[...] + r * (dxn - xn * jnp.mean(dxn * xn, axis=-1, keepdims=True))

    tile = lambda w=D_MODEL: pl.BlockSpec((tm, w), lambda i: (i, 0))
    return pl.pallas_call(
        body, name="input_backward", grid=(tokens // tm,),
        in_specs=[tile(w) for w in _SECTION_WIDTH] + [
            pl.BlockSpec((IN_WIDTH, D_MODEL), lambda i: (0, 0), pipeline_mode=pl.Buffered(1)),
            tile(), tile(), _const_spec((1, D_MODEL))],
        out_specs=(tile(), _const_spec((8, D_MODEL))),
        out_shape=(jax.ShapeDtypeStruct((tokens, D_MODEL), F32),
                   jax.ShapeDtypeStruct((8, D_MODEL), F32)),
        compiler_params=_cparams(("arbitrary",), VMEM_LIMIT),
    )(*sections, w_in_t, x, dx2, norm_g)


def _in_projection_grad(sections, h):
    tokens = h.shape[0]
    tt = min(512, tokens)
    n_tok = tokens // tt
    chunk = 512
    first = tuple(r // chunk for r in _SECTION_ROWS)
    count = tuple(w // chunk for w in _SECTION_WIDTH)

    def body(*refs):
        sec = refs[:8]
        h_ref, out_ref, acc = refs[8:]
        j, i = pl.program_id(0), pl.program_id(1)

        @pl.when(i == 0)
        def _():
            acc[...] = jnp.zeros_like(acc)

        for s in range(8):
            @pl.when(jnp.logical_and(j >= first[s], j < first[s] + count[s]))
            def _(s=s):
                acc[...] += _dot_tn(sec[s][...], h_ref[...])

        @pl.when(i == n_tok - 1)
        def _():
            out_ref[...] = acc[...].astype(BF16)

    def sec_spec(s):
        def index(j, i):
            active = jnp.logical_and(j >= first[s], j < first[s] + count[s])
            return (jnp.where(active, i, 0), jnp.clip(j - first[s], 0, count[s] - 1))
        return pl.BlockSpec((tt, chunk), index)

    return pl.pallas_call(
        body, name="in_projection_grad", grid=(IN_WIDTH // chunk, n_tok),
        in_specs=[sec_spec(s) for s in range(8)] + [pl.BlockSpec((tt, D_MODEL), lambda j, i: (i, 0))],
        out_specs=pl.BlockSpec((chunk, D_MODEL), lambda j, i: (j, 0)),
        out_shape=jax.ShapeDtypeStruct((IN_WIDTH, D_MODEL), BF16),
        scratch_shapes=[pltpu.VMEM((chunk, D_MODEL), F32)],
        compiler_params=_cparams(("parallel", "arbitrary"), VMEM_LIMIT),
    )(*sections, h)


def _adamw_math(w, g, m, v):
    m = ADAM_B1 * m + (1.0 - ADAM_B1) * g
    v = ADAM_B2 * v + (1.0 - ADAM_B2) * (g * g)
    m_hat = m / (1.0 - ADAM_B1 ** ADAM_STEP)
    v_hat = v / (1.0 - ADAM_B2 ** ADAM_STEP)
    delta = -ADAM_LR * (m_hat / (jnp.sqrt(v_hat) + ADAM_EPS) + ADAM_WD * w)
    return delta, m, v


def _sum_slots(recv_ref):
    total = recv_ref[0].astype(F32)
    for d in range(1, N_DEV):
        total = total + recv_ref[d].astype(F32)
    return total


def _sum_in_projection_grad(recv):
    tr = 192

    def body(recv_ref, out_ref):
        out_ref[...] = _sum_slots(recv_ref)

    return pl.pallas_call(
        body, name="sum_in_projection_grad", grid=(SHARD_IN // tr,),
        in_specs=[pl.BlockSpec((N_DEV, tr, D_MODEL), lambda i: (0, i, 0))],
        out_specs=pl.BlockSpec((tr, D_MODEL), lambda i: (i, 0)),
        out_shape=jax.ShapeDtypeStruct((SHARD_IN, D_MODEL), F32),
        compiler_params=_cparams(("parallel",)),
    )(recv)


def _adamw(name, w, g, m, v, tile_rows):
    rows, cols = w.shape

    def body(w_ref, g_ref, m_ref, v_ref, d_ref, nm_ref, nv_ref):
        d_ref[...], nm_ref[...], nv_ref[...] = _adamw_math(w_ref[...], g_ref[...], m_ref[...], v_ref[...])

    spec = pl.BlockSpec((tile_rows, cols), lambda i: (i, 0))
    shape = jax.ShapeDtypeStruct((rows, cols), F32)
    return pl.pallas_call(
        body, name=name, grid=(rows // tile_rows,),
        in_specs=[spec] * 4, out_specs=(spec,) * 3, out_shape=(shape,) * 3,
        compiler_params=_cparams(("parallel",)),
    )(w, g, m, v)


def _sum_adamw(name, recv, w, m, v):
    def body(recv_ref, w_ref, m_ref, v_ref, g_ref, d_ref, nm_ref, nv_ref):
        g = _sum_slots(recv_ref)
        g_ref[...] = g
        d_ref[...], nm_ref[...], nv_ref[...] = _adamw_math(w_ref[...], g, m_ref[...], v_ref[...])

    shape = jax.ShapeDtypeStruct(w.shape, F32)
    return pl.pallas_call(body, name=name, out_shape=(shape,) * 4)(recv, w, m, v)


def _pad_rows(a, rows):
    return jnp.concatenate([a, jnp.zeros((rows - a.shape[0],) + a.shape[1:], a.dtype)], axis=0)


def kernel(x, norm_g, w_in, conv_dw_w, conv_dw_b, conv_ln_g, conv_ln_b, w_conv_out, attn_sinks, w_attn_out, w_out, final_norm_g, loss_target, m_norm_g, m_w_in, m_conv_dw_w, m_conv_dw_b, m_conv_ln_g, m_conv_ln_b, m_w_conv_out, m_attn_sinks, m_w_attn_out, m_w_out, m_final_norm_g, v_norm_g, v_w_in, v_conv_dw_w, v_conv_dw_b, v_conv_ln_g, v_conv_ln_b, v_w_conv_out, v_attn_sinks, v_w_attn_out, v_w_out, v_final_norm_g):
    xs, target = x[0], loss_target[0]
    tokens = xs.shape[0]
    fg_row = final_norm_g.reshape(1, D_MODEL)

    w_in_t, w_co, w_ao, w_o, conv_w = _all_gather_weights(
        w_in[0].T.astype(BF16), w_conv_out[0].astype(BF16), w_attn_out[0].astype(BF16), w_out[0].astype(BF16),
        _pad_rows(conv_dw_w[0], CONV_PAD))

    proj, h = _in_projection(xs, norm_g, w_in_t)
    cos_t, sin_up, sin_dn = _rope_tables(tokens)
    qr, kd, vd = _rope_qkv(proj, cos_t, sin_up, sin_dn)
    cv, yc = _conv_forward(proj, conv_w, conv_dw_b, conv_ln_g, conv_ln_b, w_co)
    o, ya = _attention_forward(qr, kd, vd, proj, attn_sinks, w_ao)

    dx2, dyc, dya, dmlc, dmla, g_out, part_head = _merge_and_head(yc, ya, proj, xs, target, w_o, fg_row)
    dcv, dcg, g_co, part_conv = _conv_backward_pointwise(dyc, cv, proj, w_co, conv_ln_g, conv_ln_b)
    da, db, g_conv = _conv_backward_taps(dcv, proj, conv_w)
    dq, dkv, dag, g_ao, part_sink = _attention_backward(dya, o, qr, kd, vd, proj, attn_sinks, w_ao, cos_t, sin_up, sin_dn)
    sections = (da, db, dcg, dq, dkv, dag, dmlc, dmla)
    grad_x, part_in = _input_backward(sections, w_in_t, xs, dx2, norm_g)
    g_in_t = _in_projection_grad(sections, h)

    small = jnp.concatenate([
        part_in[0:1], part_conv[2:3], part_conv[0:1], part_conv[1:2], part_head[0:1],
        jnp.pad(part_sink[0:1], ((0, 0), (0, D_MODEL - LANES))), part_head[1:2],
        jnp.zeros((1, D_MODEL), F32)], axis=0)

    r_in_t, r_co, r_ao, r_out, r_conv, r_small = _exchange_grads(g_in_t, g_co, g_ao, g_out, g_conv, small)

    grad_w_in = _sum_in_projection_grad(r_in_t).T
    d_w_in, nm_w_in, nv_w_in = _adamw("adamw_w_in", w_in[0], grad_w_in, m_w_in[0], v_w_in[0], 256)
    sq = {}
    for nm, recv, w, m, v in (("w_conv_out", r_co, w_conv_out, m_w_conv_out, v_w_conv_out),
                              ("w_attn_out", r_ao, w_attn_out, m_w_attn_out, v_w_attn_out),
                              ("w_out", r_out, w_out, m_w_out, v_w_out)):
        sq[nm] = _sum_adamw("sum_adamw_" + nm, recv, w[0], m[0], v[0])
    conv_res = _sum_adamw("sum_adamw_conv_dw_w", r_conv.reshape(N_DEV, CONV_PAD, LANES),
                          _pad_rows(conv_dw_w[0], CONV_PAD), _pad_rows(m_conv_dw_w[0], CONV_PAD),
                          _pad_rows(v_conv_dw_w[0], CONV_PAD))
    pad_sink = lambda a: jnp.pad(a, ((0, 0), (0, D_MODEL - N_Q_HEADS)))
    zero_rows = jnp.zeros((2, D_MODEL), F32)
    stack = lambda a, b, c, d, e, f: jnp.concatenate([a, b, c, d, e.reshape(1, D_MODEL), pad_sink(f), zero_rows], axis=0)
    small_res = _sum_adamw(
        "sum_adamw_small", r_small,
        stack(norm_g, conv_dw_b, conv_ln_g, conv_ln_b, final_norm_g, attn_sinks),
        stack(m_norm_g, m_conv_dw_b, m_conv_ln_g, m_conv_ln_b, m_final_norm_g, m_attn_sinks),
        stack(v_norm_g, v_conv_dw_b, v_conv_ln_g, v_conv_ln_b, v_final_norm_g, v_attn_sinks))
    loss = jnp.sum(small_res[0][6])

    def leaf(k):
        s = small_res[k]
        return (s[0:1], (grad_w_in, d_w_in, nm_w_in, nv_w_in)[k][None], conv_res[k][None, :CONV_KERNEL],
                s[1:2], s[2:3], s[3:4], sq["w_conv_out"][k][None], s[5:6, :N_Q_HEADS],
                sq["w_attn_out"][k][None], sq["w_out"][k][None], s[4])

    return (loss, grad_x[None], *leaf(0), *leaf(1), *leaf(2), *leaf(3))
```

```python
import jax
import jax.numpy as jnp
from jax import lax
from jax.experimental import pallas as pl
from jax.experimental.pallas import tpu as pltpu

F32 = jnp.float32
BF16 = jnp.bfloat16
MESH = pl.DeviceIdType.MESH

D_MODEL = 1024
IN_WIDTH = 7680
N_DEV = 8
SHARD_IN = IN_WIDTH // N_DEV
SHARD_SQ = D_MODEL // N_DEV
CONV_KERNEL = 31
CONV_PAD = 32
HEAD_DIM = 64
N_Q_HEADS = 16
N_KV_HEADS = 4
BLOCK = 128
LANES = 128
ROPE_THETA = 10000.0
RMS_EPS = 1e-5
LN_EPS = 1e-5
NEG = -1e30
ADAM_LR = 0.001
ADAM_B1 = 0.9
ADAM_B2 = 0.999
ADAM_EPS = 1e-08
ADAM_WD = 0.01
ADAM_STEP = 10

OFF_A, OFF_B, OFF_CG, OFF_Q, OFF_KV, OFF_AG, OFF_MLC, OFF_MLA = 0, 1024, 2048, 3072, 4096, 4608, 5632, 6656
COL_A, COL_B, COL_CG, COL_Q = 0, 1, 2, 3
COL512_KV, COL512_AG, COL512_MLC, COL512_MLA = 8, 9, 11, 13
UNIT = 2 * SHARD_IN
PACK_ROWS = 400

VMEM_LIMIT = 56 * 1024 * 1024


def _cparams(sem=None, vmem=None):
    return pltpu.CompilerParams(dimension_semantics=sem, vmem_limit_bytes=vmem)


def _sig(v):
    return 1.0 / (1.0 + jnp.exp(-v))


def _dot(a, b):
    return jnp.dot(a, b, preferred_element_type=F32)


def _dot_nt(a, b):
    return lax.dot_general(a, b, (((1,), (1,)), ((), ())), preferred_element_type=F32)


def _dot_tn(a, b):
    return lax.dot_general(a, b, (((0,), (0,)), ((), ())), preferred_element_type=F32)


def _const_spec(shape):
    nd = len(shape)
    return pl.BlockSpec(shape, lambda *_: (0,) * nd)


def _mesh_pos():
    x, y, c = lax.axis_index("x"), lax.axis_index("y"), lax.axis_index("c")
    return x, y, c, 4 * x + 2 * y + c


def _peer(x, y, c, k):
    px = 1 - x if (k >> 2) & 1 else x
    py = 1 - y if (k >> 1) & 1 else y
    pc = 1 - c if k & 1 else c
    return (px, py, pc), 4 * px + 2 * py + pc


def _gather_project(x, norm_g, w_shard_t, pack):
    tokens = x.shape[0]
    tt = min(512, tokens // 2)
    n_tok = tokens // tt
    rc = min(128, tt)

    def body(x_hbm, g_ref, ws_hbm, pack_hbm, proj_hbm, h_hbm, wfull_hbm, packfull_hbm,
             w_vmem, h_vmem, x_buf, o_buf, send_sems, recv_sems, local_sems, x_sems, o_sems):
        x_, y_, c_, me = _mesh_pos()
        myself, sibling = (x_, y_, c_), (x_, y_, 1 - c_)
        chips = ((1 - x_, y_), (x_, 1 - y_), (1 - x_, 1 - y_))

        def shard(ref, idx):
            return ref.at[pl.ds(pl.multiple_of(idx * SHARD_IN, 64), SHARD_IN)]

        def copies(k, idx, to, own=False):
            def mk(a, src, dst):
                return pltpu.make_async_remote_copy(src_ref=src, dst_ref=dst, send_sem=send_sems.at[a, k],
                                                    recv_sem=recv_sems.at[a, k], device_id=to, device_id_type=MESH)
            return [mk(0, ws_hbm if own else shard(w_vmem, idx), shard(w_vmem, idx)),
                    mk(1, pack_hbm if own else packfull_hbm.at[idx], packfull_hbm.at[idx])]

        own_w = pltpu.make_async_copy(ws_hbm, shard(w_vmem, me), local_sems.at[0])
        own_p = pltpu.make_async_copy(pack_hbm, packfull_hbm.at[me], local_sems.at[1])
        own_w.start()
        own_p.start()
        sent = copies(0, me, sibling, own=True)
        for r, chip in enumerate(chips):
            sent += copies(1 + r, me, (*chip, c_), own=True)
        for cp in sent:
            cp.start()

        def x_copy(t, slot):
            return pltpu.make_async_copy(x_hbm.at[pl.ds(t * tt, tt)], x_buf.at[slot], x_sems.at[slot])

        x_copy(0, 0).start()
        for t in range(n_tok):
            slot = t % 2
            if t + 1 < n_tok:
                x_copy(t + 1, 1 - slot).start()
            x_copy(t, slot).wait()

            def chunk(r0, t=t, slot=slot):
                xv = x_buf[slot, pl.ds(r0, rc), :]
                r = lax.rsqrt(jnp.mean(xv * xv, axis=-1, keepdims=True) + RMS_EPS)
                h_vmem[pl.ds(t * tt + r0, rc), :] = (xv * r * g_ref[...]).astype(BF16)
            _row_chunks(tt, rc, chunk)
        h_out = pltpu.make_async_copy(h_vmem, h_hbm, local_sems.at[6])
        h_out.start()
        local = [own_p, h_out]

        def project_unit(q, u):
            rows = pl.ds(pl.multiple_of(q * UNIT, LANES), UNIT)
            w_out = pltpu.make_async_copy(w_vmem.at[rows], wfull_hbm.at[rows], local_sems.at[2 + u])
            w_out.start()
            local.append(w_out)

            def o_copy(slot, t):
                return pltpu.make_async_copy(
                    o_buf.at[slot], proj_hbm.at[pl.ds(pl.multiple_of(t * tt, tt), tt), rows], o_sems.at[slot])

            def tile(t, carry):
                slot = lax.rem(t, 2)

                @pl.when(t >= 2)
                def _():
                    o_copy(slot, t).wait()
                o_buf[slot] = _dot_nt(h_vmem[pl.ds(pl.multiple_of(t * tt, tt), tt), :], w_vmem[rows, :]).astype(BF16)
                o_copy(slot, t).start()
                return carry
            lax.fori_loop(0, n_tok, tile, 0)
            o_copy(0, 0).wait()
            o_copy(1, 0).wait()

        def dev(chip, core):
            return 4 * chip[0] + 2 * chip[1] + core

        def arrive_and_pass_on(r):
            for cp in copies(1 + r, dev(chips[r], c_), myself):
                cp.wait_recv()
            passed = copies(4 + r, dev(chips[r], c_), sibling)
            for cp in passed:
                cp.start()
            sent.extend(passed)

        def passed_on_to_me(r):
            for cp in copies(4 + r, dev(chips[r], 1 - c_), myself):
                cp.wait_recv()

        own_w.wait()
        for cp in copies(0, dev((x_, y_), 1 - c_), myself):
            cp.wait_recv()
        project_unit(2 * x_ + y_, 0)
        arrive_and_pass_on(0)
        arrive_and_pass_on(1)
        passed_on_to_me(0)
        project_unit(2 * chips[0][0] + chips[0][1], 1)
        arrive_and_pass_on(2)
        passed_on_to_me(1)
        project_unit(2 * chips[1][0] + chips[1][1], 2)
        passed_on_to_me(2)
        project_unit(2 * chips[2][0] + chips[2][1], 3)
        for cp in sent:
            cp.wait_send()
        for cp in local:
            cp.wait()

    hbm = pl.BlockSpec(memory_space=pltpu.HBM)
    return pl.pallas_call(
        body, name="gather_project",
        in_specs=[hbm, pl.BlockSpec(memory_space=pltpu.VMEM), hbm, hbm],
        out_specs=(hbm, hbm, hbm, hbm),
        out_shape=(jax.ShapeDtypeStruct((tokens, IN_WIDTH), BF16),
                   jax.ShapeDtypeStruct((tokens, D_MODEL), BF16),
                   jax.ShapeDtypeStruct((IN_WIDTH, D_MODEL), BF16),
                   jax.ShapeDtypeStruct((N_DEV, PACK_ROWS, D_MODEL), BF16)),
        scratch_shapes=[pltpu.VMEM((IN_WIDTH, D_MODEL), BF16),
                        pltpu.VMEM((tokens, D_MODEL), BF16),
                        pltpu.VMEM((2, tt, D_MODEL), F32),
                        pltpu.VMEM((2, tt, UNIT), BF16),
                        pltpu.SemaphoreType.DMA((2, N_DEV - 1)),
                        pltpu.SemaphoreType.DMA((2, N_DEV - 1)),
                        pltpu.SemaphoreType.DMA((7,)),
                        pltpu.SemaphoreType.DMA((2,)),
                        pltpu.SemaphoreType.DMA((2,))],
        compiler_params=_cparams(None, VMEM_LIMIT),
    )(x, norm_g, w_shard_t, pack)


def _exchange_grads(g_in_t, g_co, g_ao, g_out, g_conv, small):
    parts = (g_in_t, g_co, g_ao, g_out, g_conv, small)
    n_arr = len(parts)
    rows = (SHARD_IN, SHARD_SQ, SHARD_SQ, SHARD_SQ)

    def body(*refs):
        srcs, outs = refs[:n_arr], refs[n_arr:2 * n_arr]
        send_sems, recv_sems, local_sems = refs[2 * n_arr:]
        x, y, c, me = _mesh_pos()

        def src_block(a, idx):
            if a < 4:
                return srcs[a].at[pl.ds(pl.multiple_of(idx * rows[a], 64), rows[a])]
            if a == 4:
                return srcs[a].at[idx]
            return srcs[a]

        own = [pltpu.make_async_copy(src_block(a, me), outs[a].at[me], local_sems.at[a]) for a in range(n_arr)]
        for cp in own:
            cp.start()
        sent = []
        for k in range(1, N_DEV):
            peer, peer_idx = _peer(x, y, c, k)
            for a in range(n_arr):
                cp = pltpu.make_async_remote_copy(
                    src_ref=src_block(a, peer_idx), dst_ref=outs[a].at[me], send_sem=send_sems.at[a, k - 1],
                    recv_sem=recv_sems.at[a, k - 1], device_id=peer, device_id_type=MESH)
                cp.start()
                sent.append(cp)
        for k in range(1, N_DEV):
            peer, peer_idx = _peer(x, y, c, k)
            for a in range(n_arr):
                pltpu.make_async_remote_copy(
                    src_ref=src_block(a, me), dst_ref=outs[a].at[peer_idx], send_sem=send_sems.at[a, k - 1],
                    recv_sem=recv_sems.at[a, k - 1], device_id=peer, device_id_type=MESH).wait_recv()
        for cp in sent:
            cp.wait_send()
        for cp in own:
            cp.wait()

    hbm = pl.BlockSpec(memory_space=pltpu.HBM)
    out_shape = (
        jax.ShapeDtypeStruct((N_DEV, SHARD_IN, D_MODEL), BF16),
        jax.ShapeDtypeStruct((N_DEV, SHARD_SQ, D_MODEL), BF16),
        jax.ShapeDtypeStruct((N_DEV, SHARD_SQ, D_MODEL), BF16),
        jax.ShapeDtypeStruct((N_DEV, SHARD_SQ, D_MODEL), BF16),
        jax.ShapeDtypeStruct((N_DEV, CONV_PAD, LANES), F32),
        jax.ShapeDtypeStruct((N_DEV, 8, D_MODEL), F32),
    )
    return pl.pallas_call(
        body, name="exchange_grads", out_shape=out_shape,
        in_specs=[hbm] * n_arr, out_specs=(hbm,) * n_arr,
        scratch_shapes=[pltpu.SemaphoreType.DMA((n_arr, N_DEV - 1)),
                        pltpu.SemaphoreType.DMA((n_arr, N_DEV - 1)),
                        pltpu.SemaphoreType.DMA((n_arr,))],
    )(*parts)


def _row_chunks(total, size, fn):
    n = total // size
    if n == 1:
        fn(0)
        return

    def step(i, carry):
        fn(pl.multiple_of(i * size, size))
        return carry
    lax.fori_loop(0, n, step, 0)


def _rope_tables(tokens):
    inv_freq = ROPE_THETA ** (-jnp.arange(0, HEAD_DIM, 2, dtype=F32) / HEAD_DIM)
    ang = jnp.arange(tokens, dtype=jnp.int32).astype(F32)[:, None] * inv_freq[None, :]
    cos, sin = jnp.cos(ang), jnp.sin(ang)
    zero = jnp.zeros_like(sin)
    cos_t = jnp.tile(jnp.concatenate([cos, cos], axis=1), (1, LANES // HEAD_DIM))
    sin_up = jnp.tile(jnp.concatenate([-sin, zero], axis=1), (1, LANES // HEAD_DIM))
    sin_dn = jnp.tile(jnp.concatenate([zero, sin], axis=1), (1, LANES // HEAD_DIM))
    return cos_t, sin_up, sin_dn


def _rope(t, cos_t, sin_up, sin_dn):
    return t * cos_t + pltpu.roll(t, LANES - 32, 1) * sin_up + pltpu.roll(t, 32, 1) * sin_dn


def _rope_transposed(g, cos_t, sin_up, sin_dn):
    return g * cos_t + pltpu.roll(g * sin_up, 32, 1) + pltpu.roll(g * sin_dn, LANES - 32, 1)


def _lane_halves():
    lane = lax.broadcasted_iota(jnp.int32, (BLOCK, LANES), 1)
    return lane < HEAD_DIM


def _rope_qkv(proj, cos_t, sin_up, sin_dn):
    tokens = proj.shape[0]
    tm = min(512, tokens)
    scale = HEAD_DIM ** -0.5

    def body(q_ref, kv_ref, cos_ref, up_ref, dn_ref, qr_ref, kd_ref, vd_ref):
        lo = _lane_halves()

        def chunk(r0):
            rows = pl.ds(r0, BLOCK)
            cs, up, dn = cos_ref[rows, :], up_ref[rows, :], dn_ref[rows, :]
            for p in range(D_MODEL // LANES):
                sl = slice(LANES * p, LANES * (p + 1))
                qt = q_ref[rows, sl].astype(F32)
                qr_ref[rows, sl] = (_rope(qt, cs, up, dn) * scale).astype(BF16)
            for p in range(2):
                sl = slice(LANES * p, LANES * (p + 1))
                kt = _rope(kv_ref[rows, sl].astype(F32), cs, up, dn)
                vt = kv_ref[rows, slice(256 + LANES * p, 256 + LANES * (p + 1))].astype(F32)
                for src, dst in ((kt, kd_ref), (vt, vd_ref)):
                    first = jnp.where(lo, src, 0.0)
                    second = src - first
                    dst[rows, slice(LANES * 2 * p, LANES * (2 * p + 1))] = (first + pltpu.roll(first, HEAD_DIM, 1)).astype(BF16)
                    dst[rows, slice(LANES * (2 * p + 1), LANES * (2 * p + 2))] = (second + pltpu.roll(second, HEAD_DIM, 1)).astype(BF16)
        _row_chunks(tm, BLOCK, chunk)

    tab = pl.BlockSpec((tm, LANES), lambda i: (i, 0))
    return pl.pallas_call(
        body, name="rope_qkv", grid=(tokens // tm,),
        in_specs=[pl.BlockSpec((tm, D_MODEL), lambda i: (i, COL_Q)),
                  pl.BlockSpec((tm, 512), lambda i: (i, COL512_KV)), tab, tab, tab],
        out_specs=(pl.BlockSpec((tm, D_MODEL), lambda i: (i, 0)),
                   pl.BlockSpec((tm, 512), lambda i: (i, 0)),
                   pl.BlockSpec((tm, 512), lambda i: (i, 0))),
        out_shape=(jax.ShapeDtypeStruct((tokens, D_MODEL), BF16),
                   jax.ShapeDtypeStruct((tokens, 512), BF16),
                   jax.ShapeDtypeStruct((tokens, 512), BF16)),
        compiler_params=_cparams(("parallel",)),
    )(proj, proj, cos_t, sin_up, sin_dn)


CONV_TM = 256
N_LANE_CHUNKS = D_MODEL // LANES


def _fill_u_ext(u_ext, a_ref, b_ref, ah_ref, bh_ref, first_tile):
    for lc in range(N_LANE_CHUNKS):
        sl = slice(LANES * lc, LANES * (lc + 1))
        uh = ah_ref[:, sl].astype(F32) * _sig(bh_ref[:, sl].astype(F32))
        u_ext[lc, 0:CONV_PAD, :] = jnp.where(first_tile, 0.0, uh)
        u_ext[lc, CONV_PAD:CONV_PAD + CONV_TM, :] = a_ref[:, sl].astype(F32) * _sig(b_ref[:, sl].astype(F32))


def _conv_forward(proj, conv_w, dw_b, ln_g, ln_b, w_co):
    tokens = proj.shape[0]
    tm = CONV_TM
    halo_blocks = tm // CONV_PAD

    def body(a_ref, b_ref, ah_ref, bh_ref, cg_ref, cw_ref, dwb_ref, lng_ref, lnb_ref, wco_ref,
             cv_ref, yc_ref, u_ext, cv_scr):
        _fill_u_ext(u_ext, a_ref, b_ref, ah_ref, bh_ref, pl.program_id(0) == 0)

        def lane_chunk(lc, carry):
            for rc in range(tm // 64):
                acc = jnp.zeros((64, LANES), F32)
                for j in range(CONV_KERNEL):
                    acc = acc + cw_ref[lc, pl.ds(j, 1), :] * u_ext[lc, pl.ds(64 * rc + 2 + j, 64), :]
                cv_scr[lc, pl.ds(64 * rc, 64), :] = acc
            return carry
        lax.fori_loop(0, N_LANE_CHUNKS, lane_chunk, 0)

        cv = jnp.concatenate([cv_scr[lc] for lc in range(N_LANE_CHUNKS)], axis=1) + dwb_ref[...]
        cv_ref[...] = cv
        mu = jnp.mean(cv, axis=-1, keepdims=True)
        zc = cv - mu
        rstd = lax.rsqrt(jnp.mean(zc * zc, axis=-1, keepdims=True) + LN_EPS)
        ln = zc * rstd * lng_ref[...] + lnb_ref[...]
        cg = cg_ref[...].astype(F32)
        pc = (ln * _sig(ln)) * (cg * _sig(cg))
        yc_ref[...] = _dot(pc.astype(BF16), wco_ref[...]).astype(BF16)

    def halo_map(i):
        return (jnp.maximum(i * halo_blocks - 1, 0), 0)

    tile = lambda col: pl.BlockSpec((tm, D_MODEL), lambda i: (i, col))
    return pl.pallas_call(
        body, name="conv_forward", grid=(tokens // tm,),
        in_specs=[tile(COL_A), tile(COL_B),
                  pl.BlockSpec((CONV_PAD, D_MODEL), lambda i: (halo_map(i)[0], COL_A)),
                  pl.BlockSpec((CONV_PAD, D_MODEL), lambda i: (halo_map(i)[0], COL_B)),
                  tile(COL_CG), _const_spec((N_DEV, CONV_PAD, LANES)),
                  _const_spec((1, D_MODEL)), _const_spec((1, D_MODEL)), _const_spec((1, D_MODEL)),
                  _const_spec((D_MODEL, D_MODEL))],
        out_specs=(pl.BlockSpec((tm, D_MODEL), lambda i: (i, 0)),
                   pl.BlockSpec((tm, D_MODEL), lambda i: (i, 0))),
        out_shape=(jax.ShapeDtypeStruct((tokens, D_MODEL), F32),
                   jax.ShapeDtypeStruct((tokens, D_MODEL), BF16)),
        scratch_shapes=[pltpu.VMEM((N_LANE_CHUNKS, CONV_PAD + tm, LANES), F32),
                        pltpu.VMEM((N_LANE_CHUNKS, tm, LANES), F32)],
        compiler_params=_cparams(("parallel",), VMEM_LIMIT),
    )(proj, proj, proj, proj, proj, conv_w, dw_b, ln_g, ln_b, w_co)


def _band_masks(n):
    row = lax.broadcasted_iota(jnp.int32, (4 * BLOCK, BLOCK), 0) & (BLOCK - 1)
    col = lax.broadcasted_iota(jnp.int32, (4 * BLOCK, BLOCK), 1)
    return col <= row, jnp.logical_and(col > row, n > 0)


def _stack_heads(tile_a, tile_b, lo):
    zero = jnp.zeros_like(tile_a)
    return jnp.concatenate([jnp.where(lo, tile_a, zero), jnp.where(lo, zero, tile_a),
                            jnp.where(lo, tile_b, zero), jnp.where(lo, zero, tile_b)], axis=0)


def _unstack_heads(stacked, lo):
    s = [stacked[BLOCK * g:BLOCK * (g + 1)] for g in range(4)]
    return (jnp.where(lo, s[0], 0.0) + jnp.where(lo, 0.0, s[1]),
            jnp.where(lo, s[2], 0.0) + jnp.where(lo, 0.0, s[3]))


def _sink_column(sinks_ref, kvh):
    return jnp.concatenate([jnp.full((BLOCK, 1), sinks_ref[0, 4 * kvh + g], F32) for g in range(4)], axis=0)


def _softmax_band(q_stack, k_cur, k_prev, sink, cur_ok, prev_ok):
    s_c = jnp.where(cur_ok, _dot_nt(q_stack, k_cur), NEG)
    s_p = jnp.where(prev_ok, _dot_nt(q_stack, k_prev), NEG)
    m = jnp.maximum(jnp.maximum(jnp.max(s_c, axis=1, keepdims=True), jnp.max(s_p, axis=1, keepdims=True)), sink)
    e_c, e_p, e_s = jnp.exp(s_c - m), jnp.exp(s_p - m), jnp.exp(sink - m)
    inv = 1.0 / (jnp.sum(e_c, axis=1, keepdims=True) + jnp.sum(e_p, axis=1, keepdims=True) + e_s)
    return e_c * inv, e_p * inv, e_s * inv


def _attention_forward(qr, kd, vd, proj, sinks, w_ao):
    tokens = qr.shape[0]
    nb = tokens // BLOCK

    def body(q_ref, kc_ref, kp_ref, vc_ref, vp_ref, ag0_ref, ag1_ref, sinks_ref, wao_ref, o_ref, ya_ref, o_scr):
        n = pl.program_id(0)
        lo = _lane_halves()
        cur_ok, prev_ok = _band_masks(n)
        for kvh in range(N_KV_HEADS):
            ta, tb = slice(LANES * 2 * kvh, LANES * (2 * kvh + 1)), slice(LANES * (2 * kvh + 1), LANES * (2 * kvh + 2))
            ks = slice(LANES * kvh, LANES * (kvh + 1))
            q_stack = _stack_heads(q_ref[:, ta], q_ref[:, tb], lo)
            p_c, p_p, _ = _softmax_band(q_stack, kc_ref[:, ks], kp_ref[:, ks], _sink_column(sinks_ref, kvh), cur_ok, prev_ok)
            o_stack = _dot(p_c.astype(BF16), vc_ref[:, ks]) + _dot(p_p.astype(BF16), vp_ref[:, ks])
            o_scr[:, ta], o_scr[:, tb] = _unstack_heads(o_stack, lo)
        o = o_scr[...]
        o_ref[...] = o.astype(BF16)
        ag = jnp.concatenate([ag0_ref[...], ag1_ref[...]], axis=1).astype(F32)
        ya_ref[...] = _dot((o * (ag * _sig(ag))).astype(BF16), wao_ref[...]).astype(BF16)

    cur = lambda w, col=0: pl.BlockSpec((BLOCK, w), lambda n: (n, col))
    prev = lambda w: pl.BlockSpec((BLOCK, w), lambda n: (jnp.maximum(n - 1, 0), 0))
    return pl.pallas_call(
        body, name="attention_forward", grid=(nb,),
        in_specs=[cur(D_MODEL), cur(512), prev(512), cur(512), prev(512),
                  cur(512, COL512_AG), cur(512, COL512_AG + 1),
                  pl.BlockSpec(memory_space=pltpu.SMEM), _const_spec((D_MODEL, D_MODEL))],
        out_specs=(cur(D_MODEL), cur(D_MODEL)),
        out_shape=(jax.ShapeDtypeStruct((tokens, D_MODEL), BF16),
                   jax.ShapeDtypeStruct((tokens, D_MODEL), BF16)),
        scratch_shapes=[pltpu.VMEM((BLOCK, D_MODEL), F32)],
        compiler_params=_cparams(("parallel",), VMEM_LIMIT),
    )(qr, kd, kd, vd, vd, proj, proj, sinks, w_ao)


def _merge_and_head(yc, ya, proj, x, target, w_out, final_g):
    tokens = x.shape[0]
    tm = 256
    last = tokens // tm - 1

    def body(yc_ref, ya_ref, mlc0_ref, mlc1_ref, mla0_ref, mla1_ref, x_ref, t_ref, wout_ref, fg_ref,
             dx2_ref, dyc_ref, dya_ref, dmlc_ref, dmla_ref, gwout_ref, part_ref, gacc):
        i = pl.program_id(0)

        @pl.when(i == 0)
        def _():
            gacc[...] = jnp.zeros_like(gacc)
            part_ref[...] = jnp.zeros_like(part_ref)

        yc, ya = yc_ref[...].astype(F32), ya_ref[...].astype(F32)
        gc = _sig(jnp.concatenate([mlc0_ref[...], mlc1_ref[...]], axis=1).astype(F32))
        ga = _sig(jnp.concatenate([mla0_ref[...], mla1_ref[...]], axis=1).astype(F32))
        merged = (gc * yc + ga * ya).astype(BF16)
        x2 = x_ref[...] + _dot(merged, wout_ref[...])
        r2 = lax.rsqrt(jnp.mean(x2 * x2, axis=-1, keepdims=True) + RMS_EPS)
        x2n = x2 * r2
        fg = fg_ref[...]
        err = x2n * fg - t_ref[...]
        dy = err * (1.0 / D_MODEL)
        part_ref[0:1, :] += jnp.sum(dy * x2n, axis=0, keepdims=True)
        part_ref[1:2, :] += jnp.sum(err * err, axis=0, keepdims=True) * (0.5 / D_MODEL)
        dx2n = dy * fg
        dx2 = r2 * (dx2n - x2n * jnp.mean(dx2n * x2n, axis=-1, keepdims=True))
        dx2_ref[...] = dx2
        dx2b = dx2.astype(BF16)
        gacc[...] += _dot_tn(merged, dx2b)
        dm = _dot_nt(dx2b, wout_ref[...])
        dyc_ref[...] = (dm * gc).astype(BF16)
        dya_ref[...] = (dm * ga).astype(BF16)
        dmlc_ref[...] = (dm * yc * (gc * (1.0 - gc))).astype(BF16)
        dmla_ref[...] = (dm * ya * (ga * (1.0 - ga))).astype(BF16)

        @pl.when(i == last)
        def _():
            gwout_ref[...] = gacc[...].astype(BF16)

    tile = lambda col=0: pl.BlockSpec((tm, D_MODEL), lambda i: (i, col))
    half = lambda col: pl.BlockSpec((tm, 512), lambda i: (i, col))
    return pl.pallas_call(
        body, name="merge_and_head", grid=(tokens // tm,),
        in_specs=[tile(), tile(), half(COL512_MLC), half(COL512_MLC + 1), half(COL512_MLA), half(COL512_MLA + 1),
                  tile(), tile(), _const_spec((D_MODEL, D_MODEL)), _const_spec((1, D_MODEL))],
        out_specs=(tile(), tile(), tile(), tile(), tile(),
                   _const_spec((D_MODEL, D_MODEL)), _const_spec((8, D_MODEL))),
        out_shape=(jax.ShapeDtypeStruct((tokens, D_MODEL), F32),
                   jax.ShapeDtypeStruct((tokens, D_MODEL), BF16),
                   jax.ShapeDtypeStruct((tokens, D_MODEL), BF16),
                   jax.ShapeDtypeStruct((tokens, D_MODEL), BF16),
                   jax.ShapeDtypeStruct((tokens, D_MODEL), BF16),
                   jax.ShapeDtypeStruct((D_MODEL, D_MODEL), BF16),
                   jax.ShapeDtypeStruct((8, D_MODEL), F32)),
        scratch_shapes=[pltpu.VMEM((D_MODEL, D_MODEL), F32)],
        compiler_params=_cparams(("arbitrary",), VMEM_LIMIT),
    )(yc, ya, proj, proj, proj, proj, x, target, w_out, final_g)


def _conv_backward_pointwise(dyc, cv, proj, w_co, ln_g, ln_b):
    tokens = cv.shape[0]
    tm = 256
    last = tokens // tm - 1

    def body(dyc_ref, cv_ref, cg_ref, wco_ref, lng_ref, lnb_ref, dcv_ref, dcg_ref, gwco_ref, part_ref, gacc):
        i = pl.program_id(0)

        @pl.when(i == 0)
        def _():
            gacc[...] = jnp.zeros_like(gacc)
            part_ref[...] = jnp.zeros_like(part_ref)

        cv = cv_ref[...]
        mu = jnp.mean(cv, axis=-1, keepdims=True)
        zc = cv - mu
        rstd = lax.rsqrt(jnp.mean(zc * zc, axis=-1, keepdims=True) + LN_EPS)
        z = zc * rstd
        lng = lng_ref[...]
        ln = z * lng + lnb_ref[...]
        sl = _sig(ln)
        c = ln * sl
        cg = cg_ref[...].astype(F32)
        scg = _sig(cg)
        gate = cg * scg
        dyc = dyc_ref[...]
        gacc[...] += _dot_tn((c * gate).astype(BF16), dyc)
        dpc = _dot_nt(dyc, wco_ref[...])
        dcg_ref[...] = (dpc * c * (scg * (1.0 + cg * (1.0 - scg)))).astype(BF16)
        dln = dpc * gate * (sl * (1.0 + ln * (1.0 - sl)))
        part_ref[0:1, :] += jnp.sum(dln * z, axis=0, keepdims=True)
        part_ref[1:2, :] += jnp.sum(dln, axis=0, keepdims=True)
        dz = dln * lng
        dcv = rstd * (dz - jnp.mean(dz, axis=-1, keepdims=True) - z * jnp.mean(dz * z, axis=-1, keepdims=True))
        part_ref[2:3, :] += jnp.sum(dcv, axis=0, keepdims=True)
        dcv_ref[...] = dcv

        @pl.when(i == last)
        def _():
            gwco_ref[...] = gacc[...].astype(BF16)

    tile = lambda col=0: pl.BlockSpec((tm, D_MODEL), lambda i: (i, col))
    return pl.pallas_call(
        body, name="conv_backward_pointwise", grid=(tokens // tm,),
        in_specs=[tile(), tile(), tile(COL_CG), _const_spec((D_MODEL, D_MODEL)),
                  _const_spec((1, D_MODEL)), _const_spec((1, D_MODEL))],
        out_specs=(tile(), tile(), _const_spec((D_MODEL, D_MODEL)), _const_spec((8, D_MODEL))),
        out_shape=(jax.ShapeDtypeStruct((tokens, D_MODEL), F32),
                   jax.ShapeDtypeStruct((tokens, D_MODEL), BF16),
                   jax.ShapeDtypeStruct((D_MODEL, D_MODEL), BF16),
                   jax.ShapeDtypeStruct((8, D_MODEL), F32)),
        scratch_shapes=[pltpu.VMEM((D_MODEL, D_MODEL), F32)],
        compiler_params=_cparams(("arbitrary",), VMEM_LIMIT),
    )(dyc, cv, proj, w_co, ln_g, ln_b)


def _conv_backward_taps(dcv, proj, conv_w):
    tokens = dcv.shape[0]
    tm = CONV_TM
    nt = tokens // tm
    halo_blocks = tm // CONV_PAD

    def body(d_ref, dn_ref, a_ref, b_ref, ah_ref, bh_ref, cw_ref, da_ref, db_ref, gw_ref, u_ext, d_ext, du_scr, gw_acc):
        i = pl.program_id(0)

        @pl.when(i == 0)
        def _():
            gw_acc[...] = jnp.zeros_like(gw_acc)

        _fill_u_ext(u_ext, a_ref, b_ref, ah_ref, bh_ref, i == 0)
        for lc in range(N_LANE_CHUNKS):
            sl = slice(LANES * lc, LANES * (lc + 1))
            d_ext[lc, 0:tm, :] = d_ref[:, sl]
            d_ext[lc, tm:tm + CONV_PAD, :] = jnp.where(i == nt - 1, 0.0, dn_ref[:, sl])

        def lane_chunk(lc, carry):
            n_rc = tm // 64
            du = [jnp.zeros((64, LANES), F32) for _ in range(n_rc)]
            for j in range(CONV_KERNEL):
                w = cw_ref[lc, pl.ds(j, 1), :]
                gsum = jnp.zeros((8, LANES), F32)
                for rc in range(n_rc):
                    du[rc] = du[rc] + w * d_ext[lc, pl.ds(64 * rc + 30 - j, 64), :]
                    prod = d_ext[lc, pl.ds(64 * rc, 64), :] * u_ext[lc, pl.ds(64 * rc + 2 + j, 64), :]
                    gsum = gsum + jnp.sum(prod.reshape(8, 8, LANES), axis=0)
                gw_acc[lc, j] += gsum
            for rc in range(n_rc):
                du_scr[lc, pl.ds(64 * rc, 64), :] = du[rc]
            return carry
        lax.fori_loop(0, N_LANE_CHUNKS, lane_chunk, 0)

        du = jnp.concatenate([du_scr[lc] for lc in range(N_LANE_CHUNKS)], axis=1)
        a, b = a_ref[...].astype(F32), b_ref[...].astype(F32)
        sb = _sig(b)
        da_ref[...] = (du * sb).astype(BF16)
        db_ref[...] = (du * a * (sb * (1.0 - sb))).astype(BF16)

        @pl.when(i == nt - 1)
        def _():
            gw_ref[...] = jnp.sum(gw_acc[...], axis=2)

    def prev_halo(i):
        return jnp.maximum(i * halo_blocks - 1, 0)

    def next_halo(i):
        return jnp.minimum((i + 1) * halo_blocks, tokens // CONV_PAD - 1)

    tile = lambda col=0: pl.BlockSpec((tm, D_MODEL), lambda i: (i, col))
    return pl.pallas_call(
        body, name="conv_backward_taps", grid=(nt,),
        in_specs=[tile(), pl.BlockSpec((CONV_PAD, D_MODEL), lambda i: (next_halo(i), 0)),
                  tile(COL_A), tile(COL_B),
                  pl.BlockSpec((CONV_PAD, D_MODEL), lambda i: (prev_halo(i), COL_A)),
                  pl.BlockSpec((CONV_PAD, D_MODEL), lambda i: (prev_halo(i), COL_B)),
                  _const_spec((N_DEV, CONV_PAD, LANES))],
        out_specs=(tile(), tile(), _const_spec((N_DEV, CONV_PAD, LANES))),
        out_shape=(jax.ShapeDtypeStruct((tokens, D_MODEL), BF16),
                   jax.ShapeDtypeStruct((tokens, D_MODEL), BF16),
                   jax.ShapeDtypeStruct((N_DEV, CONV_PAD, LANES), F32)),
        scratch_shapes=[pltpu.VMEM((N_LANE_CHUNKS, CONV_PAD + tm, LANES), F32),
                        pltpu.VMEM((N_LANE_CHUNKS, tm + CONV_PAD, LANES), F32),
                        pltpu.VMEM((N_LANE_CHUNKS, tm, LANES), F32),
                        pltpu.VMEM((N_LANE_CHUNKS, CONV_PAD, 8, LANES), F32)],
        compiler_params=_cparams(("arbitrary",), VMEM_LIMIT),
    )(dcv, dcv, proj, proj, proj, proj, conv_w)


def _fold_kv_head(dup, lo, second_half):
    both = dup + pltpu.roll(dup, HEAD_DIM, 1)
    return jnp.where(lo, 0.0, both) if second_half else jnp.where(lo, both, 0.0)


def _attention_backward(dya, o, qr, kd, vd, proj, sinks, w_ao, cos_t, sin_up, sin_dn):
    tokens = qr.shape[0]
    nb = tokens // BLOCK
    scale = HEAD_DIM ** -0.5

    def body(dya_ref, o_ref, ag0_ref, ag1_ref, q_ref, kc_ref, kp_ref, vc_ref, vp_ref, sinks_ref, wao_ref,
             cos_c, up_c, dn_c, cos_p, up_p, dn_p,
             dq_ref, dkv_ref, dag_ref, gwao_ref, gsink_ref, gacc, dk_carry, dv_carry, dq_scr):
        n = pl.program_id(0)
        lo = _lane_halves()

        @pl.when(n == 0)
        def _():
            gacc[...] = jnp.zeros_like(gacc)
            gsink_ref[...] = jnp.zeros_like(gsink_ref)
            dk_carry[...] = jnp.zeros_like(dk_carry)
            dv_carry[...] = jnp.zeros_like(dv_carry)

        def emit_prev(dk_prev, dv_prev):
            for p in range(2):
                sl = slice(LANES * p, LANES * (p + 1))
                dk = _rope_transposed(dk_carry[:, sl] + dk_prev[p], cos_p[...], up_p[...], dn_p[...])
                dkv_ref[:, sl] = dk.astype(BF16)
                dkv_ref[:, slice(256 + LANES * p, 256 + LANES * (p + 1))] = (dv_carry[:, sl] + dv_prev[p]).astype(BF16)

        @pl.when(n < nb)
        def _():
            dya = dya_ref[...]
            dpa = _dot_nt(dya, wao_ref[...])
            o = o_ref[...].astype(F32)
            ag = jnp.concatenate([ag0_ref[...], ag1_ref[...]], axis=1).astype(F32)
            sg = _sig(ag)
            gate = ag * sg
            gacc[...] += _dot_tn((o * gate).astype(BF16), dya)
            dag_ref[...] = (dpa * o * (sg * (1.0 + ag * (1.0 - sg)))).astype(BF16)
            do = (dpa * gate).astype(BF16)

            cur_ok, prev_ok = _band_masks(n)
            head_lane = lax.broadcasted_iota(jnp.int32, (1, LANES), 1)
            gsink = jnp.zeros((1, LANES), F32)
            zero_tile = jnp.zeros((BLOCK, LANES), F32)
            dk_cur, dk_prev = [zero_tile, zero_tile], [zero_tile, zero_tile]
            dv_cur, dv_prev = [zero_tile, zero_tile], [zero_tile, zero_tile]
            for kvh in range(N_KV_HEADS):
                ta, tb = slice(LANES * 2 * kvh, LANES * (2 * kvh + 1)), slice(LANES * (2 * kvh + 1), LANES * (2 * kvh + 2))
                ks = slice(LANES * kvh, LANES * (kvh + 1))
                q_stack = _stack_heads(q_ref[:, ta], q_ref[:, tb], lo)
                do_stack = _stack_heads(do[:, ta], do[:, tb], lo)
                k_c, k_p, v_c, v_p = kc_ref[:, ks], kp_ref[:, ks], vc_ref[:, ks], vp_ref[:, ks]
                p_c, p_p, p_s = _softmax_band(q_stack, k_c, k_p, _sink_column(sinks_ref, kvh), cur_ok, prev_ok)
                dp_c, dp_p = _dot_nt(do_stack, v_c), _dot_nt(do_stack, v_p)
                delta = jnp.sum(p_c * dp_c, axis=1, keepdims=True) + jnp.sum(p_p * dp_p, axis=1, keepdims=True)
                ds_c = (p_c * (dp_c - delta)).astype(BF16)
                ds_p = (p_p * (dp_p - delta)).astype(BF16)
                sink_terms = p_s * delta
                for g in range(4):
                    total = jnp.sum(sink_terms[BLOCK * g:BLOCK * (g + 1)], axis=0, keepdims=True)
                    gsink = gsink - jnp.where(head_lane == 4 * kvh + g, total, 0.0)
                dq_stack = _dot(ds_c, k_c) + _dot(ds_p, k_p)
                dq_scr[:, ta], dq_scr[:, tb] = _unstack_heads(dq_stack, lo)
                tile, second = kvh // 2, kvh % 2 == 1
                dk_cur[tile] = dk_cur[tile] + _fold_kv_head(_dot_tn(ds_c, q_stack), lo, second)
                dk_prev[tile] = dk_prev[tile] + _fold_kv_head(_dot_tn(ds_p, q_stack), lo, second)
                dv_cur[tile] = dv_cur[tile] + _fold_kv_head(_dot_tn(p_c.astype(BF16), do_stack), lo, second)
                dv_prev[tile] = dv_prev[tile] + _fold_kv_head(_dot_tn(p_p.astype(BF16), do_stack), lo, second)
            gsink_ref[0:1, :] += gsink
            for p in range(D_MODEL // LANES):
                sl = slice(LANES * p, LANES * (p + 1))
                dq_ref[:, sl] = (_rope_transposed(dq_scr[:, sl], cos_c[...], up_c[...], dn_c[...]) * scale).astype(BF16)
            emit_prev(dk_prev, dv_prev)
            for p in range(2):
                sl = slice(LANES * p, LANES * (p + 1))
                dk_carry[:, sl] = dk_cur[p]
                dv_carry[:, sl] = dv_cur[p]

        @pl.when(n == nb)
        def _():
            zero_tile = jnp.zeros((BLOCK, LANES), F32)
            emit_prev([zero_tile, zero_tile], [zero_tile, zero_tile])
            gwao_ref[...] = gacc[...].astype(BF16)

    def cur_idx(n):
        return jnp.minimum(n, nb - 1)

    def prev_idx(n):
        return jnp.clip(n - 1, 0, nb - 1)

    cur = lambda w, col=0: pl.BlockSpec((BLOCK, w), lambda n: (cur_idx(n), col))
    prev = lambda w: pl.BlockSpec((BLOCK, w), lambda n: (prev_idx(n), 0))
    return pl.pallas_call(
        body, name="attention_backward", grid=(nb + 1,),
        in_specs=[cur(D_MODEL), cur(D_MODEL), cur(512, COL512_AG), cur(512, COL512_AG + 1), cur(D_MODEL),
                  cur(512), prev(512), cur(512), prev(512),
                  pl.BlockSpec(memory_space=pltpu.SMEM), _const_spec((D_MODEL, D_MODEL)),
                  cur(LANES), cur(LANES), cur(LANES), prev(LANES), prev(LANES), prev(LANES)],
        out_specs=(cur(D_MODEL), prev(512), cur(D_MODEL),
                   _const_spec((D_MODEL, D_MODEL)), _const_spec((8, LANES))),
        out_shape=(jax.ShapeDtypeStruct((tokens, D_MODEL), BF16),
                   jax.ShapeDtypeStruct((tokens, 512), BF16),
                   jax.ShapeDtypeStruct((tokens, D_MODEL), BF16),
                   jax.ShapeDtypeStruct((D_MODEL, D_MODEL), BF16),
                   jax.ShapeDtypeStruct((8, LANES), F32)),
        scratch_shapes=[pltpu.VMEM((D_MODEL, D_MODEL), F32),
                        pltpu.VMEM((BLOCK, 256), F32), pltpu.VMEM((BLOCK, 256), F32),
                        pltpu.VMEM((BLOCK, D_MODEL), F32)],
        compiler_params=_cparams(("arbitrary",), VMEM_LIMIT),
    )(dya, o, proj, proj, qr, kd, kd, vd, vd, sinks, w_ao, cos_t, sin_up, sin_dn, cos_t, sin_up, sin_dn)


_SECTION_ROWS = (OFF_A, OFF_B, OFF_CG, OFF_Q, OFF_KV, OFF_AG, OFF_MLC, OFF_MLA)
_SECTION_WIDTH = (1024, 1024, 1024, 1024, 512, 1024, 1024, 1024)


def _input_backward(sections, w_in_t, x, dx2, norm_g):
    tokens = x.shape[0]
    tm = 256

    def body(*refs):
        sec = refs[:8]
        w_ref, x_ref, dx2_ref, g_ref, gx_ref, part_ref = refs[8:]

        @pl.when(pl.program_id(0) == 0)
        def _():
            part_ref[...] = jnp.zeros_like(part_ref)

        dh = jnp.zeros((tm, D_MODEL), F32)
        for s in range(8):
            dh = dh + _dot(sec[s][...], w_ref[_SECTION_ROWS[s]:_SECTION_ROWS[s] + _SECTION_WIDTH[s], :])
        xv = x_ref[...]
        r = lax.rsqrt(jnp.mean(xv * xv, axis=-1, keepdims=True) + RMS_EPS)
        xn = xv * r
        part_ref[0:1, :] += jnp.sum(dh * xn, axis=0, keepdims=True)
        dxn = dh * g_ref[...]
        gx_ref[...] = dx2_ref[...] + r * (dxn - xn * jnp.mean(dxn * xn, axis=-1, keepdims=True))

    tile = lambda w=D_MODEL: pl.BlockSpec((tm, w), lambda i: (i, 0))
    return pl.pallas_call(
        body, name="input_backward", grid=(tokens // tm,),
        in_specs=[tile(w) for w in _SECTION_WIDTH] + [
            pl.BlockSpec((IN_WIDTH, D_MODEL), lambda i: (0, 0), pipeline_mode=pl.Buffered(1)),
            tile(), tile(), _const_spec((1, D_MODEL))],
        out_specs=(tile(), _const_spec((8, D_MODEL))),
        out_shape=(jax.ShapeDtypeStruct((tokens, D_MODEL), F32),
                   jax.ShapeDtypeStruct((8, D_MODEL), F32)),
        compiler_params=_cparams(("arbitrary",), VMEM_LIMIT),
    )(*sections, w_in_t, x, dx2, norm_g)


def _in_projection_grad(sections, h):
    tokens = h.shape[0]
    tt = min(512, tokens)
    n_tok = tokens // tt
    chunk = 512
    first = tuple(r // chunk for r in _SECTION_ROWS)
    count = tuple(w // chunk for w in _SECTION_WIDTH)

    def body(*refs):
        sec = refs[:8]
        h_ref, out_ref, acc = refs[8:]
        j, i = pl.program_id(0), pl.program_id(1)

        @pl.when(i == 0)
        def _():
            acc[...] = jnp.zeros_like(acc)

        for s in range(8):
            @pl.when(jnp.logical_and(j >= first[s], j < first[s] + count[s]))
            def _(s=s):
                acc[...] += _dot_tn(sec[s][...], h_ref[...])

        @pl.when(i == n_tok - 1)
        def _():
            out_ref[...] = acc[...].astype(BF16)

    def sec_spec(s):
        def index(j, i):
            active = jnp.logical_and(j >= first[s], j < first[s] + count[s])
            return (jnp.where(active, i, 0), jnp.clip(j - first[s], 0, count[s] - 1))
        return pl.BlockSpec((tt, chunk), index)

    return pl.pallas_call(
        body, name="in_projection_grad", grid=(IN_WIDTH // chunk, n_tok),
        in_specs=[sec_spec(s) for s in range(8)] + [pl.BlockSpec((tt, D_MODEL), lambda j, i: (i, 0))],
        out_specs=pl.BlockSpec((chunk, D_MODEL), lambda j, i: (j, 0)),
        out_shape=jax.ShapeDtypeStruct((IN_WIDTH, D_MODEL), BF16),
        scratch_shapes=[pltpu.VMEM((chunk, D_MODEL), F32)],
        compiler_params=_cparams(("parallel", "arbitrary"), VMEM_LIMIT),
    )(*sections, h)


def _adamw_math(w, g, m, v):
    m = ADAM_B1 * m + (1.0 - ADAM_B1) * g
    v = ADAM_B2 * v + (1.0 - ADAM_B2) * (g * g)
    m_hat = m / (1.0 - ADAM_B1 ** ADAM_STEP)
    v_hat = v / (1.0 - ADAM_B2 ** ADAM_STEP)
    delta = -ADAM_LR * (m_hat / (jnp.sqrt(v_hat) + ADAM_EPS) + ADAM_WD * w)
    return delta, m, v


def _sum_slots(recv_ref):
    total = recv_ref[0].astype(F32)
    for d in range(1, N_DEV):
        total = total + recv_ref[d].astype(F32)
    return total


def _sum_in_projection_grad(recv):
    tr = 192

    def body(recv_ref, out_ref):
        out_ref[...] = _sum_slots(recv_ref)

    return pl.pallas_call(
        body, name="sum_in_projection_grad", grid=(SHARD_IN // tr,),
        in_specs=[pl.BlockSpec((N_DEV, tr, D_MODEL), lambda i: (0, i, 0))],
        out_specs=pl.BlockSpec((tr, D_MODEL), lambda i: (i, 0)),
        out_shape=jax.ShapeDtypeStruct((SHARD_IN, D_MODEL), F32),
        compiler_params=_cparams(("parallel",)),
    )(recv)


def _adamw(name, w, g, m, v, tile_rows):
    rows, cols = w.shape

    def body(w_ref, g_ref, m_ref, v_ref, d_ref, nm_ref, nv_ref):
        d_ref[...], nm_ref[...], nv_ref[...] = _adamw_math(w_ref[...], g_ref[...], m_ref[...], v_ref[...])

    spec = pl.BlockSpec((tile_rows, cols), lambda i: (i, 0))
    shape = jax.ShapeDtypeStruct((rows, cols), F32)
    return pl.pallas_call(
        body, name=name, grid=(rows // tile_rows,),
        in_specs=[spec] * 4, out_specs=(spec,) * 3, out_shape=(shape,) * 3,
        compiler_params=_cparams(("parallel",)),
    )(w, g, m, v)


def _sum_adamw(name, recv, w, m, v):
    def body(recv_ref, w_ref, m_ref, v_ref, g_ref, d_ref, nm_ref, nv_ref):
        g = _sum_slots(recv_ref)
        g_ref[...] = g
        d_ref[...], nm_ref[...], nv_ref[...] = _adamw_math(w_ref[...], g, m_ref[...], v_ref[...])

    shape = jax.ShapeDtypeStruct(w.shape, F32)
    return pl.pallas_call(body, name=name, out_shape=(shape,) * 4)(recv, w, m, v)


def _pad_rows(a, rows):
    return jnp.concatenate([a, jnp.zeros((rows - a.shape[0],) + a.shape[1:], a.dtype)], axis=0)


def kernel(x, norm_g, w_in, conv_dw_w, conv_dw_b, conv_ln_g, conv_ln_b, w_conv_out, attn_sinks, w_attn_out, w_out, final_norm_g, loss_target, m_norm_g, m_w_in, m_conv_dw_w, m_conv_dw_b, m_conv_ln_g, m_conv_ln_b, m_w_conv_out, m_attn_sinks, m_w_attn_out, m_w_out, m_final_norm_g, v_norm_g, v_w_in, v_conv_dw_w, v_conv_dw_b, v_conv_ln_g, v_conv_ln_b, v_w_conv_out, v_attn_sinks, v_w_attn_out, v_w_out, v_final_norm_g):
    xs, target = x[0], loss_target[0]
    tokens = xs.shape[0]
    fg_row = final_norm_g.reshape(1, D_MODEL)

    taps_bits = lax.bitcast_convert_type(_pad_rows(conv_dw_w[0], CONV_PAD), BF16).reshape(8, D_MODEL)
    pack = jnp.concatenate([w_conv_out[0].astype(BF16), w_attn_out[0].astype(BF16), w_out[0].astype(BF16),
                            jnp.pad(taps_bits, ((0, PACK_ROWS - 3 * SHARD_SQ - 8), (0, 0)))], axis=0)
    proj, h, w_in_t, pack_full = _gather_project(xs, norm_g, w_in[0].T.astype(BF16), pack)
    w_co = pack_full[:, 0:SHARD_SQ].reshape(D_MODEL, D_MODEL)
    w_ao = pack_full[:, SHARD_SQ:2 * SHARD_SQ].reshape(D_MODEL, D_MODEL)
    w_o = pack_full[:, 2 * SHARD_SQ:3 * SHARD_SQ].reshape(D_MODEL, D_MODEL)
    conv_w = lax.bitcast_convert_type(
        pack_full[:, 3 * SHARD_SQ:3 * SHARD_SQ + 8].reshape(N_DEV, CONV_PAD, LANES, 2), F32)

    cos_t, sin_up, sin_dn = _rope_tables(tokens)
    qr, kd, vd = _rope_qkv(proj, cos_t, sin_up, sin_dn)
    cv, yc = _conv_forward(proj, conv_w, conv_dw_b, conv_ln_g, conv_ln_b, w_co)
    o, ya = _attention_forward(qr, kd, vd, proj, attn_sinks, w_ao)

    dx2, dyc, dya, dmlc, dmla, g_out, part_head = _merge_and_head(yc, ya, proj, xs, target, w_o, fg_row)
    dcv, dcg, g_co, part_conv = _conv_backward_pointwise(dyc, cv, proj, w_co, conv_ln_g, conv_ln_b)
    da, db, g_conv = _conv_backward_taps(dcv, proj, conv_w)
    dq, dkv, dag, g_ao, part_sink = _attention_backward(dya, o, qr, kd, vd, proj, attn_sinks, w_ao, cos_t, sin_up, sin_dn)
    sections = (da, db, dcg, dq, dkv, dag, dmlc, dmla)
    grad_x, part_in = _input_backward(sections, w_in_t, xs, dx2, norm_g)
    g_in_t = _in_projection_grad(sections, h)

    small = jnp.concatenate([
        part_in[0:1], part_conv[2:3], part_conv[0:1], part_conv[1:2], part_head[0:1],
        jnp.pad(part_sink[0:1], ((0, 0), (0, D_MODEL - LANES))), part_head[1:2],
        jnp.zeros((1, D_MODEL), F32)], axis=0)

    r_in_t, r_co, r_ao, r_out, r_conv, r_small = _exchange_grads(g_in_t, g_co, g_ao, g_out, g_conv, small)

    grad_w_in = _sum_in_projection_grad(r_in_t).T
    d_w_in, nm_w_in, nv_w_in = _adamw("adamw_w_in", w_in[0], grad_w_in, m_w_in[0], v_w_in[0], 256)
    sq = {}
    for nm, recv, w, m, v in (("w_conv_out", r_co, w_conv_out, m_w_conv_out, v_w_conv_out),
                              ("w_attn_out", r_ao, w_attn_out, m_w_attn_out, v_w_attn_out),
                              ("w_out", r_out, w_out, m_w_out, v_w_out)):
        sq[nm] = _sum_adamw("sum_adamw_" + nm, recv, w[0], m[0], v[0])
    conv_res = _sum_adamw("sum_adamw_conv_dw_w", r_conv.reshape(N_DEV, CONV_PAD, LANES),
                          _pad_rows(conv_dw_w[0], CONV_PAD), _pad_rows(m_conv_dw_w[0], CONV_PAD),
                          _pad_rows(v_conv_dw_w[0], CONV_PAD))
    pad_sink = lambda a: jnp.pad(a, ((0, 0), (0, D_MODEL - N_Q_HEADS)))
    zero_rows = jnp.zeros((2, D_MODEL), F32)
    stack = lambda a, b, c, d, e, f: jnp.concatenate([a, b, c, d, e.reshape(1, D_MODEL), pad_sink(f), zero_rows], axis=0)
    small_res = _sum_adamw(
        "sum_adamw_small", r_small,
        stack(norm_g, conv_dw_b, conv_ln_g, conv_ln_b, final_norm_g, attn_sinks),
        stack(m_norm_g, m_conv_dw_b, m_conv_ln_g, m_conv_ln_b, m_final_norm_g, m_attn_sinks),
        stack(v_norm_g, v_conv_dw_b, v_conv_ln_g, v_conv_ln_b, v_final_norm_g, v_attn_sinks))
    loss = jnp.sum(small_res[0][6])

    def leaf(k):
        s = small_res[k]
        return (s[0:1], (grad_w_in, d_w_in, nm_w_in, nv_w_in)[k][None], conv_res[k][None, :CONV_KERNEL],
                s[1:2], s[2:3], s[3:4], sq["w_conv_out"][k][None], s[5:6, :N_Q_HEADS],
                sq["w_attn_out"][k][None], sq["w_out"][k][None], s[4])

    return (loss, grad_x[None], *leaf(0), *leaf(1), *leaf(2), *leaf(3))
```

```python
import jax
import jax.numpy as jnp
from jax import lax
from jax.experimental import pallas as pl
from jax.experimental.pallas import tpu as pltpu

F32 = jnp.float32
BF16 = jnp.bfloat16
MESH = pl.DeviceIdType.MESH

D_MODEL = 1024
IN_WIDTH = 7680
N_DEV = 8
SHARD_IN = IN_WIDTH // N_DEV
SHARD_SQ = D_MODEL // N_DEV
CONV_KERNEL = 31
CONV_PAD = 32
HEAD_DIM = 64
N_Q_HEADS = 16
N_KV_HEADS = 4
BLOCK = 128
LANES = 128
ROPE_THETA = 10000.0
RMS_EPS = 1e-5
LN_EPS = 1e-5
NEG = -1e30
ADAM_LR = 0.001
ADAM_B1 = 0.9
ADAM_B2 = 0.999
ADAM_EPS = 1e-08
ADAM_WD = 0.01
ADAM_STEP = 10

OFF_A, OFF_B, OFF_CG, OFF_Q, OFF_KV, OFF_AG, OFF_MLC, OFF_MLA = 0, 1024, 2048, 3072, 4096, 4608, 5632, 6656
COL_A, COL_B, COL_CG, COL_Q = 0, 1, 2, 3
COL512_KV, COL512_AG, COL512_MLC, COL512_MLA = 8, 9, 11, 13
UNIT = 2 * SHARD_IN
PACK_ROWS = 400

VMEM_LIMIT = 56 * 1024 * 1024


def _cparams(sem=None, vmem=None):
    return pltpu.CompilerParams(dimension_semantics=sem, vmem_limit_bytes=vmem)


def _sig(v):
    return 1.0 / (1.0 + jnp.exp(-v))


def _dot(a, b):
    return jnp.dot(a, b, preferred_element_type=F32)


def _dot_nt(a, b):
    return lax.dot_general(a, b, (((1,), (1,)), ((), ())), preferred_element_type=F32)


def _dot_tn(a, b):
    return lax.dot_general(a, b, (((0,), (0,)), ((), ())), preferred_element_type=F32)


def _const_spec(shape):
    nd = len(shape)
    return pl.BlockSpec(shape, lambda *_: (0,) * nd)


def _mesh_pos():
    x, y, c = lax.axis_index("x"), lax.axis_index("y"), lax.axis_index("c")
    return x, y, c, 4 * x + 2 * y + c


def _peer(x, y, c, k):
    px = 1 - x if (k >> 2) & 1 else x
    py = 1 - y if (k >> 1) & 1 else y
    pc = 1 - c if k & 1 else c
    return (px, py, pc), 4 * px + 2 * py + pc


def _gather_project(x, norm_g, w_shard_t, pack):
    tokens = x.shape[0]
    tt = min(512, tokens // 2)
    n_tok = tokens // tt
    rc = min(128, tt)

    def body(x_hbm, g_ref, ws_hbm, pack_hbm, proj_hbm, h_hbm, wfull_hbm, packfull_hbm,
             w_vmem, h_vmem, x_buf, o_buf, send_sems, recv_sems, local_sems, x_sems, o_sems):
        x_, y_, c_, me = _mesh_pos()
        myself, sibling = (x_, y_, c_), (x_, y_, 1 - c_)
        chips = ((1 - x_, y_), (x_, 1 - y_), (1 - x_, 1 - y_))

        def shard(ref, idx):
            return ref.at[pl.ds(pl.multiple_of(idx * SHARD_IN, 64), SHARD_IN)]

        def copies(k, idx, to, own=False):
            def mk(a, src, dst):
                return pltpu.make_async_remote_copy(src_ref=src, dst_ref=dst, send_sem=send_sems.at[a, k],
                                                    recv_sem=recv_sems.at[a, k], device_id=to, device_id_type=MESH)
            return [mk(0, ws_hbm if own else shard(w_vmem, idx), shard(w_vmem, idx)),
                    mk(1, pack_hbm if own else packfull_hbm.at[idx], packfull_hbm.at[idx])]

        own_w = pltpu.make_async_copy(ws_hbm, shard(w_vmem, me), local_sems.at[0])
        own_p = pltpu.make_async_copy(pack_hbm, packfull_hbm.at[me], local_sems.at[1])
        own_w.start()
        own_p.start()
        sent = copies(0, me, sibling, own=True)
        for r, chip in enumerate(chips):
            sent += copies(1 + r, me, (*chip, c_), own=True)
        for cp in sent:
            cp.start()

        def x_copy(t, slot):
            return pltpu.make_async_copy(x_hbm.at[pl.ds(t * tt, tt)], x_buf.at[slot], x_sems.at[slot])

        x_copy(0, 0).start()
        for t in range(n_tok):
            slot = t % 2
            if t + 1 < n_tok:
                x_copy(t + 1, 1 - slot).start()
            x_copy(t, slot).wait()

            def chunk(r0, t=t, slot=slot):
                xv = x_buf[slot, pl.ds(r0, rc), :]
                r = lax.rsqrt(jnp.mean(xv * xv, axis=-1, keepdims=True) + RMS_EPS)
                h_vmem[pl.ds(t * tt + r0, rc), :] = (xv * r * g_ref[...]).astype(BF16)
            _row_chunks(tt, rc, chunk)
        h_out = pltpu.make_async_copy(h_vmem, h_hbm, local_sems.at[6])
        h_out.start()
        local = [own_p, h_out]

        def project_unit(q, u):
            rows = pl.ds(pl.multiple_of(q * UNIT, LANES), UNIT)
            w_out = pltpu.make_async_copy(w_vmem.at[rows], wfull_hbm.at[rows], local_sems.at[2 + u])
            w_out.start()
            local.append(w_out)

            def o_copy(slot, t):
                return pltpu.make_async_copy(
                    o_buf.at[slot], proj_hbm.at[pl.ds(pl.multiple_of(t * tt, tt), tt), rows], o_sems.at[slot])

            def tile(t, carry):
                slot = lax.rem(t, 2)

                @pl.when(t >= 2)
                def _():
                    o_copy(slot, t).wait()
                o_buf[slot] = _dot_nt(h_vmem[pl.ds(pl.multiple_of(t * tt, tt), tt), :], w_vmem[rows, :]).astype(BF16)
                o_copy(slot, t).start()
                return carry
            lax.fori_loop(0, n_tok, tile, 0)
            o_copy(0, 0).wait()
            o_copy(1, 0).wait()

        def dev(chip, core):
            return 4 * chip[0] + 2 * chip[1] + core

        def arrive_and_pass_on(r):
            for cp in copies(1 + r, dev(chips[r], c_), myself):
                cp.wait_recv()
            passed = copies(4 + r, dev(chips[r], c_), sibling)
            for cp in passed:
                cp.start()
            sent.extend(passed)

        def passed_on_to_me(r):
            for cp in copies(4 + r, dev(chips[r], 1 - c_), myself):
                cp.wait_recv()

        own_w.wait()
        for cp in copies(0, dev((x_, y_), 1 - c_), myself):
            cp.wait_recv()
        project_unit(2 * x_ + y_, 0)
        arrive_and_pass_on(0)
        arrive_and_pass_on(1)
        passed_on_to_me(0)
        project_unit(2 * chips[0][0] + chips[0][1], 1)
        arrive_and_pass_on(2)
        passed_on_to_me(1)
        project_unit(2 * chips[1][0] + chips[1][1], 2)
        passed_on_to_me(2)
        project_unit(2 * chips[2][0] + chips[2][1], 3)
        for cp in sent:
            cp.wait_send()
        for cp in local:
            cp.wait()

    hbm = pl.BlockSpec(memory_space=pltpu.HBM)
    return pl.pallas_call(
        body, name="gather_project",
        in_specs=[hbm, pl.BlockSpec(memory_space=pltpu.VMEM), hbm, hbm],
        out_specs=(hbm, hbm, hbm, hbm),
        out_shape=(jax.ShapeDtypeStruct((tokens, IN_WIDTH), BF16),
                   jax.ShapeDtypeStruct((tokens, D_MODEL), BF16),
                   jax.ShapeDtypeStruct((IN_WIDTH, D_MODEL), BF16),
                   jax.ShapeDtypeStruct((N_DEV, PACK_ROWS, D_MODEL), BF16)),
        scratch_shapes=[pltpu.VMEM((IN_WIDTH, D_MODEL), BF16),
                        pltpu.VMEM((tokens, D_MODEL), BF16),
                        pltpu.VMEM((2, tt, D_MODEL), F32),
                        pltpu.VMEM((2, tt, UNIT), BF16),
                        pltpu.SemaphoreType.DMA((2, N_DEV - 1)),
                        pltpu.SemaphoreType.DMA((2, N_DEV - 1)),
                        pltpu.SemaphoreType.DMA((7,)),
                        pltpu.SemaphoreType.DMA((2,)),
                        pltpu.SemaphoreType.DMA((2,))],
        compiler_params=_cparams(None, VMEM_LIMIT),
    )(x, norm_g, w_shard_t, pack)


HALF_ROWS = SHARD_IN + 3 * SHARD_SQ

_UNIT_PIECES = (
    ((0, 0, 1024, 0), (1, 0, 896, 1024)),
    ((1, 896, 128, 0), (2, 0, 1024, 128), (3, 0, 768, 1152)),
    ((3, 768, 256, 0), (4, 0, 512, 256), (5, 0, 1024, 768), (6, 0, 128, 1792)),
    ((6, 128, 896, 0), (7, 0, 1024, 896)),
)


def _grad_exchange(sections, h, g_co, g_ao, g_out, g_conv, small):
    tokens = h.shape[0]
    tt = min(512, tokens // 4)
    n_tok = tokens // tt
    rc = 192

    def body(*refs):
        sec = refs[:8]
        (h_hbm, gco_hbm, gao_hbm, gout_hbm, gconv_hbm, small_hbm, gmine_hbm, rconv_hbm, rsmall_hbm,
         lhs_buf, h_buf, acc, send_buf, mine_buf, stage, final,
         lhs_sems, h_sems, tail_sems, d2d_send, d2d_recv, ici_send, ici_recv,
         tiny_send, tiny_recv, local_sems) = refs[8:]
        x_, y_, c_, me = _mesh_pos()
        myself, sibling = (x_, y_, c_), (x_, y_, 1 - c_)
        chips = ((1 - x_, y_), (x_, 1 - y_), (1 - x_, 1 - y_), (x_, y_))
        squares = (gco_hbm, gao_hbm, gout_hbm)

        def remote(src, dst, send_sem, recv_sem, to):
            return pltpu.make_async_remote_copy(src_ref=src, dst_ref=dst, send_sem=send_sem, recv_sem=recv_sem,
                                                device_id=to, device_id_type=MESH)

        own_tiny = [pltpu.make_async_copy(gconv_hbm.at[me], rconv_hbm.at[me], local_sems.at[0]),
                    pltpu.make_async_copy(small_hbm, rsmall_hbm.at[me], local_sems.at[1])]
        for cp in own_tiny:
            cp.start()
        tiny = []
        for k in range(1, N_DEV):
            peer, peer_idx = _peer(x_, y_, c_, k)
            tiny += [remote(gconv_hbm.at[peer_idx], rconv_hbm.at[me], tiny_send.at[0, k - 1], tiny_recv.at[0, k - 1], peer),
                     remote(small_hbm, rsmall_hbm.at[me], tiny_send.at[1, k - 1], tiny_recv.at[1, k - 1], peer)]
        for cp in tiny:
            cp.start()

        def fetch(q, t, slot, wait):
            rows = pl.ds(pl.multiple_of(t * tt, tt), tt)
            for k in range(4):
                @pl.when(q == k)
                def _(k=k):
                    for j, (s, col, width, place) in enumerate(_UNIT_PIECES[k]):
                        cp = pltpu.make_async_copy(sec[s].at[rows, pl.ds(col, width)],
                                                   lhs_buf.at[slot, pl.ds(0, tt), pl.ds(place, width)], lhs_sems.at[slot, j])
                        cp.wait() if wait else cp.start()
            cp = pltpu.make_async_copy(h_hbm.at[rows], h_buf.at[slot], h_sems.at[slot])
            cp.wait() if wait else cp.start()

        def half_rows(core):
            return pl.ds(pl.multiple_of(core * SHARD_IN, 64), SHARD_IN)

        def d2d(u):
            return remote(send_buf, stage.at[u], d2d_send.at[u], d2d_recv.at[u], sibling)

        def ici(u):
            return remote(stage.at[u], final.at[u], ici_send.at[u], ici_recv.at[u], (*chips[u], c_))

        def tails(q):
            out = []
            for half, (core, buf) in enumerate(((1 - c_, send_buf), (c_, mine_buf))):
                for j, g in enumerate(squares):
                    rows = pl.ds(pl.multiple_of((2 * q + core) * SHARD_SQ, SHARD_SQ), SHARD_SQ)
                    out.append(pltpu.make_async_copy(g.at[rows], buf.at[pl.ds(SHARD_IN + j * SHARD_SQ, SHARD_SQ)],
                                                     tail_sems.at[3 * half + j]))
            return out

        def chip_sum(u):
            d2d(u).wait_recv()

            def chunk(r0):
                rows = pl.ds(r0, rc)
                stage[u, rows, :] = (stage[u, rows, :].astype(F32) + mine_buf[rows, :].astype(F32)).astype(BF16)
            _row_chunks(HALF_ROWS, rc, chunk)
            if u < 3:
                ici(u).start()

        def chip_of(u):
            return 2 * chips[u][0] + chips[u][1]

        for u in range(4):
            q = chip_of(u)
            if u == 0:
                for cp in tails(q):
                    cp.start()
            acc[...] = jnp.zeros_like(acc)
            fetch(q, 0, 0, wait=False)

            def tile(t, carry, u=u, q=q):
                slot = lax.rem(t, 2)

                @pl.when(t + 1 < n_tok)
                def _():
                    fetch(q, t + 1, 1 - slot, wait=False)
                fetch(q, t, slot, wait=True)
                acc[...] += _dot_tn(lhs_buf[slot], h_buf[slot])
                if u > 0:
                    @pl.when(t == 1)
                    def _():
                        chip_sum(u - 1)

                    @pl.when(t == 2)
                    def _():
                        d2d(u - 1).wait_send()
                        for cp in tails(q):
                            cp.start()
                return carry
            lax.fori_loop(0, n_tok, tile, 0)
            for cp in tails(q):
                cp.wait()

            def split(r0):
                rows = pl.ds(r0, rc)
                send_buf[rows, :] = acc[pl.ds(pl.multiple_of((1 - c_) * SHARD_IN + r0, 64), rc), :].astype(BF16)
                mine_buf[rows, :] = acc[pl.ds(pl.multiple_of(c_ * SHARD_IN + r0, 64), rc), :].astype(BF16)
            _row_chunks(SHARD_IN, rc, split)
            d2d(u).start()

        chip_sum(3)
        for u in range(3):
            remote(stage.at[u], final.at[u], ici_send.at[u], ici_recv.at[u], myself).wait_recv()

        def total(r0):
            rows = pl.ds(r0, rc)
            acc[rows, :] = ((stage[3, rows, :].astype(F32) + final[0, rows, :].astype(F32))
                            + final[1, rows, :].astype(F32)) + final[2, rows, :].astype(F32)
        _row_chunks(HALF_ROWS, rc, total)
        out = pltpu.make_async_copy(acc.at[pl.ds(0, HALF_ROWS)], gmine_hbm, local_sems.at[2])
        out.start()
        d2d(3).wait_send()
        for u in range(3):
            ici(u).wait_send()
        for k in range(1, N_DEV):
            peer, peer_idx = _peer(x_, y_, c_, k)
            remote(gconv_hbm.at[me], rconv_hbm.at[peer_idx], tiny_send.at[0, k - 1], tiny_recv.at[0, k - 1], myself).wait_recv()
            remote(small_hbm, rsmall_hbm.at[peer_idx], tiny_send.at[1, k - 1], tiny_recv.at[1, k - 1], myself).wait_recv()
        for cp in tiny:
            cp.wait_send()
        for cp in own_tiny:
            cp.wait()
        out.wait()

    hbm = pl.BlockSpec(memory_space=pltpu.HBM)
    return pl.pallas_call(
        body, name="grad_exchange",
        in_specs=[hbm] * 14, out_specs=(hbm, hbm, hbm),
        out_shape=(jax.ShapeDtypeStruct((HALF_ROWS, D_MODEL), F32),
                   jax.ShapeDtypeStruct((N_DEV, CONV_PAD, LANES), F32),
                   jax.ShapeDtypeStruct((N_DEV, 8, D_MODEL), F32)),
        scratch_shapes=[pltpu.VMEM((2, tt, UNIT), BF16),
                        pltpu.VMEM((2, tt, D_MODEL), BF16),
                        pltpu.VMEM((UNIT, D_MODEL), F32),
                        pltpu.VMEM((HALF_ROWS, D_MODEL), BF16),
                        pltpu.VMEM((HALF_ROWS, D_MODEL), BF16),
                        pltpu.VMEM((4, HALF_ROWS, D_MODEL), BF16),
                        pltpu.VMEM((3, HALF_ROWS, D_MODEL), BF16),
                        pltpu.SemaphoreType.DMA((2, 4)),
                        pltpu.SemaphoreType.DMA((2,)),
                        pltpu.SemaphoreType.DMA((6,)),
                        pltpu.SemaphoreType.DMA((4,)),
                        pltpu.SemaphoreType.DMA((4,)),
                        pltpu.SemaphoreType.DMA((3,)),
                        pltpu.SemaphoreType.DMA((3,)),
                        pltpu.SemaphoreType.DMA((2, N_DEV - 1)),
                        pltpu.SemaphoreType.DMA((2, N_DEV - 1)),
                        pltpu.SemaphoreType.DMA((3,))],
        compiler_params=_cparams(None, 60 * 1024 * 1024),
    )(*sections, h, g_co, g_ao, g_out, g_conv, small)


def _row_chunks(total, size, fn):
    n = total // size
    if n == 1:
        fn(0)
        return

    def step(i, carry):
        fn(pl.multiple_of(i * size, size))
        return carry
    lax.fori_loop(0, n, step, 0)


def _rope_tables(tokens):
    inv_freq = ROPE_THETA ** (-jnp.arange(0, HEAD_DIM, 2, dtype=F32) / HEAD_DIM)
    ang = jnp.arange(tokens, dtype=jnp.int32).astype(F32)[:, None] * inv_freq[None, :]
    cos, sin = jnp.cos(ang), jnp.sin(ang)
    zero = jnp.zeros_like(sin)
    cos_t = jnp.tile(jnp.concatenate([cos, cos], axis=1), (1, LANES // HEAD_DIM))
    sin_up = jnp.tile(jnp.concatenate([-sin, zero], axis=1), (1, LANES // HEAD_DIM))
    sin_dn = jnp.tile(jnp.concatenate([zero, sin], axis=1), (1, LANES // HEAD_DIM))
    return cos_t, sin_up, sin_dn


def _rope(t, cos_t, sin_up, sin_dn):
    return t * cos_t + pltpu.roll(t, LANES - 32, 1) * sin_up + pltpu.roll(t, 32, 1) * sin_dn


def _rope_transposed(g, cos_t, sin_up, sin_dn):
    return g * cos_t + pltpu.roll(g * sin_up, 32, 1) + pltpu.roll(g * sin_dn, LANES - 32, 1)


def _lane_halves():
    lane = lax.broadcasted_iota(jnp.int32, (BLOCK, LANES), 1)
    return lane < HEAD_DIM


def _rope_qkv(proj, cos_t, sin_up, sin_dn):
    tokens = proj.shape[0]
    tm = min(512, tokens)
    scale = HEAD_DIM ** -0.5

    def body(q_ref, kv_ref, cos_ref, up_ref, dn_ref, qr_ref, kd_ref, vd_ref):
        lo = _lane_halves()

        def chunk(r0):
            rows = pl.ds(r0, BLOCK)
            cs, up, dn = cos_ref[rows, :], up_ref[rows, :], dn_ref[rows, :]
            for p in range(D_MODEL // LANES):
                sl = slice(LANES * p, LANES * (p + 1))
                qt = q_ref[rows, sl].astype(F32)
                qr_ref[rows, sl] = (_rope(qt, cs, up, dn) * scale).astype(BF16)
            for p in range(2):
                sl = slice(LANES * p, LANES * (p + 1))
                kt = _rope(kv_ref[rows, sl].astype(F32), cs, up, dn)
                vt = kv_ref[rows, slice(256 + LANES * p, 256 + LANES * (p + 1))].astype(F32)
                for src, dst in ((kt, kd_ref), (vt, vd_ref)):
                    first = jnp.where(lo, src, 0.0)
                    second = src - first
                    dst[rows, slice(LANES * 2 * p, LANES * (2 * p + 1))] = (first + pltpu.roll(first, HEAD_DIM, 1)).astype(BF16)
                    dst[rows, slice(LANES * (2 * p + 1), LANES * (2 * p + 2))] = (second + pltpu.roll(second, HEAD_DIM, 1)).astype(BF16)
        _row_chunks(tm, BLOCK, chunk)

    tab = pl.BlockSpec((tm, LANES), lambda i: (i, 0))
    return pl.pallas_call(
        body, name="rope_qkv", grid=(tokens // tm,),
        in_specs=[pl.BlockSpec((tm, D_MODEL), lambda i: (i, COL_Q)),
                  pl.BlockSpec((tm, 512), lambda i: (i, COL512_KV)), tab, tab, tab],
        out_specs=(pl.BlockSpec((tm, D_MODEL), lambda i: (i, 0)),
                   pl.BlockSpec((tm, 512), lambda i: (i, 0)),
                   pl.BlockSpec((tm, 512), lambda i: (i, 0))),
        out_shape=(jax.ShapeDtypeStruct((tokens, D_MODEL), BF16),
                   jax.ShapeDtypeStruct((tokens, 512), BF16),
                   jax.ShapeDtypeStruct((tokens, 512), BF16)),
        compiler_params=_cparams(("parallel",)),
    )(proj, proj, cos_t, sin_up, sin_dn)


CONV_TM = 256
N_LANE_CHUNKS = D_MODEL // LANES


def _fill_u_ext(u_ext, a_ref, b_ref, ah_ref, bh_ref, first_tile):
    for lc in range(N_LANE_CHUNKS):
        sl = slice(LANES * lc, LANES * (lc + 1))
        uh = ah_ref[:, sl].astype(F32) * _sig(bh_ref[:, sl].astype(F32))
        u_ext[lc, 0:CONV_PAD, :] = jnp.where(first_tile, 0.0, uh)
        u_ext[lc, CONV_PAD:CONV_PAD + CONV_TM, :] = a_ref[:, sl].astype(F32) * _sig(b_ref[:, sl].astype(F32))


def _conv_forward(proj, conv_w, dw_b, ln_g, ln_b, w_co):
    tokens = proj.shape[0]
    tm = CONV_TM
    halo_blocks = tm // CONV_PAD

    def body(a_ref, b_ref, ah_ref, bh_ref, cg_ref, cw_ref, dwb_ref, lng_ref, lnb_ref, wco_ref,
             cv_ref, yc_ref, u_ext, cv_scr):
        _fill_u_ext(u_ext, a_ref, b_ref, ah_ref, bh_ref, pl.program_id(0) == 0)

        def lane_chunk(lc, carry):
            for rc in range(tm // 64):
                acc = jnp.zeros((64, LANES), F32)
                for j in range(CONV_KERNEL):
                    acc = acc + cw_ref[lc, pl.ds(j, 1), :] * u_ext[lc, pl.ds(64 * rc + 2 + j, 64), :]
                cv_scr[lc, pl.ds(64 * rc, 64), :] = acc
            return carry
        lax.fori_loop(0, N_LANE_CHUNKS, lane_chunk, 0)

        cv = jnp.concatenate([cv_scr[lc] for lc in range(N_LANE_CHUNKS)], axis=1) + dwb_ref[...]
        cv_ref[...] = cv
        mu = jnp.mean(cv, axis=-1, keepdims=True)
        zc = cv - mu
        rstd = lax.rsqrt(jnp.mean(zc * zc, axis=-1, keepdims=True) + LN_EPS)
        ln = zc * rstd * lng_ref[...] + lnb_ref[...]
        cg = cg_ref[...].astype(F32)
        pc = (ln * _sig(ln)) * (cg * _sig(cg))
        yc_ref[...] = _dot(pc.astype(BF16), wco_ref[...]).astype(BF16)

    def halo_map(i):
        return (jnp.maximum(i * halo_blocks - 1, 0), 0)

    tile = lambda col: pl.BlockSpec((tm, D_MODEL), lambda i: (i, col))
    return pl.pallas_call(
        body, name="conv_forward", grid=(tokens // tm,),
        in_specs=[tile(COL_A), tile(COL_B),
                  pl.BlockSpec((CONV_PAD, D_MODEL), lambda i: (halo_map(i)[0], COL_A)),
                  pl.BlockSpec((CONV_PAD, D_MODEL), lambda i: (halo_map(i)[0], COL_B)),
                  tile(COL_CG), _const_spec((N_DEV, CONV_PAD, LANES)),
                  _const_spec((1, D_MODEL)), _const_spec((1, D_MODEL)), _const_spec((1, D_MODEL)),
                  _const_spec((D_MODEL, D_MODEL))],
        out_specs=(pl.BlockSpec((tm, D_MODEL), lambda i: (i, 0)),
                   pl.BlockSpec((tm, D_MODEL), lambda i: (i, 0))),
        out_shape=(jax.ShapeDtypeStruct((tokens, D_MODEL), F32),
                   jax.ShapeDtypeStruct((tokens, D_MODEL), BF16)),
        scratch_shapes=[pltpu.VMEM((N_LANE_CHUNKS, CONV_PAD + tm, LANES), F32),
                        pltpu.VMEM((N_LANE_CHUNKS, tm, LANES), F32)],
        compiler_params=_cparams(("parallel",), VMEM_LIMIT),
    )(proj, proj, proj, proj, proj, conv_w, dw_b, ln_g, ln_b, w_co)


def _band_masks(n):
    row = lax.broadcasted_iota(jnp.int32, (4 * BLOCK, BLOCK), 0) & (BLOCK - 1)
    col = lax.broadcasted_iota(jnp.int32, (4 * BLOCK, BLOCK), 1)
    return col <= row, jnp.logical_and(col > row, n > 0)


def _stack_heads(tile_a, tile_b, lo):
    zero = jnp.zeros_like(tile_a)
    return jnp.concatenate([jnp.where(lo, tile_a, zero), jnp.where(lo, zero, tile_a),
                            jnp.where(lo, tile_b, zero), jnp.where(lo, zero, tile_b)], axis=0)


def _unstack_heads(stacked, lo):
    s = [stacked[BLOCK * g:BLOCK * (g + 1)] for g in range(4)]
    return (jnp.where(lo, s[0], 0.0) + jnp.where(lo, 0.0, s[1]),
            jnp.where(lo, s[2], 0.0) + jnp.where(lo, 0.0, s[3]))


def _sink_column(sinks_ref, kvh):
    return jnp.concatenate([jnp.full((BLOCK, 1), sinks_ref[0, 4 * kvh + g], F32) for g in range(4)], axis=0)


def _softmax_band(q_stack, k_cur, k_prev, sink, cur_ok, prev_ok):
    s_c = jnp.where(cur_ok, _dot_nt(q_stack, k_cur), NEG)
    s_p = jnp.where(prev_ok, _dot_nt(q_stack, k_prev), NEG)
    m = jnp.maximum(jnp.maximum(jnp.max(s_c, axis=1, keepdims=True), jnp.max(s_p, axis=1, keepdims=True)), sink)
    e_c, e_p, e_s = jnp.exp(s_c - m), jnp.exp(s_p - m), jnp.exp(sink - m)
    inv = 1.0 / (jnp.sum(e_c, axis=1, keepdims=True) + jnp.sum(e_p, axis=1, keepdims=True) + e_s)
    return e_c * inv, e_p * inv, e_s * inv


def _attention_forward(qr, kd, vd, proj, sinks, w_ao):
    tokens = qr.shape[0]
    nb = tokens // BLOCK

    def body(q_ref, kc_ref, kp_ref, vc_ref, vp_ref, ag0_ref, ag1_ref, sinks_ref, wao_ref, o_ref, ya_ref, o_scr):
        n = pl.program_id(0)
        lo = _lane_halves()
        cur_ok, prev_ok = _band_masks(n)
        for kvh in range(N_KV_HEADS):
            ta, tb = slice(LANES * 2 * kvh, LANES * (2 * kvh + 1)), slice(LANES * (2 * kvh + 1), LANES * (2 * kvh + 2))
            ks = slice(LANES * kvh, LANES * (kvh + 1))
            q_stack = _stack_heads(q_ref[:, ta], q_ref[:, tb], lo)
            p_c, p_p, _ = _softmax_band(q_stack, kc_ref[:, ks], kp_ref[:, ks], _sink_column(sinks_ref, kvh), cur_ok, prev_ok)
            o_stack = _dot(p_c.astype(BF16), vc_ref[:, ks]) + _dot(p_p.astype(BF16), vp_ref[:, ks])
            o_scr[:, ta], o_scr[:, tb] = _unstack_heads(o_stack, lo)
        o = o_scr[...]
        o_ref[...] = o.astype(BF16)
        ag = jnp.concatenate([ag0_ref[...], ag1_ref[...]], axis=1).astype(F32)
        ya_ref[...] = _dot((o * (ag * _sig(ag))).astype(BF16), wao_ref[...]).astype(BF16)

    cur = lambda w, col=0: pl.BlockSpec((BLOCK, w), lambda n: (n, col))
    prev = lambda w: pl.BlockSpec((BLOCK, w), lambda n: (jnp.maximum(n - 1, 0), 0))
    return pl.pallas_call(
        body, name="attention_forward", grid=(nb,),
        in_specs=[cur(D_MODEL), cur(512), prev(512), cur(512), prev(512),
                  cur(512, COL512_AG), cur(512, COL512_AG + 1),
                  pl.BlockSpec(memory_space=pltpu.SMEM), _const_spec((D_MODEL, D_MODEL))],
        out_specs=(cur(D_MODEL), cur(D_MODEL)),
        out_shape=(jax.ShapeDtypeStruct((tokens, D_MODEL), BF16),
                   jax.ShapeDtypeStruct((tokens, D_MODEL), BF16)),
        scratch_shapes=[pltpu.VMEM((BLOCK, D_MODEL), F32)],
        compiler_params=_cparams(("parallel",), VMEM_LIMIT),
    )(qr, kd, kd, vd, vd, proj, proj, sinks, w_ao)


def _merge_and_head(yc, ya, proj, x, target, w_out, final_g):
    tokens = x.shape[0]
    tm = 256
    last = tokens // tm - 1

    def body(yc_ref, ya_ref, mlc0_ref, mlc1_ref, mla0_ref, mla1_ref, x_ref, t_ref, wout_ref, fg_ref,
             dx2_ref, dyc_ref, dya_ref, dmlc_ref, dmla_ref, gwout_ref, part_ref, gacc):
        i = pl.program_id(0)

        @pl.when(i == 0)
        def _():
            gacc[...] = jnp.zeros_like(gacc)
            part_ref[...] = jnp.zeros_like(part_ref)

        yc, ya = yc_ref[...].astype(F32), ya_ref[...].astype(F32)
        gc = _sig(jnp.concatenate([mlc0_ref[...], mlc1_ref[...]], axis=1).astype(F32))
        ga = _sig(jnp.concatenate([mla0_ref[...], mla1_ref[...]], axis=1).astype(F32))
        merged = (gc * yc + ga * ya).astype(BF16)
        x2 = x_ref[...] + _dot(merged, wout_ref[...])
        r2 = lax.rsqrt(jnp.mean(x2 * x2, axis=-1, keepdims=True) + RMS_EPS)
        x2n = x2 * r2
        fg = fg_ref[...]
        err = x2n * fg - t_ref[...]
        dy = err * (1.0 / D_MODEL)
        part_ref[0:1, :] += jnp.sum(dy * x2n, axis=0, keepdims=True)
        part_ref[1:2, :] += jnp.sum(err * err, axis=0, keepdims=True) * (0.5 / D_MODEL)
        dx2n = dy * fg
        dx2 = r2 * (dx2n - x2n * jnp.mean(dx2n * x2n, axis=-1, keepdims=True))
        dx2_ref[...] = dx2
        dx2b = dx2.astype(BF16)
        gacc[...] += _dot_tn(merged, dx2b)
        dm = _dot_nt(dx2b, wout_ref[...])
        dyc_ref[...] = (dm * gc).astype(BF16)
        dya_ref[...] = (dm * ga).astype(BF16)
        dmlc_ref[...] = (dm * yc * (gc * (1.0 - gc))).astype(BF16)
        dmla_ref[...] = (dm * ya * (ga * (1.0 - ga))).astype(BF16)

        @pl.when(i == last)
        def _():
            gwout_ref[...] = gacc[...].astype(BF16)

    tile = lambda col=0: pl.BlockSpec((tm, D_MODEL), lambda i: (i, col))
    half = lambda col: pl.BlockSpec((tm, 512), lambda i: (i, col))
    return pl.pallas_call(
        body, name="merge_and_head", grid=(tokens // tm,),
        in_specs=[tile(), tile(), half(COL512_MLC), half(COL512_MLC + 1), half(COL512_MLA), half(COL512_MLA + 1),
                  tile(), tile(), _const_spec((D_MODEL, D_MODEL)), _const_spec((1, D_MODEL))],
        out_specs=(tile(), tile(), tile(), tile(), tile(),
                   _const_spec((D_MODEL, D_MODEL)), _const_spec((8, D_MODEL))),
        out_shape=(jax.ShapeDtypeStruct((tokens, D_MODEL), F32),
                   jax.ShapeDtypeStruct((tokens, D_MODEL), BF16),
                   jax.ShapeDtypeStruct((tokens, D_MODEL), BF16),
                   jax.ShapeDtypeStruct((tokens, D_MODEL), BF16),
                   jax.ShapeDtypeStruct((tokens, D_MODEL), BF16),
                   jax.ShapeDtypeStruct((D_MODEL, D_MODEL), BF16),
                   jax.ShapeDtypeStruct((8, D_MODEL), F32)),
        scratch_shapes=[pltpu.VMEM((D_MODEL, D_MODEL), F32)],
        compiler_params=_cparams(("arbitrary",), VMEM_LIMIT),
    )(yc, ya, proj, proj, proj, proj, x, target, w_out, final_g)


def _conv_backward_pointwise(dyc, cv, proj, w_co, ln_g, ln_b):
    tokens = cv.shape[0]
    tm = 256
    last = tokens // tm - 1

    def body(dyc_ref, cv_ref, cg_ref, wco_ref, lng_ref, lnb_ref, dcv_ref, dcg_ref, gwco_ref, part_ref, gacc):
        i = pl.program_id(0)

        @pl.when(i == 0)
        def _():
            gacc[...] = jnp.zeros_like(gacc)
            part_ref[...] = jnp.zeros_like(part_ref)

        cv = cv_ref[...]
        mu = jnp.mean(cv, axis=-1, keepdims=True)
        zc = cv - mu
        rstd = lax.rsqrt(jnp.mean(zc * zc, axis=-1, keepdims=True) + LN_EPS)
        z = zc * rstd
        lng = lng_ref[...]
        ln = z * lng + lnb_ref[...]
        sl = _sig(ln)
        c = ln * sl
        cg = cg_ref[...].astype(F32)
        scg = _sig(cg)
        gate = cg * scg
        dyc = dyc_ref[...]
        gacc[...] += _dot_tn((c * gate).astype(BF16), dyc)
        dpc = _dot_nt(dyc, wco_ref[...])
        dcg_ref[...] = (dpc * c * (scg * (1.0 + cg * (1.0 - scg)))).astype(BF16)
        dln = dpc * gate * (sl * (1.0 + ln * (1.0 - sl)))
        part_ref[0:1, :] += jnp.sum(dln * z, axis=0, keepdims=True)
        part_ref[1:2, :] += jnp.sum(dln, axis=0, keepdims=True)
        dz = dln * lng
        dcv = rstd * (dz - jnp.mean(dz, axis=-1, keepdims=True) - z * jnp.mean(dz * z, axis=-1, keepdims=True))
        part_ref[2:3, :] += jnp.sum(dcv, axis=0, keepdims=True)
        dcv_ref[...] = dcv

        @pl.when(i == last)
        def _():
            gwco_ref[...] = gacc[...].astype(BF16)

    tile = lambda col=0: pl.BlockSpec((tm, D_MODEL), lambda i: (i, col))
    return pl.pallas_call(
        body, name="conv_backward_pointwise", grid=(tokens // tm,),
        in_specs=[tile(), tile(), tile(COL_CG), _const_spec((D_MODEL, D_MODEL)),
                  _const_spec((1, D_MODEL)), _const_spec((1, D_MODEL))],
        out_specs=(tile(), tile(), _const_spec((D_MODEL, D_MODEL)), _const_spec((8, D_MODEL))),
        out_shape=(jax.ShapeDtypeStruct((tokens, D_MODEL), F32),
                   jax.ShapeDtypeStruct((tokens, D_MODEL), BF16),
                   jax.ShapeDtypeStruct((D_MODEL, D_MODEL), BF16),
                   jax.ShapeDtypeStruct((8, D_MODEL), F32)),
        scratch_shapes=[pltpu.VMEM((D_MODEL, D_MODEL), F32)],
        compiler_params=_cparams(("arbitrary",), VMEM_LIMIT),
    )(dyc, cv, proj, w_co, ln_g, ln_b)


def _conv_backward_taps(dcv, proj, conv_w):
    tokens = dcv.shape[0]
    tm = CONV_TM
    nt = tokens // tm
    halo_blocks = tm // CONV_PAD

    def body(d_ref, dn_ref, a_ref, b_ref, ah_ref, bh_ref, cw_ref, da_ref, db_ref, gw_ref, u_ext, d_ext, du_scr, gw_acc):
        i = pl.program_id(0)

        @pl.when(i == 0)
        def _():
            gw_acc[...] = jnp.zeros_like(gw_acc)

        _fill_u_ext(u_ext, a_ref, b_ref, ah_ref, bh_ref, i == 0)
        for lc in range(N_LANE_CHUNKS):
            sl = slice(LANES * lc, LANES * (lc + 1))
            d_ext[lc, 0:tm, :] = d_ref[:, sl]
            d_ext[lc, tm:tm + CONV_PAD, :] = jnp.where(i == nt - 1, 0.0, dn_ref[:, sl])

        def lane_chunk(lc, carry):
            n_rc = tm // 64
            du = [jnp.zeros((64, LANES), F32) for _ in range(n_rc)]
            for j in range(CONV_KERNEL):
                w = cw_ref[lc, pl.ds(j, 1), :]
                gsum = jnp.zeros((8, LANES), F32)
                for rc in range(n_rc):
                    du[rc] = du[rc] + w * d_ext[lc, pl.ds(64 * rc + 30 - j, 64), :]
                    prod = d_ext[lc, pl.ds(64 * rc, 64), :] * u_ext[lc, pl.ds(64 * rc + 2 + j, 64), :]
                    gsum = gsum + jnp.sum(prod.reshape(8, 8, LANES), axis=0)
                gw_acc[lc, j] += gsum
            for rc in range(n_rc):
                du_scr[lc, pl.ds(64 * rc, 64), :] = du[rc]
            return carry
        lax.fori_loop(0, N_LANE_CHUNKS, lane_chunk, 0)

        du = jnp.concatenate([du_scr[lc] for lc in range(N_LANE_CHUNKS)], axis=1)
        a, b = a_ref[...].astype(F32), b_ref[...].astype(F32)
        sb = _sig(b)
        da_ref[...] = (du * sb).astype(BF16)
        db_ref[...] = (du * a * (sb * (1.0 - sb))).astype(BF16)

        @pl.when(i == nt - 1)
        def _():
            gw_ref[...] = jnp.sum(gw_acc[...], axis=2)

    def prev_halo(i):
        return jnp.maximum(i * halo_blocks - 1, 0)

    def next_halo(i):
        return jnp.minimum((i + 1) * halo_blocks, tokens // CONV_PAD - 1)

    tile = lambda col=0: pl.BlockSpec((tm, D_MODEL), lambda i: (i, col))
    return pl.pallas_call(
        body, name="conv_backward_taps", grid=(nt,),
        in_specs=[tile(), pl.BlockSpec((CONV_PAD, D_MODEL), lambda i: (next_halo(i), 0)),
                  tile(COL_A), tile(COL_B),
                  pl.BlockSpec((CONV_PAD, D_MODEL), lambda i: (prev_halo(i), COL_A)),
                  pl.BlockSpec((CONV_PAD, D_MODEL), lambda i: (prev_halo(i), COL_B)),
                  _const_spec((N_DEV, CONV_PAD, LANES))],
        out_specs=(tile(), tile(), _const_spec((N_DEV, CONV_PAD, LANES))),
        out_shape=(jax.ShapeDtypeStruct((tokens, D_MODEL), BF16),
                   jax.ShapeDtypeStruct((tokens, D_MODEL), BF16),
                   jax.ShapeDtypeStruct((N_DEV, CONV_PAD, LANES), F32)),
        scratch_shapes=[pltpu.VMEM((N_LANE_CHUNKS, CONV_PAD + tm, LANES), F32),
                        pltpu.VMEM((N_LANE_CHUNKS, tm + CONV_PAD, LANES), F32),
                        pltpu.VMEM((N_LANE_CHUNKS, tm, LANES), F32),
                        pltpu.VMEM((N_LANE_CHUNKS, CONV_PAD, 8, LANES), F32)],
        compiler_params=_cparams(("arbitrary",), VMEM_LIMIT),
    )(dcv, dcv, proj, proj, proj, proj, conv_w)


def _fold_kv_head(dup, lo, second_half):
    both = dup + pltpu.roll(dup, HEAD_DIM, 1)
    return jnp.where(lo, 0.0, both) if second_half else jnp.where(lo, both, 0.0)


def _attention_backward(dya, o, qr, kd, vd, proj, sinks, w_ao, cos_t, sin_up, sin_dn):
    tokens = qr.shape[0]
    nb = tokens // BLOCK
    scale = HEAD_DIM ** -0.5

    def body(dya_ref, o_ref, ag0_ref, ag1_ref, q_ref, kc_ref, kp_ref, vc_ref, vp_ref, sinks_ref, wao_ref,
             cos_c, up_c, dn_c, cos_p, up_p, dn_p,
             dq_ref, dkv_ref, dag_ref, gwao_ref, gsink_ref, gacc, dk_carry, dv_carry, dq_scr):
        n = pl.program_id(0)
        lo = _lane_halves()

        @pl.when(n == 0)
        def _():
            gacc[...] = jnp.zeros_like(gacc)
            gsink_ref[...] = jnp.zeros_like(gsink_ref)
            dk_carry[...] = jnp.zeros_like(dk_carry)
            dv_carry[...] = jnp.zeros_like(dv_carry)

        def emit_prev(dk_prev, dv_prev):
            for p in range(2):
                sl = slice(LANES * p, LANES * (p + 1))
                dk = _rope_transposed(dk_carry[:, sl] + dk_prev[p], cos_p[...], up_p[...], dn_p[...])
                dkv_ref[:, sl] = dk.astype(BF16)
                dkv_ref[:, slice(256 + LANES * p, 256 + LANES * (p + 1))] = (dv_carry[:, sl] + dv_prev[p]).astype(BF16)

        @pl.when(n < nb)
        def _():
            dya = dya_ref[...]
            dpa = _dot_nt(dya, wao_ref[...])
            o = o_ref[...].astype(F32)
            ag = jnp.concatenate([ag0_ref[...], ag1_ref[...]], axis=1).astype(F32)
            sg = _sig(ag)
            gate = ag * sg
            gacc[...] += _dot_tn((o * gate).astype(BF16), dya)
            dag_ref[...] = (dpa * o * (sg * (1.0 + ag * (1.0 - sg)))).astype(BF16)
            do = (dpa * gate).astype(BF16)

            cur_ok, prev_ok = _band_masks(n)
            head_lane = lax.broadcasted_iota(jnp.int32, (1, LANES), 1)
            gsink = jnp.zeros((1, LANES), F32)
            zero_tile = jnp.zeros((BLOCK, LANES), F32)
            dk_cur, dk_prev = [zero_tile, zero_tile], [zero_tile, zero_tile]
            dv_cur, dv_prev = [zero_tile, zero_tile], [zero_tile, zero_tile]
            for kvh in range(N_KV_HEADS):
                ta, tb = slice(LANES * 2 * kvh, LANES * (2 * kvh + 1)), slice(LANES * (2 * kvh + 1), LANES * (2 * kvh + 2))
                ks = slice(LANES * kvh, LANES * (kvh + 1))
                q_stack = _stack_heads(q_ref[:, ta], q_ref[:, tb], lo)
                do_stack = _stack_heads(do[:, ta], do[:, tb], lo)
                k_c, k_p, v_c, v_p = kc_ref[:, ks], kp_ref[:, ks], vc_ref[:, ks], vp_ref[:, ks]
                p_c, p_p, p_s = _softmax_band(q_stack, k_c, k_p, _sink_column(sinks_ref, kvh), cur_ok, prev_ok)
                dp_c, dp_p = _dot_nt(do_stack, v_c), _dot_nt(do_stack, v_p)
                delta = jnp.sum(p_c * dp_c, axis=1, keepdims=True) + jnp.sum(p_p * dp_p, axis=1, keepdims=True)
                ds_c = (p_c * (dp_c - delta)).astype(BF16)
                ds_p = (p_p * (dp_p - delta)).astype(BF16)
                sink_terms = p_s * delta
                for g in range(4):
                    total = jnp.sum(sink_terms[BLOCK * g:BLOCK * (g + 1)], axis=0, keepdims=True)
                    gsink = gsink - jnp.where(head_lane == 4 * kvh + g, total, 0.0)
                dq_stack = _dot(ds_c, k_c) + _dot(ds_p, k_p)
                dq_scr[:, ta], dq_scr[:, tb] = _unstack_heads(dq_stack, lo)
                tile, second = kvh // 2, kvh % 2 == 1
                dk_cur[tile] = dk_cur[tile] + _fold_kv_head(_dot_tn(ds_c, q_stack), lo, second)
                dk_prev[tile] = dk_prev[tile] + _fold_kv_head(_dot_tn(ds_p, q_stack), lo, second)
                dv_cur[tile] = dv_cur[tile] + _fold_kv_head(_dot_tn(p_c.astype(BF16), do_stack), lo, second)
                dv_prev[tile] = dv_prev[tile] + _fold_kv_head(_dot_tn(p_p.astype(BF16), do_stack), lo, second)
            gsink_ref[0:1, :] += gsink
            for p in range(D_MODEL // LANES):
                sl = slice(LANES * p, LANES * (p + 1))
                dq_ref[:, sl] = (_rope_transposed(dq_scr[:, sl], cos_c[...], up_c[...], dn_c[...]) * scale).astype(BF16)
            emit_prev(dk_prev, dv_prev)
            for p in range(2):
                sl = slice(LANES * p, LANES * (p + 1))
                dk_carry[:, sl] = dk_cur[p]
                dv_carry[:, sl] = dv_cur[p]

        @pl.when(n == nb)
        def _():
            zero_tile = jnp.zeros((BLOCK, LANES), F32)
            emit_prev([zero_tile, zero_tile], [zero_tile, zero_tile])
            gwao_ref[...] = gacc[...].astype(BF16)

    def cur_idx(n):
        return jnp.minimum(n, nb - 1)

    def prev_idx(n):
        return jnp.clip(n - 1, 0, nb - 1)

    cur = lambda w, col=0: pl.BlockSpec((BLOCK, w), lambda n: (cur_idx(n), col))
    prev = lambda w: pl.BlockSpec((BLOCK, w), lambda n: (prev_idx(n), 0))
    return pl.pallas_call(
        body, name="attention_backward", grid=(nb + 1,),
        in_specs=[cur(D_MODEL), cur(D_MODEL), cur(512, COL512_AG), cur(512, COL512_AG + 1), cur(D_MODEL),
                  cur(512), prev(512), cur(512), prev(512),
                  pl.BlockSpec(memory_space=pltpu.SMEM), _const_spec((D_MODEL, D_MODEL)),
                  cur(LANES), cur(LANES), cur(LANES), prev(LANES), prev(LANES), prev(LANES)],
        out_specs=(cur(D_MODEL), prev(512), cur(D_MODEL),
                   _const_spec((D_MODEL, D_MODEL)), _const_spec((8, LANES))),
        out_shape=(jax.ShapeDtypeStruct((tokens, D_MODEL), BF16),
                   jax.ShapeDtypeStruct((tokens, 512), BF16),
                   jax.ShapeDtypeStruct((tokens, D_MODEL), BF16),
                   jax.ShapeDtypeStruct((D_MODEL, D_MODEL), BF16),
                   jax.ShapeDtypeStruct((8, LANES), F32)),
        scratch_shapes=[pltpu.VMEM((D_MODEL, D_MODEL), F32),
                        pltpu.VMEM((BLOCK, 256), F32), pltpu.VMEM((BLOCK, 256), F32),
                        pltpu.VMEM((BLOCK, D_MODEL), F32)],
        compiler_params=_cparams(("arbitrary",), VMEM_LIMIT),
    )(dya, o, proj, proj, qr, kd, kd, vd, vd, sinks, w_ao, cos_t, sin_up, sin_dn, cos_t, sin_up, sin_dn)


_SECTION_ROWS = (OFF_A, OFF_B, OFF_CG, OFF_Q, OFF_KV, OFF_AG, OFF_MLC, OFF_MLA)
_SECTION_WIDTH = (1024, 1024, 1024, 1024, 512, 1024, 1024, 1024)


def _input_backward(sections, w_in_t, x, dx2, norm_g):
    tokens = x.shape[0]
    tm = 256

    def body(*refs):
        sec = refs[:8]
        w_ref, x_ref, dx2_ref, g_ref, gx_ref, part_ref = refs[8:]

        @pl.when(pl.program_id(0) == 0)
        def _():
            part_ref[...] = jnp.zeros_like(part_ref)

        dh = jnp.zeros((tm, D_MODEL), F32)
        for s in range(8):
            dh = dh + _dot(sec[s][...], w_ref[_SECTION_ROWS[s]:_SECTION_ROWS[s] + _SECTION_WIDTH[s], :])
        xv = x_ref[...]
        r = lax.rsqrt(jnp.mean(xv * xv, axis=-1, keepdims=True) + RMS_EPS)
        xn = xv * r
        part_ref[0:1, :] += jnp.sum(dh * xn, axis=0, keepdims=True)
        dxn = dh * g_ref[...]
        gx_ref[...] = dx2_ref[...] + r * (dxn - xn * jnp.mean(dxn * xn, axis=-1, keepdims=True))

    tile = lambda w=D_MODEL: pl.BlockSpec((tm, w), lambda i: (i, 0))
    return pl.pallas_call(
        body, name="input_backward", grid=(tokens // tm,),
        in_specs=[tile(w) for w in _SECTION_WIDTH] + [
            pl.BlockSpec((IN_WIDTH, D_MODEL), lambda i: (0, 0), pipeline_mode=pl.Buffered(1)),
            tile(), tile(), _const_spec((1, D_MODEL))],
        out_specs=(tile(), _const_spec((8, D_MODEL))),
        out_shape=(jax.ShapeDtypeStruct((tokens, D_MODEL), F32),
                   jax.ShapeDtypeStruct((8, D_MODEL), F32)),
        compiler_params=_cparams(("arbitrary",), VMEM_LIMIT),
    )(*sections, w_in_t, x, dx2, norm_g)


def _adamw_math(w, g, m, v):
    m = ADAM_B1 * m + (1.0 - ADAM_B1) * g
    v = ADAM_B2 * v + (1.0 - ADAM_B2) * (g * g)
    m_hat = m / (1.0 - ADAM_B1 ** ADAM_STEP)
    v_hat = v / (1.0 - ADAM_B2 ** ADAM_STEP)
    delta = -ADAM_LR * (m_hat / (jnp.sqrt(v_hat) + ADAM_EPS) + ADAM_WD * w)
    return delta, m, v


def _sum_slots(recv_ref):
    total = recv_ref[0].astype(F32)
    for d in range(1, N_DEV):
        total = total + recv_ref[d].astype(F32)
    return total


def _adamw(name, w, g, m, v, tile_rows):
    rows, cols = w.shape

    def body(w_ref, g_ref, m_ref, v_ref, d_ref, nm_ref, nv_ref):
        d_ref[...], nm_ref[...], nv_ref[...] = _adamw_math(w_ref[...], g_ref[...], m_ref[...], v_ref[...])

    spec = pl.BlockSpec((tile_rows, cols), lambda i: (i, 0))
    shape = jax.ShapeDtypeStruct((rows, cols), F32)
    return pl.pallas_call(
        body, name=name, grid=(rows // tile_rows,),
        in_specs=[spec] * 4, out_specs=(spec,) * 3, out_shape=(shape,) * 3,
        compiler_params=_cparams(("parallel",)),
    )(w, g, m, v)


def _sum_adamw(name, recv, w, m, v):
    def body(recv_ref, w_ref, m_ref, v_ref, g_ref, d_ref, nm_ref, nv_ref):
        g = _sum_slots(recv_ref)
        g_ref[...] = g
        d_ref[...], nm_ref[...], nv_ref[...] = _adamw_math(w_ref[...], g, m_ref[...], v_ref[...])

    shape = jax.ShapeDtypeStruct(w.shape, F32)
    return pl.pallas_call(body, name=name, out_shape=(shape,) * 4)(recv, w, m, v)


def _pad_rows(a, rows):
    return jnp.concatenate([a, jnp.zeros((rows - a.shape[0],) + a.shape[1:], a.dtype)], axis=0)


def kernel(x, norm_g, w_in, conv_dw_w, conv_dw_b, conv_ln_g, conv_ln_b, w_conv_out, attn_sinks, w_attn_out, w_out, final_norm_g, loss_target, m_norm_g, m_w_in, m_conv_dw_w, m_conv_dw_b, m_conv_ln_g, m_conv_ln_b, m_w_conv_out, m_attn_sinks, m_w_attn_out, m_w_out, m_final_norm_g, v_norm_g, v_w_in, v_conv_dw_w, v_conv_dw_b, v_conv_ln_g, v_conv_ln_b, v_w_conv_out, v_attn_sinks, v_w_attn_out, v_w_out, v_final_norm_g):
    xs, target = x[0], loss_target[0]
    tokens = xs.shape[0]
    fg_row = final_norm_g.reshape(1, D_MODEL)

    taps_bits = lax.bitcast_convert_type(_pad_rows(conv_dw_w[0], CONV_PAD), BF16).reshape(8, D_MODEL)
    pack = jnp.concatenate([w_conv_out[0].astype(BF16), w_attn_out[0].astype(BF16), w_out[0].astype(BF16),
                            jnp.pad(taps_bits, ((0, PACK_ROWS - 3 * SHARD_SQ - 8), (0, 0)))], axis=0)
    proj, h, w_in_t, pack_full = _gather_project(xs, norm_g, w_in[0].T.astype(BF16), pack)
    w_co = pack_full[:, 0:SHARD_SQ].reshape(D_MODEL, D_MODEL)
    w_ao = pack_full[:, SHARD_SQ:2 * SHARD_SQ].reshape(D_MODEL, D_MODEL)
    w_o = pack_full[:, 2 * SHARD_SQ:3 * SHARD_SQ].reshape(D_MODEL, D_MODEL)
    conv_w = lax.bitcast_convert_type(
        pack_full[:, 3 * SHARD_SQ:3 * SHARD_SQ + 8].reshape(N_DEV, CONV_PAD, LANES, 2), F32)

    cos_t, sin_up, sin_dn = _rope_tables(tokens)
    qr, kd, vd = _rope_qkv(proj, cos_t, sin_up, sin_dn)
    cv, yc = _conv_forward(proj, conv_w, conv_dw_b, conv_ln_g, conv_ln_b, w_co)
    o, ya = _attention_forward(qr, kd, vd, proj, attn_sinks, w_ao)

    dx2, dyc, dya, dmlc, dmla, g_out, part_head = _merge_and_head(yc, ya, proj, xs, target, w_o, fg_row)
    dcv, dcg, g_co, part_conv = _conv_backward_pointwise(dyc, cv, proj, w_co, conv_ln_g, conv_ln_b)
    da, db, g_conv = _conv_backward_taps(dcv, proj, conv_w)
    dq, dkv, dag, g_ao, part_sink = _attention_backward(dya, o, qr, kd, vd, proj, attn_sinks, w_ao, cos_t, sin_up, sin_dn)
    sections = (da, db, dcg, dq, dkv, dag, dmlc, dmla)
    grad_x, part_in = _input_backward(sections, w_in_t, xs, dx2, norm_g)

    small = jnp.concatenate([
        part_in[0:1], part_conv[2:3], part_conv[0:1], part_conv[1:2], part_head[0:1],
        jnp.pad(part_sink[0:1], ((0, 0), (0, D_MODEL - LANES))), part_head[1:2],
        jnp.zeros((1, D_MODEL), F32)], axis=0)

    g_mine, r_conv, r_small = _grad_exchange(sections, h, g_co, g_ao, g_out, g_conv, small)

    grad_w_in = g_mine[:SHARD_IN].T
    d_w_in, nm_w_in, nv_w_in = _adamw("adamw_w_in", w_in[0], grad_w_in, m_w_in[0], v_w_in[0], 256)
    sq = {}
    for j, (nm, w, m, v) in enumerate((("w_conv_out", w_conv_out, m_w_conv_out, v_w_conv_out),
                                       ("w_attn_out", w_attn_out, m_w_attn_out, v_w_attn_out),
                                       ("w_out", w_out, m_w_out, v_w_out))):
        g = g_mine[SHARD_IN + j * SHARD_SQ:SHARD_IN + (j + 1) * SHARD_SQ]
        sq[nm] = (g,) + tuple(_adamw("adamw_" + nm, w[0], g, m[0], v[0], SHARD_SQ))
    conv_res = _sum_adamw("sum_adamw_conv_dw_w", r_conv.reshape(N_DEV, CONV_PAD, LANES),
                          _pad_rows(conv_dw_w[0], CONV_PAD), _pad_rows(m_conv_dw_w[0], CONV_PAD),
                          _pad_rows(v_conv_dw_w[0], CONV_PAD))
    pad_sink = lambda a: jnp.pad(a, ((0, 0), (0, D_MODEL - N_Q_HEADS)))
    zero_rows = jnp.zeros((2, D_MODEL), F32)
    stack = lambda a, b, c, d, e, f: jnp.concatenate([a, b, c, d, e.reshape(1, D_MODEL), pad_sink(f), zero_rows], axis=0)
    small_res = _sum_adamw(
        "sum_adamw_small", r_small,
        stack(norm_g, conv_dw_b, conv_ln_g, conv_ln_b, final_norm_g, attn_sinks),
        stack(m_norm_g, m_conv_dw_b, m_conv_ln_g, m_conv_ln_b, m_final_norm_g, m_attn_sinks),
        stack(v_norm_g, v_conv_dw_b, v_conv_ln_g, v_conv_ln_b, v_final_norm_g, v_attn_sinks))
    loss = jnp.sum(small_res[0][6])

    def leaf(k):
        s = small_res[k]
        return (s[0:1], (grad_w_in, d_w_in, nm_w_in, nv_w_in)[k][None], conv_res[k][None, :CONV_KERNEL],
                s[1:2], s[2:3], s[3:4], sq["w_conv_out"][k][None], s[5:6, :N_Q_HEADS],
                sq["w_attn_out"][k][None], sq["w_out"][k][None], s[4])

    return (loss, grad_x[None], *leaf(0), *leaf(1), *leaf(2), *leaf(3))
```

```python
import jax
import jax.numpy as jnp
from jax import lax
from jax.experimental import pallas as pl
from jax.experimental.pallas import tpu as pltpu

F32 = jnp.float32
BF16 = jnp.bfloat16
MESH = pl.DeviceIdType.MESH

D_MODEL = 1024
IN_WIDTH = 7680
N_DEV = 8
SHARD_IN = IN_WIDTH // N_DEV
SHARD_SQ = D_MODEL // N_DEV
CONV_KERNEL = 31
CONV_PAD = 32
HEAD_DIM = 64
N_Q_HEADS = 16
N_KV_HEADS = 4
BLOCK = 128
LANES = 128
ROPE_THETA = 10000.0
RMS_EPS = 1e-5
LN_EPS = 1e-5
NEG = -1e30
ADAM_LR = 0.001
ADAM_B1 = 0.9
ADAM_B2 = 0.999
ADAM_EPS = 1e-08
ADAM_WD = 0.01
ADAM_STEP = 10

OFF_A, OFF_B, OFF_CG, OFF_Q, OFF_KV, OFF_AG, OFF_MLC, OFF_MLA = 0, 1024, 2048, 3072, 4096, 4608, 5632, 6656
COL_A, COL_B, COL_CG, COL_Q = 0, 1, 2, 3
COL512_KV, COL512_AG, COL512_MLC, COL512_MLA = 8, 9, 11, 13
UNIT = 2 * SHARD_IN
PACK_ROWS = 400

VMEM_LIMIT = 56 * 1024 * 1024


def _cparams(sem=None, vmem=None):
    return pltpu.CompilerParams(dimension_semantics=sem, vmem_limit_bytes=vmem)


def _sig(v):
    return 1.0 / (1.0 + jnp.exp(-v))


def _dot(a, b):
    return jnp.dot(a, b, preferred_element_type=F32)


def _dot_nt(a, b):
    return lax.dot_general(a, b, (((1,), (1,)), ((), ())), preferred_element_type=F32)


def _dot_tn(a, b):
    return lax.dot_general(a, b, (((0,), (0,)), ((), ())), preferred_element_type=F32)


def _const_spec(shape):
    nd = len(shape)
    return pl.BlockSpec(shape, lambda *_: (0,) * nd)


def _mesh_pos():
    x, y, c = lax.axis_index("x"), lax.axis_index("y"), lax.axis_index("c")
    return x, y, c, 4 * x + 2 * y + c


def _peer(x, y, c, k):
    px = 1 - x if (k >> 2) & 1 else x
    py = 1 - y if (k >> 1) & 1 else y
    pc = 1 - c if k & 1 else c
    return (px, py, pc), 4 * px + 2 * py + pc


def _gather_project(x, norm_g, w_shard_t, pack):
    tokens = x.shape[0]
    tt = min(512, tokens // 2)
    n_tok = tokens // tt
    rc = min(128, tt)

    def body(x_hbm, g_ref, ws_hbm, pack_hbm, proj_hbm, h_hbm, wfull_hbm, packfull_hbm,
             w_vmem, h_vmem, x_buf, o_buf, send_sems, recv_sems, local_sems, x_sems, o_sems):
        x_, y_, c_, me = _mesh_pos()
        myself, sibling = (x_, y_, c_), (x_, y_, 1 - c_)
        chips = ((1 - x_, y_), (x_, 1 - y_), (1 - x_, 1 - y_))

        def shard(ref, idx):
            return ref.at[pl.ds(pl.multiple_of(idx * SHARD_IN, 64), SHARD_IN)]

        def copy(a, k, idx, to, own=False):
            if a == 0:
                src, dst = ws_hbm if own else shard(w_vmem, idx), shard(w_vmem, idx)
            else:
                src, dst = pack_hbm if own else packfull_hbm.at[idx], packfull_hbm.at[idx]
            return pltpu.make_async_remote_copy(src_ref=src, dst_ref=dst, send_sem=send_sems.at[a, k],
                                                recv_sem=recv_sems.at[a, k], device_id=to, device_id_type=MESH)

        own_w = pltpu.make_async_copy(ws_hbm, shard(w_vmem, me), local_sems.at[0])
        own_p = pltpu.make_async_copy(pack_hbm, packfull_hbm.at[me], local_sems.at[1])
        own_w.start()
        own_p.start()
        sent = []
        for a in range(2):
            sent.append(copy(a, 0, me, sibling, own=True))
            sent += [copy(a, 1 + r, me, (*chip, c_), own=True) for r, chip in enumerate(chips)]
        for cp in sent:
            cp.start()

        def x_copy(t, slot):
            return pltpu.make_async_copy(x_hbm.at[pl.ds(t * tt, tt)], x_buf.at[slot], x_sems.at[slot])

        x_copy(0, 0).start()
        for t in range(n_tok):
            slot = t % 2
            if t + 1 < n_tok:
                x_copy(t + 1, 1 - slot).start()
            x_copy(t, slot).wait()

            def chunk(r0, t=t, slot=slot):
                xv = x_buf[slot, pl.ds(r0, rc), :]
                r = lax.rsqrt(jnp.mean(xv * xv, axis=-1, keepdims=True) + RMS_EPS)
                h_vmem[pl.ds(t * tt + r0, rc), :] = (xv * r * g_ref[...]).astype(BF16)
            _row_chunks(tt, rc, chunk)
        h_out = pltpu.make_async_copy(h_vmem, h_hbm, local_sems.at[6])
        h_out.start()
        local = [own_p, h_out]

        def project_unit(q, u):
            rows = pl.ds(pl.multiple_of(q * UNIT, LANES), UNIT)
            w_out = pltpu.make_async_copy(w_vmem.at[rows], wfull_hbm.at[rows], local_sems.at[2 + u])
            w_out.start()
            local.append(w_out)

            def o_copy(slot, t):
                return pltpu.make_async_copy(
                    o_buf.at[slot], proj_hbm.at[pl.ds(pl.multiple_of(t * tt, tt), tt), rows], o_sems.at[slot])

            def tile(t, carry):
                slot = lax.rem(t, 2)

                @pl.when(t >= 2)
                def _():
                    o_copy(slot, t).wait()
                o_buf[slot] = _dot_nt(h_vmem[pl.ds(pl.multiple_of(t * tt, tt), tt), :], w_vmem[rows, :]).astype(BF16)
                o_copy(slot, t).start()
                return carry
            lax.fori_loop(0, n_tok, tile, 0)
            o_copy(0, 0).wait()
            o_copy(1, 0).wait()

        def dev(chip, core):
            return 4 * chip[0] + 2 * chip[1] + core

        def arrive_and_pass_on(a, r):
            copy(a, 1 + r, dev(chips[r], c_), myself).wait_recv()
            passed = copy(a, 4 + r, dev(chips[r], c_), sibling)
            passed.start()
            sent.append(passed)

        def passed_on_to_me(a, r):
            copy(a, 4 + r, dev(chips[r], 1 - c_), myself).wait_recv()

        own_w.wait()
        copy(0, 0, dev((x_, y_), 1 - c_), myself).wait_recv()
        project_unit(2 * x_ + y_, 0)
        arrive_and_pass_on(0, 0)
        arrive_and_pass_on(0, 1)
        passed_on_to_me(0, 0)
        project_unit(2 * chips[0][0] + chips[0][1], 1)
        arrive_and_pass_on(0, 2)
        passed_on_to_me(0, 1)
        project_unit(2 * chips[1][0] + chips[1][1], 2)
        passed_on_to_me(0, 2)
        project_unit(2 * chips[2][0] + chips[2][1], 3)
        for r in range(3):
            arrive_and_pass_on(1, r)
        copy(1, 0, dev((x_, y_), 1 - c_), myself).wait_recv()
        for r in range(3):
            passed_on_to_me(1, r)
        for cp in sent:
            cp.wait_send()
        for cp in local:
            cp.wait()

    hbm = pl.BlockSpec(memory_space=pltpu.HBM)
    return pl.pallas_call(
        body, name="gather_project",
        in_specs=[hbm, pl.BlockSpec(memory_space=pltpu.VMEM), hbm, hbm],
        out_specs=(hbm, hbm, hbm, hbm),
        out_shape=(jax.ShapeDtypeStruct((tokens, IN_WIDTH), BF16),
                   jax.ShapeDtypeStruct((tokens, D_MODEL), BF16),
                   jax.ShapeDtypeStruct((IN_WIDTH, D_MODEL), BF16),
                   jax.ShapeDtypeStruct((N_DEV, PACK_ROWS, D_MODEL), BF16)),
        scratch_shapes=[pltpu.VMEM((IN_WIDTH, D_MODEL), BF16),
                        pltpu.VMEM((tokens, D_MODEL), BF16),
                        pltpu.VMEM((2, tt, D_MODEL), F32),
                        pltpu.VMEM((2, tt, UNIT), BF16),
                        pltpu.SemaphoreType.DMA((2, N_DEV - 1)),
                        pltpu.SemaphoreType.DMA((2, N_DEV - 1)),
                        pltpu.SemaphoreType.DMA((7,)),
                        pltpu.SemaphoreType.DMA((2,)),
                        pltpu.SemaphoreType.DMA((2,))],
        compiler_params=_cparams(None, VMEM_LIMIT),
    )(x, norm_g, w_shard_t, pack)


HALF_ROWS = SHARD_IN + 3 * SHARD_SQ

GRAD_CHUNK = 384
_SECTION_ROWS = (OFF_A, OFF_B, OFF_CG, OFF_Q, OFF_KV, OFF_AG, OFF_MLC, OFF_MLA)
_SECTION_WIDTH = (1024, 1024, 1024, 1024, 512, 1024, 1024, 1024)


def _dproj_pieces(first, width):
    out = []
    for s, (start, w) in enumerate(zip(_SECTION_ROWS, _SECTION_WIDTH)):
        lo, hi = max(first, start), min(first + width, start + w)
        if lo < hi:
            out.append((s, lo - start, hi - lo, lo - first))
    return out


def _grad_exchange(sections, h_t, g_co, g_ao, g_out, g_conv, small):
    tokens = h_t.shape[1]
    n_chunk = UNIT // GRAD_CHUNK
    rc = 192

    def body(*refs):
        sec = refs[:8]
        (ht_hbm, gco_hbm, gao_hbm, gout_hbm, gconv_hbm, small_hbm, gmine_hbm, rconv_hbm, rsmall_hbm,
         lhs_buf, ht_vmem, halves, out_buf, stage, final,
         lhs_sems, ht_sem, tail_sems, d2d_send, d2d_recv, ici_send, ici_recv,
         tiny_send, tiny_recv, local_sems) = refs[8:]
        x_, y_, c_, me = _mesh_pos()
        myself, sibling = (x_, y_, c_), (x_, y_, 1 - c_)
        chips = ((1 - x_, y_), (x_, 1 - y_), (1 - x_, 1 - y_), (x_, y_))
        squares = (gco_hbm, gao_hbm, gout_hbm)

        def remote(src, dst, send_sem, recv_sem, to):
            return pltpu.make_async_remote_copy(src_ref=src, dst_ref=dst, send_sem=send_sem, recv_sem=recv_sem,
                                                device_id=to, device_id_type=MESH)

        own_tiny = [pltpu.make_async_copy(gconv_hbm.at[me], rconv_hbm.at[me], local_sems.at[0]),
                    pltpu.make_async_copy(small_hbm, rsmall_hbm.at[me], local_sems.at[1])]
        for cp in own_tiny:
            cp.start()
        tiny = []
        for k in range(1, N_DEV):
            peer, peer_idx = _peer(x_, y_, c_, k)
            tiny += [remote(gconv_hbm.at[peer_idx], rconv_hbm.at[me], tiny_send.at[0, k - 1], tiny_recv.at[0, k - 1], peer),
                     remote(small_hbm, rsmall_hbm.at[me], tiny_send.at[1, k - 1], tiny_recv.at[1, k - 1], peer)]
        for cp in tiny:
            cp.start()

        ht_in = pltpu.make_async_copy(ht_hbm, ht_vmem, ht_sem.at[0])
        ht_in.start()

        def fetch(q, j, slot, wait):
            for k in range(4):
                @pl.when(q == k)
                def _(k=k):
                    for n, (s, col, width, place) in enumerate(_dproj_pieces(k * UNIT + j * GRAD_CHUNK, GRAD_CHUNK)):
                        cp = pltpu.make_async_copy(sec[s].at[pl.ds(0, tokens), pl.ds(col, width)],
                                                   lhs_buf.at[slot, pl.ds(0, tokens), pl.ds(place, width)],
                                                   lhs_sems.at[slot, n])
                        cp.wait() if wait else cp.start()

        def d2d(u):
            return remote(halves.at[1 - c_], stage.at[u], d2d_send.at[u], d2d_recv.at[u], sibling)

        def ici(u):
            return remote(stage.at[u], final.at[u], ici_send.at[u], ici_recv.at[u], (*chips[u], c_))

        def tails(q):
            out = []
            for core in range(2):
                for n, g in enumerate(squares):
                    rows = pl.ds(pl.multiple_of((2 * q + core) * SHARD_SQ, SHARD_SQ), SHARD_SQ)
                    out.append(pltpu.make_async_copy(g.at[rows], halves.at[core, pl.ds(SHARD_IN + n * SHARD_SQ, SHARD_SQ)],
                                                     tail_sems.at[3 * core + n]))
            return out

        def chip_sum(u):
            d2d(u).wait_recv()

            def chunk(r0):
                rows = pl.ds(r0, rc)
                stage[u, rows, :] = (stage[u, rows, :].astype(F32) + halves[c_, rows, :].astype(F32)).astype(BF16)
            _row_chunks(HALF_ROWS, rc, chunk)
            if u < 3:
                ici(u).start()

        def chip_of(u):
            return 2 * chips[u][0] + chips[u][1]

        def store_rows(block, first):
            n = block.shape[0]
            for core in range(2):
                lo, hi = max(first, core * SHARD_IN), min(first + n, (core + 1) * SHARD_IN)
                if lo < hi:
                    halves[core, lo - core * SHARD_IN:hi - core * SHARD_IN, :] = block[lo - first:hi - first].astype(BF16)

        fetch(chip_of(0), 0, 0, wait=False)
        ht_in.wait()
        for u in range(4):
            q = chip_of(u)
            for j in range(n_chunk):
                slot = (u * n_chunk + j) % 2
                if j + 1 < n_chunk:
                    fetch(q, j + 1, 1 - slot, wait=False)
                elif u + 1 < 4:
                    fetch(chip_of(u + 1), 0, 1 - slot, wait=False)
                fetch(q, j, slot, wait=True)
                grad_t = _dot(ht_vmem[...], lhs_buf[slot])
                if j == 0:
                    if u > 0:
                        chip_sum(u - 1)
                        d2d(u - 1).wait_send()
                    for cp in tails(q):
                        cp.start()
                for r in range(GRAD_CHUNK // LANES):
                    store_rows(grad_t[:, LANES * r:LANES * (r + 1)].T, j * GRAD_CHUNK + LANES * r)
            for cp in tails(q):
                cp.wait()
            d2d(u).start()

        chip_sum(3)
        for u in range(3):
            remote(stage.at[u], final.at[u], ici_send.at[u], ici_recv.at[u], myself).wait_recv()

        def total(r0):
            rows = pl.ds(r0, rc)
            out_buf[rows, :] = ((stage[3, rows, :].astype(F32) + final[0, rows, :].astype(F32))
                                + final[1, rows, :].astype(F32)) + final[2, rows, :].astype(F32)
        _row_chunks(HALF_ROWS, rc, total)
        out = pltpu.make_async_copy(out_buf, gmine_hbm, local_sems.at[2])
        out.start()
        d2d(3).wait_send()
        for u in range(3):
            ici(u).wait_send()
        for k in range(1, N_DEV):
            peer, peer_idx = _peer(x_, y_, c_, k)
            remote(gconv_hbm.at[me], rconv_hbm.at[peer_idx], tiny_send.at[0, k - 1], tiny_recv.at[0, k - 1], myself).wait_recv()
            remote(small_hbm, rsmall_hbm.at[peer_idx], tiny_send.at[1, k - 1], tiny_recv.at[1, k - 1], myself).wait_recv()
        for cp in tiny:
            cp.wait_send()
        for cp in own_tiny:
            cp.wait()
        out.wait()

    hbm = pl.BlockSpec(memory_space=pltpu.HBM)
    return pl.pallas_call(
        body, name="grad_exchange",
        in_specs=[hbm] * 14, out_specs=(hbm, hbm, hbm),
        out_shape=(jax.ShapeDtypeStruct((HALF_ROWS, D_MODEL), F32),
                   jax.ShapeDtypeStruct((N_DEV, CONV_PAD, LANES), F32),
                   jax.ShapeDtypeStruct((N_DEV, 8, D_MODEL), F32)),
        scratch_shapes=[pltpu.VMEM((2, tokens, GRAD_CHUNK), BF16),
                        pltpu.VMEM((D_MODEL, tokens), BF16),
                        pltpu.VMEM((2, HALF_ROWS, D_MODEL), BF16),
                        pltpu.VMEM((HALF_ROWS, D_MODEL), F32),
                        pltpu.VMEM((4, HALF_ROWS, D_MODEL), BF16),
                        pltpu.VMEM((3, HALF_ROWS, D_MODEL), BF16),
                        pltpu.SemaphoreType.DMA((2, 3)),
                        pltpu.SemaphoreType.DMA((1,)),
                        pltpu.SemaphoreType.DMA((6,)),
                        pltpu.SemaphoreType.DMA((4,)),
                        pltpu.SemaphoreType.DMA((4,)),
                        pltpu.SemaphoreType.DMA((3,)),
                        pltpu.SemaphoreType.DMA((3,)),
                        pltpu.SemaphoreType.DMA((2, N_DEV - 1)),
                        pltpu.SemaphoreType.DMA((2, N_DEV - 1)),
                        pltpu.SemaphoreType.DMA((3,))],
        compiler_params=_cparams(None, 60 * 1024 * 1024),
    )(*sections, h_t, g_co, g_ao, g_out, g_conv, small)


def _row_chunks(total, size, fn):
    n = total // size
    if n == 1:
        fn(0)
        return

    def step(i, carry):
        fn(pl.multiple_of(i * size, size))
        return carry
    lax.fori_loop(0, n, step, 0)


def _rope_tables(tokens):
    inv_freq = ROPE_THETA ** (-jnp.arange(0, HEAD_DIM, 2, dtype=F32) / HEAD_DIM)
    ang = jnp.arange(tokens, dtype=jnp.int32).astype(F32)[:, None] * inv_freq[None, :]
    cos, sin = jnp.cos(ang), jnp.sin(ang)
    zero = jnp.zeros_like(sin)
    cos_t = jnp.tile(jnp.concatenate([cos, cos], axis=1), (1, LANES // HEAD_DIM))
    sin_up = jnp.tile(jnp.concatenate([-sin, zero], axis=1), (1, LANES // HEAD_DIM))
    sin_dn = jnp.tile(jnp.concatenate([zero, sin], axis=1), (1, LANES // HEAD_DIM))
    return cos_t, sin_up, sin_dn


def _rope(t, cos_t, sin_up, sin_dn):
    return t * cos_t + pltpu.roll(t, LANES - 32, 1) * sin_up + pltpu.roll(t, 32, 1) * sin_dn


def _rope_transposed(g, cos_t, sin_up, sin_dn):
    return g * cos_t + pltpu.roll(g * sin_up, 32, 1) + pltpu.roll(g * sin_dn, LANES - 32, 1)


def _lane_halves():
    lane = lax.broadcasted_iota(jnp.int32, (BLOCK, LANES), 1)
    return lane < HEAD_DIM


def _rope_qkv(proj, cos_t, sin_up, sin_dn):
    tokens = proj.shape[0]
    tm = min(512, tokens)
    scale = HEAD_DIM ** -0.5

    def body(q_ref, kv_ref, cos_ref, up_ref, dn_ref, qr_ref, kd_ref, vd_ref):
        lo = _lane_halves()

        def chunk(r0):
            rows = pl.ds(r0, BLOCK)
            cs, up, dn = cos_ref[rows, :], up_ref[rows, :], dn_ref[rows, :]
            for p in range(D_MODEL // LANES):
                sl = slice(LANES * p, LANES * (p + 1))
                qt = q_ref[rows, sl].astype(F32)
                qr_ref[rows, sl] = (_rope(qt, cs, up, dn) * scale).astype(BF16)
            for p in range(2):
                sl = slice(LANES * p, LANES * (p + 1))
                kt = _rope(kv_ref[rows, sl].astype(F32), cs, up, dn)
                vt = kv_ref[rows, slice(256 + LANES * p, 256 + LANES * (p + 1))].astype(F32)
                for src, dst in ((kt, kd_ref), (vt, vd_ref)):
                    first = jnp.where(lo, src, 0.0)
                    second = src - first
                    dst[rows, slice(LANES * 2 * p, LANES * (2 * p + 1))] = (first + pltpu.roll(first, HEAD_DIM, 1)).astype(BF16)
                    dst[rows, slice(LANES * (2 * p + 1), LANES * (2 * p + 2))] = (second + pltpu.roll(second, HEAD_DIM, 1)).astype(BF16)
        _row_chunks(tm, BLOCK, chunk)

    tab = pl.BlockSpec((tm, LANES), lambda i: (i, 0))
    return pl.pallas_call(
        body, name="rope_qkv", grid=(tokens // tm,),
        in_specs=[pl.BlockSpec((tm, D_MODEL), lambda i: (i, COL_Q)),
                  pl.BlockSpec((tm, 512), lambda i: (i, COL512_KV)), tab, tab, tab],
        out_specs=(pl.BlockSpec((tm, D_MODEL), lambda i: (i, 0)),
                   pl.BlockSpec((tm, 512), lambda i: (i, 0)),
                   pl.BlockSpec((tm, 512), lambda i: (i, 0))),
        out_shape=(jax.ShapeDtypeStruct((tokens, D_MODEL), BF16),
                   jax.ShapeDtypeStruct((tokens, 512), BF16),
                   jax.ShapeDtypeStruct((tokens, 512), BF16)),
        compiler_params=_cparams(("parallel",)),
    )(proj, proj, cos_t, sin_up, sin_dn)


CONV_TM = 256
N_LANE_CHUNKS = D_MODEL // LANES


def _fill_u_ext(u_ext, a_ref, b_ref, ah_ref, bh_ref, first_tile):
    for lc in range(N_LANE_CHUNKS):
        sl = slice(LANES * lc, LANES * (lc + 1))
        uh = ah_ref[:, sl].astype(F32) * _sig(bh_ref[:, sl].astype(F32))
        u_ext[lc, 0:CONV_PAD, :] = jnp.where(first_tile, 0.0, uh)
        u_ext[lc, CONV_PAD:CONV_PAD + CONV_TM, :] = a_ref[:, sl].astype(F32) * _sig(b_ref[:, sl].astype(F32))


def _conv_forward(proj, conv_w, dw_b, ln_g, ln_b, w_co):
    tokens = proj.shape[0]
    tm = CONV_TM
    halo_blocks = tm // CONV_PAD

    def body(a_ref, b_ref, ah_ref, bh_ref, cg_ref, cw_ref, dwb_ref, lng_ref, lnb_ref, wco_ref,
             cv_ref, yc_ref, u_ext, cv_scr):
        _fill_u_ext(u_ext, a_ref, b_ref, ah_ref, bh_ref, pl.program_id(0) == 0)

        def lane_chunk(lc, carry):
            for rc in range(tm // 64):
                acc = jnp.zeros((64, LANES), F32)
                for j in range(CONV_KERNEL):
                    acc = acc + cw_ref[lc, pl.ds(j, 1), :] * u_ext[lc, pl.ds(64 * rc + 2 + j, 64), :]
                cv_scr[lc, pl.ds(64 * rc, 64), :] = acc
            return carry
        lax.fori_loop(0, N_LANE_CHUNKS, lane_chunk, 0)

        cv = jnp.concatenate([cv_scr[lc] for lc in range(N_LANE_CHUNKS)], axis=1) + dwb_ref[...]
        cv_ref[...] = cv
        mu = jnp.mean(cv, axis=-1, keepdims=True)
        zc = cv - mu
        rstd = lax.rsqrt(jnp.mean(zc * zc, axis=-1, keepdims=True) + LN_EPS)
        ln = zc * rstd * lng_ref[...] + lnb_ref[...]
        cg = cg_ref[...].astype(F32)
        pc = (ln * _sig(ln)) * (cg * _sig(cg))
        yc_ref[...] = _dot(pc.astype(BF16), wco_ref[...]).astype(BF16)

    def halo_map(i):
        return (jnp.maximum(i * halo_blocks - 1, 0), 0)

    tile = lambda col: pl.BlockSpec((tm, D_MODEL), lambda i: (i, col))
    return pl.pallas_call(
        body, name="conv_forward", grid=(tokens // tm,),
        in_specs=[tile(COL_A), tile(COL_B),
                  pl.BlockSpec((CONV_PAD, D_MODEL), lambda i: (halo_map(i)[0], COL_A)),
                  pl.BlockSpec((CONV_PAD, D_MODEL), lambda i: (halo_map(i)[0], COL_B)),
                  tile(COL_CG), _const_spec((N_DEV, CONV_PAD, LANES)),
                  _const_spec((1, D_MODEL)), _const_spec((1, D_MODEL)), _const_spec((1, D_MODEL)),
                  _const_spec((D_MODEL, D_MODEL))],
        out_specs=(pl.BlockSpec((tm, D_MODEL), lambda i: (i, 0)),
                   pl.BlockSpec((tm, D_MODEL), lambda i: (i, 0))),
        out_shape=(jax.ShapeDtypeStruct((tokens, D_MODEL), F32),
                   jax.ShapeDtypeStruct((tokens, D_MODEL), BF16)),
        scratch_shapes=[pltpu.VMEM((N_LANE_CHUNKS, CONV_PAD + tm, LANES), F32),
                        pltpu.VMEM((N_LANE_CHUNKS, tm, LANES), F32)],
        compiler_params=_cparams(("parallel",), VMEM_LIMIT),
    )(proj, proj, proj, proj, proj, conv_w, dw_b, ln_g, ln_b, w_co)


def _band_masks(n):
    row = lax.broadcasted_iota(jnp.int32, (4 * BLOCK, BLOCK), 0) & (BLOCK - 1)
    col = lax.broadcasted_iota(jnp.int32, (4 * BLOCK, BLOCK), 1)
    return col <= row, jnp.logical_and(col > row, n > 0)


def _stack_heads(tile_a, tile_b, lo):
    zero = jnp.zeros_like(tile_a)
    return jnp.concatenate([jnp.where(lo, tile_a, zero), jnp.where(lo, zero, tile_a),
                            jnp.where(lo, tile_b, zero), jnp.where(lo, zero, tile_b)], axis=0)


def _unstack_heads(stacked, lo):
    s = [stacked[BLOCK * g:BLOCK * (g + 1)] for g in range(4)]
    return (jnp.where(lo, s[0], 0.0) + jnp.where(lo, 0.0, s[1]),
            jnp.where(lo, s[2], 0.0) + jnp.where(lo, 0.0, s[3]))


def _sink_column(sinks_ref, kvh):
    return jnp.concatenate([jnp.full((BLOCK, 1), sinks_ref[0, 4 * kvh + g], F32) for g in range(4)], axis=0)


def _softmax_band(q_stack, k_cur, k_prev, sink, cur_ok, prev_ok):
    s_c = jnp.where(cur_ok, _dot_nt(q_stack, k_cur), NEG)
    s_p = jnp.where(prev_ok, _dot_nt(q_stack, k_prev), NEG)
    m = jnp.maximum(jnp.maximum(jnp.max(s_c, axis=1, keepdims=True), jnp.max(s_p, axis=1, keepdims=True)), sink)
    e_c, e_p, e_s = jnp.exp(s_c - m), jnp.exp(s_p - m), jnp.exp(sink - m)
    inv = 1.0 / (jnp.sum(e_c, axis=1, keepdims=True) + jnp.sum(e_p, axis=1, keepdims=True) + e_s)
    return e_c * inv, e_p * inv, e_s * inv


def _attention_forward(qr, kd, vd, proj, sinks, w_ao):
    tokens = qr.shape[0]
    nb = tokens // BLOCK

    def body(q_ref, kc_ref, kp_ref, vc_ref, vp_ref, ag0_ref, ag1_ref, sinks_ref, wao_ref, o_ref, ya_ref, o_scr):
        n = pl.program_id(0)
        lo = _lane_halves()
        cur_ok, prev_ok = _band_masks(n)
        for kvh in range(N_KV_HEADS):
            ta, tb = slice(LANES * 2 * kvh, LANES * (2 * kvh + 1)), slice(LANES * (2 * kvh + 1), LANES * (2 * kvh + 2))
            ks = slice(LANES * kvh, LANES * (kvh + 1))
            q_stack = _stack_heads(q_ref[:, ta], q_ref[:, tb], lo)
            p_c, p_p, _ = _softmax_band(q_stack, kc_ref[:, ks], kp_ref[:, ks], _sink_column(sinks_ref, kvh), cur_ok, prev_ok)
            o_stack = _dot(p_c.astype(BF16), vc_ref[:, ks]) + _dot(p_p.astype(BF16), vp_ref[:, ks])
            o_scr[:, ta], o_scr[:, tb] = _unstack_heads(o_stack, lo)
        o = o_scr[...]
        o_ref[...] = o.astype(BF16)
        ag = jnp.concatenate([ag0_ref[...], ag1_ref[...]], axis=1).astype(F32)
        ya_ref[...] = _dot((o * (ag * _sig(ag))).astype(BF16), wao_ref[...]).astype(BF16)

    cur = lambda w, col=0: pl.BlockSpec((BLOCK, w), lambda n: (n, col))
    prev = lambda w: pl.BlockSpec((BLOCK, w), lambda n: (jnp.maximum(n - 1, 0), 0))
    return pl.pallas_call(
        body, name="attention_forward", grid=(nb,),
        in_specs=[cur(D_MODEL), cur(512), prev(512), cur(512), prev(512),
                  cur(512, COL512_AG), cur(512, COL512_AG + 1),
                  pl.BlockSpec(memory_space=pltpu.SMEM), _const_spec((D_MODEL, D_MODEL))],
        out_specs=(cur(D_MODEL), cur(D_MODEL)),
        out_shape=(jax.ShapeDtypeStruct((tokens, D_MODEL), BF16),
                   jax.ShapeDtypeStruct((tokens, D_MODEL), BF16)),
        scratch_shapes=[pltpu.VMEM((BLOCK, D_MODEL), F32)],
        compiler_params=_cparams(("parallel",), VMEM_LIMIT),
    )(qr, kd, kd, vd, vd, proj, proj, sinks, w_ao)


def _merge_and_head(yc, ya, proj, x, target, w_out, final_g):
    tokens = x.shape[0]
    tm = 256
    last = tokens // tm - 1

    def body(yc_ref, ya_ref, mlc0_ref, mlc1_ref, mla0_ref, mla1_ref, x_ref, t_ref, wout_ref, fg_ref,
             dx2_ref, dyc_ref, dya_ref, dmlc_ref, dmla_ref, gwout_ref, part_ref, gacc):
        i = pl.program_id(0)

        @pl.when(i == 0)
        def _():
            gacc[...] = jnp.zeros_like(gacc)
            part_ref[...] = jnp.zeros_like(part_ref)

        yc, ya = yc_ref[...].astype(F32), ya_ref[...].astype(F32)
        gc = _sig(jnp.concatenate([mlc0_ref[...], mlc1_ref[...]], axis=1).astype(F32))
        ga = _sig(jnp.concatenate([mla0_ref[...], mla1_ref[...]], axis=1).astype(F32))
        merged = (gc * yc + ga * ya).astype(BF16)
        x2 = x_ref[...] + _dot(merged, wout_ref[...])
        r2 = lax.rsqrt(jnp.mean(x2 * x2, axis=-1, keepdims=True) + RMS_EPS)
        x2n = x2 * r2
        fg = fg_ref[...]
        err = x2n * fg - t_ref[...]
        dy = err * (1.0 / D_MODEL)
        part_ref[0:1, :] += jnp.sum(dy * x2n, axis=0, keepdims=True)
        part_ref[1:2, :] += jnp.sum(err * err, axis=0, keepdims=True) * (0.5 / D_MODEL)
        dx2n = dy * fg
        dx2 = r2 * (dx2n - x2n * jnp.mean(dx2n * x2n, axis=-1, keepdims=True))
        dx2_ref[...] = dx2
        dx2b = dx2.astype(BF16)
        gacc[...] += _dot_tn(merged, dx2b)
        dm = _dot_nt(dx2b, wout_ref[...])
        dyc_ref[...] = (dm * gc).astype(BF16)
        dya_ref[...] = (dm * ga).astype(BF16)
        dmlc_ref[...] = (dm * yc * (gc * (1.0 - gc))).astype(BF16)
        dmla_ref[...] = (dm * ya * (ga * (1.0 - ga))).astype(BF16)

        @pl.when(i == last)
        def _():
            gwout_ref[...] = gacc[...].astype(BF16)

    tile = lambda col=0: pl.BlockSpec((tm, D_MODEL), lambda i: (i, col))
    half = lambda col: pl.BlockSpec((tm, 512), lambda i: (i, col))
    return pl.pallas_call(
        body, name="merge_and_head", grid=(tokens // tm,),
        in_specs=[tile(), tile(), half(COL512_MLC), half(COL512_MLC + 1), half(COL512_MLA), half(COL512_MLA + 1),
                  tile(), tile(), _const_spec((D_MODEL, D_MODEL)), _const_spec((1, D_MODEL))],
        out_specs=(tile(), tile(), tile(), tile(), tile(),
                   _const_spec((D_MODEL, D_MODEL)), _const_spec((8, D_MODEL))),
        out_shape=(jax.ShapeDtypeStruct((tokens, D_MODEL), F32),
                   jax.ShapeDtypeStruct((tokens, D_MODEL), BF16),
                   jax.ShapeDtypeStruct((tokens, D_MODEL), BF16),
                   jax.ShapeDtypeStruct((tokens, D_MODEL), BF16),
                   jax.ShapeDtypeStruct((tokens, D_MODEL), BF16),
                   jax.ShapeDtypeStruct((D_MODEL, D_MODEL), BF16),
                   jax.ShapeDtypeStruct((8, D_MODEL), F32)),
        scratch_shapes=[pltpu.VMEM((D_MODEL, D_MODEL), F32)],
        compiler_params=_cparams(("arbitrary",), VMEM_LIMIT),
    )(yc, ya, proj, proj, proj, proj, x, target, w_out, final_g)


def _conv_backward_pointwise(dyc, cv, proj, w_co, ln_g, ln_b):
    tokens = cv.shape[0]
    tm = 256
    last = tokens // tm - 1

    def body(dyc_ref, cv_ref, cg_ref, wco_ref, lng_ref, lnb_ref, dcv_ref, dcg_ref, gwco_ref, part_ref, gacc):
        i = pl.program_id(0)

        @pl.when(i == 0)
        def _():
            gacc[...] = jnp.zeros_like(gacc)
            part_ref[...] = jnp.zeros_like(part_ref)

        cv = cv_ref[...]
        mu = jnp.mean(cv, axis=-1, keepdims=True)
        zc = cv - mu
        rstd = lax.rsqrt(jnp.mean(zc * zc, axis=-1, keepdims=True) + LN_EPS)
        z = zc * rstd
        lng = lng_ref[...]
        ln = z * lng + lnb_ref[...]
        sl = _sig(ln)
        c = ln * sl
        cg = cg_ref[...].astype(F32)
        scg = _sig(cg)
        gate = cg * scg
        dyc = dyc_ref[...]
        gacc[...] += _dot_tn((c * gate).astype(BF16), dyc)
        dpc = _dot_nt(dyc, wco_ref[...])
        dcg_ref[...] = (dpc * c * (scg * (1.0 + cg * (1.0 - scg)))).astype(BF16)
        dln = dpc * gate * (sl * (1.0 + ln * (1.0 - sl)))
        part_ref[0:1, :] += jnp.sum(dln * z, axis=0, keepdims=True)
        part_ref[1:2, :] += jnp.sum(dln, axis=0, keepdims=True)
        dz = dln * lng
        dcv = rstd * (dz - jnp.mean(dz, axis=-1, keepdims=True) - z * jnp.mean(dz * z, axis=-1, keepdims=True))
        part_ref[2:3, :] += jnp.sum(dcv, axis=0, keepdims=True)
        dcv_ref[...] = dcv

        @pl.when(i == last)
        def _():
            gwco_ref[...] = gacc[...].astype(BF16)

    tile = lambda col=0: pl.BlockSpec((tm, D_MODEL), lambda i: (i, col))
    return pl.pallas_call(
        body, name="conv_backward_pointwise", grid=(tokens // tm,),
        in_specs=[tile(), tile(), tile(COL_CG), _const_spec((D_MODEL, D_MODEL)),
                  _const_spec((1, D_MODEL)), _const_spec((1, D_MODEL))],
        out_specs=(tile(), tile(), _const_spec((D_MODEL, D_MODEL)), _const_spec((8, D_MODEL))),
        out_shape=(jax.ShapeDtypeStruct((tokens, D_MODEL), F32),
                   jax.ShapeDtypeStruct((tokens, D_MODEL), BF16),
                   jax.ShapeDtypeStruct((D_MODEL, D_MODEL), BF16),
                   jax.ShapeDtypeStruct((8, D_MODEL), F32)),
        scratch_shapes=[pltpu.VMEM((D_MODEL, D_MODEL), F32)],
        compiler_params=_cparams(("arbitrary",), VMEM_LIMIT),
    )(dyc, cv, proj, w_co, ln_g, ln_b)


def _conv_backward_taps(dcv, proj, conv_w):
    tokens = dcv.shape[0]
    tm = CONV_TM
    nt = tokens // tm
    halo_blocks = tm // CONV_PAD

    def body(d_ref, dn_ref, a_ref, b_ref, ah_ref, bh_ref, cw_ref, da_ref, db_ref, gw_ref, u_ext, d_ext, du_scr, gw_acc):
        i = pl.program_id(0)

        @pl.when(i == 0)
        def _():
            gw_acc[...] = jnp.zeros_like(gw_acc)

        _fill_u_ext(u_ext, a_ref, b_ref, ah_ref, bh_ref, i == 0)
        for lc in range(N_LANE_CHUNKS):
            sl = slice(LANES * lc, LANES * (lc + 1))
            d_ext[lc, 0:tm, :] = d_ref[:, sl]
            d_ext[lc, tm:tm + CONV_PAD, :] = jnp.where(i == nt - 1, 0.0, dn_ref[:, sl])

        def lane_chunk(lc, carry):
            n_rc = tm // 64
            du = [jnp.zeros((64, LANES), F32) for _ in range(n_rc)]
            for j in range(CONV_KERNEL):
                w = cw_ref[lc, pl.ds(j, 1), :]
                gsum = jnp.zeros((8, LANES), F32)
                for rc in range(n_rc):
                    du[rc] = du[rc] + w * d_ext[lc, pl.ds(64 * rc + 30 - j, 64), :]
                    prod = d_ext[lc, pl.ds(64 * rc, 64), :] * u_ext[lc, pl.ds(64 * rc + 2 + j, 64), :]
                    gsum = gsum + jnp.sum(prod.reshape(8, 8, LANES), axis=0)
                gw_acc[lc, j] += gsum
            for rc in range(n_rc):
                du_scr[lc, pl.ds(64 * rc, 64), :] = du[rc]
            return carry
        lax.fori_loop(0, N_LANE_CHUNKS, lane_chunk, 0)

        du = jnp.concatenate([du_scr[lc] for lc in range(N_LANE_CHUNKS)], axis=1)
        a, b = a_ref[...].astype(F32), b_ref[...].astype(F32)
        sb = _sig(b)
        da_ref[...] = (du * sb).astype(BF16)
        db_ref[...] = (du * a * (sb * (1.0 - sb))).astype(BF16)

        @pl.when(i == nt - 1)
        def _():
            gw_ref[...] = jnp.sum(gw_acc[...], axis=2)

    def prev_halo(i):
        return jnp.maximum(i * halo_blocks - 1, 0)

    def next_halo(i):
        return jnp.minimum((i + 1) * halo_blocks, tokens // CONV_PAD - 1)

    tile = lambda col=0: pl.BlockSpec((tm, D_MODEL), lambda i: (i, col))
    return pl.pallas_call(
        body, name="conv_backward_taps", grid=(nt,),
        in_specs=[tile(), pl.BlockSpec((CONV_PAD, D_MODEL), lambda i: (next_halo(i), 0)),
                  tile(COL_A), tile(COL_B),
                  pl.BlockSpec((CONV_PAD, D_MODEL), lambda i: (prev_halo(i), COL_A)),
                  pl.BlockSpec((CONV_PAD, D_MODEL), lambda i: (prev_halo(i), COL_B)),
                  _const_spec((N_DEV, CONV_PAD, LANES))],
        out_specs=(tile(), tile(), _const_spec((N_DEV, CONV_PAD, LANES))),
        out_shape=(jax.ShapeDtypeStruct((tokens, D_MODEL), BF16),
                   jax.ShapeDtypeStruct((tokens, D_MODEL), BF16),
                   jax.ShapeDtypeStruct((N_DEV, CONV_PAD, LANES), F32)),
        scratch_shapes=[pltpu.VMEM((N_LANE_CHUNKS, CONV_PAD + tm, LANES), F32),
                        pltpu.VMEM((N_LANE_CHUNKS, tm + CONV_PAD, LANES), F32),
                        pltpu.VMEM((N_LANE_CHUNKS, tm, LANES), F32),
                        pltpu.VMEM((N_LANE_CHUNKS, CONV_PAD, 8, LANES), F32)],
        compiler_params=_cparams(("arbitrary",), VMEM_LIMIT),
    )(dcv, dcv, proj, proj, proj, proj, conv_w)


def _fold_kv_head(dup, lo, second_half):
    both = dup + pltpu.roll(dup, HEAD_DIM, 1)
    return jnp.where(lo, 0.0, both) if second_half else jnp.where(lo, both, 0.0)


def _attention_backward(dya, o, qr, kd, vd, proj, sinks, w_ao, cos_t, sin_up, sin_dn):
    tokens = qr.shape[0]
    nb = tokens // BLOCK
    scale = HEAD_DIM ** -0.5

    def body(dya_ref, o_ref, ag0_ref, ag1_ref, q_ref, kc_ref, kp_ref, vc_ref, vp_ref, sinks_ref, wao_ref,
             cos_c, up_c, dn_c, cos_p, up_p, dn_p,
             dq_ref, dkv_ref, dag_ref, gwao_ref, gsink_ref, gacc, dk_carry, dv_carry, dq_scr):
        n = pl.program_id(0)
        lo = _lane_halves()

        @pl.when(n == 0)
        def _():
            gacc[...] = jnp.zeros_like(gacc)
            gsink_ref[...] = jnp.zeros_like(gsink_ref)
            dk_carry[...] = jnp.zeros_like(dk_carry)
            dv_carry[...] = jnp.zeros_like(dv_carry)

        def emit_prev(dk_prev, dv_prev):
            for p in range(2):
                sl = slice(LANES * p, LANES * (p + 1))
                dk = _rope_transposed(dk_carry[:, sl] + dk_prev[p], cos_p[...], up_p[...], dn_p[...])
                dkv_ref[:, sl] = dk.astype(BF16)
                dkv_ref[:, slice(256 + LANES * p, 256 + LANES * (p + 1))] = (dv_carry[:, sl] + dv_prev[p]).astype(BF16)

        @pl.when(n < nb)
        def _():
            dya = dya_ref[...]
            dpa = _dot_nt(dya, wao_ref[...])
            o = o_ref[...].astype(F32)
            ag = jnp.concatenate([ag0_ref[...], ag1_ref[...]], axis=1).astype(F32)
            sg = _sig(ag)
            gate = ag * sg
            gacc[...] += _dot_tn((o * gate).astype(BF16), dya)
            dag_ref[...] = (dpa * o * (sg * (1.0 + ag * (1.0 - sg)))).astype(BF16)
            do = (dpa * gate).astype(BF16)

            cur_ok, prev_ok = _band_masks(n)
            head_lane = lax.broadcasted_iota(jnp.int32, (1, LANES), 1)
            gsink = jnp.zeros((1, LANES), F32)
            zero_tile = jnp.zeros((BLOCK, LANES), F32)
            dk_cur, dk_prev = [zero_tile, zero_tile], [zero_tile, zero_tile]
            dv_cur, dv_prev = [zero_tile, zero_tile], [zero_tile, zero_tile]
            for kvh in range(N_KV_HEADS):
                ta, tb = slice(LANES * 2 * kvh, LANES * (2 * kvh + 1)), slice(LANES * (2 * kvh + 1), LANES * (2 * kvh + 2))
                ks = slice(LANES * kvh, LANES * (kvh + 1))
                q_stack = _stack_heads(q_ref[:, ta], q_ref[:, tb], lo)
                do_stack = _stack_heads(do[:, ta], do[:, tb], lo)
                k_c, k_p, v_c, v_p = kc_ref[:, ks], kp_ref[:, ks], vc_ref[:, ks], vp_ref[:, ks]
                p_c, p_p, p_s = _softmax_band(q_stack, k_c, k_p, _sink_column(sinks_ref, kvh), cur_ok, prev_ok)
                dp_c, dp_p = _dot_nt(do_stack, v_c), _dot_nt(do_stack, v_p)
                delta = jnp.sum(p_c * dp_c, axis=1, keepdims=True) + jnp.sum(p_p * dp_p, axis=1, keepdims=True)
                ds_c = (p_c * (dp_c - delta)).astype(BF16)
                ds_p = (p_p * (dp_p - delta)).astype(BF16)
                sink_terms = p_s * delta
                for g in range(4):
                    total = jnp.sum(sink_terms[BLOCK * g:BLOCK * (g + 1)], axis=0, keepdims=True)
                    gsink = gsink - jnp.where(head_lane == 4 * kvh + g, total, 0.0)
                dq_stack = _dot(ds_c, k_c) + _dot(ds_p, k_p)
                dq_scr[:, ta], dq_scr[:, tb] = _unstack_heads(dq_stack, lo)
                tile, second = kvh // 2, kvh % 2 == 1
                dk_cur[tile] = dk_cur[tile] + _fold_kv_head(_dot_tn(ds_c, q_stack), lo, second)
                dk_prev[tile] = dk_prev[tile] + _fold_kv_head(_dot_tn(ds_p, q_stack), lo, second)
                dv_cur[tile] = dv_cur[tile] + _fold_kv_head(_dot_tn(p_c.astype(BF16), do_stack), lo, second)
                dv_prev[tile] = dv_prev[tile] + _fold_kv_head(_dot_tn(p_p.astype(BF16), do_stack), lo, second)
            gsink_ref[0:1, :] += gsink
            for p in range(D_MODEL // LANES):
                sl = slice(LANES * p, LANES * (p + 1))
                dq_ref[:, sl] = (_rope_transposed(dq_scr[:, sl], cos_c[...], up_c[...], dn_c[...]) * scale).astype(BF16)
            emit_prev(dk_prev, dv_prev)
            for p in range(2):
                sl = slice(LANES * p, LANES * (p + 1))
                dk_carry[:, sl] = dk_cur[p]
                dv_carry[:, sl] = dv_cur[p]

        @pl.when(n == nb)
        def _():
            zero_tile = jnp.zeros((BLOCK, LANES), F32)
            emit_prev([zero_tile, zero_tile], [zero_tile, zero_tile])
            gwao_ref[...] = gacc[...].astype(BF16)

    def cur_idx(n):
        return jnp.minimum(n, nb - 1)

    def prev_idx(n):
        return jnp.clip(n - 1, 0, nb - 1)

    cur = lambda w, col=0: pl.BlockSpec((BLOCK, w), lambda n: (cur_idx(n), col))
    prev = lambda w: pl.BlockSpec((BLOCK, w), lambda n: (prev_idx(n), 0))
    return pl.pallas_call(
        body, name="attention_backward", grid=(nb + 1,),
        in_specs=[cur(D_MODEL), cur(D_MODEL), cur(512, COL512_AG), cur(512, COL512_AG + 1), cur(D_MODEL),
                  cur(512), prev(512), cur(512), prev(512),
                  pl.BlockSpec(memory_space=pltpu.SMEM), _const_spec((D_MODEL, D_MODEL)),
                  cur(LANES), cur(LANES), cur(LANES), prev(LANES), prev(LANES), prev(LANES)],
        out_specs=(cur(D_MODEL), prev(512), cur(D_MODEL),
                   _const_spec((D_MODEL, D_MODEL)), _const_spec((8, LANES))),
        out_shape=(jax.ShapeDtypeStruct((tokens, D_MODEL), BF16),
                   jax.ShapeDtypeStruct((tokens, 512), BF16),
                   jax.ShapeDtypeStruct((tokens, D_MODEL), BF16),
                   jax.ShapeDtypeStruct((D_MODEL, D_MODEL), BF16),
                   jax.ShapeDtypeStruct((8, LANES), F32)),
        scratch_shapes=[pltpu.VMEM((D_MODEL, D_MODEL), F32),
                        pltpu.VMEM((BLOCK, 256), F32), pltpu.VMEM((BLOCK, 256), F32),
                        pltpu.VMEM((BLOCK, D_MODEL), F32)],
        compiler_params=_cparams(("arbitrary",), VMEM_LIMIT),
    )(dya, o, proj, proj, qr, kd, kd, vd, vd, sinks, w_ao, cos_t, sin_up, sin_dn, cos_t, sin_up, sin_dn)


def _transpose_tokens(h):
    tokens = h.shape[0]
    tt = min(512, tokens)

    def body(h_ref, out_ref):
        out_ref[...] = h_ref[...].astype(F32).T.astype(BF16)

    return pl.pallas_call(
        body, name="transpose_tokens", grid=(tokens // tt,),
        in_specs=[pl.BlockSpec((tt, D_MODEL), lambda i: (i, 0))],
        out_specs=pl.BlockSpec((D_MODEL, tt), lambda i: (0, i)),
        out_shape=jax.ShapeDtypeStruct((D_MODEL, tokens), BF16),
        compiler_params=_cparams(("parallel",)),
    )(h)


def _input_backward(sections, w_in_t, x, dx2, norm_g):
    tokens = x.shape[0]
    tm = 256

    def body(*refs):
        sec = refs[:8]
        w_ref, x_ref, dx2_ref, g_ref, gx_ref, part_ref = refs[8:]

        @pl.when(pl.program_id(0) == 0)
        def _():
            part_ref[...] = jnp.zeros_like(part_ref)

        dh = jnp.zeros((tm, D_MODEL), F32)
        for s in range(8):
            dh = dh + _dot(sec[s][...], w_ref[_SECTION_ROWS[s]:_SECTION_ROWS[s] + _SECTION_WIDTH[s], :])
        xv = x_ref[...]
        r = lax.rsqrt(jnp.mean(xv * xv, axis=-1, keepdims=True) + RMS_EPS)
        xn = xv * r
        part_ref[0:1, :] += jnp.sum(dh * xn, axis=0, keepdims=True)
        dxn = dh * g_ref[...]
        gx_ref[...] = dx2_ref[...] + r * (dxn - xn * jnp.mean(dxn * xn, axis=-1, keepdims=True))

    tile = lambda w=D_MODEL: pl.BlockSpec((tm, w), lambda i: (i, 0))
    return pl.pallas_call(
        body, name="input_backward", grid=(tokens // tm,),
        in_specs=[tile(w) for w in _SECTION_WIDTH] + [
            pl.BlockSpec((IN_WIDTH, D_MODEL), lambda i: (0, 0), pipeline_mode=pl.Buffered(1)),
            tile(), tile(), _const_spec((1, D_MODEL))],
        out_specs=(tile(), _const_spec((8, D_MODEL))),
        out_shape=(jax.ShapeDtypeStruct((tokens, D_MODEL), F32),
                   jax.ShapeDtypeStruct((8, D_MODEL), F32)),
        compiler_params=_cparams(("arbitrary",), VMEM_LIMIT),
    )(*sections, w_in_t, x, dx2, norm_g)


def _adamw_math(w, g, m, v):
    m = ADAM_B1 * m + (1.0 - ADAM_B1) * g
    v = ADAM_B2 * v + (1.0 - ADAM_B2) * (g * g)
    m_hat = m / (1.0 - ADAM_B1 ** ADAM_STEP)
    v_hat = v / (1.0 - ADAM_B2 ** ADAM_STEP)
    delta = -ADAM_LR * (m_hat / (jnp.sqrt(v_hat) + ADAM_EPS) + ADAM_WD * w)
    return delta, m, v


def _sum_slots(recv_ref):
    total = recv_ref[0].astype(F32)
    for d in range(1, N_DEV):
        total = total + recv_ref[d].astype(F32)
    return total


def _adamw(name, w, g, m, v, tile_rows):
    rows, cols = w.shape

    def body(w_ref, g_ref, m_ref, v_ref, d_ref, nm_ref, nv_ref):
        d_ref[...], nm_ref[...], nv_ref[...] = _adamw_math(w_ref[...], g_ref[...], m_ref[...], v_ref[...])

    spec = pl.BlockSpec((tile_rows, cols), lambda i: (i, 0))
    shape = jax.ShapeDtypeStruct((rows, cols), F32)
    return pl.pallas_call(
        body, name=name, grid=(rows // tile_rows,),
        in_specs=[spec] * 4, out_specs=(spec,) * 3, out_shape=(shape,) * 3,
        compiler_params=_cparams(("parallel",)),
    )(w, g, m, v)


def _sum_adamw(name, recv, w, m, v):
    def body(recv_ref, w_ref, m_ref, v_ref, g_ref, d_ref, nm_ref, nv_ref):
        g = _sum_slots(recv_ref)
        g_ref[...] = g
        d_ref[...], nm_ref[...], nv_ref[...] = _adamw_math(w_ref[...], g, m_ref[...], v_ref[...])

    shape = jax.ShapeDtypeStruct(w.shape, F32)
    return pl.pallas_call(body, name=name, out_shape=(shape,) * 4)(recv, w, m, v)


def _pad_rows(a, rows):
    return jnp.concatenate([a, jnp.zeros((rows - a.shape[0],) + a.shape[1:], a.dtype)], axis=0)


def kernel(x, norm_g, w_in, conv_dw_w, conv_dw_b, conv_ln_g, conv_ln_b, w_conv_out, attn_sinks, w_attn_out, w_out, final_norm_g, loss_target, m_norm_g, m_w_in, m_conv_dw_w, m_conv_dw_b, m_conv_ln_g, m_conv_ln_b, m_w_conv_out, m_attn_sinks, m_w_attn_out, m_w_out, m_final_norm_g, v_norm_g, v_w_in, v_conv_dw_w, v_conv_dw_b, v_conv_ln_g, v_conv_ln_b, v_w_conv_out, v_attn_sinks, v_w_attn_out, v_w_out, v_final_norm_g):
    xs, target = x[0], loss_target[0]
    tokens = xs.shape[0]
    fg_row = final_norm_g.reshape(1, D_MODEL)

    taps_bits = lax.bitcast_convert_type(_pad_rows(conv_dw_w[0], CONV_PAD), BF16).reshape(8, D_MODEL)
    pack = jnp.concatenate([w_conv_out[0].astype(BF16), w_attn_out[0].astype(BF16), w_out[0].astype(BF16),
                            jnp.pad(taps_bits, ((0, PACK_ROWS - 3 * SHARD_SQ - 8), (0, 0)))], axis=0)
    w_in_t32 = w_in[0].T
    proj, h, w_in_t, pack_full = _gather_project(xs, norm_g, w_in_t32.astype(BF16), pack)
    w_co = pack_full[:, 0:SHARD_SQ].reshape(D_MODEL, D_MODEL)
    w_ao = pack_full[:, SHARD_SQ:2 * SHARD_SQ].reshape(D_MODEL, D_MODEL)
    w_o = pack_full[:, 2 * SHARD_SQ:3 * SHARD_SQ].reshape(D_MODEL, D_MODEL)
    conv_w = lax.bitcast_convert_type(
        pack_full[:, 3 * SHARD_SQ:3 * SHARD_SQ + 8].reshape(N_DEV, CONV_PAD, LANES, 2), F32)

    cos_t, sin_up, sin_dn = _rope_tables(tokens)
    qr, kd, vd = _rope_qkv(proj, cos_t, sin_up, sin_dn)
    cv, yc = _conv_forward(proj, conv_w, conv_dw_b, conv_ln_g, conv_ln_b, w_co)
    o, ya = _attention_forward(qr, kd, vd, proj, attn_sinks, w_ao)

    dx2, dyc, dya, dmlc, dmla, g_out, part_head = _merge_and_head(yc, ya, proj, xs, target, w_o, fg_row)
    dcv, dcg, g_co, part_conv = _conv_backward_pointwise(dyc, cv, proj, w_co, conv_ln_g, conv_ln_b)
    da, db, g_conv = _conv_backward_taps(dcv, proj, conv_w)
    dq, dkv, dag, g_ao, part_sink = _attention_backward(dya, o, qr, kd, vd, proj, attn_sinks, w_ao, cos_t, sin_up, sin_dn)
    sections = (da, db, dcg, dq, dkv, dag, dmlc, dmla)
    grad_x, part_in = _input_backward(sections, w_in_t, xs, dx2, norm_g)

    small = jnp.concatenate([
        part_in[0:1], part_conv[2:3], part_conv[0:1], part_conv[1:2], part_head[0:1],
        jnp.pad(part_sink[0:1], ((0, 0), (0, D_MODEL - LANES))), part_head[1:2],
        jnp.zeros((1, D_MODEL), F32)], axis=0)

    g_mine, r_conv, r_small = _grad_exchange(sections, _transpose_tokens(h), g_co, g_ao, g_out, g_conv, small)

    g_in_t = g_mine[:SHARD_IN]
    w_in_res = _adamw("adamw_w_in", w_in_t32, g_in_t, m_w_in[0].T, v_w_in[0].T, 192)
    grad_w_in, d_w_in, nm_w_in, nv_w_in = (a.T for a in (g_in_t,) + tuple(w_in_res))
    sq = {}
    for j, (nm, w, m, v) in enumerate((("w_conv_out", w_conv_out, m_w_conv_out, v_w_conv_out),
                                       ("w_attn_out", w_attn_out, m_w_attn_out, v_w_attn_out),
                                       ("w_out", w_out, m_w_out, v_w_out))):
        g = g_mine[SHARD_IN + j * SHARD_SQ:SHARD_IN + (j + 1) * SHARD_SQ]
        sq[nm] = (g,) + tuple(_adamw("adamw_" + nm, w[0], g, m[0], v[0], SHARD_SQ))
    conv_res = _sum_adamw("sum_adamw_conv_dw_w", r_conv.reshape(N_DEV, CONV_PAD, LANES),
                          _pad_rows(conv_dw_w[0], CONV_PAD), _pad_rows(m_conv_dw_w[0], CONV_PAD),
                          _pad_rows(v_conv_dw_w[0], CONV_PAD))
    pad_sink = lambda a: jnp.pad(a, ((0, 0), (0, D_MODEL - N_Q_HEADS)))
    zero_rows = jnp.zeros((2, D_MODEL), F32)
    stack = lambda a, b, c, d, e, f: jnp.concatenate([a, b, c, d, e.reshape(1, D_MODEL), pad_sink(f), zero_rows], axis=0)
    small_res = _sum_adamw(
        "sum_adamw_small", r_small,
        stack(norm_g, conv_dw_b, conv_ln_g, conv_ln_b, final_norm_g, attn_sinks),
        stack(m_norm_g, m_conv_dw_b, m_conv_ln_g, m_conv_ln_b, m_final_norm_g, m_attn_sinks),
        stack(v_norm_g, v_conv_dw_b, v_conv_ln_g, v_conv_ln_b, v_final_norm_g, v_attn_sinks))
    loss = jnp.sum(small_res[0][6])

    def leaf(k):
        s = small_res[k]
        return (s[0:1], (grad_w_in, d_w_in, nm_w_in, nv_w_in)[k][None], conv_res[k][None, :CONV_KERNEL],
                s[1:2], s[2:3], s[3:4], sq["w_conv_out"][k][None], s[5:6, :N_Q_HEADS],
                sq["w_attn_out"][k][None], sq["w_out"][k][None], s[4])

    return (loss, grad_x[None], *leaf(0), *leaf(1), *leaf(2), *leaf(3))
```

```python
import jax
import jax.numpy as jnp
from jax import lax
from jax.experimental import pallas as pl
from jax.experimental.pallas import tpu as pltpu

F32 = jnp.float32
BF16 = jnp.bfloat16
MESH = pl.DeviceIdType.MESH

D_MODEL = 1024
IN_WIDTH = 7680
N_DEV = 8
SHARD_IN = IN_WIDTH // N_DEV
SHARD_SQ = D_MODEL // N_DEV
CONV_KERNEL = 31
CONV_PAD = 32
HEAD_DIM = 64
N_Q_HEADS = 16
N_KV_HEADS = 4
BLOCK = 128
LANES = 128
ROPE_THETA = 10000.0
RMS_EPS = 1e-5
LN_EPS = 1e-5
NEG = -1e30
ADAM_LR = 0.001
ADAM_B1 = 0.9
ADAM_B2 = 0.999
ADAM_EPS = 1e-08
ADAM_WD = 0.01
ADAM_STEP = 10

OFF_A, OFF_B, OFF_CG, OFF_Q, OFF_KV, OFF_AG, OFF_MLC, OFF_MLA = 0, 1024, 2048, 3072, 4096, 4608, 5632, 6656
COL_A, COL_B, COL_CG, COL_Q = 0, 1, 2, 3
COL512_KV, COL512_AG, COL512_MLC, COL512_MLA = 8, 9, 11, 13
UNIT = 2 * SHARD_IN
PACK_ROWS = 400

VMEM_LIMIT = 56 * 1024 * 1024


def _cparams(sem=None, vmem=None):
    return pltpu.CompilerParams(dimension_semantics=sem, vmem_limit_bytes=vmem)


def _sig(v):
    return 1.0 / (1.0 + jnp.exp(-v))


def _dot(a, b):
    return jnp.dot(a, b, preferred_element_type=F32)


def _dot_nt(a, b):
    return lax.dot_general(a, b, (((1,), (1,)), ((), ())), preferred_element_type=F32)


def _dot_tn(a, b):
    return lax.dot_general(a, b, (((0,), (0,)), ((), ())), preferred_element_type=F32)


def _const_spec(shape):
    nd = len(shape)
    return pl.BlockSpec(shape, lambda *_: (0,) * nd)


def _mesh_pos():
    x, y, c = lax.axis_index("x"), lax.axis_index("y"), lax.axis_index("c")
    return x, y, c, 4 * x + 2 * y + c


def _peer(x, y, c, k):
    px = 1 - x if (k >> 2) & 1 else x
    py = 1 - y if (k >> 1) & 1 else y
    pc = 1 - c if k & 1 else c
    return (px, py, pc), 4 * px + 2 * py + pc


def _gather_project(x, norm_g, w_shard_t, pack):
    tokens = x.shape[0]
    tt = min(512, tokens // 2)
    n_tok = tokens // tt
    rc = min(128, tt)

    def body(x_hbm, g_ref, ws_hbm, pack_hbm, proj_hbm, h_hbm, wfull_hbm, packfull_hbm,
             w_vmem, h_vmem, x_buf, o_buf, send_sems, recv_sems, local_sems, x_sems, o_sems):
        x_, y_, c_, me = _mesh_pos()
        myself, sibling = (x_, y_, c_), (x_, y_, 1 - c_)
        chips = ((1 - x_, y_), (x_, 1 - y_), (1 - x_, 1 - y_))

        def shard(ref, idx):
            return ref.at[pl.ds(pl.multiple_of(idx * SHARD_IN, 64), SHARD_IN)]

        def copy(a, k, idx, to, own=False):
            if a == 0:
                src, dst = ws_hbm if own else shard(w_vmem, idx), shard(w_vmem, idx)
            else:
                src, dst = pack_hbm if own else packfull_hbm.at[idx], packfull_hbm.at[idx]
            return pltpu.make_async_remote_copy(src_ref=src, dst_ref=dst, send_sem=send_sems.at[a, k],
                                                recv_sem=recv_sems.at[a, k], device_id=to, device_id_type=MESH)

        own_w = pltpu.make_async_copy(ws_hbm, shard(w_vmem, me), local_sems.at[0])
        own_p = pltpu.make_async_copy(pack_hbm, packfull_hbm.at[me], local_sems.at[1])
        own_w.start()
        own_p.start()
        sent = []
        for a in range(2):
            sent.append(copy(a, 0, me, sibling, own=True))
            sent += [copy(a, 1 + r, me, (*chip, c_), own=True) for r, chip in enumerate(chips)]
        for cp in sent:
            cp.start()

        def x_copy(t, slot):
            return pltpu.make_async_copy(x_hbm.at[pl.ds(t * tt, tt)], x_buf.at[slot], x_sems.at[slot])

        x_copy(0, 0).start()
        for t in range(n_tok):
            slot = t % 2
            if t + 1 < n_tok:
                x_copy(t + 1, 1 - slot).start()
            x_copy(t, slot).wait()

            def chunk(r0, t=t, slot=slot):
                xv = x_buf[slot, pl.ds(r0, rc), :]
                r = lax.rsqrt(jnp.mean(xv * xv, axis=-1, keepdims=True) + RMS_EPS)
                h_vmem[pl.ds(t * tt + r0, rc), :] = (xv * r * g_ref[...]).astype(BF16)
            _row_chunks(tt, rc, chunk)
        h_out = pltpu.make_async_copy(h_vmem, h_hbm, local_sems.at[6])
        h_out.start()
        local = [own_p, h_out]

        def project_unit(q, u):
            rows = pl.ds(pl.multiple_of(q * UNIT, LANES), UNIT)
            w_out = pltpu.make_async_copy(w_vmem.at[rows], wfull_hbm.at[rows], local_sems.at[2 + u])
            w_out.start()
            local.append(w_out)

            def o_copy(slot, t):
                return pltpu.make_async_copy(
                    o_buf.at[slot], proj_hbm.at[pl.ds(pl.multiple_of(t * tt, tt), tt), rows], o_sems.at[slot])

            def tile(t, carry):
                slot = lax.rem(t, 2)

                @pl.when(t >= 2)
                def _():
                    o_copy(slot, t).wait()
                o_buf[slot] = _dot_nt(h_vmem[pl.ds(pl.multiple_of(t * tt, tt), tt), :], w_vmem[rows, :]).astype(BF16)
                o_copy(slot, t).start()
                return carry
            lax.fori_loop(0, n_tok, tile, 0)
            o_copy(0, 0).wait()
            o_copy(1, 0).wait()

        def dev(chip, core):
            return 4 * chip[0] + 2 * chip[1] + core

        def arrive_and_pass_on(a, r):
            copy(a, 1 + r, dev(chips[r], c_), myself).wait_recv()
            passed = copy(a, 4 + r, dev(chips[r], c_), sibling)
            passed.start()
            sent.append(passed)

        def passed_on_to_me(a, r):
            copy(a, 4 + r, dev(chips[r], 1 - c_), myself).wait_recv()

        own_w.wait()
        copy(0, 0, dev((x_, y_), 1 - c_), myself).wait_recv()
        project_unit(2 * x_ + y_, 0)
        arrive_and_pass_on(0, 0)
        arrive_and_pass_on(0, 1)
        passed_on_to_me(0, 0)
        project_unit(2 * chips[0][0] + chips[0][1], 1)
        arrive_and_pass_on(0, 2)
        passed_on_to_me(0, 1)
        project_unit(2 * chips[1][0] + chips[1][1], 2)
        passed_on_to_me(0, 2)
        project_unit(2 * chips[2][0] + chips[2][1], 3)
        for r in range(3):
            arrive_and_pass_on(1, r)
        copy(1, 0, dev((x_, y_), 1 - c_), myself).wait_recv()
        for r in range(3):
            passed_on_to_me(1, r)
        for cp in sent:
            cp.wait_send()
        for cp in local:
            cp.wait()

    hbm = pl.BlockSpec(memory_space=pltpu.HBM)
    return pl.pallas_call(
        body, name="gather_project",
        in_specs=[hbm, pl.BlockSpec(memory_space=pltpu.VMEM), hbm, hbm],
        out_specs=(hbm, hbm, hbm, hbm),
        out_shape=(jax.ShapeDtypeStruct((tokens, IN_WIDTH), BF16),
                   jax.ShapeDtypeStruct((tokens, D_MODEL), BF16),
                   jax.ShapeDtypeStruct((IN_WIDTH, D_MODEL), BF16),
                   jax.ShapeDtypeStruct((N_DEV, PACK_ROWS, D_MODEL), BF16)),
        scratch_shapes=[pltpu.VMEM((IN_WIDTH, D_MODEL), BF16),
                        pltpu.VMEM((tokens, D_MODEL), BF16),
                        pltpu.VMEM((2, tt, D_MODEL), F32),
                        pltpu.VMEM((2, tt, UNIT), BF16),
                        pltpu.SemaphoreType.DMA((2, N_DEV - 1)),
                        pltpu.SemaphoreType.DMA((2, N_DEV - 1)),
                        pltpu.SemaphoreType.DMA((7,)),
                        pltpu.SemaphoreType.DMA((2,)),
                        pltpu.SemaphoreType.DMA((2,))],
        compiler_params=_cparams(None, VMEM_LIMIT),
    )(x, norm_g, w_shard_t, pack)


HALF_ROWS = SHARD_IN + 3 * SHARD_SQ

GRAD_CHUNK = 384
_SECTION_ROWS = (OFF_A, OFF_B, OFF_CG, OFF_Q, OFF_KV, OFF_AG, OFF_MLC, OFF_MLA)
_SECTION_WIDTH = (1024, 1024, 1024, 1024, 512, 1024, 1024, 1024)


def _dproj_pieces(first, width):
    out = []
    for s, (start, w) in enumerate(zip(_SECTION_ROWS, _SECTION_WIDTH)):
        lo, hi = max(first, start), min(first + width, start + w)
        if lo < hi:
            out.append((s, lo - start, hi - lo, lo - first))
    return out


def _grad_exchange(sections, h_t, g_co, g_ao, g_out, g_conv, small):
    tokens = h_t.shape[1]
    n_chunk = UNIT // GRAD_CHUNK
    rc = 192

    def body(*refs):
        sec = refs[:8]
        (ht_hbm, gco_hbm, gao_hbm, gout_hbm, gconv_hbm, small_hbm, gmine_hbm, rconv_hbm, rsmall_hbm,
         lhs_buf, ht_vmem, halves, out_buf, stage, final,
         lhs_sems, ht_sem, tail_sems, d2d_send, d2d_recv, ici_send, ici_recv,
         tiny_send, tiny_recv, local_sems) = refs[8:]
        x_, y_, c_, me = _mesh_pos()
        myself, sibling = (x_, y_, c_), (x_, y_, 1 - c_)
        chips = ((1 - x_, 1 - y_), (1 - x_, y_), (x_, 1 - y_), (x_, y_))
        squares = (gco_hbm, gao_hbm, gout_hbm)

        def remote(src, dst, send_sem, recv_sem, to):
            return pltpu.make_async_remote_copy(src_ref=src, dst_ref=dst, send_sem=send_sem, recv_sem=recv_sem,
                                                device_id=to, device_id_type=MESH)

        own_tiny = [pltpu.make_async_copy(gconv_hbm.at[me], rconv_hbm.at[me], local_sems.at[0]),
                    pltpu.make_async_copy(small_hbm, rsmall_hbm.at[me], local_sems.at[1])]
        for cp in own_tiny:
            cp.start()
        tiny = []
        for k in range(1, N_DEV):
            peer, peer_idx = _peer(x_, y_, c_, k)
            tiny += [remote(gconv_hbm.at[peer_idx], rconv_hbm.at[me], tiny_send.at[0, k - 1], tiny_recv.at[0, k - 1], peer),
                     remote(small_hbm, rsmall_hbm.at[me], tiny_send.at[1, k - 1], tiny_recv.at[1, k - 1], peer)]
        for cp in tiny:
            cp.start()

        ht_in = pltpu.make_async_copy(ht_hbm, ht_vmem, ht_sem.at[0])
        ht_in.start()

        def fetch(q, j, slot, wait):
            for k in range(4):
                @pl.when(q == k)
                def _(k=k):
                    for n, (s, col, width, place) in enumerate(_dproj_pieces(k * UNIT + j * GRAD_CHUNK, GRAD_CHUNK)):
                        cp = pltpu.make_async_copy(sec[s].at[pl.ds(0, tokens), pl.ds(col, width)],
                                                   lhs_buf.at[slot, pl.ds(0, tokens), pl.ds(place, width)],
                                                   lhs_sems.at[slot, n])
                        cp.wait() if wait else cp.start()

        def d2d(u):
            return remote(halves.at[1 - c_], stage.at[u], d2d_send.at[u], d2d_recv.at[u], sibling)

        def ici(u):
            return remote(stage.at[u], final.at[u], ici_send.at[u], ici_recv.at[u], (*chips[u], c_))

        def tails(q):
            out = []
            for core in range(2):
                for n, g in enumerate(squares):
                    rows = pl.ds(pl.multiple_of((2 * q + core) * SHARD_SQ, SHARD_SQ), SHARD_SQ)
                    out.append(pltpu.make_async_copy(g.at[rows], halves.at[core, pl.ds(SHARD_IN + n * SHARD_SQ, SHARD_SQ)],
                                                     tail_sems.at[3 * core + n]))
            return out

        def chip_sum(u):
            d2d(u).wait_recv()

            def chunk(r0):
                rows = pl.ds(r0, rc)
                stage[u, rows, :] = (stage[u, rows, :].astype(F32) + halves[c_, rows, :].astype(F32)).astype(BF16)
            _row_chunks(HALF_ROWS, rc, chunk)
            if u < 3:
                ici(u).start()

        def chip_of(u):
            return 2 * chips[u][0] + chips[u][1]

        def store_rows(block, first):
            n = block.shape[0]
            for core in range(2):
                lo, hi = max(first, core * SHARD_IN), min(first + n, (core + 1) * SHARD_IN)
                if lo < hi:
                    halves[core, lo - core * SHARD_IN:hi - core * SHARD_IN, :] = block[lo - first:hi - first].astype(BF16)

        fetch(chip_of(0), 0, 0, wait=False)
        ht_in.wait()
        for u in range(4):
            q = chip_of(u)
            for j in range(n_chunk):
                slot = (u * n_chunk + j) % 2
                if j + 1 < n_chunk:
                    fetch(q, j + 1, 1 - slot, wait=False)
                elif u + 1 < 4:
                    fetch(chip_of(u + 1), 0, 1 - slot, wait=False)
                fetch(q, j, slot, wait=True)
                grad_t = _dot(ht_vmem[...], lhs_buf[slot])
                if j == 0:
                    if u > 0:
                        chip_sum(u - 1)
                        d2d(u - 1).wait_send()
                    for cp in tails(q):
                        cp.start()
                for r in range(GRAD_CHUNK // LANES):
                    store_rows(grad_t[:, LANES * r:LANES * (r + 1)].T, j * GRAD_CHUNK + LANES * r)
            for cp in tails(q):
                cp.wait()
            d2d(u).start()

        chip_sum(3)
        for u in range(3):
            remote(stage.at[u], final.at[u], ici_send.at[u], ici_recv.at[u], myself).wait_recv()

        def total(r0):
            rows = pl.ds(r0, rc)
            out_buf[rows, :] = ((stage[3, rows, :].astype(F32) + final[0, rows, :].astype(F32))
                                + final[1, rows, :].astype(F32)) + final[2, rows, :].astype(F32)
        _row_chunks(HALF_ROWS, rc, total)
        out = pltpu.make_async_copy(out_buf, gmine_hbm, local_sems.at[2])
        out.start()
        d2d(3).wait_send()
        for u in range(3):
            ici(u).wait_send()
        for k in range(1, N_DEV):
            peer, peer_idx = _peer(x_, y_, c_, k)
            remote(gconv_hbm.at[me], rconv_hbm.at[peer_idx], tiny_send.at[0, k - 1], tiny_recv.at[0, k - 1], myself).wait_recv()
            remote(small_hbm, rsmall_hbm.at[peer_idx], tiny_send.at[1, k - 1], tiny_recv.at[1, k - 1], myself).wait_recv()
        for cp in tiny:
            cp.wait_send()
        for cp in own_tiny:
            cp.wait()
        out.wait()

    hbm = pl.BlockSpec(memory_space=pltpu.HBM)
    return pl.pallas_call(
        body, name="grad_exchange",
        in_specs=[hbm] * 14, out_specs=(hbm, hbm, hbm),
        out_shape=(jax.ShapeDtypeStruct((HALF_ROWS, D_MODEL), F32),
                   jax.ShapeDtypeStruct((N_DEV, CONV_PAD, LANES), F32),
                   jax.ShapeDtypeStruct((N_DEV, 8, D_MODEL), F32)),
        scratch_shapes=[pltpu.VMEM((2, tokens, GRAD_CHUNK), BF16),
                        pltpu.VMEM((D_MODEL, tokens), BF16),
                        pltpu.VMEM((2, HALF_ROWS, D_MODEL), BF16),
                        pltpu.VMEM((HALF_ROWS, D_MODEL), F32),
                        pltpu.VMEM((4, HALF_ROWS, D_MODEL), BF16),
                        pltpu.VMEM((3, HALF_ROWS, D_MODEL), BF16),
                        pltpu.SemaphoreType.DMA((2, 3)),
                        pltpu.SemaphoreType.DMA((1,)),
                        pltpu.SemaphoreType.DMA((6,)),
                        pltpu.SemaphoreType.DMA((4,)),
                        pltpu.SemaphoreType.DMA((4,)),
                        pltpu.SemaphoreType.DMA((3,)),
                        pltpu.SemaphoreType.DMA((3,)),
                        pltpu.SemaphoreType.DMA((2, N_DEV - 1)),
                        pltpu.SemaphoreType.DMA((2, N_DEV - 1)),
                        pltpu.SemaphoreType.DMA((3,))],
        compiler_params=_cparams(None, 60 * 1024 * 1024),
    )(*sections, h_t, g_co, g_ao, g_out, g_conv, small)


def _row_chunks(total, size, fn):
    n = total // size
    if n == 1:
        fn(0)
        return

    def step(i, carry):
        fn(pl.multiple_of(i * size, size))
        return carry
    lax.fori_loop(0, n, step, 0)


def _rope_tables(tokens):
    inv_freq = ROPE_THETA ** (-jnp.arange(0, HEAD_DIM, 2, dtype=F32) / HEAD_DIM)
    ang = jnp.arange(tokens, dtype=jnp.int32).astype(F32)[:, None] * inv_freq[None, :]
    cos, sin = jnp.cos(ang), jnp.sin(ang)
    zero = jnp.zeros_like(sin)
    cos_t = jnp.tile(jnp.concatenate([cos, cos], axis=1), (1, LANES // HEAD_DIM))
    sin_up = jnp.tile(jnp.concatenate([-sin, zero], axis=1), (1, LANES // HEAD_DIM))
    sin_dn = jnp.tile(jnp.concatenate([zero, sin], axis=1), (1, LANES // HEAD_DIM))
    return cos_t, sin_up, sin_dn


def _rope(t, cos_t, sin_up, sin_dn):
    return t * cos_t + pltpu.roll(t, LANES - 32, 1) * sin_up + pltpu.roll(t, 32, 1) * sin_dn


def _rope_transposed(g, cos_t, sin_up, sin_dn):
    return g * cos_t + pltpu.roll(g * sin_up, 32, 1) + pltpu.roll(g * sin_dn, LANES - 32, 1)


def _lane_halves():
    lane = lax.broadcasted_iota(jnp.int32, (BLOCK, LANES), 1)
    return lane < HEAD_DIM


def _rope_qkv(proj, cos_t, sin_up, sin_dn):
    tokens = proj.shape[0]
    tm = min(512, tokens)
    scale = HEAD_DIM ** -0.5

    def body(q_ref, kv_ref, cos_ref, up_ref, dn_ref, qr_ref, kd_ref, vd_ref):
        lo = _lane_halves()

        def chunk(r0):
            rows = pl.ds(r0, BLOCK)
            cs, up, dn = cos_ref[rows, :], up_ref[rows, :], dn_ref[rows, :]
            for p in range(D_MODEL // LANES):
                sl = slice(LANES * p, LANES * (p + 1))
                qt = q_ref[rows, sl].astype(F32)
                qr_ref[rows, sl] = (_rope(qt, cs, up, dn) * scale).astype(BF16)
            for p in range(2):
                sl = slice(LANES * p, LANES * (p + 1))
                kt = _rope(kv_ref[rows, sl].astype(F32), cs, up, dn)
                vt = kv_ref[rows, slice(256 + LANES * p, 256 + LANES * (p + 1))].astype(F32)
                for src, dst in ((kt, kd_ref), (vt, vd_ref)):
                    first = jnp.where(lo, src, 0.0)
                    second = src - first
                    dst[rows, slice(LANES * 2 * p, LANES * (2 * p + 1))] = (first + pltpu.roll(first, HEAD_DIM, 1)).astype(BF16)
                    dst[rows, slice(LANES * (2 * p + 1), LANES * (2 * p + 2))] = (second + pltpu.roll(second, HEAD_DIM, 1)).astype(BF16)
        _row_chunks(tm, BLOCK, chunk)

    tab = pl.BlockSpec((tm, LANES), lambda i: (i, 0))
    return pl.pallas_call(
        body, name="rope_qkv", grid=(tokens // tm,),
        in_specs=[pl.BlockSpec((tm, D_MODEL), lambda i: (i, COL_Q)),
                  pl.BlockSpec((tm, 512), lambda i: (i, COL512_KV)), tab, tab, tab],
        out_specs=(pl.BlockSpec((tm, D_MODEL), lambda i: (i, 0)),
                   pl.BlockSpec((tm, 512), lambda i: (i, 0)),
                   pl.BlockSpec((tm, 512), lambda i: (i, 0))),
        out_shape=(jax.ShapeDtypeStruct((tokens, D_MODEL), BF16),
                   jax.ShapeDtypeStruct((tokens, 512), BF16),
                   jax.ShapeDtypeStruct((tokens, 512), BF16)),
        compiler_params=_cparams(("parallel",)),
    )(proj, proj, cos_t, sin_up, sin_dn)


CONV_TM = 256
N_LANE_CHUNKS = D_MODEL // LANES


def _fill_u_ext(u_ext, a_ref, b_ref, ah_ref, bh_ref, first_tile):
    for lc in range(N_LANE_CHUNKS):
        sl = slice(LANES * lc, LANES * (lc + 1))
        uh = ah_ref[:, sl].astype(F32) * _sig(bh_ref[:, sl].astype(F32))
        u_ext[lc, 0:CONV_PAD, :] = jnp.where(first_tile, 0.0, uh)
        u_ext[lc, CONV_PAD:CONV_PAD + CONV_TM, :] = a_ref[:, sl].astype(F32) * _sig(b_ref[:, sl].astype(F32))


def _conv_forward(proj, conv_w, dw_b, ln_g, ln_b, w_co):
    tokens = proj.shape[0]
    tm = CONV_TM
    halo_blocks = tm // CONV_PAD

    def body(a_ref, b_ref, ah_ref, bh_ref, cg_ref, cw_ref, dwb_ref, lng_ref, lnb_ref, wco_ref,
             cv_ref, yc_ref, u_ext, cv_scr):
        _fill_u_ext(u_ext, a_ref, b_ref, ah_ref, bh_ref, pl.program_id(0) == 0)

        def lane_chunk(lc, carry):
            for rc in range(tm // 64):
                acc = jnp.zeros((64, LANES), F32)
                for j in range(CONV_KERNEL):
                    acc = acc + cw_ref[lc, pl.ds(j, 1), :] * u_ext[lc, pl.ds(64 * rc + 2 + j, 64), :]
                cv_scr[lc, pl.ds(64 * rc, 64), :] = acc
            return carry
        lax.fori_loop(0, N_LANE_CHUNKS, lane_chunk, 0)

        cv = jnp.concatenate([cv_scr[lc] for lc in range(N_LANE_CHUNKS)], axis=1) + dwb_ref[...]
        cv_ref[...] = cv
        mu = jnp.mean(cv, axis=-1, keepdims=True)
        zc = cv - mu
        rstd = lax.rsqrt(jnp.mean(zc * zc, axis=-1, keepdims=True) + LN_EPS)
        ln = zc * rstd * lng_ref[...] + lnb_ref[...]
        cg = cg_ref[...].astype(F32)
        pc = (ln * _sig(ln)) * (cg * _sig(cg))
        yc_ref[...] = _dot(pc.astype(BF16), wco_ref[...]).astype(BF16)

    def halo_map(i):
        return (jnp.maximum(i * halo_blocks - 1, 0), 0)

    tile = lambda col: pl.BlockSpec((tm, D_MODEL), lambda i: (i, col))
    return pl.pallas_call(
        body, name="conv_forward", grid=(tokens // tm,),
        in_specs=[tile(COL_A), tile(COL_B),
                  pl.BlockSpec((CONV_PAD, D_MODEL), lambda i: (halo_map(i)[0], COL_A)),
                  pl.BlockSpec((CONV_PAD, D_MODEL), lambda i: (halo_map(i)[0], COL_B)),
                  tile(COL_CG), _const_spec((N_DEV, CONV_PAD, LANES)),
                  _const_spec((1, D_MODEL)), _const_spec((1, D_MODEL)), _const_spec((1, D_MODEL)),
                  _const_spec((D_MODEL, D_MODEL))],
        out_specs=(pl.BlockSpec((tm, D_MODEL), lambda i: (i, 0)),
                   pl.BlockSpec((tm, D_MODEL), lambda i: (i, 0))),
        out_shape=(jax.ShapeDtypeStruct((tokens, D_MODEL), F32),
                   jax.ShapeDtypeStruct((tokens, D_MODEL), BF16)),
        scratch_shapes=[pltpu.VMEM((N_LANE_CHUNKS, CONV_PAD + tm, LANES), F32),
                        pltpu.VMEM((N_LANE_CHUNKS, tm, LANES), F32)],
        compiler_params=_cparams(("parallel",), VMEM_LIMIT),
    )(proj, proj, proj, proj, proj, conv_w, dw_b, ln_g, ln_b, w_co)


def _band_masks(n):
    row = lax.broadcasted_iota(jnp.int32, (4 * BLOCK, BLOCK), 0) & (BLOCK - 1)
    col = lax.broadcasted_iota(jnp.int32, (4 * BLOCK, BLOCK), 1)
    return col <= row, jnp.logical_and(col > row, n > 0)


def _stack_heads(tile_a, tile_b, lo):
    zero = jnp.zeros_like(tile_a)
    return jnp.concatenate([jnp.where(lo, tile_a, zero), jnp.where(lo, zero, tile_a),
                            jnp.where(lo, tile_b, zero), jnp.where(lo, zero, tile_b)], axis=0)


def _unstack_heads(stacked, lo):
    s = [stacked[BLOCK * g:BLOCK * (g + 1)] for g in range(4)]
    return (jnp.where(lo, s[0], 0.0) + jnp.where(lo, 0.0, s[1]),
            jnp.where(lo, s[2], 0.0) + jnp.where(lo, 0.0, s[3]))


def _sink_column(sinks_ref, kvh):
    return jnp.concatenate([jnp.full((BLOCK, 1), sinks_ref[0, 4 * kvh + g], F32) for g in range(4)], axis=0)


def _softmax_band(q_stack, k_cur, k_prev, sink, cur_ok, prev_ok):
    s_c = jnp.where(cur_ok, _dot_nt(q_stack, k_cur), NEG)
    s_p = jnp.where(prev_ok, _dot_nt(q_stack, k_prev), NEG)
    m = jnp.maximum(jnp.maximum(jnp.max(s_c, axis=1, keepdims=True), jnp.max(s_p, axis=1, keepdims=True)), sink)
    e_c, e_p, e_s = jnp.exp(s_c - m), jnp.exp(s_p - m), jnp.exp(sink - m)
    inv = 1.0 / (jnp.sum(e_c, axis=1, keepdims=True) + jnp.sum(e_p, axis=1, keepdims=True) + e_s)
    return e_c * inv, e_p * inv, e_s * inv


def _attention_forward(qr, kd, vd, proj, sinks, w_ao):
    tokens = qr.shape[0]
    tm = min(512, tokens)
    per_tile = tm // BLOCK

    def body(q_ref, kc_ref, kp_ref, vc_ref, vp_ref, ag0_ref, ag1_ref, sinks_ref, wao_ref, o_ref, ya_ref,
             k_ext, v_ext, o_scr):
        i = pl.program_id(0)
        lo = _lane_halves()
        k_ext[0:BLOCK, :], k_ext[BLOCK:BLOCK + tm, :] = kp_ref[...], kc_ref[...]
        v_ext[0:BLOCK, :], v_ext[BLOCK:BLOCK + tm, :] = vp_ref[...], vc_ref[...]

        def block(b, carry):
            r0 = pl.multiple_of(b * BLOCK, BLOCK)
            rows, before = pl.ds(r0 + BLOCK, BLOCK), pl.ds(r0, BLOCK)
            cur_ok, prev_ok = _band_masks(i * per_tile + b)
            for kvh in range(N_KV_HEADS):
                ta, tb = slice(LANES * 2 * kvh, LANES * (2 * kvh + 1)), slice(LANES * (2 * kvh + 1), LANES * (2 * kvh + 2))
                ks = slice(LANES * kvh, LANES * (kvh + 1))
                q_stack = _stack_heads(q_ref[pl.ds(r0, BLOCK), ta], q_ref[pl.ds(r0, BLOCK), tb], lo)
                p_c, p_p, _ = _softmax_band(q_stack, k_ext[rows, ks], k_ext[before, ks],
                                            _sink_column(sinks_ref, kvh), cur_ok, prev_ok)
                o_stack = _dot(p_c.astype(BF16), v_ext[rows, ks]) + _dot(p_p.astype(BF16), v_ext[before, ks])
                o_scr[pl.ds(r0, BLOCK), ta], o_scr[pl.ds(r0, BLOCK), tb] = _unstack_heads(o_stack, lo)
            return carry
        lax.fori_loop(0, per_tile, block, 0)
        o = o_scr[...]
        o_ref[...] = o.astype(BF16)
        ag = jnp.concatenate([ag0_ref[...], ag1_ref[...]], axis=1).astype(F32)
        ya_ref[...] = _dot((o * (ag * _sig(ag))).astype(BF16), wao_ref[...]).astype(BF16)

    cur = lambda w, col=0: pl.BlockSpec((tm, w), lambda i: (i, col))
    prev = lambda w: pl.BlockSpec((BLOCK, w), lambda i: (jnp.maximum(i * per_tile - 1, 0), 0))
    return pl.pallas_call(
        body, name="attention_forward", grid=(tokens // tm,),
        in_specs=[cur(D_MODEL), cur(512), prev(512), cur(512), prev(512),
                  cur(512, COL512_AG), cur(512, COL512_AG + 1),
                  pl.BlockSpec(memory_space=pltpu.SMEM), _const_spec((D_MODEL, D_MODEL))],
        out_specs=(cur(D_MODEL), cur(D_MODEL)),
        out_shape=(jax.ShapeDtypeStruct((tokens, D_MODEL), BF16),
                   jax.ShapeDtypeStruct((tokens, D_MODEL), BF16)),
        scratch_shapes=[pltpu.VMEM((BLOCK + tm, 512), BF16), pltpu.VMEM((BLOCK + tm, 512), BF16),
                        pltpu.VMEM((tm, D_MODEL), F32)],
        compiler_params=_cparams(("parallel",), VMEM_LIMIT),
    )(qr, kd, kd, vd, vd, proj, proj, sinks, w_ao)


def _merge_and_head(yc, ya, proj, x, target, w_out, final_g):
    tokens = x.shape[0]
    tm = 256
    last = tokens // tm - 1

    def body(yc_ref, ya_ref, mlc0_ref, mlc1_ref, mla0_ref, mla1_ref, x_ref, t_ref, wout_ref, fg_ref,
             dx2_ref, dyc_ref, dya_ref, dmlc_ref, dmla_ref, gwout_ref, part_ref, gacc):
        i = pl.program_id(0)

        @pl.when(i == 0)
        def _():
            gacc[...] = jnp.zeros_like(gacc)
            part_ref[...] = jnp.zeros_like(part_ref)

        yc, ya = yc_ref[...].astype(F32), ya_ref[...].astype(F32)
        gc = _sig(jnp.concatenate([mlc0_ref[...], mlc1_ref[...]], axis=1).astype(F32))
        ga = _sig(jnp.concatenate([mla0_ref[...], mla1_ref[...]], axis=1).astype(F32))
        merged = (gc * yc + ga * ya).astype(BF16)
        x2 = x_ref[...] + _dot(merged, wout_ref[...])
        r2 = lax.rsqrt(jnp.mean(x2 * x2, axis=-1, keepdims=True) + RMS_EPS)
        x2n = x2 * r2
        fg = fg_ref[...]
        err = x2n * fg - t_ref[...]
        dy = err * (1.0 / D_MODEL)
        part_ref[0:1, :] += jnp.sum(dy * x2n, axis=0, keepdims=True)
        part_ref[1:2, :] += jnp.sum(err * err, axis=0, keepdims=True) * (0.5 / D_MODEL)
        dx2n = dy * fg
        dx2 = r2 * (dx2n - x2n * jnp.mean(dx2n * x2n, axis=-1, keepdims=True))
        dx2_ref[...] = dx2
        dx2b = dx2.astype(BF16)
        gacc[...] += _dot_tn(merged, dx2b)
        dm = _dot_nt(dx2b, wout_ref[...])
        dyc_ref[...] = (dm * gc).astype(BF16)
        dya_ref[...] = (dm * ga).astype(BF16)
        dmlc_ref[...] = (dm * yc * (gc * (1.0 - gc))).astype(BF16)
        dmla_ref[...] = (dm * ya * (ga * (1.0 - ga))).astype(BF16)

        @pl.when(i == last)
        def _():
            gwout_ref[...] = gacc[...].astype(BF16)

    tile = lambda col=0: pl.BlockSpec((tm, D_MODEL), lambda i: (i, col))
    half = lambda col: pl.BlockSpec((tm, 512), lambda i: (i, col))
    return pl.pallas_call(
        body, name="merge_and_head", grid=(tokens // tm,),
        in_specs=[tile(), tile(), half(COL512_MLC), half(COL512_MLC + 1), half(COL512_MLA), half(COL512_MLA + 1),
                  tile(), tile(), _const_spec((D_MODEL, D_MODEL)), _const_spec((1, D_MODEL))],
        out_specs=(tile(), tile(), tile(), tile(), tile(),
                   _const_spec((D_MODEL, D_MODEL)), _const_spec((8, D_MODEL))),
        out_shape=(jax.ShapeDtypeStruct((tokens, D_MODEL), F32),
                   jax.ShapeDtypeStruct((tokens, D_MODEL), BF16),
                   jax.ShapeDtypeStruct((tokens, D_MODEL), BF16),
                   jax.ShapeDtypeStruct((tokens, D_MODEL), BF16),
                   jax.ShapeDtypeStruct((tokens, D_MODEL), BF16),
                   jax.ShapeDtypeStruct((D_MODEL, D_MODEL), BF16),
                   jax.ShapeDtypeStruct((8, D_MODEL), F32)),
        scratch_shapes=[pltpu.VMEM((D_MODEL, D_MODEL), F32)],
        compiler_params=_cparams(("arbitrary",), VMEM_LIMIT),
    )(yc, ya, proj, proj, proj, proj, x, target, w_out, final_g)


def _conv_backward_pointwise(dyc, cv, proj, w_co, ln_g, ln_b):
    tokens = cv.shape[0]
    tm = 256
    last = tokens // tm - 1

    def body(dyc_ref, cv_ref, cg_ref, wco_ref, lng_ref, lnb_ref, dcv_ref, dcg_ref, gwco_ref, part_ref, gacc):
        i = pl.program_id(0)

        @pl.when(i == 0)
        def _():
            gacc[...] = jnp.zeros_like(gacc)
            part_ref[...] = jnp.zeros_like(part_ref)

        cv = cv_ref[...]
        mu = jnp.mean(cv, axis=-1, keepdims=True)
        zc = cv - mu
        rstd = lax.rsqrt(jnp.mean(zc * zc, axis=-1, keepdims=True) + LN_EPS)
        z = zc * rstd
        lng = lng_ref[...]
        ln = z * lng + lnb_ref[...]
        sl = _sig(ln)
        c = ln * sl
        cg = cg_ref[...].astype(F32)
        scg = _sig(cg)
        gate = cg * scg
        dyc = dyc_ref[...]
        gacc[...] += _dot_tn((c * gate).astype(BF16), dyc)
        dpc = _dot_nt(dyc, wco_ref[...])
        dcg_ref[...] = (dpc * c * (scg * (1.0 + cg * (1.0 - scg)))).astype(BF16)
        dln = dpc * gate * (sl * (1.0 + ln * (1.0 - sl)))
        part_ref[0:1, :] += jnp.sum(dln * z, axis=0, keepdims=True)
        part_ref[1:2, :] += jnp.sum(dln, axis=0, keepdims=True)
        dz = dln * lng
        dcv = rstd * (dz - jnp.mean(dz, axis=-1, keepdims=True) - z * jnp.mean(dz * z, axis=-1, keepdims=True))
        part_ref[2:3, :] += jnp.sum(dcv, axis=0, keepdims=True)
        dcv_ref[...] = dcv

        @pl.when(i == last)
        def _():
            gwco_ref[...] = gacc[...].astype(BF16)

    tile = lambda col=0: pl.BlockSpec((tm, D_MODEL), lambda i: (i, col))
    return pl.pallas_call(
        body, name="conv_backward_pointwise", grid=(tokens // tm,),
        in_specs=[tile(), tile(), tile(COL_CG), _const_spec((D_MODEL, D_MODEL)),
                  _const_spec((1, D_MODEL)), _const_spec((1, D_MODEL))],
        out_specs=(tile(), tile(), _const_spec((D_MODEL, D_MODEL)), _const_spec((8, D_MODEL))),
        out_shape=(jax.ShapeDtypeStruct((tokens, D_MODEL), F32),
                   jax.ShapeDtypeStruct((tokens, D_MODEL), BF16),
                   jax.ShapeDtypeStruct((D_MODEL, D_MODEL), BF16),
                   jax.ShapeDtypeStruct((8, D_MODEL), F32)),
        scratch_shapes=[pltpu.VMEM((D_MODEL, D_MODEL), F32)],
        compiler_params=_cparams(("arbitrary",), VMEM_LIMIT),
    )(dyc, cv, proj, w_co, ln_g, ln_b)


def _conv_backward_taps(dcv, proj, conv_w):
    tokens = dcv.shape[0]
    tm = CONV_TM
    nt = tokens // tm
    halo_blocks = tm // CONV_PAD

    def body(d_ref, dn_ref, a_ref, b_ref, ah_ref, bh_ref, cw_ref, da_ref, db_ref, gw_ref, u_ext, d_ext, du_scr, gw_acc):
        i = pl.program_id(0)

        @pl.when(i == 0)
        def _():
            gw_acc[...] = jnp.zeros_like(gw_acc)

        _fill_u_ext(u_ext, a_ref, b_ref, ah_ref, bh_ref, i == 0)
        for lc in range(N_LANE_CHUNKS):
            sl = slice(LANES * lc, LANES * (lc + 1))
            d_ext[lc, 0:tm, :] = d_ref[:, sl]
            d_ext[lc, tm:tm + CONV_PAD, :] = jnp.where(i == nt - 1, 0.0, dn_ref[:, sl])

        def lane_chunk(lc, carry):
            n_rc = tm // 64
            du = [jnp.zeros((64, LANES), F32) for _ in range(n_rc)]
            for j in range(CONV_KERNEL):
                w = cw_ref[lc, pl.ds(j, 1), :]
                gsum = jnp.zeros((8, LANES), F32)
                for rc in range(n_rc):
                    du[rc] = du[rc] + w * d_ext[lc, pl.ds(64 * rc + 30 - j, 64), :]
                    prod = d_ext[lc, pl.ds(64 * rc, 64), :] * u_ext[lc, pl.ds(64 * rc + 2 + j, 64), :]
                    gsum = gsum + jnp.sum(prod.reshape(8, 8, LANES), axis=0)
                gw_acc[lc, j] += gsum
            for rc in range(n_rc):
                du_scr[lc, pl.ds(64 * rc, 64), :] = du[rc]
            return carry
        lax.fori_loop(0, N_LANE_CHUNKS, lane_chunk, 0)

        du = jnp.concatenate([du_scr[lc] for lc in range(N_LANE_CHUNKS)], axis=1)
        a, b = a_ref[...].astype(F32), b_ref[...].astype(F32)
        sb = _sig(b)
        da_ref[...] = (du * sb).astype(BF16)
        db_ref[...] = (du * a * (sb * (1.0 - sb))).astype(BF16)

        @pl.when(i == nt - 1)
        def _():
            gw_ref[...] = jnp.sum(gw_acc[...], axis=2)

    def prev_halo(i):
        return jnp.maximum(i * halo_blocks - 1, 0)

    def next_halo(i):
        return jnp.minimum((i + 1) * halo_blocks, tokens // CONV_PAD - 1)

    tile = lambda col=0: pl.BlockSpec((tm, D_MODEL), lambda i: (i, col))
    return pl.pallas_call(
        body, name="conv_backward_taps", grid=(nt,),
        in_specs=[tile(), pl.BlockSpec((CONV_PAD, D_MODEL), lambda i: (next_halo(i), 0)),
                  tile(COL_A), tile(COL_B),
                  pl.BlockSpec((CONV_PAD, D_MODEL), lambda i: (prev_halo(i), COL_A)),
                  pl.BlockSpec((CONV_PAD, D_MODEL), lambda i: (prev_halo(i), COL_B)),
                  _const_spec((N_DEV, CONV_PAD, LANES))],
        out_specs=(tile(), tile(), _const_spec((N_DEV, CONV_PAD, LANES))),
        out_shape=(jax.ShapeDtypeStruct((tokens, D_MODEL), BF16),
                   jax.ShapeDtypeStruct((tokens, D_MODEL), BF16),
                   jax.ShapeDtypeStruct((N_DEV, CONV_PAD, LANES), F32)),
        scratch_shapes=[pltpu.VMEM((N_LANE_CHUNKS, CONV_PAD + tm, LANES), F32),
                        pltpu.VMEM((N_LANE_CHUNKS, tm + CONV_PAD, LANES), F32),
                        pltpu.VMEM((N_LANE_CHUNKS, tm, LANES), F32),
                        pltpu.VMEM((N_LANE_CHUNKS, CONV_PAD, 8, LANES), F32)],
        compiler_params=_cparams(("arbitrary",), VMEM_LIMIT),
    )(dcv, dcv, proj, proj, proj, proj, conv_w)


def _fold_kv_head(dup, lo, second_half):
    both = dup + pltpu.roll(dup, HEAD_DIM, 1)
    return jnp.where(lo, 0.0, both) if second_half else jnp.where(lo, both, 0.0)


def _attention_backward(dya, o, qr, kd, vd, proj, sinks, w_ao, cos_t, sin_up, sin_dn):
    tokens = qr.shape[0]
    tm = min(512, tokens)
    per_tile = tm // BLOCK
    nt = tokens // tm
    scale = HEAD_DIM ** -0.5

    def body(dya_ref, o_ref, ag0_ref, ag1_ref, q_ref, kc_ref, kp_ref, vc_ref, vp_ref, sinks_ref, wao_ref,
             cos_c, up_c, dn_c, cos_p, up_p, dn_p,
             dq_ref, dkv_ref, dag_ref, gwao_ref, gsink_ref,
             gacc, k_ext, v_ext, dk_ext, dv_ext, dk_carry, dv_carry, do_scr, dq_scr):
        i = pl.program_id(0)
        lo = _lane_halves()

        @pl.when(i == 0)
        def _():
            gacc[...] = jnp.zeros_like(gacc)
            gsink_ref[...] = jnp.zeros_like(gsink_ref)
            dk_carry[...] = jnp.zeros_like(dk_carry)
            dv_carry[...] = jnp.zeros_like(dv_carry)
        dk_ext[...] = jnp.zeros_like(dk_ext)
        dv_ext[...] = jnp.zeros_like(dv_ext)

        @pl.when(i < nt)
        def _():
            dya = dya_ref[...]
            dpa = _dot_nt(dya, wao_ref[...])
            o = o_ref[...].astype(F32)
            ag = jnp.concatenate([ag0_ref[...], ag1_ref[...]], axis=1).astype(F32)
            sg = _sig(ag)
            gate = ag * sg
            gacc[...] += _dot_tn((o * gate).astype(BF16), dya)
            dag_ref[...] = (dpa * o * (sg * (1.0 + ag * (1.0 - sg)))).astype(BF16)
            do_scr[...] = (dpa * gate).astype(BF16)
            k_ext[0:BLOCK, :], k_ext[BLOCK:BLOCK + tm, :] = kp_ref[...], kc_ref[...]
            v_ext[0:BLOCK, :], v_ext[BLOCK:BLOCK + tm, :] = vp_ref[...], vc_ref[...]

            def block(b, carry):
                r0 = pl.multiple_of(b * BLOCK, BLOCK)
                mine, rows, before = pl.ds(r0, BLOCK), pl.ds(r0 + BLOCK, BLOCK), pl.ds(r0, BLOCK)
                cur_ok, prev_ok = _band_masks(i * per_tile + b)
                head_lane = lax.broadcasted_iota(jnp.int32, (1, LANES), 1)
                gsink = jnp.zeros((1, LANES), F32)
                zero_tile = jnp.zeros((BLOCK, LANES), F32)
                dk_cur, dk_prev = [zero_tile, zero_tile], [zero_tile, zero_tile]
                dv_cur, dv_prev = [zero_tile, zero_tile], [zero_tile, zero_tile]
                for kvh in range(N_KV_HEADS):
                    ta, tb = slice(LANES * 2 * kvh, LANES * (2 * kvh + 1)), slice(LANES * (2 * kvh + 1), LANES * (2 * kvh + 2))
                    ks = slice(LANES * kvh, LANES * (kvh + 1))
                    q_stack = _stack_heads(q_ref[mine, ta], q_ref[mine, tb], lo)
                    do_stack = _stack_heads(do_scr[mine, ta], do_scr[mine, tb], lo)
                    k_c, k_p, v_c, v_p = k_ext[rows, ks], k_ext[before, ks], v_ext[rows, ks], v_ext[before, ks]
                    p_c, p_p, p_s = _softmax_band(q_stack, k_c, k_p, _sink_column(sinks_ref, kvh), cur_ok, prev_ok)
                    dp_c, dp_p = _dot_nt(do_stack, v_c), _dot_nt(do_stack, v_p)
                    delta = jnp.sum(p_c * dp_c, axis=1, keepdims=True) + jnp.sum(p_p * dp_p, axis=1, keepdims=True)
                    ds_c = (p_c * (dp_c - delta)).astype(BF16)
                    ds_p = (p_p * (dp_p - delta)).astype(BF16)
                    sink_terms = p_s * delta
                    for g in range(4):
                        total = jnp.sum(sink_terms[BLOCK * g:BLOCK * (g + 1)], axis=0, keepdims=True)
                        gsink = gsink - jnp.where(head_lane == 4 * kvh + g, total, 0.0)
                    dq_stack = _dot(ds_c, k_c) + _dot(ds_p, k_p)
                    dq_scr[mine, ta], dq_scr[mine, tb] = _unstack_heads(dq_stack, lo)
                    tile, second = kvh // 2, kvh % 2 == 1
                    dk_cur[tile] = dk_cur[tile] + _fold_kv_head(_dot_tn(ds_c, q_stack), lo, second)
                    dk_prev[tile] = dk_prev[tile] + _fold_kv_head(_dot_tn(ds_p, q_stack), lo, second)
                    dv_cur[tile] = dv_cur[tile] + _fold_kv_head(_dot_tn(p_c.astype(BF16), do_stack), lo, second)
                    dv_prev[tile] = dv_prev[tile] + _fold_kv_head(_dot_tn(p_p.astype(BF16), do_stack), lo, second)
                gsink_ref[0:1, :] += gsink
                cs, up, dn = cos_c[mine, :], up_c[mine, :], dn_c[mine, :]
                for p in range(D_MODEL // LANES):
                    sl = slice(LANES * p, LANES * (p + 1))
                    dq_ref[mine, sl] = (_rope_transposed(dq_scr[mine, sl], cs, up, dn) * scale).astype(BF16)
                for p in range(2):
                    sl = slice(LANES * p, LANES * (p + 1))
                    dk_ext[before, sl] += dk_prev[p]
                    dk_ext[rows, sl] += dk_cur[p]
                    dv_ext[before, sl] += dv_prev[p]
                    dv_ext[rows, sl] += dv_cur[p]
                return carry
            lax.fori_loop(0, per_tile, block, 0)

        last = slice(tm - BLOCK, tm)
        dk_carry[last, :] += dk_ext[0:BLOCK, :]
        dv_carry[last, :] += dv_ext[0:BLOCK, :]
        for p in range(2):
            sl = slice(LANES * p, LANES * (p + 1))
            dkv_ref[:, sl] = _rope_transposed(dk_carry[:, sl], cos_p[...], up_p[...], dn_p[...]).astype(BF16)
            dkv_ref[:, slice(256 + LANES * p, 256 + LANES * (p + 1))] = dv_carry[:, sl].astype(BF16)
        dk_carry[...] = dk_ext[BLOCK:BLOCK + tm, :]
        dv_carry[...] = dv_ext[BLOCK:BLOCK + tm, :]

        @pl.when(i == nt)
        def _():
            gwao_ref[...] = gacc[...].astype(BF16)

    def cur_idx(i):
        return jnp.minimum(i, nt - 1)

    def prev_idx(i):
        return jnp.clip(i - 1, 0, nt - 1)

    cur = lambda w, col=0: pl.BlockSpec((tm, w), lambda i: (cur_idx(i), col))
    prev = lambda w: pl.BlockSpec((tm, w), lambda i: (prev_idx(i), 0))
    before = lambda w: pl.BlockSpec((BLOCK, w), lambda i: (jnp.maximum(cur_idx(i) * per_tile - 1, 0), 0))
    return pl.pallas_call(
        body, name="attention_backward", grid=(nt + 1,),
        in_specs=[cur(D_MODEL), cur(D_MODEL), cur(512, COL512_AG), cur(512, COL512_AG + 1), cur(D_MODEL),
                  cur(512), before(512), cur(512), before(512),
                  pl.BlockSpec(memory_space=pltpu.SMEM), _const_spec((D_MODEL, D_MODEL)),
                  cur(LANES), cur(LANES), cur(LANES), prev(LANES), prev(LANES), prev(LANES)],
        out_specs=(cur(D_MODEL), prev(512), cur(D_MODEL),
                   _const_spec((D_MODEL, D_MODEL)), _const_spec((8, LANES))),
        out_shape=(jax.ShapeDtypeStruct((tokens, D_MODEL), BF16),
                   jax.ShapeDtypeStruct((tokens, 512), BF16),
                   jax.ShapeDtypeStruct((tokens, D_MODEL), BF16),
                   jax.ShapeDtypeStruct((D_MODEL, D_MODEL), BF16),
                   jax.ShapeDtypeStruct((8, LANES), F32)),
        scratch_shapes=[pltpu.VMEM((D_MODEL, D_MODEL), F32),
                        pltpu.VMEM((BLOCK + tm, 512), BF16), pltpu.VMEM((BLOCK + tm, 512), BF16),
                        pltpu.VMEM((BLOCK + tm, 256), F32), pltpu.VMEM((BLOCK + tm, 256), F32),
                        pltpu.VMEM((tm, 256), F32), pltpu.VMEM((tm, 256), F32),
                        pltpu.VMEM((tm, D_MODEL), BF16), pltpu.VMEM((tm, D_MODEL), F32)],
        compiler_params=_cparams(("arbitrary",), VMEM_LIMIT),
    )(dya, o, proj, proj, qr, kd, kd, vd, vd, sinks, w_ao, cos_t, sin_up, sin_dn, cos_t, sin_up, sin_dn)


def _transpose_tokens(h):
    tokens = h.shape[0]
    tt = min(512, tokens)

    def body(h_ref, out_ref):
        out_ref[...] = h_ref[...].astype(F32).T.astype(BF16)

    return pl.pallas_call(
        body, name="transpose_tokens", grid=(tokens // tt,),
        in_specs=[pl.BlockSpec((tt, D_MODEL), lambda i: (i, 0))],
        out_specs=pl.BlockSpec((D_MODEL, tt), lambda i: (0, i)),
        out_shape=jax.ShapeDtypeStruct((D_MODEL, tokens), BF16),
        compiler_params=_cparams(("parallel",)),
    )(h)


def _input_backward(sections, w_in_t, x, dx2, norm_g):
    tokens = x.shape[0]
    tm = 256

    def body(*refs):
        sec = refs[:8]
        w_ref, x_ref, dx2_ref, g_ref, gx_ref, part_ref = refs[8:]

        @pl.when(pl.program_id(0) == 0)
        def _():
            part_ref[...] = jnp.zeros_like(part_ref)

        dh = jnp.zeros((tm, D_MODEL), F32)
        for s in range(8):
            dh = dh + _dot(sec[s][...], w_ref[_SECTION_ROWS[s]:_SECTION_ROWS[s] + _SECTION_WIDTH[s], :])
        xv = x_ref[...]
        r = lax.rsqrt(jnp.mean(xv * xv, axis=-1, keepdims=True) + RMS_EPS)
        xn = xv * r
        part_ref[0:1, :] += jnp.sum(dh * xn, axis=0, keepdims=True)
        dxn = dh * g_ref[...]
        gx_ref[...] = dx2_ref[...] + r * (dxn - xn * jnp.mean(dxn * xn, axis=-1, keepdims=True))

    tile = lambda w=D_MODEL: pl.BlockSpec((tm, w), lambda i: (i, 0))
    return pl.pallas_call(
        body, name="input_backward", grid=(tokens // tm,),
        in_specs=[tile(w) for w in _SECTION_WIDTH] + [
            pl.BlockSpec((IN_WIDTH, D_MODEL), lambda i: (0, 0), pipeline_mode=pl.Buffered(1)),
            tile(), tile(), _const_spec((1, D_MODEL))],
        out_specs=(tile(), _const_spec((8, D_MODEL))),
        out_shape=(jax.ShapeDtypeStruct((tokens, D_MODEL), F32),
                   jax.ShapeDtypeStruct((8, D_MODEL), F32)),
        compiler_params=_cparams(("arbitrary",), VMEM_LIMIT),
    )(*sections, w_in_t, x, dx2, norm_g)


def _adamw_math(w, g, m, v):
    m = ADAM_B1 * m + (1.0 - ADAM_B1) * g
    v = ADAM_B2 * v + (1.0 - ADAM_B2) * (g * g)
    m_hat = m / (1.0 - ADAM_B1 ** ADAM_STEP)
    v_hat = v / (1.0 - ADAM_B2 ** ADAM_STEP)
    delta = -ADAM_LR * (m_hat / (jnp.sqrt(v_hat) + ADAM_EPS) + ADAM_WD * w)
    return delta, m, v


def _sum_slots(recv_ref):
    total = recv_ref[0].astype(F32)
    for d in range(1, N_DEV):
        total = total + recv_ref[d].astype(F32)
    return total


def _adamw(name, w, g, m, v, tile_rows):
    rows, cols = w.shape

    def body(w_ref, g_ref, m_ref, v_ref, d_ref, nm_ref, nv_ref):
        d_ref[...], nm_ref[...], nv_ref[...] = _adamw_math(w_ref[...], g_ref[...], m_ref[...], v_ref[...])

    spec = pl.BlockSpec((tile_rows, cols), lambda i: (i, 0))
    shape = jax.ShapeDtypeStruct((rows, cols), F32)
    return pl.pallas_call(
        body, name=name, grid=(rows // tile_rows,),
        in_specs=[spec] * 4, out_specs=(spec,) * 3, out_shape=(shape,) * 3,
        compiler_params=_cparams(("parallel",)),
    )(w, g, m, v)


def _sum_adamw(name, recv, w, m, v):
    def body(recv_ref, w_ref, m_ref, v_ref, g_ref, d_ref, nm_ref, nv_ref):
        g = _sum_slots(recv_ref)
        g_ref[...] = g
        d_ref[...], nm_ref[...], nv_ref[...] = _adamw_math(w_ref[...], g, m_ref[...], v_ref[...])

    shape = jax.ShapeDtypeStruct(w.shape, F32)
    return pl.pallas_call(body, name=name, out_shape=(shape,) * 4)(recv, w, m, v)


def _pad_rows(a, rows):
    return jnp.concatenate([a, jnp.zeros((rows - a.shape[0],) + a.shape[1:], a.dtype)], axis=0)


def kernel(x, norm_g, w_in, conv_dw_w, conv_dw_b, conv_ln_g, conv_ln_b, w_conv_out, attn_sinks, w_attn_out, w_out, final_norm_g, loss_target, m_norm_g, m_w_in, m_conv_dw_w, m_conv_dw_b, m_conv_ln_g, m_conv_ln_b, m_w_conv_out, m_attn_sinks, m_w_attn_out, m_w_out, m_final_norm_g, v_norm_g, v_w_in, v_conv_dw_w, v_conv_dw_b, v_conv_ln_g, v_conv_ln_b, v_w_conv_out, v_attn_sinks, v_w_attn_out, v_w_out, v_final_norm_g):
    xs, target = x[0], loss_target[0]
    tokens = xs.shape[0]
    fg_row = final_norm_g.reshape(1, D_MODEL)

    taps_bits = lax.bitcast_convert_type(_pad_rows(conv_dw_w[0], CONV_PAD), BF16).reshape(8, D_MODEL)
    pack = jnp.concatenate([w_conv_out[0].astype(BF16), w_attn_out[0].astype(BF16), w_out[0].astype(BF16),
                            jnp.pad(taps_bits, ((0, PACK_ROWS - 3 * SHARD_SQ - 8), (0, 0)))], axis=0)
    w_in_t32 = w_in[0].T
    proj, h, w_in_t, pack_full = _gather_project(xs, norm_g, w_in_t32.astype(BF16), pack)
    w_co = pack_full[:, 0:SHARD_SQ].reshape(D_MODEL, D_MODEL)
    w_ao = pack_full[:, SHARD_SQ:2 * SHARD_SQ].reshape(D_MODEL, D_MODEL)
    w_o = pack_full[:, 2 * SHARD_SQ:3 * SHARD_SQ].reshape(D_MODEL, D_MODEL)
    conv_w = lax.bitcast_convert_type(
        pack_full[:, 3 * SHARD_SQ:3 * SHARD_SQ + 8].reshape(N_DEV, CONV_PAD, LANES, 2), F32)

    cos_t, sin_up, sin_dn = _rope_tables(tokens)
    qr, kd, vd = _rope_qkv(proj, cos_t, sin_up, sin_dn)
    cv, yc = _conv_forward(proj, conv_w, conv_dw_b, conv_ln_g, conv_ln_b, w_co)
    o, ya = _attention_forward(qr, kd, vd, proj, attn_sinks, w_ao)

    dx2, dyc, dya, dmlc, dmla, g_out, part_head = _merge_and_head(yc, ya, proj, xs, target, w_o, fg_row)
    dcv, dcg, g_co, part_conv = _conv_backward_pointwise(dyc, cv, proj, w_co, conv_ln_g, conv_ln_b)
    da, db, g_conv = _conv_backward_taps(dcv, proj, conv_w)
    dq, dkv, dag, g_ao, part_sink = _attention_backward(dya, o, qr, kd, vd, proj, attn_sinks, w_ao, cos_t, sin_up, sin_dn)
    sections = (da, db, dcg, dq, dkv, dag, dmlc, dmla)
    grad_x, part_in = _input_backward(sections, w_in_t, xs, dx2, norm_g)

    small = jnp.concatenate([
        part_in[0:1], part_conv[2:3], part_conv[0:1], part_conv[1:2], part_head[0:1],
        jnp.pad(part_sink[0:1], ((0, 0), (0, D_MODEL - LANES))), part_head[1:2],
        jnp.zeros((1, D_MODEL), F32)], axis=0)

    g_mine, r_conv, r_small = _grad_exchange(sections, _transpose_tokens(h), g_co, g_ao, g_out, g_conv, small)

    g_in_t = g_mine[:SHARD_IN]
    w_in_res = _adamw("adamw_w_in", w_in_t32, g_in_t, m_w_in[0].T, v_w_in[0].T, 192)
    grad_w_in, d_w_in, nm_w_in, nv_w_in = (a.T for a in (g_in_t,) + tuple(w_in_res))
    sq = {}
    for j, (nm, w, m, v) in enumerate((("w_conv_out", w_conv_out, m_w_conv_out, v_w_conv_out),
                                       ("w_attn_out", w_attn_out, m_w_attn_out, v_w_attn_out),
                                       ("w_out", w_out, m_w_out, v_w_out))):
        g = g_mine[SHARD_IN + j * SHARD_SQ:SHARD_IN + (j + 1) * SHARD_SQ]
        sq[nm] = (g,) + tuple(_adamw("adamw_" + nm, w[0], g, m[0], v[0], SHARD_SQ))
    conv_res = _sum_adamw("sum_adamw_conv_dw_w", r_conv.reshape(N_DEV, CONV_PAD, LANES),
                          _pad_rows(conv_dw_w[0], CONV_PAD), _pad_rows(m_conv_dw_w[0], CONV_PAD),
                          _pad_rows(v_conv_dw_w[0], CONV_PAD))
    pad_sink = lambda a: jnp.pad(a, ((0, 0), (0, D_MODEL - N_Q_HEADS)))
    zero_rows = jnp.zeros((2, D_MODEL), F32)
    stack = lambda a, b, c, d, e, f: jnp.concatenate([a, b, c, d, e.reshape(1, D_MODEL), pad_sink(f), zero_rows], axis=0)
    small_res = _sum_adamw(
        "sum_adamw_small", r_small,
        stack(norm_g, conv_dw_b, conv_ln_g, conv_ln_b, final_norm_g, attn_sinks),
        stack(m_norm_g, m_conv_dw_b, m_conv_ln_g, m_conv_ln_b, m_final_norm_g, m_attn_sinks),
        stack(v_norm_g, v_conv_dw_b, v_conv_ln_g, v_conv_ln_b, v_final_norm_g, v_attn_sinks))
    loss = jnp.sum(small_res[0][6])

    def leaf(k):
        s = small_res[k]
        return (s[0:1], (grad_w_in, d_w_in, nm_w_in, nv_w_in)[k][None], conv_res[k][None, :CONV_KERNEL],
                s[1:2], s[2:3], s[3:4], sq["w_conv_out"][k][None], s[5:6, :N_Q_HEADS],
                sq["w_attn_out"][k][None], sq["w_out"][k][None], s[4])

    return (loss, grad_x[None], *leaf(0), *leaf(1), *leaf(2), *leaf(3))
```

```python
import jax
import jax.numpy as jnp
from jax import lax
from jax.experimental import pallas as pl
from jax.experimental.pallas import tpu as pltpu

F32 = jnp.float32
BF16 = jnp.bfloat16
MESH = pl.DeviceIdType.MESH

D_MODEL = 1024
IN_WIDTH = 7680
N_DEV = 8
SHARD_IN = IN_WIDTH // N_DEV
SHARD_SQ = D_MODEL // N_DEV
CONV_KERNEL = 31
CONV_PAD = 32
HEAD_DIM = 64
N_Q_HEADS = 16
N_KV_HEADS = 4
BLOCK = 128
LANES = 128
ROPE_THETA = 10000.0
RMS_EPS = 1e-5
LN_EPS = 1e-5
NEG = -1e30
ADAM_LR = 0.001
ADAM_B1 = 0.9
ADAM_B2 = 0.999
ADAM_EPS = 1e-08
ADAM_WD = 0.01
ADAM_STEP = 10

OFF_A, OFF_B, OFF_CG, OFF_Q, OFF_KV, OFF_AG, OFF_MLC, OFF_MLA = 0, 1024, 2048, 3072, 4096, 4608, 5632, 6656
COL_A, COL_B, COL_CG, COL_Q = 0, 1, 2, 3
COL512_KV, COL512_AG, COL512_MLC, COL512_MLA = 8, 9, 11, 13
UNIT = 2 * SHARD_IN
PACK_ROWS = 400

VMEM_LIMIT = 56 * 1024 * 1024


def _cparams(sem=None, vmem=None):
    return pltpu.CompilerParams(dimension_semantics=sem, vmem_limit_bytes=vmem)


def _sig(v):
    return 1.0 / (1.0 + jnp.exp(-v))


def _dot(a, b):
    return jnp.dot(a, b, preferred_element_type=F32)


def _dot_nt(a, b):
    return lax.dot_general(a, b, (((1,), (1,)), ((), ())), preferred_element_type=F32)


def _dot_tn(a, b):
    return lax.dot_general(a, b, (((0,), (0,)), ((), ())), preferred_element_type=F32)


def _const_spec(shape):
    nd = len(shape)
    return pl.BlockSpec(shape, lambda *_: (0,) * nd)


def _mesh_pos():
    x, y, c = lax.axis_index("x"), lax.axis_index("y"), lax.axis_index("c")
    return x, y, c, 4 * x + 2 * y + c


def _peer(x, y, c, k):
    px = 1 - x if (k >> 2) & 1 else x
    py = 1 - y if (k >> 1) & 1 else y
    pc = 1 - c if k & 1 else c
    return (px, py, pc), 4 * px + 2 * py + pc


def _gather_project(x, norm_g, w_shard_t, pack):
    tokens = x.shape[0]
    tt = min(512, tokens // 2)
    n_tok = tokens // tt
    rc = min(128, tt)

    def body(x_hbm, g_ref, ws_hbm, pack_hbm, proj_hbm, h_hbm, wfull_hbm, packfull_hbm,
             w_vmem, h_vmem, x_buf, o_buf, send_sems, recv_sems, local_sems, x_sems, o_sems):
        x_, y_, c_, me = _mesh_pos()
        myself, sibling = (x_, y_, c_), (x_, y_, 1 - c_)
        chips = ((1 - x_, y_), (x_, 1 - y_), (1 - x_, 1 - y_))

        def shard(ref, idx):
            return ref.at[pl.ds(pl.multiple_of(idx * SHARD_IN, 64), SHARD_IN)]

        def copy(a, k, idx, to, own=False):
            if a == 0:
                src, dst = ws_hbm if own else shard(w_vmem, idx), shard(w_vmem, idx)
            else:
                src, dst = pack_hbm if own else packfull_hbm.at[idx], packfull_hbm.at[idx]
            return pltpu.make_async_remote_copy(src_ref=src, dst_ref=dst, send_sem=send_sems.at[a, k],
                                                recv_sem=recv_sems.at[a, k], device_id=to, device_id_type=MESH)

        own_w = pltpu.make_async_copy(ws_hbm, shard(w_vmem, me), local_sems.at[0])
        own_p = pltpu.make_async_copy(pack_hbm, packfull_hbm.at[me], local_sems.at[1])
        own_w.start()
        own_p.start()
        sent = []
        for a in range(2):
            sent.append(copy(a, 0, me, sibling, own=True))
            sent += [copy(a, 1 + r, me, (*chip, c_), own=True) for r, chip in enumerate(chips)]
        for cp in sent:
            cp.start()

        def x_copy(t, slot):
            return pltpu.make_async_copy(x_hbm.at[pl.ds(t * tt, tt)], x_buf.at[slot], x_sems.at[slot])

        x_copy(0, 0).start()
        for t in range(n_tok):
            slot = t % 2
            if t + 1 < n_tok:
                x_copy(t + 1, 1 - slot).start()
            x_copy(t, slot).wait()

            def chunk(r0, t=t, slot=slot):
                xv = x_buf[slot, pl.ds(r0, rc), :]
                r = lax.rsqrt(jnp.mean(xv * xv, axis=-1, keepdims=True) + RMS_EPS)
                h_vmem[pl.ds(t * tt + r0, rc), :] = (xv * r * g_ref[...]).astype(BF16)
            _row_chunks(tt, rc, chunk)
        h_out = pltpu.make_async_copy(h_vmem, h_hbm, local_sems.at[6])
        h_out.start()
        local = [own_p, h_out]

        def project_unit(q, u):
            rows = pl.ds(pl.multiple_of(q * UNIT, LANES), UNIT)
            w_out = pltpu.make_async_copy(w_vmem.at[rows], wfull_hbm.at[rows], local_sems.at[2 + u])
            w_out.start()
            local.append(w_out)

            def o_copy(slot, t):
                return pltpu.make_async_copy(
                    o_buf.at[slot], proj_hbm.at[pl.ds(pl.multiple_of(t * tt, tt), tt), rows], o_sems.at[slot])

            def tile(t, carry):
                slot = lax.rem(t, 2)

                @pl.when(t >= 2)
                def _():
                    o_copy(slot, t).wait()
                o_buf[slot] = _dot_nt(h_vmem[pl.ds(pl.multiple_of(t * tt, tt), tt), :], w_vmem[rows, :]).astype(BF16)
                o_copy(slot, t).start()
                return carry
            lax.fori_loop(0, n_tok, tile, 0)
            o_copy(0, 0).wait()
            o_copy(1, 0).wait()

        def dev(chip, core):
            return 4 * chip[0] + 2 * chip[1] + core

        def arrive_and_pass_on(a, r):
            copy(a, 1 + r, dev(chips[r], c_), myself).wait_recv()
            passed = copy(a, 4 + r, dev(chips[r], c_), sibling)
            passed.start()
            sent.append(passed)

        def passed_on_to_me(a, r):
            copy(a, 4 + r, dev(chips[r], 1 - c_), myself).wait_recv()

        own_w.wait()
        copy(0, 0, dev((x_, y_), 1 - c_), myself).wait_recv()
        project_unit(2 * x_ + y_, 0)
        arrive_and_pass_on(0, 0)
        arrive_and_pass_on(0, 1)
        passed_on_to_me(0, 0)
        project_unit(2 * chips[0][0] + chips[0][1], 1)
        arrive_and_pass_on(0, 2)
        passed_on_to_me(0, 1)
        project_unit(2 * chips[1][0] + chips[1][1], 2)
        passed_on_to_me(0, 2)
        project_unit(2 * chips[2][0] + chips[2][1], 3)
        for r in range(3):
            arrive_and_pass_on(1, r)
        copy(1, 0, dev((x_, y_), 1 - c_), myself).wait_recv()
        for r in range(3):
            passed_on_to_me(1, r)
        for cp in sent:
            cp.wait_send()
        for cp in local:
            cp.wait()

    hbm = pl.BlockSpec(memory_space=pltpu.HBM)
    return pl.pallas_call(
        body, name="gather_project",
        in_specs=[hbm, pl.BlockSpec(memory_space=pltpu.VMEM), hbm, hbm],
        out_specs=(hbm, hbm, hbm, hbm),
        out_shape=(jax.ShapeDtypeStruct((tokens, IN_WIDTH), BF16),
                   jax.ShapeDtypeStruct((tokens, D_MODEL), BF16),
                   jax.ShapeDtypeStruct((IN_WIDTH, D_MODEL), BF16),
                   jax.ShapeDtypeStruct((N_DEV, PACK_ROWS, D_MODEL), BF16)),
        scratch_shapes=[pltpu.VMEM((IN_WIDTH, D_MODEL), BF16),
                        pltpu.VMEM((tokens, D_MODEL), BF16),
                        pltpu.VMEM((2, tt, D_MODEL), F32),
                        pltpu.VMEM((2, tt, UNIT), BF16),
                        pltpu.SemaphoreType.DMA((2, N_DEV - 1)),
                        pltpu.SemaphoreType.DMA((2, N_DEV - 1)),
                        pltpu.SemaphoreType.DMA((7,)),
                        pltpu.SemaphoreType.DMA((2,)),
                        pltpu.SemaphoreType.DMA((2,))],
        compiler_params=_cparams(None, VMEM_LIMIT),
    )(x, norm_g, w_shard_t, pack)


HALF_ROWS = SHARD_IN + 3 * SHARD_SQ

GRAD_CHUNK = 384
_SECTION_ROWS = (OFF_A, OFF_B, OFF_CG, OFF_Q, OFF_KV, OFF_AG, OFF_MLC, OFF_MLA)
_SECTION_WIDTH = (1024, 1024, 1024, 1024, 512, 1024, 1024, 1024)


def _dproj_pieces(first, width):
    out = []
    for s, (start, w) in enumerate(zip(_SECTION_ROWS, _SECTION_WIDTH)):
        lo, hi = max(first, start), min(first + width, start + w)
        if lo < hi:
            out.append((s, lo - start, hi - lo, lo - first))
    return out


def _grad_exchange(sections, h_t, g_co, g_ao, g_out, g_conv, small):
    tokens = h_t.shape[1]
    n_chunk = UNIT // GRAD_CHUNK
    rc = 192

    def body(*refs):
        sec = refs[:8]
        (ht_hbm, gco_hbm, gao_hbm, gout_hbm, gconv_hbm, small_hbm, gmine_hbm, rconv_hbm, rsmall_hbm,
         lhs_buf, ht_vmem, halves, out_buf, stage, final,
         lhs_sems, ht_sem, tail_sems, d2d_send, d2d_recv, ici_send, ici_recv,
         tiny_send, tiny_recv, local_sems) = refs[8:]
        x_, y_, c_, me = _mesh_pos()
        myself, sibling = (x_, y_, c_), (x_, y_, 1 - c_)
        chips = ((1 - x_, 1 - y_), (1 - x_, y_), (x_, 1 - y_), (x_, y_))
        squares = (gco_hbm, gao_hbm, gout_hbm)

        def remote(src, dst, send_sem, recv_sem, to):
            return pltpu.make_async_remote_copy(src_ref=src, dst_ref=dst, send_sem=send_sem, recv_sem=recv_sem,
                                                device_id=to, device_id_type=MESH)

        own_tiny = [pltpu.make_async_copy(gconv_hbm.at[me], rconv_hbm.at[me], local_sems.at[0]),
                    pltpu.make_async_copy(small_hbm, rsmall_hbm.at[me], local_sems.at[1])]
        for cp in own_tiny:
            cp.start()
        tiny = []
        for k in range(1, N_DEV):
            peer, peer_idx = _peer(x_, y_, c_, k)
            tiny += [remote(gconv_hbm.at[peer_idx], rconv_hbm.at[me], tiny_send.at[0, k - 1], tiny_recv.at[0, k - 1], peer),
                     remote(small_hbm, rsmall_hbm.at[me], tiny_send.at[1, k - 1], tiny_recv.at[1, k - 1], peer)]
        for cp in tiny:
            cp.start()

        ht_in = pltpu.make_async_copy(ht_hbm, ht_vmem, ht_sem.at[0])
        ht_in.start()

        def fetch(q, j, slot, wait):
            for k in range(4):
                @pl.when(q == k)
                def _(k=k):
                    for n, (s, col, width, place) in enumerate(_dproj_pieces(k * UNIT + j * GRAD_CHUNK, GRAD_CHUNK)):
                        cp = pltpu.make_async_copy(sec[s].at[pl.ds(0, tokens), pl.ds(col, width)],
                                                   lhs_buf.at[slot, pl.ds(0, tokens), pl.ds(place, width)],
                                                   lhs_sems.at[slot, n])
                        cp.wait() if wait else cp.start()

        def d2d(u):
            return remote(halves.at[1 - c_], stage.at[u], d2d_send.at[u], d2d_recv.at[u], sibling)

        def ici(u):
            return remote(stage.at[u], final.at[u], ici_send.at[u], ici_recv.at[u], (*chips[u], c_))

        def tails(q):
            out = []
            for core in range(2):
                for n, g in enumerate(squares):
                    rows = pl.ds(pl.multiple_of((2 * q + core) * SHARD_SQ, SHARD_SQ), SHARD_SQ)
                    out.append(pltpu.make_async_copy(g.at[rows], halves.at[core, pl.ds(SHARD_IN + n * SHARD_SQ, SHARD_SQ)],
                                                     tail_sems.at[3 * core + n]))
            return out

        def chip_sum(u):
            d2d(u).wait_recv()

            def chunk(r0):
                rows = pl.ds(r0, rc)
                stage[u, rows, :] = (stage[u, rows, :].astype(F32) + halves[c_, rows, :].astype(F32)).astype(BF16)
            _row_chunks(HALF_ROWS, rc, chunk)
            if u < 3:
                ici(u).start()

        def chip_of(u):
            return 2 * chips[u][0] + chips[u][1]

        def store_rows(block, first):
            n = block.shape[0]
            for core in range(2):
                lo, hi = max(first, core * SHARD_IN), min(first + n, (core + 1) * SHARD_IN)
                if lo < hi:
                    halves[core, lo - core * SHARD_IN:hi - core * SHARD_IN, :] = block[lo - first:hi - first].astype(BF16)

        fetch(chip_of(0), 0, 0, wait=False)
        ht_in.wait()
        for u in range(4):
            q = chip_of(u)
            for j in range(n_chunk):
                slot = (u * n_chunk + j) % 2
                if j + 1 < n_chunk:
                    fetch(q, j + 1, 1 - slot, wait=False)
                elif u + 1 < 4:
                    fetch(chip_of(u + 1), 0, 1 - slot, wait=False)
                fetch(q, j, slot, wait=True)
                grad_t = _dot(ht_vmem[...], lhs_buf[slot])
                if j == 0:
                    if u > 0:
                        chip_sum(u - 1)
                        d2d(u - 1).wait_send()
                    for cp in tails(q):
                        cp.start()
                for r in range(GRAD_CHUNK // LANES):
                    store_rows(grad_t[:, LANES * r:LANES * (r + 1)].T, j * GRAD_CHUNK + LANES * r)
            for cp in tails(q):
                cp.wait()
            d2d(u).start()

        chip_sum(3)
        for u in range(3):
            remote(stage.at[u], final.at[u], ici_send.at[u], ici_recv.at[u], myself).wait_recv()

        def total(r0):
            rows = pl.ds(r0, rc)
            out_buf[rows, :] = ((stage[3, rows, :].astype(F32) + final[0, rows, :].astype(F32))
                                + final[1, rows, :].astype(F32)) + final[2, rows, :].astype(F32)
        _row_chunks(HALF_ROWS, rc, total)
        out = pltpu.make_async_copy(out_buf, gmine_hbm, local_sems.at[2])
        out.start()
        d2d(3).wait_send()
        for u in range(3):
            ici(u).wait_send()
        for k in range(1, N_DEV):
            peer, peer_idx = _peer(x_, y_, c_, k)
            remote(gconv_hbm.at[me], rconv_hbm.at[peer_idx], tiny_send.at[0, k - 1], tiny_recv.at[0, k - 1], myself).wait_recv()
            remote(small_hbm, rsmall_hbm.at[peer_idx], tiny_send.at[1, k - 1], tiny_recv.at[1, k - 1], myself).wait_recv()
        for cp in tiny:
            cp.wait_send()
        for cp in own_tiny:
            cp.wait()
        out.wait()

    hbm = pl.BlockSpec(memory_space=pltpu.HBM)
    return pl.pallas_call(
        body, name="grad_exchange",
        in_specs=[hbm] * 14, out_specs=(hbm, hbm, hbm),
        out_shape=(jax.ShapeDtypeStruct((HALF_ROWS, D_MODEL), F32),
                   jax.ShapeDtypeStruct((N_DEV, CONV_PAD, LANES), F32),
                   jax.ShapeDtypeStruct((N_DEV, 8, D_MODEL), F32)),
        scratch_shapes=[pltpu.VMEM((2, tokens, GRAD_CHUNK), BF16),
                        pltpu.VMEM((D_MODEL, tokens), BF16),
                        pltpu.VMEM((2, HALF_ROWS, D_MODEL), BF16),
                        pltpu.VMEM((HALF_ROWS, D_MODEL), F32),
                        pltpu.VMEM((4, HALF_ROWS, D_MODEL), BF16),
                        pltpu.VMEM((3, HALF_ROWS, D_MODEL), BF16),
                        pltpu.SemaphoreType.DMA((2, 3)),
                        pltpu.SemaphoreType.DMA((1,)),
                        pltpu.SemaphoreType.DMA((6,)),
                        pltpu.SemaphoreType.DMA((4,)),
                        pltpu.SemaphoreType.DMA((4,)),
                        pltpu.SemaphoreType.DMA((3,)),
                        pltpu.SemaphoreType.DMA((3,)),
                        pltpu.SemaphoreType.DMA((2, N_DEV - 1)),
                        pltpu.SemaphoreType.DMA((2, N_DEV - 1)),
                        pltpu.SemaphoreType.DMA((3,))],
        compiler_params=_cparams(None, 60 * 1024 * 1024),
    )(*sections, h_t, g_co, g_ao, g_out, g_conv, small)


def _row_chunks(total, size, fn):
    n = total // size
    if n == 1:
        fn(0)
        return

    def step(i, carry):
        fn(pl.multiple_of(i * size, size))
        return carry
    lax.fori_loop(0, n, step, 0)


def _rope_tables(tokens):
    inv_freq = ROPE_THETA ** (-jnp.arange(0, HEAD_DIM, 2, dtype=F32) / HEAD_DIM)
    ang = jnp.arange(tokens, dtype=jnp.int32).astype(F32)[:, None] * inv_freq[None, :]
    cos, sin = jnp.cos(ang), jnp.sin(ang)
    zero = jnp.zeros_like(sin)
    cos_t = jnp.tile(jnp.concatenate([cos, cos], axis=1), (1, LANES // HEAD_DIM))
    sin_up = jnp.tile(jnp.concatenate([-sin, zero], axis=1), (1, LANES // HEAD_DIM))
    sin_dn = jnp.tile(jnp.concatenate([zero, sin], axis=1), (1, LANES // HEAD_DIM))
    return cos_t, sin_up, sin_dn


def _rope(t, cos_t, sin_up, sin_dn):
    return t * cos_t + pltpu.roll(t, LANES - 32, 1) * sin_up + pltpu.roll(t, 32, 1) * sin_dn


def _rope_transposed(g, cos_t, sin_up, sin_dn):
    return g * cos_t + pltpu.roll(g * sin_up, 32, 1) + pltpu.roll(g * sin_dn, LANES - 32, 1)


def _lane_halves():
    lane = lax.broadcasted_iota(jnp.int32, (BLOCK, LANES), 1)
    return lane < HEAD_DIM


def _rope_qkv(proj, cos_t, sin_up, sin_dn):
    tokens = proj.shape[0]
    tm = min(512, tokens)
    scale = HEAD_DIM ** -0.5

    def body(q_ref, kv_ref, cos_ref, up_ref, dn_ref, qr_ref, kd_ref, vd_ref):
        lo = _lane_halves()

        def chunk(r0):
            rows = pl.ds(r0, BLOCK)
            cs, up, dn = cos_ref[rows, :], up_ref[rows, :], dn_ref[rows, :]
            for p in range(D_MODEL // LANES):
                sl = slice(LANES * p, LANES * (p + 1))
                qt = q_ref[rows, sl].astype(F32)
                qr_ref[rows, sl] = (_rope(qt, cs, up, dn) * scale).astype(BF16)
            for p in range(2):
                sl = slice(LANES * p, LANES * (p + 1))
                kt = _rope(kv_ref[rows, sl].astype(F32), cs, up, dn)
                vt = kv_ref[rows, slice(256 + LANES * p, 256 + LANES * (p + 1))].astype(F32)
                for src, dst in ((kt, kd_ref), (vt, vd_ref)):
                    first = jnp.where(lo, src, 0.0)
                    second = src - first
                    dst[rows, slice(LANES * 2 * p, LANES * (2 * p + 1))] = (first + pltpu.roll(first, HEAD_DIM, 1)).astype(BF16)
                    dst[rows, slice(LANES * (2 * p + 1), LANES * (2 * p + 2))] = (second + pltpu.roll(second, HEAD_DIM, 1)).astype(BF16)
        _row_chunks(tm, BLOCK, chunk)

    tab = pl.BlockSpec((tm, LANES), lambda i: (i, 0))
    return pl.pallas_call(
        body, name="rope_qkv", grid=(tokens // tm,),
        in_specs=[pl.BlockSpec((tm, D_MODEL), lambda i: (i, COL_Q)),
                  pl.BlockSpec((tm, 512), lambda i: (i, COL512_KV)), tab, tab, tab],
        out_specs=(pl.BlockSpec((tm, D_MODEL), lambda i: (i, 0)),
                   pl.BlockSpec((tm, 512), lambda i: (i, 0)),
                   pl.BlockSpec((tm, 512), lambda i: (i, 0))),
        out_shape=(jax.ShapeDtypeStruct((tokens, D_MODEL), BF16),
                   jax.ShapeDtypeStruct((tokens, 512), BF16),
                   jax.ShapeDtypeStruct((tokens, 512), BF16)),
        compiler_params=_cparams(("parallel",)),
    )(proj, proj, cos_t, sin_up, sin_dn)


CONV_TM = 256
N_LANE_CHUNKS = D_MODEL // LANES


def _fill_u_ext(u_ext, a_ref, b_ref, ah_ref, bh_ref, first_tile):
    for lc in range(N_LANE_CHUNKS):
        sl = slice(LANES * lc, LANES * (lc + 1))
        uh = ah_ref[:, sl].astype(F32) * _sig(bh_ref[:, sl].astype(F32))
        u_ext[lc, 0:CONV_PAD, :] = jnp.where(first_tile, 0.0, uh)
        u_ext[lc, CONV_PAD:CONV_PAD + CONV_TM, :] = a_ref[:, sl].astype(F32) * _sig(b_ref[:, sl].astype(F32))


def _conv_forward(proj, conv_w, dw_b, ln_g, ln_b, w_co):
    tokens = proj.shape[0]
    tm = CONV_TM
    halo_blocks = tm // CONV_PAD

    def body(a_ref, b_ref, ah_ref, bh_ref, cg_ref, cw_ref, dwb_ref, lng_ref, lnb_ref, wco_ref,
             cv_ref, yc_ref, u_ext, cv_scr):
        _fill_u_ext(u_ext, a_ref, b_ref, ah_ref, bh_ref, pl.program_id(0) == 0)

        def lane_chunk(lc, carry):
            for rc in range(tm // 64):
                acc = jnp.zeros((64, LANES), F32)
                for j in range(CONV_KERNEL):
                    acc = acc + cw_ref[lc, pl.ds(j, 1), :] * u_ext[lc, pl.ds(64 * rc + 2 + j, 64), :]
                cv_scr[lc, pl.ds(64 * rc, 64), :] = acc
            return carry
        lax.fori_loop(0, N_LANE_CHUNKS, lane_chunk, 0)

        cv = jnp.concatenate([cv_scr[lc] for lc in range(N_LANE_CHUNKS)], axis=1) + dwb_ref[...]
        cv_ref[...] = cv
        mu = jnp.mean(cv, axis=-1, keepdims=True)
        zc = cv - mu
        rstd = lax.rsqrt(jnp.mean(zc * zc, axis=-1, keepdims=True) + LN_EPS)
        ln = zc * rstd * lng_ref[...] + lnb_ref[...]
        cg = cg_ref[...].astype(F32)
        pc = (ln * _sig(ln)) * (cg * _sig(cg))
        yc_ref[...] = _dot(pc.astype(BF16), wco_ref[...]).astype(BF16)

    def halo_map(i):
        return (jnp.maximum(i * halo_blocks - 1, 0), 0)

    tile = lambda col: pl.BlockSpec((tm, D_MODEL), lambda i: (i, col))
    return pl.pallas_call(
        body, name="conv_forward", grid=(tokens // tm,),
        in_specs=[tile(COL_A), tile(COL_B),
                  pl.BlockSpec((CONV_PAD, D_MODEL), lambda i: (halo_map(i)[0], COL_A)),
                  pl.BlockSpec((CONV_PAD, D_MODEL), lambda i: (halo_map(i)[0], COL_B)),
                  tile(COL_CG), _const_spec((N_DEV, CONV_PAD, LANES)),
                  _const_spec((1, D_MODEL)), _const_spec((1, D_MODEL)), _const_spec((1, D_MODEL)),
                  _const_spec((D_MODEL, D_MODEL))],
        out_specs=(pl.BlockSpec((tm, D_MODEL), lambda i: (i, 0)),
                   pl.BlockSpec((tm, D_MODEL), lambda i: (i, 0))),
        out_shape=(jax.ShapeDtypeStruct((tokens, D_MODEL), F32),
                   jax.ShapeDtypeStruct((tokens, D_MODEL), BF16)),
        scratch_shapes=[pltpu.VMEM((N_LANE_CHUNKS, CONV_PAD + tm, LANES), F32),
                        pltpu.VMEM((N_LANE_CHUNKS, tm, LANES), F32)],
        compiler_params=_cparams(("parallel",), VMEM_LIMIT),
    )(proj, proj, proj, proj, proj, conv_w, dw_b, ln_g, ln_b, w_co)


def _band_mask(n):
    row = lax.broadcasted_iota(jnp.int32, (4 * BLOCK, 2 * BLOCK), 0) & (BLOCK - 1)
    col = lax.broadcasted_iota(jnp.int32, (4 * BLOCK, 2 * BLOCK), 1)
    before = jnp.logical_and(jnp.logical_and(col < BLOCK, col > row), n > 0)
    return jnp.logical_or(before, jnp.logical_and(col >= BLOCK, col - BLOCK <= row))


def _stack_heads(tile_a, tile_b, lo):
    zero = jnp.zeros_like(tile_a)
    return jnp.concatenate([jnp.where(lo, tile_a, zero), jnp.where(lo, zero, tile_a),
                            jnp.where(lo, tile_b, zero), jnp.where(lo, zero, tile_b)], axis=0)


def _unstack_heads(stacked, lo):
    s = [stacked[BLOCK * g:BLOCK * (g + 1)] for g in range(4)]
    return (jnp.where(lo, s[0], 0.0) + jnp.where(lo, 0.0, s[1]),
            jnp.where(lo, s[2], 0.0) + jnp.where(lo, 0.0, s[3]))


def _band_exp(q_stack, k2, sinks_ref, kvh, mask):
    s = jnp.where(mask, _dot_nt(q_stack, k2), NEG)
    sink = jnp.concatenate([jnp.full((BLOCK, LANES), sinks_ref[0, 4 * kvh + g], F32) for g in range(4)], axis=0)
    m = jnp.max(jnp.maximum(s[:, :BLOCK], s[:, BLOCK:]), axis=1, keepdims=True)
    m = jnp.maximum(jnp.broadcast_to(m, (4 * BLOCK, LANES)), sink)
    return jnp.exp(s[:, :BLOCK] - m), jnp.exp(s[:, BLOCK:] - m), jnp.exp(sink - m)


def _row_sums(e_bf16):
    return _dot(e_bf16, jnp.ones((2 * BLOCK, LANES), BF16))


def _attention_forward(qr, kd, vd, proj, sinks, w_ao):
    tokens = qr.shape[0]
    tm = min(512, tokens)
    per_tile = tm // BLOCK

    def body(q_ref, kc_ref, kp_ref, vc_ref, vp_ref, ag0_ref, ag1_ref, sinks_ref, wao_ref, o_ref, ya_ref,
             k_ext, v_ext, o_scr):
        i = pl.program_id(0)
        lo = _lane_halves()
        k_ext[0:BLOCK, :], k_ext[BLOCK:BLOCK + tm, :] = kp_ref[...], kc_ref[...]
        v_ext[0:BLOCK, :], v_ext[BLOCK:BLOCK + tm, :] = vp_ref[...], vc_ref[...]

        def block(b, carry):
            r0 = pl.multiple_of(b * BLOCK, BLOCK)
            band = pl.ds(r0, 2 * BLOCK)
            mask = _band_mask(i * per_tile + b)
            for kvh in range(N_KV_HEADS):
                ta, tb = slice(LANES * 2 * kvh, LANES * (2 * kvh + 1)), slice(LANES * (2 * kvh + 1), LANES * (2 * kvh + 2))
                ks = slice(LANES * kvh, LANES * (kvh + 1))
                q_stack = _stack_heads(q_ref[pl.ds(r0, BLOCK), ta], q_ref[pl.ds(r0, BLOCK), tb], lo)
                e_p, e_c, e_s = _band_exp(q_stack, k_ext[band, ks], sinks_ref, kvh, mask)
                e = jnp.concatenate([e_p, e_c], axis=1).astype(BF16)
                o_stack = _dot(e, v_ext[band, ks]) / (_row_sums(e) + e_s)
                o_scr[pl.ds(r0, BLOCK), ta], o_scr[pl.ds(r0, BLOCK), tb] = _unstack_heads(o_stack, lo)
            return carry
        lax.fori_loop(0, per_tile, block, 0)
        o = o_scr[...]
        o_ref[...] = o.astype(BF16)
        ag = jnp.concatenate([ag0_ref[...], ag1_ref[...]], axis=1).astype(F32)
        ya_ref[...] = _dot((o * (ag * _sig(ag))).astype(BF16), wao_ref[...]).astype(BF16)

    cur = lambda w, col=0: pl.BlockSpec((tm, w), lambda i: (i, col))
    prev = lambda w: pl.BlockSpec((BLOCK, w), lambda i: (jnp.maximum(i * per_tile - 1, 0), 0))
    return pl.pallas_call(
        body, name="attention_forward", grid=(tokens // tm,),
        in_specs=[cur(D_MODEL), cur(512), prev(512), cur(512), prev(512),
                  cur(512, COL512_AG), cur(512, COL512_AG + 1),
                  pl.BlockSpec(memory_space=pltpu.SMEM), _const_spec((D_MODEL, D_MODEL))],
        out_specs=(cur(D_MODEL), cur(D_MODEL)),
        out_shape=(jax.ShapeDtypeStruct((tokens, D_MODEL), BF16),
                   jax.ShapeDtypeStruct((tokens, D_MODEL), BF16)),
        scratch_shapes=[pltpu.VMEM((BLOCK + tm, 512), BF16), pltpu.VMEM((BLOCK + tm, 512), BF16),
                        pltpu.VMEM((tm, D_MODEL), F32)],
        compiler_params=_cparams(("parallel",), VMEM_LIMIT),
    )(qr, kd, kd, vd, vd, proj, proj, sinks, w_ao)


def _merge_and_head(yc, ya, proj, x, target, w_out, final_g):
    tokens = x.shape[0]
    tm = 256
    last = tokens // tm - 1

    def body(yc_ref, ya_ref, mlc0_ref, mlc1_ref, mla0_ref, mla1_ref, x_ref, t_ref, wout_ref, fg_ref,
             dx2_ref, dyc_ref, dya_ref, dmlc_ref, dmla_ref, gwout_ref, part_ref, gacc):
        i = pl.program_id(0)

        @pl.when(i == 0)
        def _():
            gacc[...] = jnp.zeros_like(gacc)
            part_ref[...] = jnp.zeros_like(part_ref)

        yc, ya = yc_ref[...].astype(F32), ya_ref[...].astype(F32)
        gc = _sig(jnp.concatenate([mlc0_ref[...], mlc1_ref[...]], axis=1).astype(F32))
        ga = _sig(jnp.concatenate([mla0_ref[...], mla1_ref[...]], axis=1).astype(F32))
        merged = (gc * yc + ga * ya).astype(BF16)
        x2 = x_ref[...] + _dot(merged, wout_ref[...])
        r2 = lax.rsqrt(jnp.mean(x2 * x2, axis=-1, keepdims=True) + RMS_EPS)
        x2n = x2 * r2
        fg = fg_ref[...]
        err = x2n * fg - t_ref[...]
        dy = err * (1.0 / D_MODEL)
        part_ref[0:1, :] += jnp.sum(dy * x2n, axis=0, keepdims=True)
        part_ref[1:2, :] += jnp.sum(err * err, axis=0, keepdims=True) * (0.5 / D_MODEL)
        dx2n = dy * fg
        dx2 = r2 * (dx2n - x2n * jnp.mean(dx2n * x2n, axis=-1, keepdims=True))
        dx2_ref[...] = dx2
        dx2b = dx2.astype(BF16)
        gacc[...] += _dot_tn(merged, dx2b)
        dm = _dot_nt(dx2b, wout_ref[...])
        dyc_ref[...] = (dm * gc).astype(BF16)
        dya_ref[...] = (dm * ga).astype(BF16)
        dmlc_ref[...] = (dm * yc * (gc * (1.0 - gc))).astype(BF16)
        dmla_ref[...] = (dm * ya * (ga * (1.0 - ga))).astype(BF16)

        @pl.when(i == last)
        def _():
            gwout_ref[...] = gacc[...].astype(BF16)

    tile = lambda col=0: pl.BlockSpec((tm, D_MODEL), lambda i: (i, col))
    half = lambda col: pl.BlockSpec((tm, 512), lambda i: (i, col))
    return pl.pallas_call(
        body, name="merge_and_head", grid=(tokens // tm,),
        in_specs=[tile(), tile(), half(COL512_MLC), half(COL512_MLC + 1), half(COL512_MLA), half(COL512_MLA + 1),
                  tile(), tile(), _const_spec((D_MODEL, D_MODEL)), _const_spec((1, D_MODEL))],
        out_specs=(tile(), tile(), tile(), tile(), tile(),
                   _const_spec((D_MODEL, D_MODEL)), _const_spec((8, D_MODEL))),
        out_shape=(jax.ShapeDtypeStruct((tokens, D_MODEL), F32),
                   jax.ShapeDtypeStruct((tokens, D_MODEL), BF16),
                   jax.ShapeDtypeStruct((tokens, D_MODEL), BF16),
                   jax.ShapeDtypeStruct((tokens, D_MODEL), BF16),
                   jax.ShapeDtypeStruct((tokens, D_MODEL), BF16),
                   jax.ShapeDtypeStruct((D_MODEL, D_MODEL), BF16),
                   jax.ShapeDtypeStruct((8, D_MODEL), F32)),
        scratch_shapes=[pltpu.VMEM((D_MODEL, D_MODEL), F32)],
        compiler_params=_cparams(("arbitrary",), VMEM_LIMIT),
    )(yc, ya, proj, proj, proj, proj, x, target, w_out, final_g)


def _conv_backward_pointwise(dyc, cv, proj, w_co, ln_g, ln_b):
    tokens = cv.shape[0]
    tm = 256
    last = tokens // tm - 1

    def body(dyc_ref, cv_ref, cg_ref, wco_ref, lng_ref, lnb_ref, dcv_ref, dcg_ref, gwco_ref, part_ref, gacc):
        i = pl.program_id(0)

        @pl.when(i == 0)
        def _():
            gacc[...] = jnp.zeros_like(gacc)
            part_ref[...] = jnp.zeros_like(part_ref)

        cv = cv_ref[...]
        mu = jnp.mean(cv, axis=-1, keepdims=True)
        zc = cv - mu
        rstd = lax.rsqrt(jnp.mean(zc * zc, axis=-1, keepdims=True) + LN_EPS)
        z = zc * rstd
        lng = lng_ref[...]
        ln = z * lng + lnb_ref[...]
        sl = _sig(ln)
        c = ln * sl
        cg = cg_ref[...].astype(F32)
        scg = _sig(cg)
        gate = cg * scg
        dyc = dyc_ref[...]
        gacc[...] += _dot_tn((c * gate).astype(BF16), dyc)
        dpc = _dot_nt(dyc, wco_ref[...])
        dcg_ref[...] = (dpc * c * (scg * (1.0 + cg * (1.0 - scg)))).astype(BF16)
        dln = dpc * gate * (sl * (1.0 + ln * (1.0 - sl)))
        part_ref[0:1, :] += jnp.sum(dln * z, axis=0, keepdims=True)
        part_ref[1:2, :] += jnp.sum(dln, axis=0, keepdims=True)
        dz = dln * lng
        dcv = rstd * (dz - jnp.mean(dz, axis=-1, keepdims=True) - z * jnp.mean(dz * z, axis=-1, keepdims=True))
        part_ref[2:3, :] += jnp.sum(dcv, axis=0, keepdims=True)
        dcv_ref[...] = dcv

        @pl.when(i == last)
        def _():
            gwco_ref[...] = gacc[...].astype(BF16)

    tile = lambda col=0: pl.BlockSpec((tm, D_MODEL), lambda i: (i, col))
    return pl.pallas_call(
        body, name="conv_backward_pointwise", grid=(tokens // tm,),
        in_specs=[tile(), tile(), tile(COL_CG), _const_spec((D_MODEL, D_MODEL)),
                  _const_spec((1, D_MODEL)), _const_spec((1, D_MODEL))],
        out_specs=(tile(), tile(), _const_spec((D_MODEL, D_MODEL)), _const_spec((8, D_MODEL))),
        out_shape=(jax.ShapeDtypeStruct((tokens, D_MODEL), F32),
                   jax.ShapeDtypeStruct((tokens, D_MODEL), BF16),
                   jax.ShapeDtypeStruct((D_MODEL, D_MODEL), BF16),
                   jax.ShapeDtypeStruct((8, D_MODEL), F32)),
        scratch_shapes=[pltpu.VMEM((D_MODEL, D_MODEL), F32)],
        compiler_params=_cparams(("arbitrary",), VMEM_LIMIT),
    )(dyc, cv, proj, w_co, ln_g, ln_b)


def _conv_backward_taps(dcv, proj, conv_w):
    tokens = dcv.shape[0]
    tm = CONV_TM
    nt = tokens // tm
    halo_blocks = tm // CONV_PAD

    def body(d_ref, dn_ref, a_ref, b_ref, ah_ref, bh_ref, cw_ref, da_ref, db_ref, gw_ref, u_ext, d_ext, du_scr, gw_acc):
        i = pl.program_id(0)

        @pl.when(i == 0)
        def _():
            gw_acc[...] = jnp.zeros_like(gw_acc)

        _fill_u_ext(u_ext, a_ref, b_ref, ah_ref, bh_ref, i == 0)
        for lc in range(N_LANE_CHUNKS):
            sl = slice(LANES * lc, LANES * (lc + 1))
            d_ext[lc, 0:tm, :] = d_ref[:, sl]
            d_ext[lc, tm:tm + CONV_PAD, :] = jnp.where(i == nt - 1, 0.0, dn_ref[:, sl])

        def lane_chunk(lc, carry):
            n_rc = tm // 64
            du = [jnp.zeros((64, LANES), F32) for _ in range(n_rc)]
            for j in range(CONV_KERNEL):
                w = cw_ref[lc, pl.ds(j, 1), :]
                gsum = jnp.zeros((8, LANES), F32)
                for rc in range(n_rc):
                    du[rc] = du[rc] + w * d_ext[lc, pl.ds(64 * rc + 30 - j, 64), :]
                    prod = d_ext[lc, pl.ds(64 * rc, 64), :] * u_ext[lc, pl.ds(64 * rc + 2 + j, 64), :]
                    gsum = gsum + jnp.sum(prod.reshape(8, 8, LANES), axis=0)
                gw_acc[lc, j] += gsum
            for rc in range(n_rc):
                du_scr[lc, pl.ds(64 * rc, 64), :] = du[rc]
            return carry
        lax.fori_loop(0, N_LANE_CHUNKS, lane_chunk, 0)

        du = jnp.concatenate([du_scr[lc] for lc in range(N_LANE_CHUNKS)], axis=1)
        a, b = a_ref[...].astype(F32), b_ref[...].astype(F32)
        sb = _sig(b)
        da_ref[...] = (du * sb).astype(BF16)
        db_ref[...] = (du * a * (sb * (1.0 - sb))).astype(BF16)

        @pl.when(i == nt - 1)
        def _():
            gw_ref[...] = jnp.sum(gw_acc[...], axis=2)

    def prev_halo(i):
        return jnp.maximum(i * halo_blocks - 1, 0)

    def next_halo(i):
        return jnp.minimum((i + 1) * halo_blocks, tokens // CONV_PAD - 1)

    tile = lambda col=0: pl.BlockSpec((tm, D_MODEL), lambda i: (i, col))
    return pl.pallas_call(
        body, name="conv_backward_taps", grid=(nt,),
        in_specs=[tile(), pl.BlockSpec((CONV_PAD, D_MODEL), lambda i: (next_halo(i), 0)),
                  tile(COL_A), tile(COL_B),
                  pl.BlockSpec((CONV_PAD, D_MODEL), lambda i: (prev_halo(i), COL_A)),
                  pl.BlockSpec((CONV_PAD, D_MODEL), lambda i: (prev_halo(i), COL_B)),
                  _const_spec((N_DEV, CONV_PAD, LANES))],
        out_specs=(tile(), tile(), _const_spec((N_DEV, CONV_PAD, LANES))),
        out_shape=(jax.ShapeDtypeStruct((tokens, D_MODEL), BF16),
                   jax.ShapeDtypeStruct((tokens, D_MODEL), BF16),
                   jax.ShapeDtypeStruct((N_DEV, CONV_PAD, LANES), F32)),
        scratch_shapes=[pltpu.VMEM((N_LANE_CHUNKS, CONV_PAD + tm, LANES), F32),
                        pltpu.VMEM((N_LANE_CHUNKS, tm + CONV_PAD, LANES), F32),
                        pltpu.VMEM((N_LANE_CHUNKS, tm, LANES), F32),
                        pltpu.VMEM((N_LANE_CHUNKS, CONV_PAD, 8, LANES), F32)],
        compiler_params=_cparams(("arbitrary",), VMEM_LIMIT),
    )(dcv, dcv, proj, proj, proj, proj, conv_w)


def _fold_kv_head(dup, lo, second_half):
    both = dup + pltpu.roll(dup, HEAD_DIM, 1)
    lo = lax.broadcasted_iota(jnp.int32, dup.shape, 1) < HEAD_DIM
    return jnp.where(lo, 0.0, both) if second_half else jnp.where(lo, both, 0.0)


def _attention_backward(dya, o, qr, kd, vd, proj, sinks, w_ao, cos_t, sin_up, sin_dn):
    tokens = qr.shape[0]
    tm = min(512, tokens)
    per_tile = tm // BLOCK
    nt = tokens // tm
    scale = HEAD_DIM ** -0.5

    def body(dya_ref, o_ref, ag0_ref, ag1_ref, q_ref, kc_ref, kp_ref, vc_ref, vp_ref, sinks_ref, wao_ref,
             cos_c, up_c, dn_c, cos_p, up_p, dn_p,
             dq_ref, dkv_ref, dag_ref, gwao_ref, gsink_ref,
             gacc, k_ext, v_ext, dk_ext, dv_ext, dk_carry, dv_carry, do_scr, dq_scr):
        i = pl.program_id(0)
        lo = _lane_halves()

        @pl.when(i == 0)
        def _():
            gacc[...] = jnp.zeros_like(gacc)
            gsink_ref[...] = jnp.zeros_like(gsink_ref)
            dk_carry[...] = jnp.zeros_like(dk_carry)
            dv_carry[...] = jnp.zeros_like(dv_carry)
        dk_ext[...] = jnp.zeros_like(dk_ext)
        dv_ext[...] = jnp.zeros_like(dv_ext)

        @pl.when(i < nt)
        def _():
            dya = dya_ref[...]
            dpa = _dot_nt(dya, wao_ref[...])
            o = o_ref[...].astype(F32)
            ag = jnp.concatenate([ag0_ref[...], ag1_ref[...]], axis=1).astype(F32)
            sg = _sig(ag)
            gate = ag * sg
            gacc[...] += _dot_tn((o * gate).astype(BF16), dya)
            dag_ref[...] = (dpa * o * (sg * (1.0 + ag * (1.0 - sg)))).astype(BF16)
            do_scr[...] = (dpa * gate).astype(BF16)
            k_ext[0:BLOCK, :], k_ext[BLOCK:BLOCK + tm, :] = kp_ref[...], kc_ref[...]
            v_ext[0:BLOCK, :], v_ext[BLOCK:BLOCK + tm, :] = vp_ref[...], vc_ref[...]

            def block(b, carry):
                r0 = pl.multiple_of(b * BLOCK, BLOCK)
                mine, band = pl.ds(r0, BLOCK), pl.ds(r0, 2 * BLOCK)
                mask = _band_mask(i * per_tile + b)
                head_lane = lax.broadcasted_iota(jnp.int32, (1, LANES), 1)
                gsink = jnp.zeros((1, LANES), F32)
                zero_band = jnp.zeros((2 * BLOCK, LANES), F32)
                dk_band, dv_band = [zero_band, zero_band], [zero_band, zero_band]
                for kvh in range(N_KV_HEADS):
                    ta, tb = slice(LANES * 2 * kvh, LANES * (2 * kvh + 1)), slice(LANES * (2 * kvh + 1), LANES * (2 * kvh + 2))
                    ks = slice(LANES * kvh, LANES * (kvh + 1))
                    q_stack = _stack_heads(q_ref[mine, ta], q_ref[mine, tb], lo)
                    do_stack = _stack_heads(do_scr[mine, ta], do_scr[mine, tb], lo)
                    k2, v2 = k_ext[band, ks], v_ext[band, ks]
                    e_p, e_c, e_s = _band_exp(q_stack, k2, sinks_ref, kvh, mask)
                    inv = 1.0 / (_row_sums(jnp.concatenate([e_p, e_c], axis=1).astype(BF16)) + e_s)
                    p_p, p_c = e_p * inv, e_c * inv
                    dp = _dot_nt(do_stack, v2)
                    dp_p, dp_c = dp[:, :BLOCK], dp[:, BLOCK:]
                    delta = jnp.broadcast_to(jnp.sum(p_p * dp_p + p_c * dp_c, axis=1, keepdims=True), (4 * BLOCK, LANES))
                    ds = jnp.concatenate([p_p * (dp_p - delta), p_c * (dp_c - delta)], axis=1).astype(BF16)
                    sink_terms = e_s * inv * delta
                    for g in range(4):
                        total = jnp.sum(sink_terms[BLOCK * g:BLOCK * (g + 1)], axis=0, keepdims=True)
                        gsink = gsink - jnp.where(head_lane == 4 * kvh + g, total, 0.0)
                    dq_scr[mine, ta], dq_scr[mine, tb] = _unstack_heads(_dot(ds, k2), lo)
                    tile, second = kvh // 2, kvh % 2 == 1
                    dk_band[tile] = dk_band[tile] + _fold_kv_head(_dot_tn(ds, q_stack), lo, second)
                    dv_band[tile] = dv_band[tile] + _fold_kv_head(
                        _dot_tn(jnp.concatenate([p_p, p_c], axis=1).astype(BF16), do_stack), lo, second)
                gsink_ref[0:1, :] += gsink
                cs, up, dn = cos_c[mine, :], up_c[mine, :], dn_c[mine, :]
                for p in range(D_MODEL // LANES):
                    sl = slice(LANES * p, LANES * (p + 1))
                    dq_ref[mine, sl] = (_rope_transposed(dq_scr[mine, sl], cs, up, dn) * scale).astype(BF16)
                for p in range(2):
                    sl = slice(LANES * p, LANES * (p + 1))
                    dk_ext[band, sl] += dk_band[p]
                    dv_ext[band, sl] += dv_band[p]
                return carry
            lax.fori_loop(0, per_tile, block, 0)

        last = slice(tm - BLOCK, tm)
        dk_carry[last, :] += dk_ext[0:BLOCK, :]
        dv_carry[last, :] += dv_ext[0:BLOCK, :]
        for p in range(2):
            sl = slice(LANES * p, LANES * (p + 1))
            dkv_ref[:, sl] = _rope_transposed(dk_carry[:, sl], cos_p[...], up_p[...], dn_p[...]).astype(BF16)
            dkv_ref[:, slice(256 + LANES * p, 256 + LANES * (p + 1))] = dv_carry[:, sl].astype(BF16)
        dk_carry[...] = dk_ext[BLOCK:BLOCK + tm, :]
        dv_carry[...] = dv_ext[BLOCK:BLOCK + tm, :]

        @pl.when(i == nt)
        def _():
            gwao_ref[...] = gacc[...].astype(BF16)

    def cur_idx(i):
        return jnp.minimum(i, nt - 1)

    def prev_idx(i):
        return jnp.clip(i - 1, 0, nt - 1)

    cur = lambda w, col=0: pl.BlockSpec((tm, w), lambda i: (cur_idx(i), col))
    prev = lambda w: pl.BlockSpec((tm, w), lambda i: (prev_idx(i), 0))
    before = lambda w: pl.BlockSpec((BLOCK, w), lambda i: (jnp.maximum(cur_idx(i) * per_tile - 1, 0), 0))
    return pl.pallas_call(
        body, name="attention_backward", grid=(nt + 1,),
        in_specs=[cur(D_MODEL), cur(D_MODEL), cur(512, COL512_AG), cur(512, COL512_AG + 1), cur(D_MODEL),
                  cur(512), before(512), cur(512), before(512),
                  pl.BlockSpec(memory_space=pltpu.SMEM), _const_spec((D_MODEL, D_MODEL)),
                  cur(LANES), cur(LANES), cur(LANES), prev(LANES), prev(LANES), prev(LANES)],
        out_specs=(cur(D_MODEL), prev(512), cur(D_MODEL),
                   _const_spec((D_MODEL, D_MODEL)), _const_spec((8, LANES))),
        out_shape=(jax.ShapeDtypeStruct((tokens, D_MODEL), BF16),
                   jax.ShapeDtypeStruct((tokens, 512), BF16),
                   jax.ShapeDtypeStruct((tokens, D_MODEL), BF16),
                   jax.ShapeDtypeStruct((D_MODEL, D_MODEL), BF16),
                   jax.ShapeDtypeStruct((8, LANES), F32)),
        scratch_shapes=[pltpu.VMEM((D_MODEL, D_MODEL), F32),
                        pltpu.VMEM((BLOCK + tm, 512), BF16), pltpu.VMEM((BLOCK + tm, 512), BF16),
                        pltpu.VMEM((BLOCK + tm, 256), F32), pltpu.VMEM((BLOCK + tm, 256), F32),
                        pltpu.VMEM((tm, 256), F32), pltpu.VMEM((tm, 256), F32),
                        pltpu.VMEM((tm, D_MODEL), BF16), pltpu.VMEM((tm, D_MODEL), F32)],
        compiler_params=_cparams(("arbitrary",), VMEM_LIMIT),
    )(dya, o, proj, proj, qr, kd, kd, vd, vd, sinks, w_ao, cos_t, sin_up, sin_dn, cos_t, sin_up, sin_dn)


def _transpose_tokens(h):
    tokens = h.shape[0]
    tt = min(512, tokens)

    def body(h_ref, out_ref):
        out_ref[...] = h_ref[...].astype(F32).T.astype(BF16)

    return pl.pallas_call(
        body, name="transpose_tokens", grid=(tokens // tt,),
        in_specs=[pl.BlockSpec((tt, D_MODEL), lambda i: (i, 0))],
        out_specs=pl.BlockSpec((D_MODEL, tt), lambda i: (0, i)),
        out_shape=jax.ShapeDtypeStruct((D_MODEL, tokens), BF16),
        compiler_params=_cparams(("parallel",)),
    )(h)


def _input_backward(sections, w_in_t, x, dx2, norm_g):
    tokens = x.shape[0]
    tm = 256

    def body(*refs):
        sec = refs[:8]
        w_ref, x_ref, dx2_ref, g_ref, gx_ref, part_ref = refs[8:]

        @pl.when(pl.program_id(0) == 0)
        def _():
            part_ref[...] = jnp.zeros_like(part_ref)

        dh = jnp.zeros((tm, D_MODEL), F32)
        for s in range(8):
            dh = dh + _dot(sec[s][...], w_ref[_SECTION_ROWS[s]:_SECTION_ROWS[s] + _SECTION_WIDTH[s], :])
        xv = x_ref[...]
        r = lax.rsqrt(jnp.mean(xv * xv, axis=-1, keepdims=True) + RMS_EPS)
        xn = xv * r
        part_ref[0:1, :] += jnp.sum(dh * xn, axis=0, keepdims=True)
        dxn = dh * g_ref[...]
        gx_ref[...] = dx2_ref[...] + r * (dxn - xn * jnp.mean(dxn * xn, axis=-1, keepdims=True))

    tile = lambda w=D_MODEL: pl.BlockSpec((tm, w), lambda i: (i, 0))
    return pl.pallas_call(
        body, name="input_backward", grid=(tokens // tm,),
        in_specs=[tile(w) for w in _SECTION_WIDTH] + [
            pl.BlockSpec((IN_WIDTH, D_MODEL), lambda i: (0, 0), pipeline_mode=pl.Buffered(1)),
            tile(), tile(), _const_spec((1, D_MODEL))],
        out_specs=(tile(), _const_spec((8, D_MODEL))),
        out_shape=(jax.ShapeDtypeStruct((tokens, D_MODEL), F32),
                   jax.ShapeDtypeStruct((8, D_MODEL), F32)),
        compiler_params=_cparams(("arbitrary",), VMEM_LIMIT),
    )(*sections, w_in_t, x, dx2, norm_g)


def _adamw_math(w, g, m, v):
    m = ADAM_B1 * m + (1.0 - ADAM_B1) * g
    v = ADAM_B2 * v + (1.0 - ADAM_B2) * (g * g)
    m_hat = m / (1.0 - ADAM_B1 ** ADAM_STEP)
    v_hat = v / (1.0 - ADAM_B2 ** ADAM_STEP)
    delta = -ADAM_LR * (m_hat / (jnp.sqrt(v_hat) + ADAM_EPS) + ADAM_WD * w)
    return delta, m, v


def _sum_slots(recv_ref):
    total = recv_ref[0].astype(F32)
    for d in range(1, N_DEV):
        total = total + recv_ref[d].astype(F32)
    return total


def _adamw(name, w, g, m, v, tile_rows):
    rows, cols = w.shape

    def body(w_ref, g_ref, m_ref, v_ref, d_ref, nm_ref, nv_ref):
        d_ref[...], nm_ref[...], nv_ref[...] = _adamw_math(w_ref[...], g_ref[...], m_ref[...], v_ref[...])

    spec = pl.BlockSpec((tile_rows, cols), lambda i: (i, 0))
    shape = jax.ShapeDtypeStruct((rows, cols), F32)
    return pl.pallas_call(
        body, name=name, grid=(rows // tile_rows,),
        in_specs=[spec] * 4, out_specs=(spec,) * 3, out_shape=(shape,) * 3,
        compiler_params=_cparams(("parallel",)),
    )(w, g, m, v)


def _sum_adamw(name, recv, w, m, v):
    def body(recv_ref, w_ref, m_ref, v_ref, g_ref, d_ref, nm_ref, nv_ref):
        g = _sum_slots(recv_ref)
        g_ref[...] = g
        d_ref[...], nm_ref[...], nv_ref[...] = _adamw_math(w_ref[...], g, m_ref[...], v_ref[...])

    shape = jax.ShapeDtypeStruct(w.shape, F32)
    return pl.pallas_call(body, name=name, out_shape=(shape,) * 4)(recv, w, m, v)


def _pad_rows(a, rows):
    return jnp.concatenate([a, jnp.zeros((rows - a.shape[0],) + a.shape[1:], a.dtype)], axis=0)


def kernel(x, norm_g, w_in, conv_dw_w, conv_dw_b, conv_ln_g, conv_ln_b, w_conv_out, attn_sinks, w_attn_out, w_out, final_norm_g, loss_target, m_norm_g, m_w_in, m_conv_dw_w, m_conv_dw_b, m_conv_ln_g, m_conv_ln_b, m_w_conv_out, m_attn_sinks, m_w_attn_out, m_w_out, m_final_norm_g, v_norm_g, v_w_in, v_conv_dw_w, v_conv_dw_b, v_conv_ln_g, v_conv_ln_b, v_w_conv_out, v_attn_sinks, v_w_attn_out, v_w_out, v_final_norm_g):
    xs, target = x[0], loss_target[0]
    tokens = xs.shape[0]
    fg_row = final_norm_g.reshape(1, D_MODEL)

    taps_bits = lax.bitcast_convert_type(_pad_rows(conv_dw_w[0], CONV_PAD), BF16).reshape(8, D_MODEL)
    pack = jnp.concatenate([w_conv_out[0].astype(BF16), w_attn_out[0].astype(BF16), w_out[0].astype(BF16),
                            jnp.pad(taps_bits, ((0, PACK_ROWS - 3 * SHARD_SQ - 8), (0, 0)))], axis=0)
    w_in_t32 = w_in[0].T
    proj, h, w_in_t, pack_full = _gather_project(xs, norm_g, w_in_t32.astype(BF16), pack)
    w_co = pack_full[:, 0:SHARD_SQ].reshape(D_MODEL, D_MODEL)
    w_ao = pack_full[:, SHARD_SQ:2 * SHARD_SQ].reshape(D_MODEL, D_MODEL)
    w_o = pack_full[:, 2 * SHARD_SQ:3 * SHARD_SQ].reshape(D_MODEL, D_MODEL)
    conv_w = lax.bitcast_convert_type(
        pack_full[:, 3 * SHARD_SQ:3 * SHARD_SQ + 8].reshape(N_DEV, CONV_PAD, LANES, 2), F32)

    cos_t, sin_up, sin_dn = _rope_tables(tokens)
    qr, kd, vd = _rope_qkv(proj, cos_t, sin_up, sin_dn)
    cv, yc = _conv_forward(proj, conv_w, conv_dw_b, conv_ln_g, conv_ln_b, w_co)
    o, ya = _attention_forward(qr, kd, vd, proj, attn_sinks, w_ao)

    dx2, dyc, dya, dmlc, dmla, g_out, part_head = _merge_and_head(yc, ya, proj, xs, target, w_o, fg_row)
    dcv, dcg, g_co, part_conv = _conv_backward_pointwise(dyc, cv, proj, w_co, conv_ln_g, conv_ln_b)
    da, db, g_conv = _conv_backward_taps(dcv, proj, conv_w)
    dq, dkv, dag, g_ao, part_sink = _attention_backward(dya, o, qr, kd, vd, proj, attn_sinks, w_ao, cos_t, sin_up, sin_dn)
    sections = (da, db, dcg, dq, dkv, dag, dmlc, dmla)
    grad_x, part_in = _input_backward(sections, w_in_t, xs, dx2, norm_g)

    small = jnp.concatenate([
        part_in[0:1], part_conv[2:3], part_conv[0:1], part_conv[1:2], part_head[0:1],
        jnp.pad(part_sink[0:1], ((0, 0), (0, D_MODEL - LANES))), part_head[1:2],
        jnp.zeros((1, D_MODEL), F32)], axis=0)

    g_mine, r_conv, r_small = _grad_exchange(sections, _transpose_tokens(h), g_co, g_ao, g_out, g_conv, small)

    g_in_t = g_mine[:SHARD_IN]
    w_in_res = _adamw("adamw_w_in", w_in_t32, g_in_t, m_w_in[0].T, v_w_in[0].T, 192)
    grad_w_in, d_w_in, nm_w_in, nv_w_in = (a.T for a in (g_in_t,) + tuple(w_in_res))
    sq = {}
    for j, (nm, w, m, v) in enumerate((("w_conv_out", w_conv_out, m_w_conv_out, v_w_conv_out),
                                       ("w_attn_out", w_attn_out, m_w_attn_out, v_w_attn_out),
                                       ("w_out", w_out, m_w_out, v_w_out))):
        g = g_mine[SHARD_IN + j * SHARD_SQ:SHARD_IN + (j + 1) * SHARD_SQ]
        sq[nm] = (g,) + tuple(_adamw("adamw_" + nm, w[0], g, m[0], v[0], SHARD_SQ))
    conv_res = _sum_adamw("sum_adamw_conv_dw_w", r_conv.reshape(N_DEV, CONV_PAD, LANES),
                          _pad_rows(conv_dw_w[0], CONV_PAD), _pad_rows(m_conv_dw_w[0], CONV_PAD),
                          _pad_rows(v_conv_dw_w[0], CONV_PAD))
    pad_sink = lambda a: jnp.pad(a, ((0, 0), (0, D_MODEL - N_Q_HEADS)))
    zero_rows = jnp.zeros((2, D_MODEL), F32)
    stack = lambda a, b, c, d, e, f: jnp.concatenate([a, b, c, d, e.reshape(1, D_MODEL), pad_sink(f), zero_rows], axis=0)
    small_res = _sum_adamw(
        "sum_adamw_small", r_small,
        stack(norm_g, conv_dw_b, conv_ln_g, conv_ln_b, final_norm_g, attn_sinks),
        stack(m_norm_g, m_conv_dw_b, m_conv_ln_g, m_conv_ln_b, m_final_norm_g, m_attn_sinks),
        stack(v_norm_g, v_conv_dw_b, v_conv_ln_g, v_conv_ln_b, v_final_norm_g, v_attn_sinks))
    loss = jnp.sum(small_res[0][6])

    def leaf(k):
        s = small_res[k]
        return (s[0:1], (grad_w_in, d_w_in, nm_w_in, nv_w_in)[k][None], conv_res[k][None, :CONV_KERNEL],
                s[1:2], s[2:3], s[3:4], sq["w_conv_out"][k][None], s[5:6, :N_Q_HEADS],
                sq["w_attn_out"][k][None], sq["w_out"][k][None], s[4])

    return (loss, grad_x[None], *leaf(0), *leaf(1), *leaf(2), *leaf(3))
```

```python
import jax
import jax.numpy as jnp
from jax import lax
from jax.experimental import pallas as pl
from jax.experimental.pallas import tpu as pltpu

F32 = jnp.float32
BF16 = jnp.bfloat16
MESH = pl.DeviceIdType.MESH

D_MODEL = 1024
IN_WIDTH = 7680
N_DEV = 8
SHARD_IN = IN_WIDTH // N_DEV
SHARD_SQ = D_MODEL // N_DEV
CONV_KERNEL = 31
CONV_PAD = 32
HEAD_DIM = 64
N_Q_HEADS = 16
N_KV_HEADS = 4
BLOCK = 128
LANES = 128
ROPE_THETA = 10000.0
RMS_EPS = 1e-5
LN_EPS = 1e-5
NEG = -1e30
ADAM_LR = 0.001
ADAM_B1 = 0.9
ADAM_B2 = 0.999
ADAM_EPS = 1e-08
ADAM_WD = 0.01
ADAM_STEP = 10

OFF_A, OFF_B, OFF_CG, OFF_Q, OFF_KV, OFF_AG, OFF_MLC, OFF_MLA = 0, 1024, 2048, 3072, 4096, 4608, 5632, 6656
COL_A, COL_B, COL_CG, COL_Q = 0, 1, 2, 3
COL512_KV, COL512_AG, COL512_MLC, COL512_MLA = 8, 9, 11, 13
UNIT = 2 * SHARD_IN
PACK_ROWS = 400

VMEM_LIMIT = 56 * 1024 * 1024


def _cparams(sem=None, vmem=None):
    return pltpu.CompilerParams(dimension_semantics=sem, vmem_limit_bytes=vmem)


def _sig(v):
    return 0.5 * jnp.tanh(0.5 * v) + 0.5


def _dot(a, b):
    return jnp.dot(a, b, preferred_element_type=F32)


def _dot_nt(a, b):
    return lax.dot_general(a, b, (((1,), (1,)), ((), ())), preferred_element_type=F32)


def _dot_tn(a, b):
    return lax.dot_general(a, b, (((0,), (0,)), ((), ())), preferred_element_type=F32)


def _const_spec(shape):
    nd = len(shape)
    return pl.BlockSpec(shape, lambda *_: (0,) * nd)


def _pack_weight_spec(j):
    return pl.BlockSpec((N_DEV, SHARD_SQ, D_MODEL), lambda *_: (0, j, 0))


def _square(w_ref):
    return w_ref[...].reshape(D_MODEL, D_MODEL)


def _mesh_pos():
    x, y, c = lax.axis_index("x"), lax.axis_index("y"), lax.axis_index("c")
    return x, y, c, 4 * x + 2 * y + c


def _peer(x, y, c, k):
    px = 1 - x if (k >> 2) & 1 else x
    py = 1 - y if (k >> 1) & 1 else y
    pc = 1 - c if k & 1 else c
    return (px, py, pc), 4 * px + 2 * py + pc


def _gather_project(x, norm_g, w_shard_t, pack):
    tokens = x.shape[0]
    tt = min(512, tokens // 2)
    n_tok = tokens // tt
    rc = min(128, tt)

    def body(x_hbm, g_ref, ws_hbm, pack_hbm, proj_hbm, h_hbm, wfull_hbm, packfull_hbm,
             w_vmem, h_vmem, x_buf, o_buf, send_sems, recv_sems, local_sems, x_sems, o_sems):
        x_, y_, c_, me = _mesh_pos()
        myself, sibling = (x_, y_, c_), (x_, y_, 1 - c_)
        chips = ((1 - x_, y_), (x_, 1 - y_), (1 - x_, 1 - y_))

        def shard(ref, idx):
            return ref.at[pl.ds(pl.multiple_of(idx * SHARD_IN, 64), SHARD_IN)]

        def copy(a, k, idx, to, own=False):
            if a == 0:
                src, dst = ws_hbm if own else shard(w_vmem, idx), shard(w_vmem, idx)
            else:
                src, dst = pack_hbm if own else packfull_hbm.at[idx], packfull_hbm.at[idx]
            return pltpu.make_async_remote_copy(src_ref=src, dst_ref=dst, send_sem=send_sems.at[a, k],
                                                recv_sem=recv_sems.at[a, k], device_id=to, device_id_type=MESH)

        own_w = pltpu.make_async_copy(ws_hbm, shard(w_vmem, me), local_sems.at[0])
        own_p = pltpu.make_async_copy(pack_hbm, packfull_hbm.at[me], local_sems.at[1])
        own_w.start()
        own_p.start()
        sent = []
        for a in range(2):
            sent.append(copy(a, 0, me, sibling, own=True))
            sent += [copy(a, 1 + r, me, (*chip, c_), own=True) for r, chip in enumerate(chips)]
        for cp in sent:
            cp.start()

        def x_copy(t, slot):
            return pltpu.make_async_copy(x_hbm.at[pl.ds(t * tt, tt)], x_buf.at[slot], x_sems.at[slot])

        x_copy(0, 0).start()
        for t in range(n_tok):
            slot = t % 2
            if t + 1 < n_tok:
                x_copy(t + 1, 1 - slot).start()
            x_copy(t, slot).wait()

            def chunk(r0, t=t, slot=slot):
                xv = x_buf[slot, pl.ds(r0, rc), :]
                r = lax.rsqrt(jnp.mean(xv * xv, axis=-1, keepdims=True) + RMS_EPS)
                h_vmem[pl.ds(t * tt + r0, rc), :] = (xv * r * g_ref[...]).astype(BF16)
            _row_chunks(tt, rc, chunk)
        h_out = pltpu.make_async_copy(h_vmem, h_hbm, local_sems.at[6])
        h_out.start()
        local = [own_p, h_out]

        def project_unit(q, u):
            rows = pl.ds(pl.multiple_of(q * UNIT, LANES), UNIT)
            w_out = pltpu.make_async_copy(w_vmem.at[rows], wfull_hbm.at[rows], local_sems.at[2 + u])
            w_out.start()
            local.append(w_out)

            def o_copy(slot, t):
                return pltpu.make_async_copy(
                    o_buf.at[slot], proj_hbm.at[pl.ds(pl.multiple_of(t * tt, tt), tt), rows], o_sems.at[slot])

            def tile(t, carry):
                slot = lax.rem(t, 2)

                @pl.when(t >= 2)
                def _():
                    o_copy(slot, t).wait()
                o_buf[slot] = _dot_nt(h_vmem[pl.ds(pl.multiple_of(t * tt, tt), tt), :], w_vmem[rows, :]).astype(BF16)
                o_copy(slot, t).start()
                return carry
            lax.fori_loop(0, n_tok, tile, 0)
            o_copy(0, 0).wait()
            o_copy(1, 0).wait()

        def dev(chip, core):
            return 4 * chip[0] + 2 * chip[1] + core

        def arrive_and_pass_on(a, r):
            copy(a, 1 + r, dev(chips[r], c_), myself).wait_recv()
            passed = copy(a, 4 + r, dev(chips[r], c_), sibling)
            passed.start()
            sent.append(passed)

        def passed_on_to_me(a, r):
            copy(a, 4 + r, dev(chips[r], 1 - c_), myself).wait_recv()

        own_w.wait()
        copy(0, 0, dev((x_, y_), 1 - c_), myself).wait_recv()
        project_unit(2 * x_ + y_, 0)
        arrive_and_pass_on(0, 0)
        arrive_and_pass_on(0, 1)
        passed_on_to_me(0, 0)
        project_unit(2 * chips[0][0] + chips[0][1], 1)
        arrive_and_pass_on(0, 2)
        passed_on_to_me(0, 1)
        project_unit(2 * chips[1][0] + chips[1][1], 2)
        passed_on_to_me(0, 2)
        project_unit(2 * chips[2][0] + chips[2][1], 3)
        for r in range(3):
            arrive_and_pass_on(1, r)
        copy(1, 0, dev((x_, y_), 1 - c_), myself).wait_recv()
        for r in range(3):
            passed_on_to_me(1, r)
        for cp in sent:
            cp.wait_send()
        for cp in local:
            cp.wait()

    hbm = pl.BlockSpec(memory_space=pltpu.HBM)
    return pl.pallas_call(
        body, name="gather_project",
        in_specs=[hbm, pl.BlockSpec(memory_space=pltpu.VMEM), hbm, hbm],
        out_specs=(hbm, hbm, hbm, hbm),
        out_shape=(jax.ShapeDtypeStruct((tokens, IN_WIDTH), BF16),
                   jax.ShapeDtypeStruct((tokens, D_MODEL), BF16),
                   jax.ShapeDtypeStruct((IN_WIDTH, D_MODEL), BF16),
                   jax.ShapeDtypeStruct((N_DEV, PACK_ROWS, D_MODEL), BF16)),
        scratch_shapes=[pltpu.VMEM((IN_WIDTH, D_MODEL), BF16),
                        pltpu.VMEM((tokens, D_MODEL), BF16),
                        pltpu.VMEM((2, tt, D_MODEL), F32),
                        pltpu.VMEM((2, tt, UNIT), BF16),
                        pltpu.SemaphoreType.DMA((2, N_DEV - 1)),
                        pltpu.SemaphoreType.DMA((2, N_DEV - 1)),
                        pltpu.SemaphoreType.DMA((7,)),
                        pltpu.SemaphoreType.DMA((2,)),
                        pltpu.SemaphoreType.DMA((2,))],
        compiler_params=_cparams(None, VMEM_LIMIT),
    )(x, norm_g, w_shard_t, pack)


HALF_ROWS = SHARD_IN + 3 * SHARD_SQ

GRAD_CHUNK = 384
_SECTION_ROWS = (OFF_A, OFF_B, OFF_CG, OFF_Q, OFF_KV, OFF_AG, OFF_MLC, OFF_MLA)
_SECTION_WIDTH = (1024, 1024, 1024, 1024, 512, 1024, 1024, 1024)


def _dproj_pieces(first, width):
    out = []
    for s, (start, w) in enumerate(zip(_SECTION_ROWS, _SECTION_WIDTH)):
        lo, hi = max(first, start), min(first + width, start + w)
        if lo < hi:
            out.append((s, lo - start, hi - lo, lo - first))
    return out


def _grad_exchange(sections, h_t, g_co, g_ao, g_out, g_conv, small):
    tokens = h_t.shape[1]
    n_chunk = UNIT // GRAD_CHUNK
    rc = 192

    def body(*refs):
        sec = refs[:8]
        (ht_hbm, gco_hbm, gao_hbm, gout_hbm, gconv_hbm, small_hbm, gmine_hbm, rconv_hbm, rsmall_hbm,
         lhs_buf, ht_vmem, halves, out_buf, stage, final,
         lhs_sems, ht_sem, tail_sems, d2d_send, d2d_recv, ici_send, ici_recv,
         tiny_send, tiny_recv, local_sems) = refs[8:]
        x_, y_, c_, me = _mesh_pos()
        myself, sibling = (x_, y_, c_), (x_, y_, 1 - c_)
        chips = ((1 - x_, 1 - y_), (1 - x_, y_), (x_, 1 - y_), (x_, y_))
        squares = (gco_hbm, gao_hbm, gout_hbm)

        def remote(src, dst, send_sem, recv_sem, to):
            return pltpu.make_async_remote_copy(src_ref=src, dst_ref=dst, send_sem=send_sem, recv_sem=recv_sem,
                                                device_id=to, device_id_type=MESH)

        own_tiny = [pltpu.make_async_copy(gconv_hbm.at[me], rconv_hbm.at[me], local_sems.at[0]),
                    pltpu.make_async_copy(small_hbm, rsmall_hbm.at[me], local_sems.at[1])]
        for cp in own_tiny:
            cp.start()
        tiny = []
        for k in range(1, N_DEV):
            peer, peer_idx = _peer(x_, y_, c_, k)
            tiny += [remote(gconv_hbm.at[peer_idx], rconv_hbm.at[me], tiny_send.at[0, k - 1], tiny_recv.at[0, k - 1], peer),
                     remote(small_hbm, rsmall_hbm.at[me], tiny_send.at[1, k - 1], tiny_recv.at[1, k - 1], peer)]
        for cp in tiny:
            cp.start()

        ht_in = pltpu.make_async_copy(ht_hbm, ht_vmem, ht_sem.at[0])
        ht_in.start()

        def fetch(q, j, slot, wait):
            for k in range(4):
                @pl.when(q == k)
                def _(k=k):
                    for n, (s, col, width, place) in enumerate(_dproj_pieces(k * UNIT + j * GRAD_CHUNK, GRAD_CHUNK)):
                        cp = pltpu.make_async_copy(sec[s].at[pl.ds(0, tokens), pl.ds(col, width)],
                                                   lhs_buf.at[slot, pl.ds(0, tokens), pl.ds(place, width)],
                                                   lhs_sems.at[slot, n])
                        cp.wait() if wait else cp.start()

        def d2d(u):
            return remote(halves.at[1 - c_], stage.at[u], d2d_send.at[u], d2d_recv.at[u], sibling)

        def ici(u):
            return remote(stage.at[u], final.at[u], ici_send.at[u], ici_recv.at[u], (*chips[u], c_))

        def tails(q):
            out = []
            for core in range(2):
                for n, g in enumerate(squares):
                    rows = pl.ds(pl.multiple_of((2 * q + core) * SHARD_SQ, SHARD_SQ), SHARD_SQ)
                    out.append(pltpu.make_async_copy(g.at[rows], halves.at[core, pl.ds(SHARD_IN + n * SHARD_SQ, SHARD_SQ)],
                                                     tail_sems.at[3 * core + n]))
            return out

        def chip_sum(u):
            d2d(u).wait_recv()

            def chunk(r0):
                rows = pl.ds(r0, rc)
                stage[u, rows, :] = (stage[u, rows, :].astype(F32) + halves[c_, rows, :].astype(F32)).astype(BF16)
            _row_chunks(HALF_ROWS, rc, chunk)
            if u < 3:
                ici(u).start()

        def chip_of(u):
            return 2 * chips[u][0] + chips[u][1]

        def store_rows(block, first):
            n = block.shape[0]
            for core in range(2):
                lo, hi = max(first, core * SHARD_IN), min(first + n, (core + 1) * SHARD_IN)
                if lo < hi:
                    halves[core, lo - core * SHARD_IN:hi - core * SHARD_IN, :] = block[lo - first:hi - first].astype(BF16)

        fetch(chip_of(0), 0, 0, wait=False)
        ht_in.wait()
        for u in range(4):
            q = chip_of(u)
            for j in range(n_chunk):
                slot = (u * n_chunk + j) % 2
                if j + 1 < n_chunk:
                    fetch(q, j + 1, 1 - slot, wait=False)
                elif u + 1 < 4:
                    fetch(chip_of(u + 1), 0, 1 - slot, wait=False)
                fetch(q, j, slot, wait=True)
                grad_t = _dot(ht_vmem[...], lhs_buf[slot])
                if j == 0:
                    if u > 0:
                        chip_sum(u - 1)
                        d2d(u - 1).wait_send()
                    for cp in tails(q):
                        cp.start()
                for r in range(GRAD_CHUNK // LANES):
                    store_rows(grad_t[:, LANES * r:LANES * (r + 1)].T, j * GRAD_CHUNK + LANES * r)
            for cp in tails(q):
                cp.wait()
            d2d(u).start()

        chip_sum(3)
        for u in range(3):
            remote(stage.at[u], final.at[u], ici_send.at[u], ici_recv.at[u], myself).wait_recv()

        def total(r0):
            rows = pl.ds(r0, rc)
            out_buf[rows, :] = ((stage[3, rows, :].astype(F32) + final[0, rows, :].astype(F32))
                                + final[1, rows, :].astype(F32)) + final[2, rows, :].astype(F32)
        _row_chunks(HALF_ROWS, rc, total)
        out = pltpu.make_async_copy(out_buf, gmine_hbm, local_sems.at[2])
        out.start()
        d2d(3).wait_send()
        for u in range(3):
            ici(u).wait_send()
        for k in range(1, N_DEV):
            peer, peer_idx = _peer(x_, y_, c_, k)
            remote(gconv_hbm.at[me], rconv_hbm.at[peer_idx], tiny_send.at[0, k - 1], tiny_recv.at[0, k - 1], myself).wait_recv()
            remote(small_hbm, rsmall_hbm.at[peer_idx], tiny_send.at[1, k - 1], tiny_recv.at[1, k - 1], myself).wait_recv()
        for cp in tiny:
            cp.wait_send()
        for cp in own_tiny:
            cp.wait()
        out.wait()

    hbm = pl.BlockSpec(memory_space=pltpu.HBM)
    return pl.pallas_call(
        body, name="grad_exchange",
        in_specs=[hbm] * 14, out_specs=(hbm, hbm, hbm),
        out_shape=(jax.ShapeDtypeStruct((HALF_ROWS, D_MODEL), F32),
                   jax.ShapeDtypeStruct((N_DEV, CONV_PAD, LANES), F32),
                   jax.ShapeDtypeStruct((N_DEV, 8, D_MODEL), F32)),
        scratch_shapes=[pltpu.VMEM((2, tokens, GRAD_CHUNK), BF16),
                        pltpu.VMEM((D_MODEL, tokens), BF16),
                        pltpu.VMEM((2, HALF_ROWS, D_MODEL), BF16),
                        pltpu.VMEM((HALF_ROWS, D_MODEL), F32),
                        pltpu.VMEM((4, HALF_ROWS, D_MODEL), BF16),
                        pltpu.VMEM((3, HALF_ROWS, D_MODEL), BF16),
                        pltpu.SemaphoreType.DMA((2, 3)),
                        pltpu.SemaphoreType.DMA((1,)),
                        pltpu.SemaphoreType.DMA((6,)),
                        pltpu.SemaphoreType.DMA((4,)),
                        pltpu.SemaphoreType.DMA((4,)),
                        pltpu.SemaphoreType.DMA((3,)),
                        pltpu.SemaphoreType.DMA((3,)),
                        pltpu.SemaphoreType.DMA((2, N_DEV - 1)),
                        pltpu.SemaphoreType.DMA((2, N_DEV - 1)),
                        pltpu.SemaphoreType.DMA((3,))],
        compiler_params=_cparams(None, 60 * 1024 * 1024),
    )(*sections, h_t, g_co, g_ao, g_out, g_conv, small)


def _row_chunks(total, size, fn):
    n = total // size
    if n == 1:
        fn(0)
        return

    def step(i, carry):
        fn(pl.multiple_of(i * size, size))
        return carry
    lax.fori_loop(0, n, step, 0)


def _rope_tables(tokens):
    inv_freq = ROPE_THETA ** (-jnp.arange(0, HEAD_DIM, 2, dtype=F32) / HEAD_DIM)
    ang = jnp.arange(tokens, dtype=jnp.int32).astype(F32)[:, None] * inv_freq[None, :]
    cos, sin = jnp.cos(ang), jnp.sin(ang)
    zero = jnp.zeros_like(sin)
    cos_t = jnp.tile(jnp.concatenate([cos, cos], axis=1), (1, LANES // HEAD_DIM))
    sin_up = jnp.tile(jnp.concatenate([-sin, zero], axis=1), (1, LANES // HEAD_DIM))
    sin_dn = jnp.tile(jnp.concatenate([zero, sin], axis=1), (1, LANES // HEAD_DIM))
    return cos_t, sin_up, sin_dn


def _rope(t, cos_t, sin_up, sin_dn):
    return t * cos_t + pltpu.roll(t, LANES - 32, 1) * sin_up + pltpu.roll(t, 32, 1) * sin_dn


def _rope_transposed(g, cos_t, sin_up, sin_dn):
    return g * cos_t + pltpu.roll(g * sin_up, 32, 1) + pltpu.roll(g * sin_dn, LANES - 32, 1)


def _lane_halves():
    lane = lax.broadcasted_iota(jnp.int32, (BLOCK, LANES), 1)
    return lane < HEAD_DIM


def _rope_qkv(proj, cos_t, sin_up, sin_dn):
    tokens = proj.shape[0]
    tm = min(512, tokens)
    scale = HEAD_DIM ** -0.5

    def body(q_ref, kv_ref, cos_ref, up_ref, dn_ref, qr_ref, kd_ref, vd_ref):
        lo = _lane_halves()

        def chunk(r0):
            rows = pl.ds(r0, BLOCK)
            cs, up, dn = cos_ref[rows, :], up_ref[rows, :], dn_ref[rows, :]
            for p in range(D_MODEL // LANES):
                sl = slice(LANES * p, LANES * (p + 1))
                qt = q_ref[rows, sl].astype(F32)
                qr_ref[rows, sl] = (_rope(qt, cs, up, dn) * scale).astype(BF16)
            for p in range(2):
                sl = slice(LANES * p, LANES * (p + 1))
                kt = _rope(kv_ref[rows, sl].astype(F32), cs, up, dn)
                vt = kv_ref[rows, slice(256 + LANES * p, 256 + LANES * (p + 1))].astype(F32)
                for src, dst in ((kt, kd_ref), (vt, vd_ref)):
                    first = jnp.where(lo, src, 0.0)
                    second = src - first
                    dst[rows, slice(LANES * 2 * p, LANES * (2 * p + 1))] = (first + pltpu.roll(first, HEAD_DIM, 1)).astype(BF16)
                    dst[rows, slice(LANES * (2 * p + 1), LANES * (2 * p + 2))] = (second + pltpu.roll(second, HEAD_DIM, 1)).astype(BF16)
        _row_chunks(tm, BLOCK, chunk)

    tab = pl.BlockSpec((tm, LANES), lambda i: (i, 0))
    return pl.pallas_call(
        body, name="rope_qkv", grid=(tokens // tm,),
        in_specs=[pl.BlockSpec((tm, D_MODEL), lambda i: (i, COL_Q)),
                  pl.BlockSpec((tm, 512), lambda i: (i, COL512_KV)), tab, tab, tab],
        out_specs=(pl.BlockSpec((tm, D_MODEL), lambda i: (i, 0)),
                   pl.BlockSpec((tm, 512), lambda i: (i, 0)),
                   pl.BlockSpec((tm, 512), lambda i: (i, 0))),
        out_shape=(jax.ShapeDtypeStruct((tokens, D_MODEL), BF16),
                   jax.ShapeDtypeStruct((tokens, 512), BF16),
                   jax.ShapeDtypeStruct((tokens, 512), BF16)),
        compiler_params=_cparams(("parallel",)),
    )(proj, proj, cos_t, sin_up, sin_dn)


CONV_TM = 256
N_LANE_CHUNKS = D_MODEL // LANES


def _fill_u_ext(u_ext, a_ref, b_ref, ah_ref, bh_ref, first_tile):
    for lc in range(N_LANE_CHUNKS):
        sl = slice(LANES * lc, LANES * (lc + 1))
        uh = ah_ref[:, sl].astype(F32) * _sig(bh_ref[:, sl].astype(F32))
        u_ext[lc, 0:CONV_PAD, :] = jnp.where(first_tile, 0.0, uh)
        u_ext[lc, CONV_PAD:CONV_PAD + CONV_TM, :] = a_ref[:, sl].astype(F32) * _sig(b_ref[:, sl].astype(F32))


def _conv_forward(proj, conv_w, dw_b, ln_g, ln_b, w_co):
    tokens = proj.shape[0]
    tm = CONV_TM
    halo_blocks = tm // CONV_PAD

    def body(a_ref, b_ref, ah_ref, bh_ref, cg_ref, cw_ref, dwb_ref, lng_ref, lnb_ref, wco_ref,
             cv_ref, yc_ref, u_ext, cv_scr):
        _fill_u_ext(u_ext, a_ref, b_ref, ah_ref, bh_ref, pl.program_id(0) == 0)

        def lane_chunk(lc, carry):
            for rc in range(tm // 64):
                acc = jnp.zeros((64, LANES), F32)
                for j in range(CONV_KERNEL):
                    acc = acc + cw_ref[lc, pl.ds(j, 1), :] * u_ext[lc, pl.ds(64 * rc + 2 + j, 64), :]
                cv_scr[lc, pl.ds(64 * rc, 64), :] = acc
            return carry
        lax.fori_loop(0, N_LANE_CHUNKS, lane_chunk, 0)

        cv = jnp.concatenate([cv_scr[lc] for lc in range(N_LANE_CHUNKS)], axis=1) + dwb_ref[...]
        cv_ref[...] = cv
        mu = jnp.mean(cv, axis=-1, keepdims=True)
        zc = cv - mu
        rstd = lax.rsqrt(jnp.mean(zc * zc, axis=-1, keepdims=True) + LN_EPS)
        ln = zc * rstd * lng_ref[...] + lnb_ref[...]
        cg = cg_ref[...].astype(F32)
        pc = (ln * _sig(ln)) * (cg * _sig(cg))
        yc_ref[...] = _dot(pc.astype(BF16), _square(wco_ref)).astype(BF16)

    def halo_map(i):
        return (jnp.maximum(i * halo_blocks - 1, 0), 0)

    tile = lambda col: pl.BlockSpec((tm, D_MODEL), lambda i: (i, col))
    return pl.pallas_call(
        body, name="conv_forward", grid=(tokens // tm,),
        in_specs=[tile(COL_A), tile(COL_B),
                  pl.BlockSpec((CONV_PAD, D_MODEL), lambda i: (halo_map(i)[0], COL_A)),
                  pl.BlockSpec((CONV_PAD, D_MODEL), lambda i: (halo_map(i)[0], COL_B)),
                  tile(COL_CG), _const_spec((N_DEV, CONV_PAD, LANES)),
                  _const_spec((1, D_MODEL)), _const_spec((1, D_MODEL)), _const_spec((1, D_MODEL)),
                  _pack_weight_spec(0)],
        out_specs=(pl.BlockSpec((tm, D_MODEL), lambda i: (i, 0)),
                   pl.BlockSpec((tm, D_MODEL), lambda i: (i, 0))),
        out_shape=(jax.ShapeDtypeStruct((tokens, D_MODEL), F32),
                   jax.ShapeDtypeStruct((tokens, D_MODEL), BF16)),
        scratch_shapes=[pltpu.VMEM((N_LANE_CHUNKS, CONV_PAD + tm, LANES), F32),
                        pltpu.VMEM((N_LANE_CHUNKS, tm, LANES), F32)],
        compiler_params=_cparams(("parallel",), VMEM_LIMIT),
    )(proj, proj, proj, proj, proj, conv_w, dw_b, ln_g, ln_b, w_co)


def _band_mask(n):
    row = lax.broadcasted_iota(jnp.int32, (4 * BLOCK, 2 * BLOCK), 0) & (BLOCK - 1)
    col = lax.broadcasted_iota(jnp.int32, (4 * BLOCK, 2 * BLOCK), 1)
    before = jnp.logical_and(jnp.logical_and(col < BLOCK, col > row), n > 0)
    return jnp.logical_or(before, jnp.logical_and(col >= BLOCK, col - BLOCK <= row))


def _stack_heads(tile_a, tile_b, lo):
    zero = jnp.zeros_like(tile_a)
    return jnp.concatenate([jnp.where(lo, tile_a, zero), jnp.where(lo, zero, tile_a),
                            jnp.where(lo, tile_b, zero), jnp.where(lo, zero, tile_b)], axis=0)


def _unstack_heads(stacked, lo):
    s = [stacked[BLOCK * g:BLOCK * (g + 1)] for g in range(4)]
    return (jnp.where(lo, s[0], 0.0) + jnp.where(lo, 0.0, s[1]),
            jnp.where(lo, s[2], 0.0) + jnp.where(lo, 0.0, s[3]))


def _band_exp(q_stack, k2, sinks_ref, kvh, mask):
    s = jnp.where(mask, _dot_nt(q_stack, k2), NEG)
    sink = jnp.concatenate([jnp.full((BLOCK, LANES), sinks_ref[0, 4 * kvh + g], F32) for g in range(4)], axis=0)
    m = jnp.max(jnp.maximum(s[:, :BLOCK], s[:, BLOCK:]), axis=1, keepdims=True)
    m = jnp.maximum(jnp.broadcast_to(m, (4 * BLOCK, LANES)), sink)
    return jnp.exp(s[:, :BLOCK] - m), jnp.exp(s[:, BLOCK:] - m), jnp.exp(sink - m)


def _row_sums(e_bf16):
    return _dot(e_bf16, jnp.ones((2 * BLOCK, LANES), BF16))


def _attention_forward(qr, kd, vd, proj, sinks, w_ao):
    tokens = qr.shape[0]
    tm = min(512, tokens)
    per_tile = tm // BLOCK

    def body(q_ref, kc_ref, kp_ref, vc_ref, vp_ref, ag0_ref, ag1_ref, sinks_ref, wao_ref, o_ref, ya_ref,
             k_ext, v_ext, o_scr):
        i = pl.program_id(0)
        lo = _lane_halves()
        k_ext[0:BLOCK, :], k_ext[BLOCK:BLOCK + tm, :] = kp_ref[...], kc_ref[...]
        v_ext[0:BLOCK, :], v_ext[BLOCK:BLOCK + tm, :] = vp_ref[...], vc_ref[...]

        def block(b, carry):
            r0 = pl.multiple_of(b * BLOCK, BLOCK)
            band = pl.ds(r0, 2 * BLOCK)
            mask = _band_mask(i * per_tile + b)
            for kvh in range(N_KV_HEADS):
                ta, tb = slice(LANES * 2 * kvh, LANES * (2 * kvh + 1)), slice(LANES * (2 * kvh + 1), LANES * (2 * kvh + 2))
                ks = slice(LANES * kvh, LANES * (kvh + 1))
                q_stack = _stack_heads(q_ref[pl.ds(r0, BLOCK), ta], q_ref[pl.ds(r0, BLOCK), tb], lo)
                e_p, e_c, e_s = _band_exp(q_stack, k_ext[band, ks], sinks_ref, kvh, mask)
                e = jnp.concatenate([e_p, e_c], axis=1).astype(BF16)
                o_stack = _dot(e, v_ext[band, ks]) / (_row_sums(e) + e_s)
                o_scr[pl.ds(r0, BLOCK), ta], o_scr[pl.ds(r0, BLOCK), tb] = _unstack_heads(o_stack, lo)
            return carry
        lax.fori_loop(0, per_tile, block, 0)
        o = o_scr[...]
        o_ref[...] = o.astype(BF16)
        ag = jnp.concatenate([ag0_ref[...], ag1_ref[...]], axis=1).astype(F32)
        ya_ref[...] = _dot((o * (ag * _sig(ag))).astype(BF16), _square(wao_ref)).astype(BF16)

    cur = lambda w, col=0: pl.BlockSpec((tm, w), lambda i: (i, col))
    prev = lambda w: pl.BlockSpec((BLOCK, w), lambda i: (jnp.maximum(i * per_tile - 1, 0), 0))
    return pl.pallas_call(
        body, name="attention_forward", grid=(tokens // tm,),
        in_specs=[cur(D_MODEL), cur(512), prev(512), cur(512), prev(512),
                  cur(512, COL512_AG), cur(512, COL512_AG + 1),
                  pl.BlockSpec(memory_space=pltpu.SMEM), _pack_weight_spec(1)],
        out_specs=(cur(D_MODEL), cur(D_MODEL)),
        out_shape=(jax.ShapeDtypeStruct((tokens, D_MODEL), BF16),
                   jax.ShapeDtypeStruct((tokens, D_MODEL), BF16)),
        scratch_shapes=[pltpu.VMEM((BLOCK + tm, 512), BF16), pltpu.VMEM((BLOCK + tm, 512), BF16),
                        pltpu.VMEM((tm, D_MODEL), F32)],
        compiler_params=_cparams(("parallel",), VMEM_LIMIT),
    )(qr, kd, kd, vd, vd, proj, proj, sinks, w_ao)


def _merge_and_head(yc, ya, proj, x, target, w_out, final_g):
    tokens = x.shape[0]
    tm = min(512, tokens)
    last = tokens // tm - 1

    def body(yc_ref, ya_ref, mlc0_ref, mlc1_ref, mla0_ref, mla1_ref, x_ref, t_ref, wout_ref, fg_ref,
             dx2_ref, dyc_ref, dya_ref, dmlc_ref, dmla_ref, gwout_ref, part_ref, gacc):
        i = pl.program_id(0)

        @pl.when(i == 0)
        def _():
            gacc[...] = jnp.zeros_like(gacc)
            part_ref[...] = jnp.zeros_like(part_ref)

        yc, ya = yc_ref[...].astype(F32), ya_ref[...].astype(F32)
        gc = _sig(jnp.concatenate([mlc0_ref[...], mlc1_ref[...]], axis=1).astype(F32))
        ga = _sig(jnp.concatenate([mla0_ref[...], mla1_ref[...]], axis=1).astype(F32))
        merged = (gc * yc + ga * ya).astype(BF16)
        x2 = x_ref[...] + _dot(merged, _square(wout_ref))
        r2 = lax.rsqrt(jnp.mean(x2 * x2, axis=-1, keepdims=True) + RMS_EPS)
        x2n = x2 * r2
        fg = fg_ref[...]
        err = x2n * fg - t_ref[...]
        dy = err * (1.0 / D_MODEL)
        part_ref[0:1, :] += jnp.sum(dy * x2n, axis=0, keepdims=True)
        part_ref[1:2, :] += jnp.sum(err * err, axis=0, keepdims=True) * (0.5 / D_MODEL)
        dx2n = dy * fg
        dx2 = r2 * (dx2n - x2n * jnp.mean(dx2n * x2n, axis=-1, keepdims=True))
        dx2_ref[...] = dx2
        dx2b = dx2.astype(BF16)
        gacc[...] += _dot_tn(merged, dx2b)
        dm = _dot_nt(dx2b, _square(wout_ref))
        dyc_ref[...] = (dm * gc).astype(BF16)
        dya_ref[...] = (dm * ga).astype(BF16)
        dmlc_ref[...] = (dm * yc * (gc * (1.0 - gc))).astype(BF16)
        dmla_ref[...] = (dm * ya * (ga * (1.0 - ga))).astype(BF16)

        @pl.when(i == last)
        def _():
            gwout_ref[...] = gacc[...].astype(BF16)

    tile = lambda col=0: pl.BlockSpec((tm, D_MODEL), lambda i: (i, col))
    half = lambda col: pl.BlockSpec((tm, 512), lambda i: (i, col))
    return pl.pallas_call(
        body, name="merge_and_head", grid=(tokens // tm,),
        in_specs=[tile(), tile(), half(COL512_MLC), half(COL512_MLC + 1), half(COL512_MLA), half(COL512_MLA + 1),
                  tile(), tile(), _pack_weight_spec(2), _const_spec((1, D_MODEL))],
        out_specs=(tile(), tile(), tile(), tile(), tile(),
                   _const_spec((D_MODEL, D_MODEL)), _const_spec((8, D_MODEL))),
        out_shape=(jax.ShapeDtypeStruct((tokens, D_MODEL), F32),
                   jax.ShapeDtypeStruct((tokens, D_MODEL), BF16),
                   jax.ShapeDtypeStruct((tokens, D_MODEL), BF16),
                   jax.ShapeDtypeStruct((tokens, D_MODEL), BF16),
                   jax.ShapeDtypeStruct((tokens, D_MODEL), BF16),
                   jax.ShapeDtypeStruct((D_MODEL, D_MODEL), BF16),
                   jax.ShapeDtypeStruct((8, D_MODEL), F32)),
        scratch_shapes=[pltpu.VMEM((D_MODEL, D_MODEL), F32)],
        compiler_params=_cparams(("arbitrary",), VMEM_LIMIT),
    )(yc, ya, proj, proj, proj, proj, x, target, w_out, final_g)


def _conv_backward_pointwise(dyc, cv, proj, w_co, ln_g, ln_b):
    tokens = cv.shape[0]
    tm = min(512, tokens)
    last = tokens // tm - 1

    def body(dyc_ref, cv_ref, cg_ref, wco_ref, lng_ref, lnb_ref, dcv_ref, dcg_ref, gwco_ref, part_ref, gacc):
        i = pl.program_id(0)

        @pl.when(i == 0)
        def _():
            gacc[...] = jnp.zeros_like(gacc)
            part_ref[...] = jnp.zeros_like(part_ref)

        cv = cv_ref[...]
        mu = jnp.mean(cv, axis=-1, keepdims=True)
        zc = cv - mu
        rstd = lax.rsqrt(jnp.mean(zc * zc, axis=-1, keepdims=True) + LN_EPS)
        z = zc * rstd
        lng = lng_ref[...]
        ln = z * lng + lnb_ref[...]
        sl = _sig(ln)
        c = ln * sl
        cg = cg_ref[...].astype(F32)
        scg = _sig(cg)
        gate = cg * scg
        dyc = dyc_ref[...]
        gacc[...] += _dot_tn((c * gate).astype(BF16), dyc)
        dpc = _dot_nt(dyc, _square(wco_ref))
        dcg_ref[...] = (dpc * c * (scg * (1.0 + cg * (1.0 - scg)))).astype(BF16)
        dln = dpc * gate * (sl * (1.0 + ln * (1.0 - sl)))
        part_ref[0:1, :] += jnp.sum(dln * z, axis=0, keepdims=True)
        part_ref[1:2, :] += jnp.sum(dln, axis=0, keepdims=True)
        dz = dln * lng
        dcv = rstd * (dz - jnp.mean(dz, axis=-1, keepdims=True) - z * jnp.mean(dz * z, axis=-1, keepdims=True))
        part_ref[2:3, :] += jnp.sum(dcv, axis=0, keepdims=True)
        dcv_ref[...] = dcv

        @pl.when(i == last)
        def _():
            gwco_ref[...] = gacc[...].astype(BF16)

    tile = lambda col=0: pl.BlockSpec((tm, D_MODEL), lambda i: (i, col))
    return pl.pallas_call(
        body, name="conv_backward_pointwise", grid=(tokens // tm,),
        in_specs=[tile(), tile(), tile(COL_CG), _pack_weight_spec(0),
                  _const_spec((1, D_MODEL)), _const_spec((1, D_MODEL))],
        out_specs=(tile(), tile(), _const_spec((D_MODEL, D_MODEL)), _const_spec((8, D_MODEL))),
        out_shape=(jax.ShapeDtypeStruct((tokens, D_MODEL), F32),
                   jax.ShapeDtypeStruct((tokens, D_MODEL), BF16),
                   jax.ShapeDtypeStruct((D_MODEL, D_MODEL), BF16),
                   jax.ShapeDtypeStruct((8, D_MODEL), F32)),
        scratch_shapes=[pltpu.VMEM((D_MODEL, D_MODEL), F32)],
        compiler_params=_cparams(("arbitrary",), VMEM_LIMIT),
    )(dyc, cv, proj, w_co, ln_g, ln_b)


def _conv_backward_taps(dcv, proj, conv_w):
    tokens = dcv.shape[0]
    tm = CONV_TM
    nt = tokens // tm
    halo_blocks = tm // CONV_PAD

    def body(d_ref, dn_ref, a_ref, b_ref, ah_ref, bh_ref, cw_ref, da_ref, db_ref, gw_ref, u_ext, d_ext, du_scr, gw_acc):
        i = pl.program_id(0)

        @pl.when(i == 0)
        def _():
            gw_acc[...] = jnp.zeros_like(gw_acc)

        _fill_u_ext(u_ext, a_ref, b_ref, ah_ref, bh_ref, i == 0)
        for lc in range(N_LANE_CHUNKS):
            sl = slice(LANES * lc, LANES * (lc + 1))
            d_ext[lc, 0:tm, :] = d_ref[:, sl]
            d_ext[lc, tm:tm + CONV_PAD, :] = jnp.where(i == nt - 1, 0.0, dn_ref[:, sl])

        def lane_chunk(lc, carry):
            n_rc = tm // 64
            du = [jnp.zeros((64, LANES), F32) for _ in range(n_rc)]
            for j in range(CONV_KERNEL):
                w = cw_ref[lc, pl.ds(j, 1), :]
                gsum = jnp.zeros((8, LANES), F32)
                for rc in range(n_rc):
                    du[rc] = du[rc] + w * d_ext[lc, pl.ds(64 * rc + 30 - j, 64), :]
                    prod = d_ext[lc, pl.ds(64 * rc, 64), :] * u_ext[lc, pl.ds(64 * rc + 2 + j, 64), :]
                    gsum = gsum + jnp.sum(prod.reshape(8, 8, LANES), axis=0)
                gw_acc[lc, j] += gsum
            for rc in range(n_rc):
                du_scr[lc, pl.ds(64 * rc, 64), :] = du[rc]
            return carry
        lax.fori_loop(0, N_LANE_CHUNKS, lane_chunk, 0)

        du = jnp.concatenate([du_scr[lc] for lc in range(N_LANE_CHUNKS)], axis=1)
        a, b = a_ref[...].astype(F32), b_ref[...].astype(F32)
        sb = _sig(b)
        da_ref[...] = (du * sb).astype(BF16)
        db_ref[...] = (du * a * (sb * (1.0 - sb))).astype(BF16)

        @pl.when(i == nt - 1)
        def _():
            gw_ref[...] = jnp.sum(gw_acc[...], axis=2)

    def prev_halo(i):
        return jnp.maximum(i * halo_blocks - 1, 0)

    def next_halo(i):
        return jnp.minimum((i + 1) * halo_blocks, tokens // CONV_PAD - 1)

    tile = lambda col=0: pl.BlockSpec((tm, D_MODEL), lambda i: (i, col))
    return pl.pallas_call(
        body, name="conv_backward_taps", grid=(nt,),
        in_specs=[tile(), pl.BlockSpec((CONV_PAD, D_MODEL), lambda i: (next_halo(i), 0)),
                  tile(COL_A), tile(COL_B),
                  pl.BlockSpec((CONV_PAD, D_MODEL), lambda i: (prev_halo(i), COL_A)),
                  pl.BlockSpec((CONV_PAD, D_MODEL), lambda i: (prev_halo(i), COL_B)),
                  _const_spec((N_DEV, CONV_PAD, LANES))],
        out_specs=(tile(), tile(), _const_spec((N_DEV, CONV_PAD, LANES))),
        out_shape=(jax.ShapeDtypeStruct((tokens, D_MODEL), BF16),
                   jax.ShapeDtypeStruct((tokens, D_MODEL), BF16),
                   jax.ShapeDtypeStruct((N_DEV, CONV_PAD, LANES), F32)),
        scratch_shapes=[pltpu.VMEM((N_LANE_CHUNKS, CONV_PAD + tm, LANES), F32),
                        pltpu.VMEM((N_LANE_CHUNKS, tm + CONV_PAD, LANES), F32),
                        pltpu.VMEM((N_LANE_CHUNKS, tm, LANES), F32),
                        pltpu.VMEM((N_LANE_CHUNKS, CONV_PAD, 8, LANES), F32)],
        compiler_params=_cparams(("arbitrary",), VMEM_LIMIT),
    )(dcv, dcv, proj, proj, proj, proj, conv_w)


def _fold_kv_head(dup, lo, second_half):
    both = dup + pltpu.roll(dup, HEAD_DIM, 1)
    lo = lax.broadcasted_iota(jnp.int32, dup.shape, 1) < HEAD_DIM
    return jnp.where(lo, 0.0, both) if second_half else jnp.where(lo, both, 0.0)


def _attention_backward(dya, o, qr, kd, vd, proj, sinks, w_ao, cos_t, sin_up, sin_dn):
    tokens = qr.shape[0]
    tm = min(512, tokens)
    per_tile = tm // BLOCK
    nt = tokens // tm
    scale = HEAD_DIM ** -0.5

    def body(dya_ref, o_ref, ag0_ref, ag1_ref, q_ref, kc_ref, kp_ref, vc_ref, vp_ref, sinks_ref, wao_ref,
             cos_c, up_c, dn_c, cos_p, up_p, dn_p,
             dq_ref, dkv_ref, dag_ref, gwao_ref, gsink_ref,
             gacc, k_ext, v_ext, dk_ext, dv_ext, dk_carry, dv_carry, do_scr, dq_scr):
        i = pl.program_id(0)
        lo = _lane_halves()

        @pl.when(i == 0)
        def _():
            gacc[...] = jnp.zeros_like(gacc)
            gsink_ref[...] = jnp.zeros_like(gsink_ref)
            dk_carry[...] = jnp.zeros_like(dk_carry)
            dv_carry[...] = jnp.zeros_like(dv_carry)
        dk_ext[...] = jnp.zeros_like(dk_ext)
        dv_ext[...] = jnp.zeros_like(dv_ext)

        @pl.when(i < nt)
        def _():
            dya = dya_ref[...]
            dpa = _dot_nt(dya, _square(wao_ref))
            o = o_ref[...].astype(F32)
            ag = jnp.concatenate([ag0_ref[...], ag1_ref[...]], axis=1).astype(F32)
            sg = _sig(ag)
            gate = ag * sg
            gacc[...] += _dot_tn((o * gate).astype(BF16), dya)
            dag_ref[...] = (dpa * o * (sg * (1.0 + ag * (1.0 - sg)))).astype(BF16)
            do_scr[...] = (dpa * gate).astype(BF16)
            k_ext[0:BLOCK, :], k_ext[BLOCK:BLOCK + tm, :] = kp_ref[...], kc_ref[...]
            v_ext[0:BLOCK, :], v_ext[BLOCK:BLOCK + tm, :] = vp_ref[...], vc_ref[...]

            def block(b, carry):
                r0 = pl.multiple_of(b * BLOCK, BLOCK)
                mine, band = pl.ds(r0, BLOCK), pl.ds(r0, 2 * BLOCK)
                mask = _band_mask(i * per_tile + b)
                head_lane = lax.broadcasted_iota(jnp.int32, (1, LANES), 1)
                gsink = jnp.zeros((1, LANES), F32)
                zero_band = jnp.zeros((2 * BLOCK, LANES), F32)
                dk_band, dv_band = [zero_band, zero_band], [zero_band, zero_band]
                for kvh in range(N_KV_HEADS):
                    ta, tb = slice(LANES * 2 * kvh, LANES * (2 * kvh + 1)), slice(LANES * (2 * kvh + 1), LANES * (2 * kvh + 2))
                    ks = slice(LANES * kvh, LANES * (kvh + 1))
                    q_stack = _stack_heads(q_ref[mine, ta], q_ref[mine, tb], lo)
                    do_stack = _stack_heads(do_scr[mine, ta], do_scr[mine, tb], lo)
                    k2, v2 = k_ext[band, ks], v_ext[band, ks]
                    e_p, e_c, e_s = _band_exp(q_stack, k2, sinks_ref, kvh, mask)
                    inv = 1.0 / (_row_sums(jnp.concatenate([e_p, e_c], axis=1).astype(BF16)) + e_s)
                    p_p, p_c = e_p * inv, e_c * inv
                    dp = _dot_nt(do_stack, v2)
                    dp_p, dp_c = dp[:, :BLOCK], dp[:, BLOCK:]
                    delta = jnp.broadcast_to(jnp.sum(p_p * dp_p + p_c * dp_c, axis=1, keepdims=True), (4 * BLOCK, LANES))
                    ds = jnp.concatenate([p_p * (dp_p - delta), p_c * (dp_c - delta)], axis=1).astype(BF16)
                    sink_terms = e_s * inv * delta
                    for g in range(4):
                        total = jnp.sum(sink_terms[BLOCK * g:BLOCK * (g + 1)], axis=0, keepdims=True)
                        gsink = gsink - jnp.where(head_lane == 4 * kvh + g, total, 0.0)
                    dq_scr[mine, ta], dq_scr[mine, tb] = _unstack_heads(_dot(ds, k2), lo)
                    tile, second = kvh // 2, kvh % 2 == 1
                    dk_band[tile] = dk_band[tile] + _fold_kv_head(_dot_tn(ds, q_stack), lo, second)
                    dv_band[tile] = dv_band[tile] + _fold_kv_head(
                        _dot_tn(jnp.concatenate([p_p, p_c], axis=1).astype(BF16), do_stack), lo, second)
                gsink_ref[0:1, :] += gsink
                cs, up, dn = cos_c[mine, :], up_c[mine, :], dn_c[mine, :]
                for p in range(D_MODEL // LANES):
                    sl = slice(LANES * p, LANES * (p + 1))
                    dq_ref[mine, sl] = (_rope_transposed(dq_scr[mine, sl], cs, up, dn) * scale).astype(BF16)
                for p in range(2):
                    sl = slice(LANES * p, LANES * (p + 1))
                    dk_ext[band, sl] += dk_band[p]
                    dv_ext[band, sl] += dv_band[p]
                return carry
            lax.fori_loop(0, per_tile, block, 0)

        last = slice(tm - BLOCK, tm)
        dk_carry[last, :] += dk_ext[0:BLOCK, :]
        dv_carry[last, :] += dv_ext[0:BLOCK, :]
        for p in range(2):
            sl = slice(LANES * p, LANES * (p + 1))
            dkv_ref[:, sl] = _rope_transposed(dk_carry[:, sl], cos_p[...], up_p[...], dn_p[...]).astype(BF16)
            dkv_ref[:, slice(256 + LANES * p, 256 + LANES * (p + 1))] = dv_carry[:, sl].astype(BF16)
        dk_carry[...] = dk_ext[BLOCK:BLOCK + tm, :]
        dv_carry[...] = dv_ext[BLOCK:BLOCK + tm, :]

        @pl.when(i == nt)
        def _():
            gwao_ref[...] = gacc[...].astype(BF16)

    def cur_idx(i):
        return jnp.minimum(i, nt - 1)

    def prev_idx(i):
        return jnp.clip(i - 1, 0, nt - 1)

    cur = lambda w, col=0: pl.BlockSpec((tm, w), lambda i: (cur_idx(i), col))
    prev = lambda w: pl.BlockSpec((tm, w), lambda i: (prev_idx(i), 0))
    before = lambda w: pl.BlockSpec((BLOCK, w), lambda i: (jnp.maximum(cur_idx(i) * per_tile - 1, 0), 0))
    return pl.pallas_call(
        body, name="attention_backward", grid=(nt + 1,),
        in_specs=[cur(D_MODEL), cur(D_MODEL), cur(512, COL512_AG), cur(512, COL512_AG + 1), cur(D_MODEL),
                  cur(512), before(512), cur(512), before(512),
                  pl.BlockSpec(memory_space=pltpu.SMEM), _pack_weight_spec(1),
                  cur(LANES), cur(LANES), cur(LANES), prev(LANES), prev(LANES), prev(LANES)],
        out_specs=(cur(D_MODEL), prev(512), cur(D_MODEL),
                   _const_spec((D_MODEL, D_MODEL)), _const_spec((8, LANES))),
        out_shape=(jax.ShapeDtypeStruct((tokens, D_MODEL), BF16),
                   jax.ShapeDtypeStruct((tokens, 512), BF16),
                   jax.ShapeDtypeStruct((tokens, D_MODEL), BF16),
                   jax.ShapeDtypeStruct((D_MODEL, D_MODEL), BF16),
                   jax.ShapeDtypeStruct((8, LANES), F32)),
        scratch_shapes=[pltpu.VMEM((D_MODEL, D_MODEL), F32),
                        pltpu.VMEM((BLOCK + tm, 512), BF16), pltpu.VMEM((BLOCK + tm, 512), BF16),
                        pltpu.VMEM((BLOCK + tm, 256), F32), pltpu.VMEM((BLOCK + tm, 256), F32),
                        pltpu.VMEM((tm, 256), F32), pltpu.VMEM((tm, 256), F32),
                        pltpu.VMEM((tm, D_MODEL), BF16), pltpu.VMEM((tm, D_MODEL), F32)],
        compiler_params=_cparams(("arbitrary",), VMEM_LIMIT),
    )(dya, o, proj, proj, qr, kd, kd, vd, vd, sinks, w_ao, cos_t, sin_up, sin_dn, cos_t, sin_up, sin_dn)


def _transpose_tokens(h):
    tokens = h.shape[0]
    tt = min(512, tokens)

    def body(h_ref, out_ref):
        out_ref[...] = h_ref[...].astype(F32).T.astype(BF16)

    return pl.pallas_call(
        body, name="transpose_tokens", grid=(tokens // tt,),
        in_specs=[pl.BlockSpec((tt, D_MODEL), lambda i: (i, 0))],
        out_specs=pl.BlockSpec((D_MODEL, tt), lambda i: (0, i)),
        out_shape=jax.ShapeDtypeStruct((D_MODEL, tokens), BF16),
        compiler_params=_cparams(("parallel",)),
    )(h)


def _input_backward(sections, w_in_t, x, dx2, norm_g):
    tokens = x.shape[0]
    tm = 256

    def body(*refs):
        sec = refs[:8]
        w_ref, x_ref, dx2_ref, g_ref, gx_ref, part_ref = refs[8:]

        @pl.when(pl.program_id(0) == 0)
        def _():
            part_ref[...] = jnp.zeros_like(part_ref)

        dh = jnp.zeros((tm, D_MODEL), F32)
        for s in range(8):
            dh = dh + _dot(sec[s][...], w_ref[_SECTION_ROWS[s]:_SECTION_ROWS[s] + _SECTION_WIDTH[s], :])
        xv = x_ref[...]
        r = lax.rsqrt(jnp.mean(xv * xv, axis=-1, keepdims=True) + RMS_EPS)
        xn = xv * r
        part_ref[0:1, :] += jnp.sum(dh * xn, axis=0, keepdims=True)
        dxn = dh * g_ref[...]
        gx_ref[...] = dx2_ref[...] + r * (dxn - xn * jnp.mean(dxn * xn, axis=-1, keepdims=True))

    tile = lambda w=D_MODEL: pl.BlockSpec((tm, w), lambda i: (i, 0))
    return pl.pallas_call(
        body, name="input_backward", grid=(tokens // tm,),
        in_specs=[tile(w) for w in _SECTION_WIDTH] + [
            pl.BlockSpec((IN_WIDTH, D_MODEL), lambda i: (0, 0), pipeline_mode=pl.Buffered(1)),
            tile(), tile(), _const_spec((1, D_MODEL))],
        out_specs=(tile(), _const_spec((8, D_MODEL))),
        out_shape=(jax.ShapeDtypeStruct((tokens, D_MODEL), F32),
                   jax.ShapeDtypeStruct((8, D_MODEL), F32)),
        compiler_params=_cparams(("arbitrary",), VMEM_LIMIT),
    )(*sections, w_in_t, x, dx2, norm_g)


def _adamw_math(w, g, m, v):
    m = ADAM_B1 * m + (1.0 - ADAM_B1) * g
    v = ADAM_B2 * v + (1.0 - ADAM_B2) * (g * g)
    m_hat = m / (1.0 - ADAM_B1 ** ADAM_STEP)
    v_hat = v / (1.0 - ADAM_B2 ** ADAM_STEP)
    delta = -ADAM_LR * (m_hat / (jnp.sqrt(v_hat) + ADAM_EPS) + ADAM_WD * w)
    return delta, m, v


def _sum_slots(recv_ref):
    total = recv_ref[0].astype(F32)
    for d in range(1, N_DEV):
        total = total + recv_ref[d].astype(F32)
    return total


def _adamw(name, w, g, m, v, tile_rows):
    rows, cols = w.shape

    def body(w_ref, g_ref, m_ref, v_ref, d_ref, nm_ref, nv_ref):
        d_ref[...], nm_ref[...], nv_ref[...] = _adamw_math(w_ref[...], g_ref[...], m_ref[...], v_ref[...])

    spec = pl.BlockSpec((tile_rows, cols), lambda i: (i, 0))
    shape = jax.ShapeDtypeStruct((rows, cols), F32)
    return pl.pallas_call(
        body, name=name, grid=(rows // tile_rows,),
        in_specs=[spec] * 4, out_specs=(spec,) * 3, out_shape=(shape,) * 3,
        compiler_params=_cparams(("parallel",)),
    )(w, g, m, v)


def _sum_adamw(name, recv, w, m, v):
    def body(recv_ref, w_ref, m_ref, v_ref, g_ref, d_ref, nm_ref, nv_ref):
        g = _sum_slots(recv_ref)
        g_ref[...] = g
        d_ref[...], nm_ref[...], nv_ref[...] = _adamw_math(w_ref[...], g, m_ref[...], v_ref[...])

    shape = jax.ShapeDtypeStruct(w.shape, F32)
    return pl.pallas_call(body, name=name, out_shape=(shape,) * 4)(recv, w, m, v)


def _pad_rows(a, rows):
    return jnp.concatenate([a, jnp.zeros((rows - a.shape[0],) + a.shape[1:], a.dtype)], axis=0)


def kernel(x, norm_g, w_in, conv_dw_w, conv_dw_b, conv_ln_g, conv_ln_b, w_conv_out, attn_sinks, w_attn_out, w_out, final_norm_g, loss_target, m_norm_g, m_w_in, m_conv_dw_w, m_conv_dw_b, m_conv_ln_g, m_conv_ln_b, m_w_conv_out, m_attn_sinks, m_w_attn_out, m_w_out, m_final_norm_g, v_norm_g, v_w_in, v_conv_dw_w, v_conv_dw_b, v_conv_ln_g, v_conv_ln_b, v_w_conv_out, v_attn_sinks, v_w_attn_out, v_w_out, v_final_norm_g):
    xs, target = x[0], loss_target[0]
    tokens = xs.shape[0]
    fg_row = final_norm_g.reshape(1, D_MODEL)

    taps_bits = lax.bitcast_convert_type(_pad_rows(conv_dw_w[0], CONV_PAD), BF16).reshape(8, D_MODEL)
    pack = jnp.concatenate([w_conv_out[0].astype(BF16), w_attn_out[0].astype(BF16), w_out[0].astype(BF16),
                            jnp.pad(taps_bits, ((0, PACK_ROWS - 3 * SHARD_SQ - 8), (0, 0)))], axis=0)
    w_in_t32 = w_in[0].T
    proj, h, w_in_t, pack_full = _gather_project(xs, norm_g, w_in_t32.astype(BF16), pack)
    w_co = w_ao = w_o = pack_full
    conv_w = lax.bitcast_convert_type(
        pack_full[:, 3 * SHARD_SQ:3 * SHARD_SQ + 8].reshape(N_DEV, CONV_PAD, LANES, 2), F32)

    cos_t, sin_up, sin_dn = _rope_tables(tokens)
    qr, kd, vd = _rope_qkv(proj, cos_t, sin_up, sin_dn)
    cv, yc = _conv_forward(proj, conv_w, conv_dw_b, conv_ln_g, conv_ln_b, w_co)
    o, ya = _attention_forward(qr, kd, vd, proj, attn_sinks, w_ao)

    dx2, dyc, dya, dmlc, dmla, g_out, part_head = _merge_and_head(yc, ya, proj, xs, target, w_o, fg_row)
    dcv, dcg, g_co, part_conv = _conv_backward_pointwise(dyc, cv, proj, w_co, conv_ln_g, conv_ln_b)
    da, db, g_conv = _conv_backward_taps(dcv, proj, conv_w)
    dq, dkv, dag, g_ao, part_sink = _attention_backward(dya, o, qr, kd, vd, proj, attn_sinks, w_ao, cos_t, sin_up, sin_dn)
    sections = (da, db, dcg, dq, dkv, dag, dmlc, dmla)
    grad_x, part_in = _input_backward(sections, w_in_t, xs, dx2, norm_g)

    small = jnp.concatenate([
        part_in[0:1], part_conv[2:3], part_conv[0:1], part_conv[1:2], part_head[0:1],
        jnp.pad(part_sink[0:1], ((0, 0), (0, D_MODEL - LANES))), part_head[1:2],
        jnp.zeros((1, D_MODEL), F32)], axis=0)

    g_mine, r_conv, r_small = _grad_exchange(sections, _transpose_tokens(h), g_co, g_ao, g_out, g_conv, small)

    g_in_t = g_mine[:SHARD_IN]
    w_in_res = _adamw("adamw_w_in", w_in_t32, g_in_t, m_w_in[0].T, v_w_in[0].T, 192)
    grad_w_in, d_w_in, nm_w_in, nv_w_in = (a.T for a in (g_in_t,) + tuple(w_in_res))
    sq = {}
    for j, (nm, w, m, v) in enumerate((("w_conv_out", w_conv_out, m_w_conv_out, v_w_conv_out),
                                       ("w_attn_out", w_attn_out, m_w_attn_out, v_w_attn_out),
                                       ("w_out", w_out, m_w_out, v_w_out))):
        g = g_mine[SHARD_IN + j * SHARD_SQ:SHARD_IN + (j + 1) * SHARD_SQ]
        sq[nm] = (g,) + tuple(_adamw("adamw_" + nm, w[0], g, m[0], v[0], SHARD_SQ))
    conv_res = _sum_adamw("sum_adamw_conv_dw_w", r_conv.reshape(N_DEV, CONV_PAD, LANES),
                          _pad_rows(conv_dw_w[0], CONV_PAD), _pad_rows(m_conv_dw_w[0], CONV_PAD),
                          _pad_rows(v_conv_dw_w[0], CONV_PAD))
    pad_sink = lambda a: jnp.pad(a, ((0, 0), (0, D_MODEL - N_Q_HEADS)))
    zero_rows = jnp.zeros((2, D_MODEL), F32)
    stack = lambda a, b, c, d, e, f: jnp.concatenate([a, b, c, d, e.reshape(1, D_MODEL), pad_sink(f), zero_rows], axis=0)
    small_res = _sum_adamw(
        "sum_adamw_small", r_small,
        stack(norm_g, conv_dw_b, conv_ln_g, conv_ln_b, final_norm_g, attn_sinks),
        stack(m_norm_g, m_conv_dw_b, m_conv_ln_g, m_conv_ln_b, m_final_norm_g, m_attn_sinks),
        stack(v_norm_g, v_conv_dw_b, v_conv_ln_g, v_conv_ln_b, v_final_norm_g, v_attn_sinks))
    loss = jnp.sum(small_res[0][6])

    def leaf(k):
        s = small_res[k]
        return (s[0:1], (grad_w_in, d_w_in, nm_w_in, nv_w_in)[k][None], conv_res[k][None, :CONV_KERNEL],
                s[1:2], s[2:3], s[3:4], sq["w_conv_out"][k][None], s[5:6, :N_Q_HEADS],
                sq["w_attn_out"][k][None], sq["w_out"][k][None], s[4])

    return (loss, grad_x[None], *leaf(0), *leaf(1), *leaf(2), *leaf(3))
```

```python
import jax
import jax.numpy as jnp
from jax import lax
from jax.experimental import pallas as pl
from jax.experimental.pallas import tpu as pltpu

F32 = jnp.float32
BF16 = jnp.bfloat16
MESH = pl.DeviceIdType.MESH

D_MODEL = 1024
IN_WIDTH = 7680
N_DEV = 8
SHARD_IN = IN_WIDTH // N_DEV
SHARD_SQ = D_MODEL // N_DEV
CONV_KERNEL = 31
CONV_PAD = 32
HEAD_DIM = 64
N_Q_HEADS = 16
N_KV_HEADS = 4
BLOCK = 128
LANES = 128
ROPE_THETA = 10000.0
RMS_EPS = 1e-5
LN_EPS = 1e-5
NEG = -1e30
ADAM_LR = 0.001
ADAM_B1 = 0.9
ADAM_B2 = 0.999
ADAM_EPS = 1e-08
ADAM_WD = 0.01
ADAM_STEP = 10

OFF_A, OFF_B, OFF_CG, OFF_Q, OFF_KV, OFF_AG, OFF_MLC, OFF_MLA = 0, 1024, 2048, 3072, 4096, 4608, 5632, 6656
COL_A, COL_B, COL_CG, COL_Q = 0, 1, 2, 3
COL512_KV, COL512_AG, COL512_MLC, COL512_MLA = 8, 9, 11, 13
UNIT = 2 * SHARD_IN
PACK_ROWS = 400

VMEM_LIMIT = 56 * 1024 * 1024


def _cparams(sem=None, vmem=None):
    return pltpu.CompilerParams(dimension_semantics=sem, vmem_limit_bytes=vmem)


def _sig(v):
    return 0.5 * jnp.tanh(0.5 * v) + 0.5


def _dot(a, b):
    return jnp.dot(a, b, preferred_element_type=F32)


def _dot_nt(a, b):
    return lax.dot_general(a, b, (((1,), (1,)), ((), ())), preferred_element_type=F32)


def _dot_tn(a, b):
    return lax.dot_general(a, b, (((0,), (0,)), ((), ())), preferred_element_type=F32)


def _const_spec(shape):
    nd = len(shape)
    return pl.BlockSpec(shape, lambda *_: (0,) * nd)


def _pack_weight_spec(j):
    return pl.BlockSpec((N_DEV, SHARD_SQ, D_MODEL), lambda *_: (0, j, 0))


def _square(w_ref):
    return w_ref[...].reshape(D_MODEL, D_MODEL)


def _mesh_pos():
    x, y, c = lax.axis_index("x"), lax.axis_index("y"), lax.axis_index("c")
    return x, y, c, 4 * x + 2 * y + c


def _peer(x, y, c, k):
    px = 1 - x if (k >> 2) & 1 else x
    py = 1 - y if (k >> 1) & 1 else y
    pc = 1 - c if k & 1 else c
    return (px, py, pc), 4 * px + 2 * py + pc


def _gather_project(x, norm_g, w_shard_t, pack):
    tokens = x.shape[0]
    tt = min(512, tokens // 2)
    n_tok = tokens // tt
    rc = min(128, tt)

    def body(x_hbm, g_ref, ws_hbm, pack_hbm, proj_hbm, h_hbm, wfull_hbm, packfull_hbm,
             w_vmem, h_vmem, x_buf, o_buf, send_sems, recv_sems, local_sems, x_sems, o_sems):
        x_, y_, c_, me = _mesh_pos()
        myself, sibling = (x_, y_, c_), (x_, y_, 1 - c_)
        chips = ((1 - x_, y_), (x_, 1 - y_), (1 - x_, 1 - y_))

        def shard(ref, idx):
            return ref.at[pl.ds(pl.multiple_of(idx * SHARD_IN, 64), SHARD_IN)]

        def copy(a, k, idx, to, own=False):
            if a == 0:
                src, dst = ws_hbm if own else shard(w_vmem, idx), shard(w_vmem, idx)
            else:
                src, dst = pack_hbm if own else packfull_hbm.at[idx], packfull_hbm.at[idx]
            return pltpu.make_async_remote_copy(src_ref=src, dst_ref=dst, send_sem=send_sems.at[a, k],
                                                recv_sem=recv_sems.at[a, k], device_id=to, device_id_type=MESH)

        own_w = pltpu.make_async_copy(ws_hbm, shard(w_vmem, me), local_sems.at[0])
        own_p = pltpu.make_async_copy(pack_hbm, packfull_hbm.at[me], local_sems.at[1])
        own_w.start()
        own_p.start()
        sent = []
        for a in range(2):
            sent.append(copy(a, 0, me, sibling, own=True))
            sent += [copy(a, 1 + r, me, (*chip, c_), own=True) for r, chip in enumerate(chips)]
        for cp in sent:
            cp.start()

        def x_copy(t, slot):
            return pltpu.make_async_copy(x_hbm.at[pl.ds(t * tt, tt)], x_buf.at[slot], x_sems.at[slot])

        x_copy(0, 0).start()
        for t in range(n_tok):
            slot = t % 2
            if t + 1 < n_tok:
                x_copy(t + 1, 1 - slot).start()
            x_copy(t, slot).wait()

            def chunk(r0, t=t, slot=slot):
                xv = x_buf[slot, pl.ds(r0, rc), :]
                r = lax.rsqrt(jnp.mean(xv * xv, axis=-1, keepdims=True) + RMS_EPS)
                h_vmem[pl.ds(t * tt + r0, rc), :] = (xv * r * g_ref[...]).astype(BF16)
            _row_chunks(tt, rc, chunk)
        h_out = pltpu.make_async_copy(h_vmem, h_hbm, local_sems.at[6])
        h_out.start()
        local = [own_p, h_out]

        def project_unit(q, u):
            rows = pl.ds(pl.multiple_of(q * UNIT, LANES), UNIT)
            w_out = pltpu.make_async_copy(w_vmem.at[rows], wfull_hbm.at[rows], local_sems.at[2 + u])
            w_out.start()
            local.append(w_out)

            def o_copy(slot, t):
                return pltpu.make_async_copy(
                    o_buf.at[slot], proj_hbm.at[pl.ds(pl.multiple_of(t * tt, tt), tt), rows], o_sems.at[slot])

            def tile(t, carry):
                slot = lax.rem(t, 2)

                @pl.when(t >= 2)
                def _():
                    o_copy(slot, t).wait()
                o_buf[slot] = _dot_nt(h_vmem[pl.ds(pl.multiple_of(t * tt, tt), tt), :], w_vmem[rows, :]).astype(BF16)
                o_copy(slot, t).start()
                return carry
            lax.fori_loop(0, n_tok, tile, 0)
            o_copy(0, 0).wait()
            o_copy(1, 0).wait()

        def dev(chip, core):
            return 4 * chip[0] + 2 * chip[1] + core

        def arrive_and_pass_on(a, r):
            copy(a, 1 + r, dev(chips[r], c_), myself).wait_recv()
            passed = copy(a, 4 + r, dev(chips[r], c_), sibling)
            passed.start()
            sent.append(passed)

        def passed_on_to_me(a, r):
            copy(a, 4 + r, dev(chips[r], 1 - c_), myself).wait_recv()

        own_w.wait()
        copy(0, 0, dev((x_, y_), 1 - c_), myself).wait_recv()
        project_unit(2 * x_ + y_, 0)
        arrive_and_pass_on(0, 0)
        arrive_and_pass_on(0, 1)
        passed_on_to_me(0, 0)
        project_unit(2 * chips[0][0] + chips[0][1], 1)
        arrive_and_pass_on(0, 2)
        passed_on_to_me(0, 1)
        project_unit(2 * chips[1][0] + chips[1][1], 2)
        passed_on_to_me(0, 2)
        project_unit(2 * chips[2][0] + chips[2][1], 3)
        for r in range(3):
            arrive_and_pass_on(1, r)
        copy(1, 0, dev((x_, y_), 1 - c_), myself).wait_recv()
        for r in range(3):
            passed_on_to_me(1, r)
        for cp in sent:
            cp.wait_send()
        for cp in local:
            cp.wait()

    hbm = pl.BlockSpec(memory_space=pltpu.HBM)
    return pl.pallas_call(
        body, name="gather_project",
        in_specs=[hbm, pl.BlockSpec(memory_space=pltpu.VMEM), hbm, hbm],
        out_specs=(hbm, hbm, hbm, hbm),
        out_shape=(jax.ShapeDtypeStruct((tokens, IN_WIDTH), BF16),
                   jax.ShapeDtypeStruct((tokens, D_MODEL), BF16),
                   jax.ShapeDtypeStruct((IN_WIDTH, D_MODEL), BF16),
                   jax.ShapeDtypeStruct((N_DEV, PACK_ROWS, D_MODEL), BF16)),
        scratch_shapes=[pltpu.VMEM((IN_WIDTH, D_MODEL), BF16),
                        pltpu.VMEM((tokens, D_MODEL), BF16),
                        pltpu.VMEM((2, tt, D_MODEL), F32),
                        pltpu.VMEM((2, tt, UNIT), BF16),
                        pltpu.SemaphoreType.DMA((2, N_DEV - 1)),
                        pltpu.SemaphoreType.DMA((2, N_DEV - 1)),
                        pltpu.SemaphoreType.DMA((7,)),
                        pltpu.SemaphoreType.DMA((2,)),
                        pltpu.SemaphoreType.DMA((2,))],
        compiler_params=_cparams(None, VMEM_LIMIT),
    )(x, norm_g, w_shard_t, pack)


TAIL_ROWS = 3 * SHARD_SQ
HALF_ROWS = SHARD_IN + TAIL_ROWS

GRAD_CHUNK = 384
_SECTION_ROWS = (OFF_A, OFF_B, OFF_CG, OFF_Q, OFF_KV, OFF_AG, OFF_MLC, OFF_MLA)
_SECTION_WIDTH = (1024, 1024, 1024, 1024, 512, 1024, 1024, 1024)


def _dproj_pieces(first, width):
    out = []
    for s, (start, w) in enumerate(zip(_SECTION_ROWS, _SECTION_WIDTH)):
        lo, hi = max(first, start), min(first + width, start + w)
        if lo < hi:
            out.append((s, lo - start, hi - lo, lo - first))
    return out


def _grad_exchange(sections, h_t, g_co, g_ao, g_out, g_conv, small):
    tokens = h_t.shape[1]
    n_chunk = UNIT // GRAD_CHUNK
    rc = 192

    def body(*refs):
        sec = refs[:8]
        (ht_hbm, gco_hbm, gao_hbm, gout_hbm, gconv_hbm, small_hbm, gmine_hbm, rconv_hbm, rsmall_hbm,
         lhs_buf, ht_vmem, halves, tail_buf, out_buf, stage, final,
         lhs_sems, ht_sem, tail_in_sems, d2d_send, d2d_recv, ici_send, ici_recv,
         d2d_tail_send, d2d_tail_recv, ici_tail_send, ici_tail_recv,
         tiny_send, tiny_recv, local_sems) = refs[8:]
        head_rows, tail_rows = pl.ds(0, SHARD_IN), pl.ds(SHARD_IN, TAIL_ROWS)
        x_, y_, c_, me = _mesh_pos()
        myself, sibling = (x_, y_, c_), (x_, y_, 1 - c_)
        chips = ((1 - x_, 1 - y_), (1 - x_, y_), (x_, 1 - y_), (x_, y_))
        squares = (gco_hbm, gao_hbm, gout_hbm)

        def remote(src, dst, send_sem, recv_sem, to):
            return pltpu.make_async_remote_copy(src_ref=src, dst_ref=dst, send_sem=send_sem, recv_sem=recv_sem,
                                                device_id=to, device_id_type=MESH)

        own_tiny = [pltpu.make_async_copy(gconv_hbm.at[me], rconv_hbm.at[me], local_sems.at[0]),
                    pltpu.make_async_copy(small_hbm, rsmall_hbm.at[me], local_sems.at[1])]
        for cp in own_tiny:
            cp.start()
        tiny = []
        for k in range(1, N_DEV):
            peer, peer_idx = _peer(x_, y_, c_, k)
            tiny += [remote(gconv_hbm.at[peer_idx], rconv_hbm.at[me], tiny_send.at[0, k - 1], tiny_recv.at[0, k - 1], peer),
                     remote(small_hbm, rsmall_hbm.at[me], tiny_send.at[1, k - 1], tiny_recv.at[1, k - 1], peer)]
        for cp in tiny:
            cp.start()

        ht_in = pltpu.make_async_copy(ht_hbm, ht_vmem, ht_sem.at[0])
        ht_in.start()

        def fetch(q, j, slot, wait):
            for k in range(4):
                @pl.when(q == k)
                def _(k=k):
                    for n, (s, col, width, place) in enumerate(_dproj_pieces(k * UNIT + j * GRAD_CHUNK, GRAD_CHUNK)):
                        cp = pltpu.make_async_copy(sec[s].at[pl.ds(0, tokens), pl.ds(col, width)],
                                                   lhs_buf.at[slot, pl.ds(0, tokens), pl.ds(place, width)],
                                                   lhs_sems.at[slot, n])
                        cp.wait() if wait else cp.start()

        def chip_of(u):
            return 2 * chips[u][0] + chips[u][1]

        def d2d(u):
            return remote(halves.at[1 - c_], stage.at[u, head_rows], d2d_send.at[u], d2d_recv.at[u], sibling)

        def ici(u):
            return remote(stage.at[u, head_rows], final.at[u, head_rows], ici_send.at[u], ici_recv.at[u], (*chips[u], c_))

        def d2d_tail(u):
            return remote(tail_buf.at[u, 1 - c_], stage.at[u, tail_rows], d2d_tail_send.at[u], d2d_tail_recv.at[u], sibling)

        def ici_tail(u):
            return remote(stage.at[u, tail_rows], final.at[u, tail_rows], ici_tail_send.at[u], ici_tail_recv.at[u],
                          (*chips[u], c_))

        def tail_in(u):
            out = []
            for core in range(2):
                for n, g in enumerate(squares):
                    rows = pl.ds(pl.multiple_of((2 * chip_of(u) + core) * SHARD_SQ, SHARD_SQ), SHARD_SQ)
                    out.append(pltpu.make_async_copy(g.at[rows], tail_buf.at[u, core, pl.ds(n * SHARD_SQ, SHARD_SQ)],
                                                     tail_in_sems.at[u, 3 * core + n]))
            return out

        def add_mine(u, first, count, mine):
            def chunk(r0):
                rows = pl.ds(pl.multiple_of(first + r0, 64), rc)
                stage[u, rows, :] = (stage[u, rows, :].astype(F32) + mine(pl.ds(r0, rc)).astype(F32)).astype(BF16)
            _row_chunks(count, rc, chunk)

        def chip_sum(u):
            d2d(u).wait_recv()
            add_mine(u, 0, SHARD_IN, lambda rows: halves[c_, rows, :])
            if u < 3:
                ici(u).start()

        for u in range(4):
            for cp in tail_in(u):
                cp.start()
        for u in range(4):
            for cp in tail_in(u):
                cp.wait()
            d2d_tail(u).start()
        for u in range(4):
            d2d_tail(u).wait_recv()
            add_mine(u, SHARD_IN, TAIL_ROWS, lambda rows, u=u: tail_buf[u, c_, rows, :])
            if u < 3:
                ici_tail(u).start()

        def store_rows(block, first):
            n = block.shape[0]
            for core in range(2):
                lo, hi = max(first, core * SHARD_IN), min(first + n, (core + 1) * SHARD_IN)
                if lo < hi:
                    halves[core, lo - core * SHARD_IN:hi - core * SHARD_IN, :] = block[lo - first:hi - first].astype(BF16)

        fetch(chip_of(0), 0, 0, wait=False)
        ht_in.wait()
        for u in range(4):
            q = chip_of(u)
            for j in range(n_chunk):
                slot = (u * n_chunk + j) % 2
                if j + 1 < n_chunk:
                    fetch(q, j + 1, 1 - slot, wait=False)
                elif u + 1 < 4:
                    fetch(chip_of(u + 1), 0, 1 - slot, wait=False)
                fetch(q, j, slot, wait=True)
                grad_t = _dot(ht_vmem[...], lhs_buf[slot])
                if j == 0 and u > 0:
                    chip_sum(u - 1)
                    d2d(u - 1).wait_send()
                for r in range(GRAD_CHUNK // LANES):
                    store_rows(grad_t[:, LANES * r:LANES * (r + 1)].T, j * GRAD_CHUNK + LANES * r)
            d2d(u).start()

        chip_sum(3)
        for u in range(3):
            remote(stage.at[u, head_rows], final.at[u, head_rows], ici_send.at[u], ici_recv.at[u], myself).wait_recv()
            remote(stage.at[u, tail_rows], final.at[u, tail_rows], ici_tail_send.at[u], ici_tail_recv.at[u],
                   myself).wait_recv()

        def total(r0):
            rows = pl.ds(r0, rc)
            out_buf[rows, :] = ((stage[3, rows, :].astype(F32) + final[0, rows, :].astype(F32))
                                + final[1, rows, :].astype(F32)) + final[2, rows, :].astype(F32)
        _row_chunks(HALF_ROWS, rc, total)
        out = pltpu.make_async_copy(out_buf, gmine_hbm, local_sems.at[2])
        out.start()
        d2d(3).wait_send()
        for u in range(4):
            d2d_tail(u).wait_send()
        for u in range(3):
            ici(u).wait_send()
            ici_tail(u).wait_send()
        for k in range(1, N_DEV):
            peer, peer_idx = _peer(x_, y_, c_, k)
            remote(gconv_hbm.at[me], rconv_hbm.at[peer_idx], tiny_send.at[0, k - 1], tiny_recv.at[0, k - 1], myself).wait_recv()
            remote(small_hbm, rsmall_hbm.at[peer_idx], tiny_send.at[1, k - 1], tiny_recv.at[1, k - 1], myself).wait_recv()
        for cp in tiny:
            cp.wait_send()
        for cp in own_tiny:
            cp.wait()
        out.wait()

    hbm = pl.BlockSpec(memory_space=pltpu.HBM)
    return pl.pallas_call(
        body, name="grad_exchange",
        in_specs=[hbm] * 14, out_specs=(hbm, hbm, hbm),
        out_shape=(jax.ShapeDtypeStruct((HALF_ROWS, D_MODEL), F32),
                   jax.ShapeDtypeStruct((N_DEV, CONV_PAD, LANES), F32),
                   jax.ShapeDtypeStruct((N_DEV, 8, D_MODEL), F32)),
        scratch_shapes=[pltpu.VMEM((2, tokens, GRAD_CHUNK), BF16),
                        pltpu.VMEM((D_MODEL, tokens), BF16),
                        pltpu.VMEM((2, SHARD_IN, D_MODEL), BF16),
                        pltpu.VMEM((4, 2, TAIL_ROWS, D_MODEL), BF16),
                        pltpu.VMEM((HALF_ROWS, D_MODEL), F32),
                        pltpu.VMEM((4, HALF_ROWS, D_MODEL), BF16),
                        pltpu.VMEM((3, HALF_ROWS, D_MODEL), BF16),
                        pltpu.SemaphoreType.DMA((2, 3)),
                        pltpu.SemaphoreType.DMA((1,)),
                        pltpu.SemaphoreType.DMA((4, 6)),
                        pltpu.SemaphoreType.DMA((4,)),
                        pltpu.SemaphoreType.DMA((4,)),
                        pltpu.SemaphoreType.DMA((3,)),
                        pltpu.SemaphoreType.DMA((3,)),
                        pltpu.SemaphoreType.DMA((4,)),
                        pltpu.SemaphoreType.DMA((4,)),
                        pltpu.SemaphoreType.DMA((3,)),
                        pltpu.SemaphoreType.DMA((3,)),
                        pltpu.SemaphoreType.DMA((2, N_DEV - 1)),
                        pltpu.SemaphoreType.DMA((2, N_DEV - 1)),
                        pltpu.SemaphoreType.DMA((3,))],
        compiler_params=_cparams(None, 60 * 1024 * 1024),
    )(*sections, h_t, g_co, g_ao, g_out, g_conv, small)


def _row_chunks(total, size, fn):
    n = total // size
    if n == 1:
        fn(0)
        return

    def step(i, carry):
        fn(pl.multiple_of(i * size, size))
        return carry
    lax.fori_loop(0, n, step, 0)


def _rope_tables(tokens):
    inv_freq = ROPE_THETA ** (-jnp.arange(0, HEAD_DIM, 2, dtype=F32) / HEAD_DIM)
    ang = jnp.arange(tokens, dtype=jnp.int32).astype(F32)[:, None] * inv_freq[None, :]
    cos, sin = jnp.cos(ang), jnp.sin(ang)
    zero = jnp.zeros_like(sin)
    cos_t = jnp.tile(jnp.concatenate([cos, cos], axis=1), (1, LANES // HEAD_DIM))
    sin_up = jnp.tile(jnp.concatenate([-sin, zero], axis=1), (1, LANES // HEAD_DIM))
    sin_dn = jnp.tile(jnp.concatenate([zero, sin], axis=1), (1, LANES // HEAD_DIM))
    return cos_t, sin_up, sin_dn


def _rope(t, cos_t, sin_up, sin_dn):
    return t * cos_t + pltpu.roll(t, LANES - 32, 1) * sin_up + pltpu.roll(t, 32, 1) * sin_dn


def _rope_transposed(g, cos_t, sin_up, sin_dn):
    return g * cos_t + pltpu.roll(g * sin_up, 32, 1) + pltpu.roll(g * sin_dn, LANES - 32, 1)


def _lane_halves():
    lane = lax.broadcasted_iota(jnp.int32, (BLOCK, LANES), 1)
    return lane < HEAD_DIM


def _rope_qkv(proj, cos_t, sin_up, sin_dn):
    tokens = proj.shape[0]
    tm = min(512, tokens)
    scale = HEAD_DIM ** -0.5

    def body(q_ref, kv_ref, cos_ref, up_ref, dn_ref, qr_ref, kd_ref, vd_ref):
        lo = _lane_halves()

        def chunk(r0):
            rows = pl.ds(r0, BLOCK)
            cs, up, dn = cos_ref[rows, :], up_ref[rows, :], dn_ref[rows, :]
            for p in range(D_MODEL // LANES):
                sl = slice(LANES * p, LANES * (p + 1))
                qt = q_ref[rows, sl].astype(F32)
                qr_ref[rows, sl] = (_rope(qt, cs, up, dn) * scale).astype(BF16)
            for p in range(2):
                sl = slice(LANES * p, LANES * (p + 1))
                kt = _rope(kv_ref[rows, sl].astype(F32), cs, up, dn)
                vt = kv_ref[rows, slice(256 + LANES * p, 256 + LANES * (p + 1))].astype(F32)
                for src, dst in ((kt, kd_ref), (vt, vd_ref)):
                    first = jnp.where(lo, src, 0.0)
                    second = src - first
                    dst[rows, slice(LANES * 2 * p, LANES * (2 * p + 1))] = (first + pltpu.roll(first, HEAD_DIM, 1)).astype(BF16)
                    dst[rows, slice(LANES * (2 * p + 1), LANES * (2 * p + 2))] = (second + pltpu.roll(second, HEAD_DIM, 1)).astype(BF16)
        _row_chunks(tm, BLOCK, chunk)

    tab = pl.BlockSpec((tm, LANES), lambda i: (i, 0))
    return pl.pallas_call(
        body, name="rope_qkv", grid=(tokens // tm,),
        in_specs=[pl.BlockSpec((tm, D_MODEL), lambda i: (i, COL_Q)),
                  pl.BlockSpec((tm, 512), lambda i: (i, COL512_KV)), tab, tab, tab],
        out_specs=(pl.BlockSpec((tm, D_MODEL), lambda i: (i, 0)),
                   pl.BlockSpec((tm, 512), lambda i: (i, 0)),
                   pl.BlockSpec((tm, 512), lambda i: (i, 0))),
        out_shape=(jax.ShapeDtypeStruct((tokens, D_MODEL), BF16),
                   jax.ShapeDtypeStruct((tokens, 512), BF16),
                   jax.ShapeDtypeStruct((tokens, 512), BF16)),
        compiler_params=_cparams(("parallel",)),
    )(proj, proj, cos_t, sin_up, sin_dn)


CONV_TM = 256
N_LANE_CHUNKS = D_MODEL // LANES


def _fill_u_ext(u_ext, a_ref, b_ref, ah_ref, bh_ref, first_tile):
    for lc in range(N_LANE_CHUNKS):
        sl = slice(LANES * lc, LANES * (lc + 1))
        uh = ah_ref[:, sl].astype(F32) * _sig(bh_ref[:, sl].astype(F32))
        u_ext[lc, 0:CONV_PAD, :] = jnp.where(first_tile, 0.0, uh)
        u_ext[lc, CONV_PAD:CONV_PAD + CONV_TM, :] = a_ref[:, sl].astype(F32) * _sig(b_ref[:, sl].astype(F32))


def _conv_forward(proj, conv_w, dw_b, ln_g, ln_b, w_co):
    tokens = proj.shape[0]
    tm = CONV_TM
    halo_blocks = tm // CONV_PAD

    def body(a_ref, b_ref, ah_ref, bh_ref, cg_ref, cw_ref, dwb_ref, lng_ref, lnb_ref, wco_ref,
             cv_ref, yc_ref, u_ext, cv_scr):
        _fill_u_ext(u_ext, a_ref, b_ref, ah_ref, bh_ref, pl.program_id(0) == 0)

        def lane_chunk(lc, carry):
            for rc in range(tm // 64):
                acc = jnp.zeros((64, LANES), F32)
                for j in range(CONV_KERNEL):
                    acc = acc + cw_ref[lc, pl.ds(j, 1), :] * u_ext[lc, pl.ds(64 * rc + 2 + j, 64), :]
                cv_scr[lc, pl.ds(64 * rc, 64), :] = acc
            return carry
        lax.fori_loop(0, N_LANE_CHUNKS, lane_chunk, 0)

        cv = jnp.concatenate([cv_scr[lc] for lc in range(N_LANE_CHUNKS)], axis=1) + dwb_ref[...]
        cv_ref[...] = cv
        mu = jnp.mean(cv, axis=-1, keepdims=True)
        zc = cv - mu
        rstd = lax.rsqrt(jnp.mean(zc * zc, axis=-1, keepdims=True) + LN_EPS)
        ln = zc * rstd * lng_ref[...] + lnb_ref[...]
        cg = cg_ref[...].astype(F32)
        pc = (ln * _sig(ln)) * (cg * _sig(cg))
        yc_ref[...] = _dot(pc.astype(BF16), _square(wco_ref)).astype(BF16)

    def halo_map(i):
        return (jnp.maximum(i * halo_blocks - 1, 0), 0)

    tile = lambda col: pl.BlockSpec((tm, D_MODEL), lambda i: (i, col))
    return pl.pallas_call(
        body, name="conv_forward", grid=(tokens // tm,),
        in_specs=[tile(COL_A), tile(COL_B),
                  pl.BlockSpec((CONV_PAD, D_MODEL), lambda i: (halo_map(i)[0], COL_A)),
                  pl.BlockSpec((CONV_PAD, D_MODEL), lambda i: (halo_map(i)[0], COL_B)),
                  tile(COL_CG), _const_spec((N_DEV, CONV_PAD, LANES)),
                  _const_spec((1, D_MODEL)), _const_spec((1, D_MODEL)), _const_spec((1, D_MODEL)),
                  _pack_weight_spec(0)],
        out_specs=(pl.BlockSpec((tm, D_MODEL), lambda i: (i, 0)),
                   pl.BlockSpec((tm, D_MODEL), lambda i: (i, 0))),
        out_shape=(jax.ShapeDtypeStruct((tokens, D_MODEL), F32),
                   jax.ShapeDtypeStruct((tokens, D_MODEL), BF16)),
        scratch_shapes=[pltpu.VMEM((N_LANE_CHUNKS, CONV_PAD + tm, LANES), F32),
                        pltpu.VMEM((N_LANE_CHUNKS, tm, LANES), F32)],
        compiler_params=_cparams(("parallel",), VMEM_LIMIT),
    )(proj, proj, proj, proj, proj, conv_w, dw_b, ln_g, ln_b, w_co)


def _band_mask(n):
    row = lax.broadcasted_iota(jnp.int32, (4 * BLOCK, 2 * BLOCK), 0) & (BLOCK - 1)
    col = lax.broadcasted_iota(jnp.int32, (4 * BLOCK, 2 * BLOCK), 1)
    before = jnp.logical_and(jnp.logical_and(col < BLOCK, col > row), n > 0)
    return jnp.logical_or(before, jnp.logical_and(col >= BLOCK, col - BLOCK <= row))


def _stack_heads(tile_a, tile_b, lo):
    zero = jnp.zeros_like(tile_a)
    return jnp.concatenate([jnp.where(lo, tile_a, zero), jnp.where(lo, zero, tile_a),
                            jnp.where(lo, tile_b, zero), jnp.where(lo, zero, tile_b)], axis=0)


def _unstack_heads(stacked, lo):
    s = [stacked[BLOCK * g:BLOCK * (g + 1)] for g in range(4)]
    return (jnp.where(lo, s[0], 0.0) + jnp.where(lo, 0.0, s[1]),
            jnp.where(lo, s[2], 0.0) + jnp.where(lo, 0.0, s[3]))


def _band_exp(q_stack, k2, sinks_ref, kvh, mask):
    s = jnp.where(mask, _dot_nt(q_stack, k2), NEG)
    sink = jnp.concatenate([jnp.full((BLOCK, LANES), sinks_ref[0, 4 * kvh + g], F32) for g in range(4)], axis=0)
    m = jnp.max(jnp.maximum(s[:, :BLOCK], s[:, BLOCK:]), axis=1, keepdims=True)
    m = jnp.maximum(jnp.broadcast_to(m, (4 * BLOCK, LANES)), sink)
    return jnp.exp(s[:, :BLOCK] - m), jnp.exp(s[:, BLOCK:] - m), jnp.exp(sink - m)


def _row_sums(e_bf16):
    return _dot(e_bf16, jnp.ones((2 * BLOCK, LANES), BF16))


def _attention_forward(qr, kd, vd, proj, sinks, w_ao):
    tokens = qr.shape[0]
    tm = min(512, tokens)
    per_tile = tm // BLOCK

    def body(q_ref, kc_ref, kp_ref, vc_ref, vp_ref, ag0_ref, ag1_ref, sinks_ref, wao_ref, o_ref, ya_ref,
             k_ext, v_ext, o_scr):
        i = pl.program_id(0)
        lo = _lane_halves()
        k_ext[0:BLOCK, :], k_ext[BLOCK:BLOCK + tm, :] = kp_ref[...], kc_ref[...]
        v_ext[0:BLOCK, :], v_ext[BLOCK:BLOCK + tm, :] = vp_ref[...], vc_ref[...]

        def block(b, carry):
            r0 = pl.multiple_of(b * BLOCK, BLOCK)
            band = pl.ds(r0, 2 * BLOCK)
            mask = _band_mask(i * per_tile + b)
            for kvh in range(N_KV_HEADS):
                ta, tb = slice(LANES * 2 * kvh, LANES * (2 * kvh + 1)), slice(LANES * (2 * kvh + 1), LANES * (2 * kvh + 2))
                ks = slice(LANES * kvh, LANES * (kvh + 1))
                q_stack = _stack_heads(q_ref[pl.ds(r0, BLOCK), ta], q_ref[pl.ds(r0, BLOCK), tb], lo)
                e_p, e_c, e_s = _band_exp(q_stack, k_ext[band, ks], sinks_ref, kvh, mask)
                e = jnp.concatenate([e_p, e_c], axis=1).astype(BF16)
                o_stack = _dot(e, v_ext[band, ks]) / (_row_sums(e) + e_s)
                o_scr[pl.ds(r0, BLOCK), ta], o_scr[pl.ds(r0, BLOCK), tb] = _unstack_heads(o_stack, lo)
            return carry
        lax.fori_loop(0, per_tile, block, 0)
        o = o_scr[...]
        o_ref[...] = o.astype(BF16)
        ag = jnp.concatenate([ag0_ref[...], ag1_ref[...]], axis=1).astype(F32)
        ya_ref[...] = _dot((o * (ag * _sig(ag))).astype(BF16), _square(wao_ref)).astype(BF16)

    cur = lambda w, col=0: pl.BlockSpec((tm, w), lambda i: (i, col))
    prev = lambda w: pl.BlockSpec((BLOCK, w), lambda i: (jnp.maximum(i * per_tile - 1, 0), 0))
    return pl.pallas_call(
        body, name="attention_forward", grid=(tokens // tm,),
        in_specs=[cur(D_MODEL), cur(512), prev(512), cur(512), prev(512),
                  cur(512, COL512_AG), cur(512, COL512_AG + 1),
                  pl.BlockSpec(memory_space=pltpu.SMEM), _pack_weight_spec(1)],
        out_specs=(cur(D_MODEL), cur(D_MODEL)),
        out_shape=(jax.ShapeDtypeStruct((tokens, D_MODEL), BF16),
                   jax.ShapeDtypeStruct((tokens, D_MODEL), BF16)),
        scratch_shapes=[pltpu.VMEM((BLOCK + tm, 512), BF16), pltpu.VMEM((BLOCK + tm, 512), BF16),
                        pltpu.VMEM((tm, D_MODEL), F32)],
        compiler_params=_cparams(("parallel",), VMEM_LIMIT),
    )(qr, kd, kd, vd, vd, proj, proj, sinks, w_ao)


def _merge_and_head(yc, ya, proj, x, target, w_out, final_g):
    tokens = x.shape[0]
    tm = min(512, tokens)
    last = tokens // tm - 1

    def body(yc_ref, ya_ref, mlc0_ref, mlc1_ref, mla0_ref, mla1_ref, x_ref, t_ref, wout_ref, fg_ref,
             dx2_ref, dyc_ref, dya_ref, dmlc_ref, dmla_ref, gwout_ref, part_ref, gacc):
        i = pl.program_id(0)

        @pl.when(i == 0)
        def _():
            gacc[...] = jnp.zeros_like(gacc)
            part_ref[...] = jnp.zeros_like(part_ref)

        yc, ya = yc_ref[...].astype(F32), ya_ref[...].astype(F32)
        gc = _sig(jnp.concatenate([mlc0_ref[...], mlc1_ref[...]], axis=1).astype(F32))
        ga = _sig(jnp.concatenate([mla0_ref[...], mla1_ref[...]], axis=1).astype(F32))
        merged = (gc * yc + ga * ya).astype(BF16)
        x2 = x_ref[...] + _dot(merged, _square(wout_ref))
        r2 = lax.rsqrt(jnp.mean(x2 * x2, axis=-1, keepdims=True) + RMS_EPS)
        x2n = x2 * r2
        fg = fg_ref[...]
        err = x2n * fg - t_ref[...]
        dy = err * (1.0 / D_MODEL)
        part_ref[0:1, :] += jnp.sum(dy * x2n, axis=0, keepdims=True)
        part_ref[1:2, :] += jnp.sum(err * err, axis=0, keepdims=True) * (0.5 / D_MODEL)
        dx2n = dy * fg
        dx2 = r2 * (dx2n - x2n * jnp.mean(dx2n * x2n, axis=-1, keepdims=True))
        dx2_ref[...] = dx2
        dx2b = dx2.astype(BF16)
        gacc[...] += _dot_tn(merged, dx2b)
        dm = _dot_nt(dx2b, _square(wout_ref))
        dyc_ref[...] = (dm * gc).astype(BF16)
        dya_ref[...] = (dm * ga).astype(BF16)
        dmlc_ref[...] = (dm * yc * (gc * (1.0 - gc))).astype(BF16)
        dmla_ref[...] = (dm * ya * (ga * (1.0 - ga))).astype(BF16)

        @pl.when(i == last)
        def _():
            gwout_ref[...] = gacc[...].astype(BF16)

    tile = lambda col=0: pl.BlockSpec((tm, D_MODEL), lambda i: (i, col))
    half = lambda col: pl.BlockSpec((tm, 512), lambda i: (i, col))
    return pl.pallas_call(
        body, name="merge_and_head", grid=(tokens // tm,),
        in_specs=[tile(), tile(), half(COL512_MLC), half(COL512_MLC + 1), half(COL512_MLA), half(COL512_MLA + 1),
                  tile(), tile(), _pack_weight_spec(2), _const_spec((1, D_MODEL))],
        out_specs=(tile(), tile(), tile(), tile(), tile(),
                   _const_spec((D_MODEL, D_MODEL)), _const_spec((8, D_MODEL))),
        out_shape=(jax.ShapeDtypeStruct((tokens, D_MODEL), F32),
                   jax.ShapeDtypeStruct((tokens, D_MODEL), BF16),
                   jax.ShapeDtypeStruct((tokens, D_MODEL), BF16),
                   jax.ShapeDtypeStruct((tokens, D_MODEL), BF16),
                   jax.ShapeDtypeStruct((tokens, D_MODEL), BF16),
                   jax.ShapeDtypeStruct((D_MODEL, D_MODEL), BF16),
                   jax.ShapeDtypeStruct((8, D_MODEL), F32)),
        scratch_shapes=[pltpu.VMEM((D_MODEL, D_MODEL), F32)],
        compiler_params=_cparams(("arbitrary",), VMEM_LIMIT),
    )(yc, ya, proj, proj, proj, proj, x, target, w_out, final_g)


def _conv_backward_pointwise(dyc, cv, proj, w_co, ln_g, ln_b):
    tokens = cv.shape[0]
    tm = min(512, tokens)
    last = tokens // tm - 1

    def body(dyc_ref, cv_ref, cg_ref, wco_ref, lng_ref, lnb_ref, dcv_ref, dcg_ref, gwco_ref, part_ref, gacc):
        i = pl.program_id(0)

        @pl.when(i == 0)
        def _():
            gacc[...] = jnp.zeros_like(gacc)
            part_ref[...] = jnp.zeros_like(part_ref)

        cv = cv_ref[...]
        mu = jnp.mean(cv, axis=-1, keepdims=True)
        zc = cv - mu
        rstd = lax.rsqrt(jnp.mean(zc * zc, axis=-1, keepdims=True) + LN_EPS)
        z = zc * rstd
        lng = lng_ref[...]
        ln = z * lng + lnb_ref[...]
        sl = _sig(ln)
        c = ln * sl
        cg = cg_ref[...].astype(F32)
        scg = _sig(cg)
        gate = cg * scg
        dyc = dyc_ref[...]
        gacc[...] += _dot_tn((c * gate).astype(BF16), dyc)
        dpc = _dot_nt(dyc, _square(wco_ref))
        dcg_ref[...] = (dpc * c * (scg * (1.0 + cg * (1.0 - scg)))).astype(BF16)
        dln = dpc * gate * (sl * (1.0 + ln * (1.0 - sl)))
        part_ref[0:1, :] += jnp.sum(dln * z, axis=0, keepdims=True)
        part_ref[1:2, :] += jnp.sum(dln, axis=0, keepdims=True)
        dz = dln * lng
        dcv = rstd * (dz - jnp.mean(dz, axis=-1, keepdims=True) - z * jnp.mean(dz * z, axis=-1, keepdims=True))
        part_ref[2:3, :] += jnp.sum(dcv, axis=0, keepdims=True)
        dcv_ref[...] = dcv

        @pl.when(i == last)
        def _():
            gwco_ref[...] = gacc[...].astype(BF16)

    tile = lambda col=0: pl.BlockSpec((tm, D_MODEL), lambda i: (i, col))
    return pl.pallas_call(
        body, name="conv_backward_pointwise", grid=(tokens // tm,),
        in_specs=[tile(), tile(), tile(COL_CG), _pack_weight_spec(0),
                  _const_spec((1, D_MODEL)), _const_spec((1, D_MODEL))],
        out_specs=(tile(), tile(), _const_spec((D_MODEL, D_MODEL)), _const_spec((8, D_MODEL))),
        out_shape=(jax.ShapeDtypeStruct((tokens, D_MODEL), F32),
                   jax.ShapeDtypeStruct((tokens, D_MODEL), BF16),
                   jax.ShapeDtypeStruct((D_MODEL, D_MODEL), BF16),
                   jax.ShapeDtypeStruct((8, D_MODEL), F32)),
        scratch_shapes=[pltpu.VMEM((D_MODEL, D_MODEL), F32)],
        compiler_params=_cparams(("arbitrary",), VMEM_LIMIT),
    )(dyc, cv, proj, w_co, ln_g, ln_b)


def _conv_backward_taps(dcv, proj, conv_w):
    tokens = dcv.shape[0]
    tm = CONV_TM
    nt = tokens // tm
    halo_blocks = tm // CONV_PAD

    def body(d_ref, dn_ref, a_ref, b_ref, ah_ref, bh_ref, cw_ref, da_ref, db_ref, gw_ref, u_ext, d_ext, du_scr, gw_acc):
        i = pl.program_id(0)

        @pl.when(i == 0)
        def _():
            gw_acc[...] = jnp.zeros_like(gw_acc)

        _fill_u_ext(u_ext, a_ref, b_ref, ah_ref, bh_ref, i == 0)
        for lc in range(N_LANE_CHUNKS):
            sl = slice(LANES * lc, LANES * (lc + 1))
            d_ext[lc, 0:tm, :] = d_ref[:, sl]
            d_ext[lc, tm:tm + CONV_PAD, :] = jnp.where(i == nt - 1, 0.0, dn_ref[:, sl])

        def lane_chunk(lc, carry):
            n_rc = tm // 64
            du = [jnp.zeros((64, LANES), F32) for _ in range(n_rc)]
            for j in range(CONV_KERNEL):
                w = cw_ref[lc, pl.ds(j, 1), :]
                gsum = jnp.zeros((8, LANES), F32)
                for rc in range(n_rc):
                    du[rc] = du[rc] + w * d_ext[lc, pl.ds(64 * rc + 30 - j, 64), :]
                    prod = d_ext[lc, pl.ds(64 * rc, 64), :] * u_ext[lc, pl.ds(64 * rc + 2 + j, 64), :]
                    gsum = gsum + jnp.sum(prod.reshape(8, 8, LANES), axis=0)
                gw_acc[lc, j] += gsum
            for rc in range(n_rc):
                du_scr[lc, pl.ds(64 * rc, 64), :] = du[rc]
            return carry
        lax.fori_loop(0, N_LANE_CHUNKS, lane_chunk, 0)

        du = jnp.concatenate([du_scr[lc] for lc in range(N_LANE_CHUNKS)], axis=1)
        a, b = a_ref[...].astype(F32), b_ref[...].astype(F32)
        sb = _sig(b)
        da_ref[...] = (du * sb).astype(BF16)
        db_ref[...] = (du * a * (sb * (1.0 - sb))).astype(BF16)

        @pl.when(i == nt - 1)
        def _():
            gw_ref[...] = jnp.sum(gw_acc[...], axis=2)

    def prev_halo(i):
        return jnp.maximum(i * halo_blocks - 1, 0)

    def next_halo(i):
        return jnp.minimum((i + 1) * halo_blocks, tokens // CONV_PAD - 1)

    tile = lambda col=0: pl.BlockSpec((tm, D_MODEL), lambda i: (i, col))
    return pl.pallas_call(
        body, name="conv_backward_taps", grid=(nt,),
        in_specs=[tile(), pl.BlockSpec((CONV_PAD, D_MODEL), lambda i: (next_halo(i), 0)),
                  tile(COL_A), tile(COL_B),
                  pl.BlockSpec((CONV_PAD, D_MODEL), lambda i: (prev_halo(i), COL_A)),
                  pl.BlockSpec((CONV_PAD, D_MODEL), lambda i: (prev_halo(i), COL_B)),
                  _const_spec((N_DEV, CONV_PAD, LANES))],
        out_specs=(tile(), tile(), _const_spec((N_DEV, CONV_PAD, LANES))),
        out_shape=(jax.ShapeDtypeStruct((tokens, D_MODEL), BF16),
                   jax.ShapeDtypeStruct((tokens, D_MODEL), BF16),
                   jax.ShapeDtypeStruct((N_DEV, CONV_PAD, LANES), F32)),
        scratch_shapes=[pltpu.VMEM((N_LANE_CHUNKS, CONV_PAD + tm, LANES), F32),
                        pltpu.VMEM((N_LANE_CHUNKS, tm + CONV_PAD, LANES), F32),
                        pltpu.VMEM((N_LANE_CHUNKS, tm, LANES), F32),
                        pltpu.VMEM((N_LANE_CHUNKS, CONV_PAD, 8, LANES), F32)],
        compiler_params=_cparams(("arbitrary",), VMEM_LIMIT),
    )(dcv, dcv, proj, proj, proj, proj, conv_w)


def _fold_kv_head(dup, lo, second_half):
    both = dup + pltpu.roll(dup, HEAD_DIM, 1)
    lo = lax.broadcasted_iota(jnp.int32, dup.shape, 1) < HEAD_DIM
    return jnp.where(lo, 0.0, both) if second_half else jnp.where(lo, both, 0.0)


def _attention_backward(dya, o, qr, kd, vd, proj, sinks, w_ao, cos_t, sin_up, sin_dn):
    tokens = qr.shape[0]
    tm = min(512, tokens)
    per_tile = tm // BLOCK
    nt = tokens // tm
    scale = HEAD_DIM ** -0.5

    def body(dya_ref, o_ref, ag0_ref, ag1_ref, q_ref, kc_ref, kp_ref, vc_ref, vp_ref, sinks_ref, wao_ref,
             cos_c, up_c, dn_c, cos_p, up_p, dn_p,
             dq_ref, dkv_ref, dag_ref, gwao_ref, gsink_ref,
             gacc, k_ext, v_ext, dk_ext, dv_ext, dk_carry, dv_carry, do_scr, dq_scr):
        i = pl.program_id(0)
        lo = _lane_halves()

        @pl.when(i == 0)
        def _():
            gacc[...] = jnp.zeros_like(gacc)
            gsink_ref[...] = jnp.zeros_like(gsink_ref)
            dk_carry[...] = jnp.zeros_like(dk_carry)
            dv_carry[...] = jnp.zeros_like(dv_carry)
        dk_ext[...] = jnp.zeros_like(dk_ext)
        dv_ext[...] = jnp.zeros_like(dv_ext)

        @pl.when(i < nt)
        def _():
            dya = dya_ref[...]
            dpa = _dot_nt(dya, _square(wao_ref))
            o = o_ref[...].astype(F32)
            ag = jnp.concatenate([ag0_ref[...], ag1_ref[...]], axis=1).astype(F32)
            sg = _sig(ag)
            gate = ag * sg
            gacc[...] += _dot_tn((o * gate).astype(BF16), dya)
            dag_ref[...] = (dpa * o * (sg * (1.0 + ag * (1.0 - sg)))).astype(BF16)
            do_scr[...] = (dpa * gate).astype(BF16)
            k_ext[0:BLOCK, :], k_ext[BLOCK:BLOCK + tm, :] = kp_ref[...], kc_ref[...]
            v_ext[0:BLOCK, :], v_ext[BLOCK:BLOCK + tm, :] = vp_ref[...], vc_ref[...]

            def block(b, carry):
                r0 = pl.multiple_of(b * BLOCK, BLOCK)
                mine, band = pl.ds(r0, BLOCK), pl.ds(r0, 2 * BLOCK)
                mask = _band_mask(i * per_tile + b)
                head_lane = lax.broadcasted_iota(jnp.int32, (1, LANES), 1)
                gsink = jnp.zeros((1, LANES), F32)
                zero_band = jnp.zeros((2 * BLOCK, LANES), F32)
                dk_band, dv_band = [zero_band, zero_band], [zero_band, zero_band]
                for kvh in range(N_KV_HEADS):
                    ta, tb = slice(LANES * 2 * kvh, LANES * (2 * kvh + 1)), slice(LANES * (2 * kvh + 1), LANES * (2 * kvh + 2))
                    ks = slice(LANES * kvh, LANES * (kvh + 1))
                    q_stack = _stack_heads(q_ref[mine, ta], q_ref[mine, tb], lo)
                    do_stack = _stack_heads(do_scr[mine, ta], do_scr[mine, tb], lo)
                    k2, v2 = k_ext[band, ks], v_ext[band, ks]
                    e_p, e_c, e_s = _band_exp(q_stack, k2, sinks_ref, kvh, mask)
                    inv = 1.0 / (_row_sums(jnp.concatenate([e_p, e_c], axis=1).astype(BF16)) + e_s)
                    p_p, p_c = e_p * inv, e_c * inv
                    dp = _dot_nt(do_stack, v2)
                    dp_p, dp_c = dp[:, :BLOCK], dp[:, BLOCK:]
                    delta = jnp.broadcast_to(jnp.sum(p_p * dp_p + p_c * dp_c, axis=1, keepdims=True), (4 * BLOCK, LANES))
                    ds = jnp.concatenate([p_p * (dp_p - delta), p_c * (dp_c - delta)], axis=1).astype(BF16)
                    sink_terms = e_s * inv * delta
                    for g in range(4):
                        total = jnp.sum(sink_terms[BLOCK * g:BLOCK * (g + 1)], axis=0, keepdims=True)
                        gsink = gsink - jnp.where(head_lane == 4 * kvh + g, total, 0.0)
                    dq_scr[mine, ta], dq_scr[mine, tb] = _unstack_heads(_dot(ds, k2), lo)
                    tile, second = kvh // 2, kvh % 2 == 1
                    dk_band[tile] = dk_band[tile] + _fold_kv_head(_dot_tn(ds, q_stack), lo, second)
                    dv_band[tile] = dv_band[tile] + _fold_kv_head(
                        _dot_tn(jnp.concatenate([p_p, p_c], axis=1).astype(BF16), do_stack), lo, second)
                gsink_ref[0:1, :] += gsink
                cs, up, dn = cos_c[mine, :], up_c[mine, :], dn_c[mine, :]
                for p in range(D_MODEL // LANES):
                    sl = slice(LANES * p, LANES * (p + 1))
                    dq_ref[mine, sl] = (_rope_transposed(dq_scr[mine, sl], cs, up, dn) * scale).astype(BF16)
                for p in range(2):
                    sl = slice(LANES * p, LANES * (p + 1))
                    dk_ext[band, sl] += dk_band[p]
                    dv_ext[band, sl] += dv_band[p]
                return carry
            lax.fori_loop(0, per_tile, block, 0)

        last = slice(tm - BLOCK, tm)
        dk_carry[last, :] += dk_ext[0:BLOCK, :]
        dv_carry[last, :] += dv_ext[0:BLOCK, :]
        for p in range(2):
            sl = slice(LANES * p, LANES * (p + 1))
            dkv_ref[:, sl] = _rope_transposed(dk_carry[:, sl], cos_p[...], up_p[...], dn_p[...]).astype(BF16)
            dkv_ref[:, slice(256 + LANES * p, 256 + LANES * (p + 1))] = dv_carry[:, sl].astype(BF16)
        dk_carry[...] = dk_ext[BLOCK:BLOCK + tm, :]
        dv_carry[...] = dv_ext[BLOCK:BLOCK + tm, :]

        @pl.when(i == nt)
        def _():
            gwao_ref[...] = gacc[...].astype(BF16)

    def cur_idx(i):
        return jnp.minimum(i, nt - 1)

    def prev_idx(i):
        return jnp.clip(i - 1, 0, nt - 1)

    cur = lambda w, col=0: pl.BlockSpec((tm, w), lambda i: (cur_idx(i), col))
    prev = lambda w: pl.BlockSpec((tm, w), lambda i: (prev_idx(i), 0))
    before = lambda w: pl.BlockSpec((BLOCK, w), lambda i: (jnp.maximum(cur_idx(i) * per_tile - 1, 0), 0))
    return pl.pallas_call(
        body, name="attention_backward", grid=(nt + 1,),
        in_specs=[cur(D_MODEL), cur(D_MODEL), cur(512, COL512_AG), cur(512, COL512_AG + 1), cur(D_MODEL),
                  cur(512), before(512), cur(512), before(512),
                  pl.BlockSpec(memory_space=pltpu.SMEM), _pack_weight_spec(1),
                  cur(LANES), cur(LANES), cur(LANES), prev(LANES), prev(LANES), prev(LANES)],
        out_specs=(cur(D_MODEL), prev(512), cur(D_MODEL),
                   _const_spec((D_MODEL, D_MODEL)), _const_spec((8, LANES))),
        out_shape=(jax.ShapeDtypeStruct((tokens, D_MODEL), BF16),
                   jax.ShapeDtypeStruct((tokens, 512), BF16),
                   jax.ShapeDtypeStruct((tokens, D_MODEL), BF16),
                   jax.ShapeDtypeStruct((D_MODEL, D_MODEL), BF16),
                   jax.ShapeDtypeStruct((8, LANES), F32)),
        scratch_shapes=[pltpu.VMEM((D_MODEL, D_MODEL), F32),
                        pltpu.VMEM((BLOCK + tm, 512), BF16), pltpu.VMEM((BLOCK + tm, 512), BF16),
                        pltpu.VMEM((BLOCK + tm, 256), F32), pltpu.VMEM((BLOCK + tm, 256), F32),
                        pltpu.VMEM((tm, 256), F32), pltpu.VMEM((tm, 256), F32),
                        pltpu.VMEM((tm, D_MODEL), BF16), pltpu.VMEM((tm, D_MODEL), F32)],
        compiler_params=_cparams(("arbitrary",), VMEM_LIMIT),
    )(dya, o, proj, proj, qr, kd, kd, vd, vd, sinks, w_ao, cos_t, sin_up, sin_dn, cos_t, sin_up, sin_dn)


def _transpose_tokens(h):
    tokens = h.shape[0]
    tt = min(512, tokens)

    def body(h_ref, out_ref):
        out_ref[...] = h_ref[...].astype(F32).T.astype(BF16)

    return pl.pallas_call(
        body, name="transpose_tokens", grid=(tokens // tt,),
        in_specs=[pl.BlockSpec((tt, D_MODEL), lambda i: (i, 0))],
        out_specs=pl.BlockSpec((D_MODEL, tt), lambda i: (0, i)),
        out_shape=jax.ShapeDtypeStruct((D_MODEL, tokens), BF16),
        compiler_params=_cparams(("parallel",)),
    )(h)


def _input_backward(sections, w_in_t, x, dx2, norm_g):
    tokens = x.shape[0]
    tm = 256

    def body(*refs):
        sec = refs[:8]
        w_ref, x_ref, dx2_ref, g_ref, gx_ref, part_ref = refs[8:]

        @pl.when(pl.program_id(0) == 0)
        def _():
            part_ref[...] = jnp.zeros_like(part_ref)

        dh = jnp.zeros((tm, D_MODEL), F32)
        for s in range(8):
            dh = dh + _dot(sec[s][...], w_ref[_SECTION_ROWS[s]:_SECTION_ROWS[s] + _SECTION_WIDTH[s], :])
        xv = x_ref[...]
        r = lax.rsqrt(jnp.mean(xv * xv, axis=-1, keepdims=True) + RMS_EPS)
        xn = xv * r
        part_ref[0:1, :] += jnp.sum(dh * xn, axis=0, keepdims=True)
        dxn = dh * g_ref[...]
        gx_ref[...] = dx2_ref[...] + r * (dxn - xn * jnp.mean(dxn * xn, axis=-1, keepdims=True))

    tile = lambda w=D_MODEL: pl.BlockSpec((tm, w), lambda i: (i, 0))
    return pl.pallas_call(
        body, name="input_backward", grid=(tokens // tm,),
        in_specs=[tile(w) for w in _SECTION_WIDTH] + [
            pl.BlockSpec((IN_WIDTH, D_MODEL), lambda i: (0, 0), pipeline_mode=pl.Buffered(1)),
            tile(), tile(), _const_spec((1, D_MODEL))],
        out_specs=(tile(), _const_spec((8, D_MODEL))),
        out_shape=(jax.ShapeDtypeStruct((tokens, D_MODEL), F32),
                   jax.ShapeDtypeStruct((8, D_MODEL), F32)),
        compiler_params=_cparams(("arbitrary",), VMEM_LIMIT),
    )(*sections, w_in_t, x, dx2, norm_g)


def _adamw_math(w, g, m, v):
    m = ADAM_B1 * m + (1.0 - ADAM_B1) * g
    v = ADAM_B2 * v + (1.0 - ADAM_B2) * (g * g)
    m_hat = m / (1.0 - ADAM_B1 ** ADAM_STEP)
    v_hat = v / (1.0 - ADAM_B2 ** ADAM_STEP)
    delta = -ADAM_LR * (m_hat / (jnp.sqrt(v_hat) + ADAM_EPS) + ADAM_WD * w)
    return delta, m, v


def _sum_slots(recv_ref):
    total = recv_ref[0].astype(F32)
    for d in range(1, N_DEV):
        total = total + recv_ref[d].astype(F32)
    return total


def _adamw(name, w, g, m, v, tile_rows):
    rows, cols = w.shape

    def body(w_ref, g_ref, m_ref, v_ref, d_ref, nm_ref, nv_ref):
        d_ref[...], nm_ref[...], nv_ref[...] = _adamw_math(w_ref[...], g_ref[...], m_ref[...], v_ref[...])

    spec = pl.BlockSpec((tile_rows, cols), lambda i: (i, 0))
    shape = jax.ShapeDtypeStruct((rows, cols), F32)
    return pl.pallas_call(
        body, name=name, grid=(rows // tile_rows,),
        in_specs=[spec] * 4, out_specs=(spec,) * 3, out_shape=(shape,) * 3,
        compiler_params=_cparams(("parallel",)),
    )(w, g, m, v)


def _sum_adamw(name, recv, w, m, v):
    def body(recv_ref, w_ref, m_ref, v_ref, g_ref, d_ref, nm_ref, nv_ref):
        g = _sum_slots(recv_ref)
        g_ref[...] = g
        d_ref[...], nm_ref[...], nv_ref[...] = _adamw_math(w_ref[...], g, m_ref[...], v_ref[...])

    shape = jax.ShapeDtypeStruct(w.shape, F32)
    return pl.pallas_call(body, name=name, out_shape=(shape,) * 4)(recv, w, m, v)


def _pad_rows(a, rows):
    return jnp.concatenate([a, jnp.zeros((rows - a.shape[0],) + a.shape[1:], a.dtype)], axis=0)


def kernel(x, norm_g, w_in, conv_dw_w, conv_dw_b, conv_ln_g, conv_ln_b, w_conv_out, attn_sinks, w_attn_out, w_out, final_norm_g, loss_target, m_norm_g, m_w_in, m_conv_dw_w, m_conv_dw_b, m_conv_ln_g, m_conv_ln_b, m_w_conv_out, m_attn_sinks, m_w_attn_out, m_w_out, m_final_norm_g, v_norm_g, v_w_in, v_conv_dw_w, v_conv_dw_b, v_conv_ln_g, v_conv_ln_b, v_w_conv_out, v_attn_sinks, v_w_attn_out, v_w_out, v_final_norm_g):
    xs, target = x[0], loss_target[0]
    tokens = xs.shape[0]
    fg_row = final_norm_g.reshape(1, D_MODEL)

    taps_bits = lax.bitcast_convert_type(_pad_rows(conv_dw_w[0], CONV_PAD), BF16).reshape(8, D_MODEL)
    pack = jnp.concatenate([w_conv_out[0].astype(BF16), w_attn_out[0].astype(BF16), w_out[0].astype(BF16),
                            jnp.pad(taps_bits, ((0, PACK_ROWS - 3 * SHARD_SQ - 8), (0, 0)))], axis=0)
    w_in_t32 = w_in[0].T
    proj, h, w_in_t, pack_full = _gather_project(xs, norm_g, w_in_t32.astype(BF16), pack)
    w_co = w_ao = w_o = pack_full
    conv_w = lax.bitcast_convert_type(
        pack_full[:, 3 * SHARD_SQ:3 * SHARD_SQ + 8].reshape(N_DEV, CONV_PAD, LANES, 2), F32)

    cos_t, sin_up, sin_dn = _rope_tables(tokens)
    qr, kd, vd = _rope_qkv(proj, cos_t, sin_up, sin_dn)
    cv, yc = _conv_forward(proj, conv_w, conv_dw_b, conv_ln_g, conv_ln_b, w_co)
    o, ya = _attention_forward(qr, kd, vd, proj, attn_sinks, w_ao)

    dx2, dyc, dya, dmlc, dmla, g_out, part_head = _merge_and_head(yc, ya, proj, xs, target, w_o, fg_row)
    dcv, dcg, g_co, part_conv = _conv_backward_pointwise(dyc, cv, proj, w_co, conv_ln_g, conv_ln_b)
    da, db, g_conv = _conv_backward_taps(dcv, proj, conv_w)
    dq, dkv, dag, g_ao, part_sink = _attention_backward(dya, o, qr, kd, vd, proj, attn_sinks, w_ao, cos_t, sin_up, sin_dn)
    sections = (da, db, dcg, dq, dkv, dag, dmlc, dmla)
    grad_x, part_in = _input_backward(sections, w_in_t, xs, dx2, norm_g)

    small = jnp.concatenate([
        part_in[0:1], part_conv[2:3], part_conv[0:1], part_conv[1:2], part_head[0:1],
        jnp.pad(part_sink[0:1], ((0, 0), (0, D_MODEL - LANES))), part_head[1:2],
        jnp.zeros((1, D_MODEL), F32)], axis=0)

    g_mine, r_conv, r_small = _grad_exchange(sections, _transpose_tokens(h), g_co, g_ao, g_out, g_conv, small)

    g_in_t = g_mine[:SHARD_IN]
    w_in_res = _adamw("adamw_w_in", w_in_t32, g_in_t, m_w_in[0].T, v_w_in[0].T, 192)
    grad_w_in, d_w_in, nm_w_in, nv_w_in = (a.T for a in (g_in_t,) + tuple(w_in_res))
    sq = {}
    for j, (nm, w, m, v) in enumerate((("w_conv_out", w_conv_out, m_w_conv_out, v_w_conv_out),
                                       ("w_attn_out", w_attn_out, m_w_attn_out, v_w_attn_out),
                                       ("w_out", w_out, m_w_out, v_w_out))):
        g = g_mine[SHARD_IN + j * SHARD_SQ:SHARD_IN + (j + 1) * SHARD_SQ]
        sq[nm] = (g,) + tuple(_adamw("adamw_" + nm, w[0], g, m[0], v[0], SHARD_SQ))
    conv_res = _sum_adamw("sum_adamw_conv_dw_w", r_conv.reshape(N_DEV, CONV_PAD, LANES),
                          _pad_rows(conv_dw_w[0], CONV_PAD), _pad_rows(m_conv_dw_w[0], CONV_PAD),
                          _pad_rows(v_conv_dw_w[0], CONV_PAD))
    pad_sink = lambda a: jnp.pad(a, ((0, 0), (0, D_MODEL - N_Q_HEADS)))
    zero_rows = jnp.zeros((2, D_MODEL), F32)
    stack = lambda a, b, c, d, e, f: jnp.concatenate([a, b, c, d, e.reshape(1, D_MODEL), pad_sink(f), zero_rows], axis=0)
    small_res = _sum_adamw(
        "sum_adamw_small", r_small,
        stack(norm_g, conv_dw_b, conv_ln_g, conv_ln_b, final_norm_g, attn_sinks),
        stack(m_norm_g, m_conv_dw_b, m_conv_ln_g, m_conv_ln_b, m_final_norm_g, m_attn_sinks),
        stack(v_norm_g, v_conv_dw_b, v_conv_ln_g, v_conv_ln_b, v_final_norm_g, v_attn_sinks))
    loss = jnp.sum(small_res[0][6])

    def leaf(k):
        s = small_res[k]
        return (s[0:1], (grad_w_in, d_w_in, nm_w_in, nv_w_in)[k][None], conv_res[k][None, :CONV_KERNEL],
                s[1:2], s[2:3], s[3:4], sq["w_conv_out"][k][None], s[5:6, :N_Q_HEADS],
                sq["w_attn_out"][k][None], sq["w_out"][k][None], s[4])

    return (loss, grad_x[None], *leaf(0), *leaf(1), *leaf(2), *leaf(3))
```

```python
import jax
import jax.numpy as jnp
from jax import lax
from jax.experimental import pallas as pl
from jax.experimental.pallas import tpu as pltpu

F32 = jnp.float32
BF16 = jnp.bfloat16
MESH = pl.DeviceIdType.MESH

D_MODEL = 1024
IN_WIDTH = 7680
N_DEV = 8
SHARD_IN = IN_WIDTH // N_DEV
SHARD_SQ = D_MODEL // N_DEV
CONV_KERNEL = 31
CONV_PAD = 32
HEAD_DIM = 64
N_Q_HEADS = 16
N_KV_HEADS = 4
BLOCK = 128
LANES = 128
ROPE_THETA = 10000.0
RMS_EPS = 1e-5
LN_EPS = 1e-5
NEG = -1e30
ADAM_LR = 0.001
ADAM_B1 = 0.9
ADAM_B2 = 0.999
ADAM_EPS = 1e-08
ADAM_WD = 0.01
ADAM_STEP = 10

OFF_A, OFF_B, OFF_CG, OFF_Q, OFF_KV, OFF_AG, OFF_MLC, OFF_MLA = 0, 1024, 2048, 3072, 4096, 4608, 5632, 6656
COL_A, COL_B, COL_CG, COL_Q = 0, 1, 2, 3
COL512_KV, COL512_AG, COL512_MLC, COL512_MLA = 8, 9, 11, 13
UNIT = 2 * SHARD_IN
PACK_ROWS = 400

VMEM_LIMIT = 56 * 1024 * 1024


def _cparams(sem=None, vmem=None):
    return pltpu.CompilerParams(dimension_semantics=sem, vmem_limit_bytes=vmem)


def _sig(v):
    return 0.5 * jnp.tanh(0.5 * v) + 0.5


def _dot(a, b):
    return jnp.dot(a, b, preferred_element_type=F32)


def _dot_nt(a, b):
    return lax.dot_general(a, b, (((1,), (1,)), ((), ())), preferred_element_type=F32)


def _dot_tn(a, b):
    return lax.dot_general(a, b, (((0,), (0,)), ((), ())), preferred_element_type=F32)


def _const_spec(shape):
    nd = len(shape)
    return pl.BlockSpec(shape, lambda *_: (0,) * nd)


def _pack_weight_spec(j):
    return pl.BlockSpec((N_DEV, SHARD_SQ, D_MODEL), lambda *_: (0, j, 0))


def _square(w_ref):
    return w_ref[...].reshape(D_MODEL, D_MODEL)


def _mesh_pos():
    x, y, c = lax.axis_index("x"), lax.axis_index("y"), lax.axis_index("c")
    return x, y, c, 4 * x + 2 * y + c


def _peer(x, y, c, k):
    px = 1 - x if (k >> 2) & 1 else x
    py = 1 - y if (k >> 1) & 1 else y
    pc = 1 - c if k & 1 else c
    return (px, py, pc), 4 * px + 2 * py + pc


def _gather_project(x, norm_g, w_shard_t, pack):
    tokens = x.shape[0]
    tt = min(512, tokens // 2)
    n_tok = tokens // tt
    rc = min(128, tt)

    def body(x_hbm, g_ref, ws_hbm, pack_hbm, proj_hbm, h_hbm, wfull_hbm, packfull_hbm,
             w_vmem, h_vmem, x_buf, o_buf, send_sems, recv_sems, local_sems, x_sems, o_sems):
        x_, y_, c_, me = _mesh_pos()
        myself, sibling = (x_, y_, c_), (x_, y_, 1 - c_)
        chips = ((1 - x_, y_), (x_, 1 - y_), (1 - x_, 1 - y_))

        def shard(ref, idx):
            return ref.at[pl.ds(pl.multiple_of(idx * SHARD_IN, 64), SHARD_IN)]

        def copy(a, k, idx, to, own=False):
            if a == 0:
                src, dst = ws_hbm if own else shard(w_vmem, idx), shard(w_vmem, idx)
            else:
                src, dst = pack_hbm if own else packfull_hbm.at[idx], packfull_hbm.at[idx]
            return pltpu.make_async_remote_copy(src_ref=src, dst_ref=dst, send_sem=send_sems.at[a, k],
                                                recv_sem=recv_sems.at[a, k], device_id=to, device_id_type=MESH)

        own_w = pltpu.make_async_copy(ws_hbm, shard(w_vmem, me), local_sems.at[0])
        own_p = pltpu.make_async_copy(pack_hbm, packfull_hbm.at[me], local_sems.at[1])
        own_w.start()
        own_p.start()
        sent = []
        for a in range(2):
            sent.append(copy(a, 0, me, sibling, own=True))
            sent += [copy(a, 1 + r, me, (*chip, c_), own=True) for r, chip in enumerate(chips)]
        for cp in sent:
            cp.start()

        def x_copy(t, slot):
            return pltpu.make_async_copy(x_hbm.at[pl.ds(t * tt, tt)], x_buf.at[slot], x_sems.at[slot])

        x_copy(0, 0).start()
        for t in range(n_tok):
            slot = t % 2
            if t + 1 < n_tok:
                x_copy(t + 1, 1 - slot).start()
            x_copy(t, slot).wait()

            def chunk(r0, t=t, slot=slot):
                xv = x_buf[slot, pl.ds(r0, rc), :]
                r = lax.rsqrt(jnp.mean(xv * xv, axis=-1, keepdims=True) + RMS_EPS)
                h_vmem[pl.ds(t * tt + r0, rc), :] = (xv * r * g_ref[...]).astype(BF16)
            _row_chunks(tt, rc, chunk)
        h_out = pltpu.make_async_copy(h_vmem, h_hbm, local_sems.at[6])
        h_out.start()
        local = [own_p, h_out]

        def project_unit(q, u):
            rows = pl.ds(pl.multiple_of(q * UNIT, LANES), UNIT)
            w_out = pltpu.make_async_copy(w_vmem.at[rows], wfull_hbm.at[rows], local_sems.at[2 + u])
            w_out.start()
            local.append(w_out)

            def o_copy(slot, t):
                return pltpu.make_async_copy(
                    o_buf.at[slot], proj_hbm.at[pl.ds(pl.multiple_of(t * tt, tt), tt), rows], o_sems.at[slot])

            def tile(t, carry):
                slot = lax.rem(t, 2)

                @pl.when(t >= 2)
                def _():
                    o_copy(slot, t).wait()
                o_buf[slot] = _dot_nt(h_vmem[pl.ds(pl.multiple_of(t * tt, tt), tt), :], w_vmem[rows, :]).astype(BF16)
                o_copy(slot, t).start()
                return carry
            lax.fori_loop(0, n_tok, tile, 0)
            o_copy(0, 0).wait()
            o_copy(1, 0).wait()

        def dev(chip, core):
            return 4 * chip[0] + 2 * chip[1] + core

        def arrive_and_pass_on(a, r):
            copy(a, 1 + r, dev(chips[r], c_), myself).wait_recv()
            passed = copy(a, 4 + r, dev(chips[r], c_), sibling)
            passed.start()
            sent.append(passed)

        def passed_on_to_me(a, r):
            copy(a, 4 + r, dev(chips[r], 1 - c_), myself).wait_recv()

        own_w.wait()
        copy(0, 0, dev((x_, y_), 1 - c_), myself).wait_recv()
        project_unit(2 * x_ + y_, 0)
        arrive_and_pass_on(0, 0)
        arrive_and_pass_on(0, 1)
        passed_on_to_me(0, 0)
        project_unit(2 * chips[0][0] + chips[0][1], 1)
        arrive_and_pass_on(0, 2)
        passed_on_to_me(0, 1)
        project_unit(2 * chips[1][0] + chips[1][1], 2)
        passed_on_to_me(0, 2)
        project_unit(2 * chips[2][0] + chips[2][1], 3)
        for r in range(3):
            arrive_and_pass_on(1, r)
        copy(1, 0, dev((x_, y_), 1 - c_), myself).wait_recv()
        for r in range(3):
            passed_on_to_me(1, r)
        for cp in sent:
            cp.wait_send()
        for cp in local:
            cp.wait()

    hbm = pl.BlockSpec(memory_space=pltpu.HBM)
    return pl.pallas_call(
        body, name="gather_project",
        in_specs=[hbm, pl.BlockSpec(memory_space=pltpu.VMEM), hbm, hbm],
        out_specs=(hbm, hbm, hbm, hbm),
        out_shape=(jax.ShapeDtypeStruct((tokens, IN_WIDTH), BF16),
                   jax.ShapeDtypeStruct((tokens, D_MODEL), BF16),
                   jax.ShapeDtypeStruct((IN_WIDTH, D_MODEL), BF16),
                   jax.ShapeDtypeStruct((N_DEV, PACK_ROWS, D_MODEL), BF16)),
        scratch_shapes=[pltpu.VMEM((IN_WIDTH, D_MODEL), BF16),
                        pltpu.VMEM((tokens, D_MODEL), BF16),
                        pltpu.VMEM((2, tt, D_MODEL), F32),
                        pltpu.VMEM((2, tt, UNIT), BF16),
                        pltpu.SemaphoreType.DMA((2, N_DEV - 1)),
                        pltpu.SemaphoreType.DMA((2, N_DEV - 1)),
                        pltpu.SemaphoreType.DMA((7,)),
                        pltpu.SemaphoreType.DMA((2,)),
                        pltpu.SemaphoreType.DMA((2,))],
        compiler_params=_cparams(None, VMEM_LIMIT),
    )(x, norm_g, w_shard_t, pack)


TAIL_ROWS = 3 * SHARD_SQ
HALF_ROWS = SHARD_IN + TAIL_ROWS

GRAD_CHUNK = 384
_SECTION_ROWS = (OFF_A, OFF_B, OFF_CG, OFF_Q, OFF_KV, OFF_AG, OFF_MLC, OFF_MLA)
_SECTION_WIDTH = (1024, 1024, 1024, 1024, 512, 1024, 1024, 1024)


def _dproj_pieces(first, width):
    out = []
    for s, (start, w) in enumerate(zip(_SECTION_ROWS, _SECTION_WIDTH)):
        lo, hi = max(first, start), min(first + width, start + w)
        if lo < hi:
            out.append((s, lo - start, hi - lo, lo - first))
    return out


def _grad_exchange(sections, h_t, g_co, g_ao, g_out, g_conv, small):
    tokens = h_t.shape[1]
    n_chunk = UNIT // GRAD_CHUNK
    rc = 192

    def body(*refs):
        sec = refs[:8]
        (ht_hbm, gco_hbm, gao_hbm, gout_hbm, gconv_hbm, small_hbm, gmine_hbm, rconv_hbm, rsmall_hbm,
         lhs_buf, ht_vmem, halves, tail_buf, out_buf, stage, final,
         lhs_sems, ht_sem, tail_in_sems, d2d_send, d2d_recv, ici_send, ici_recv,
         d2d_tail_send, d2d_tail_recv, ici_tail_send, ici_tail_recv,
         tiny_send, tiny_recv, local_sems) = refs[8:]
        head_rows, tail_rows = pl.ds(0, SHARD_IN), pl.ds(SHARD_IN, TAIL_ROWS)
        x_, y_, c_, me = _mesh_pos()
        myself, sibling = (x_, y_, c_), (x_, y_, 1 - c_)
        chips = ((1 - x_, 1 - y_), (1 - x_, y_), (x_, 1 - y_), (x_, y_))
        squares = (gco_hbm, gao_hbm, gout_hbm)

        def remote(src, dst, send_sem, recv_sem, to):
            return pltpu.make_async_remote_copy(src_ref=src, dst_ref=dst, send_sem=send_sem, recv_sem=recv_sem,
                                                device_id=to, device_id_type=MESH)

        own_tiny = [pltpu.make_async_copy(gconv_hbm.at[me], rconv_hbm.at[me], local_sems.at[0]),
                    pltpu.make_async_copy(small_hbm, rsmall_hbm.at[me], local_sems.at[1])]
        for cp in own_tiny:
            cp.start()
        tiny = []
        for k in range(1, N_DEV):
            peer, peer_idx = _peer(x_, y_, c_, k)
            tiny += [remote(gconv_hbm.at[peer_idx], rconv_hbm.at[me], tiny_send.at[0, k - 1], tiny_recv.at[0, k - 1], peer),
                     remote(small_hbm, rsmall_hbm.at[me], tiny_send.at[1, k - 1], tiny_recv.at[1, k - 1], peer)]
        for cp in tiny:
            cp.start()

        ht_in = pltpu.make_async_copy(ht_hbm, ht_vmem, ht_sem.at[0])
        ht_in.start()

        def fetch(q, j, slot, wait):
            for k in range(4):
                @pl.when(q == k)
                def _(k=k):
                    for n, (s, col, width, place) in enumerate(_dproj_pieces(k * UNIT + j * GRAD_CHUNK, GRAD_CHUNK)):
                        cp = pltpu.make_async_copy(sec[s].at[pl.ds(0, tokens), pl.ds(col, width)],
                                                   lhs_buf.at[slot, pl.ds(0, tokens), pl.ds(place, width)],
                                                   lhs_sems.at[slot, n])
                        cp.wait() if wait else cp.start()

        def chip_of(u):
            return 2 * chips[u][0] + chips[u][1]

        def d2d(u):
            return remote(halves.at[1 - c_], stage.at[u, head_rows], d2d_send.at[u], d2d_recv.at[u], sibling)

        def ici(u):
            return remote(stage.at[u, head_rows], final.at[u, head_rows], ici_send.at[u], ici_recv.at[u], (*chips[u], c_))

        def d2d_tail(u):
            return remote(tail_buf.at[u, 1 - c_], stage.at[u, tail_rows], d2d_tail_send.at[u], d2d_tail_recv.at[u], sibling)

        def ici_tail(u):
            return remote(stage.at[u, tail_rows], final.at[u, tail_rows], ici_tail_send.at[u], ici_tail_recv.at[u],
                          (*chips[u], c_))

        def tail_in(u):
            out = []
            for core in range(2):
                for n, g in enumerate(squares):
                    rows = pl.ds(pl.multiple_of((2 * chip_of(u) + core) * SHARD_SQ, SHARD_SQ), SHARD_SQ)
                    out.append(pltpu.make_async_copy(g.at[rows], tail_buf.at[u, core, pl.ds(n * SHARD_SQ, SHARD_SQ)],
                                                     tail_in_sems.at[u, 3 * core + n]))
            return out

        def add_mine(u, first, count, mine):
            def chunk(r0):
                rows = pl.ds(pl.multiple_of(first + r0, 64), rc)
                stage[u, rows, :] = (stage[u, rows, :].astype(F32) + mine(pl.ds(r0, rc)).astype(F32)).astype(BF16)
            _row_chunks(count, rc, chunk)

        def chip_sum(u):
            d2d(u).wait_recv()
            add_mine(u, 0, SHARD_IN, lambda rows: halves[c_, rows, :])
            if u < 3:
                ici(u).start()

        for u in range(4):
            for cp in tail_in(u):
                cp.start()
        for u in range(4):
            for cp in tail_in(u):
                cp.wait()
            d2d_tail(u).start()
        for u in range(4):
            d2d_tail(u).wait_recv()
            add_mine(u, SHARD_IN, TAIL_ROWS, lambda rows, u=u: tail_buf[u, c_, rows, :])
            if u < 3:
                ici_tail(u).start()

        def store_rows(block, first):
            n = block.shape[0]
            for core in range(2):
                lo, hi = max(first, core * SHARD_IN), min(first + n, (core + 1) * SHARD_IN)
                if lo < hi:
                    halves[core, lo - core * SHARD_IN:hi - core * SHARD_IN, :] = block[lo - first:hi - first].astype(BF16)

        fetch(chip_of(0), 0, 0, wait=False)
        ht_in.wait()
        for u in range(4):
            q = chip_of(u)
            for j in range(n_chunk):
                slot = (u * n_chunk + j) % 2
                if j + 1 < n_chunk:
                    fetch(q, j + 1, 1 - slot, wait=False)
                elif u + 1 < 4:
                    fetch(chip_of(u + 1), 0, 1 - slot, wait=False)
                fetch(q, j, slot, wait=True)
                grad_t = _dot(ht_vmem[...], lhs_buf[slot])
                if j == 0 and u > 0:
                    chip_sum(u - 1)
                    d2d(u - 1).wait_send()
                for r in range(GRAD_CHUNK // LANES):
                    store_rows(grad_t[:, LANES * r:LANES * (r + 1)].T, j * GRAD_CHUNK + LANES * r)
            d2d(u).start()

        chip_sum(3)
        for u in range(3):
            remote(stage.at[u, head_rows], final.at[u, head_rows], ici_send.at[u], ici_recv.at[u], myself).wait_recv()
            remote(stage.at[u, tail_rows], final.at[u, tail_rows], ici_tail_send.at[u], ici_tail_recv.at[u],
                   myself).wait_recv()

        def total(r0):
            rows = pl.ds(r0, rc)
            out_buf[rows, :] = ((stage[3, rows, :].astype(F32) + final[0, rows, :].astype(F32))
                                + final[1, rows, :].astype(F32)) + final[2, rows, :].astype(F32)
        _row_chunks(HALF_ROWS, rc, total)
        out = pltpu.make_async_copy(out_buf, gmine_hbm, local_sems.at[2])
        out.start()
        d2d(3).wait_send()
        for u in range(4):
            d2d_tail(u).wait_send()
        for u in range(3):
            ici(u).wait_send()
            ici_tail(u).wait_send()
        for k in range(1, N_DEV):
            peer, peer_idx = _peer(x_, y_, c_, k)
            remote(gconv_hbm.at[me], rconv_hbm.at[peer_idx], tiny_send.at[0, k - 1], tiny_recv.at[0, k - 1], myself).wait_recv()
            remote(small_hbm, rsmall_hbm.at[peer_idx], tiny_send.at[1, k - 1], tiny_recv.at[1, k - 1], myself).wait_recv()
        for cp in tiny:
            cp.wait_send()
        for cp in own_tiny:
            cp.wait()
        out.wait()

    hbm = pl.BlockSpec(memory_space=pltpu.HBM)
    return pl.pallas_call(
        body, name="grad_exchange",
        in_specs=[hbm] * 14, out_specs=(hbm, hbm, hbm),
        out_shape=(jax.ShapeDtypeStruct((HALF_ROWS, D_MODEL), F32),
                   jax.ShapeDtypeStruct((N_DEV, CONV_PAD, LANES), F32),
                   jax.ShapeDtypeStruct((N_DEV, 8, D_MODEL), F32)),
        scratch_shapes=[pltpu.VMEM((2, tokens, GRAD_CHUNK), BF16),
                        pltpu.VMEM((D_MODEL, tokens), BF16),
                        pltpu.VMEM((2, SHARD_IN, D_MODEL), BF16),
                        pltpu.VMEM((4, 2, TAIL_ROWS, D_MODEL), BF16),
                        pltpu.VMEM((HALF_ROWS, D_MODEL), F32),
                        pltpu.VMEM((4, HALF_ROWS, D_MODEL), BF16),
                        pltpu.VMEM((3, HALF_ROWS, D_MODEL), BF16),
                        pltpu.SemaphoreType.DMA((2, 3)),
                        pltpu.SemaphoreType.DMA((1,)),
                        pltpu.SemaphoreType.DMA((4, 6)),
                        pltpu.SemaphoreType.DMA((4,)),
                        pltpu.SemaphoreType.DMA((4,)),
                        pltpu.SemaphoreType.DMA((3,)),
                        pltpu.SemaphoreType.DMA((3,)),
                        pltpu.SemaphoreType.DMA((4,)),
                        pltpu.SemaphoreType.DMA((4,)),
                        pltpu.SemaphoreType.DMA((3,)),
                        pltpu.SemaphoreType.DMA((3,)),
                        pltpu.SemaphoreType.DMA((2, N_DEV - 1)),
                        pltpu.SemaphoreType.DMA((2, N_DEV - 1)),
                        pltpu.SemaphoreType.DMA((3,))],
        compiler_params=_cparams(None, 60 * 1024 * 1024),
    )(*sections, h_t, g_co, g_ao, g_out, g_conv, small)


def _row_chunks(total, size, fn):
    n = total // size
    if n == 1:
        fn(0)
        return

    def step(i, carry):
        fn(pl.multiple_of(i * size, size))
        return carry
    lax.fori_loop(0, n, step, 0)


def _rope_tables(tokens):
    inv_freq = ROPE_THETA ** (-jnp.arange(0, HEAD_DIM, 2, dtype=F32) / HEAD_DIM)
    ang = jnp.arange(tokens, dtype=jnp.int32).astype(F32)[:, None] * inv_freq[None, :]
    cos, sin = jnp.cos(ang), jnp.sin(ang)
    zero = jnp.zeros_like(sin)
    cos_t = jnp.tile(jnp.concatenate([cos, cos], axis=1), (1, LANES // HEAD_DIM))
    sin_up = jnp.tile(jnp.concatenate([-sin, zero], axis=1), (1, LANES // HEAD_DIM))
    sin_dn = jnp.tile(jnp.concatenate([zero, sin], axis=1), (1, LANES // HEAD_DIM))
    return cos_t, sin_up, sin_dn


def _rope(t, cos_t, sin_up, sin_dn):
    return t * cos_t + pltpu.roll(t, LANES - 32, 1) * sin_up + pltpu.roll(t, 32, 1) * sin_dn


def _rope_transposed(g, cos_t, sin_up, sin_dn):
    return g * cos_t + pltpu.roll(g * sin_up, 32, 1) + pltpu.roll(g * sin_dn, LANES - 32, 1)


def _lane_halves():
    lane = lax.broadcasted_iota(jnp.int32, (BLOCK, LANES), 1)
    return lane < HEAD_DIM


def _rope_qkv(proj, cos_t, sin_up, sin_dn):
    tokens = proj.shape[0]
    tm = min(512, tokens)
    scale = HEAD_DIM ** -0.5

    def body(q_ref, kv_ref, cos_ref, up_ref, dn_ref, qr_ref, kd_ref, vd_ref):
        lo = _lane_halves()

        def chunk(r0):
            rows = pl.ds(r0, BLOCK)
            cs, up, dn = cos_ref[rows, :], up_ref[rows, :], dn_ref[rows, :]
            for p in range(D_MODEL // LANES):
                sl = slice(LANES * p, LANES * (p + 1))
                qt = q_ref[rows, sl].astype(F32)
                qr_ref[rows, sl] = (_rope(qt, cs, up, dn) * scale).astype(BF16)
            for p in range(2):
                sl = slice(LANES * p, LANES * (p + 1))
                kt = _rope(kv_ref[rows, sl].astype(F32), cs, up, dn)
                vt = kv_ref[rows, slice(256 + LANES * p, 256 + LANES * (p + 1))].astype(F32)
                for src, dst in ((kt, kd_ref), (vt, vd_ref)):
                    first = jnp.where(lo, src, 0.0)
                    second = src - first
                    dst[rows, slice(LANES * 2 * p, LANES * (2 * p + 1))] = (first + pltpu.roll(first, HEAD_DIM, 1)).astype(BF16)
                    dst[rows, slice(LANES * (2 * p + 1), LANES * (2 * p + 2))] = (second + pltpu.roll(second, HEAD_DIM, 1)).astype(BF16)
        _row_chunks(tm, BLOCK, chunk)

    tab = pl.BlockSpec((tm, LANES), lambda i: (i, 0))
    return pl.pallas_call(
        body, name="rope_qkv", grid=(tokens // tm,),
        in_specs=[pl.BlockSpec((tm, D_MODEL), lambda i: (i, COL_Q)),
                  pl.BlockSpec((tm, 512), lambda i: (i, COL512_KV)), tab, tab, tab],
        out_specs=(pl.BlockSpec((tm, D_MODEL), lambda i: (i, 0)),
                   pl.BlockSpec((tm, 512), lambda i: (i, 0)),
                   pl.BlockSpec((tm, 512), lambda i: (i, 0))),
        out_shape=(jax.ShapeDtypeStruct((tokens, D_MODEL), BF16),
                   jax.ShapeDtypeStruct((tokens, 512), BF16),
                   jax.ShapeDtypeStruct((tokens, 512), BF16)),
        compiler_params=_cparams(("parallel",)),
    )(proj, proj, cos_t, sin_up, sin_dn)


CONV_TM = 256
N_LANE_CHUNKS = D_MODEL // LANES


def _fill_u_ext(u_ext, a_ref, b_ref, ah_ref, bh_ref, first_tile):
    for lc in range(N_LANE_CHUNKS):
        sl = slice(LANES * lc, LANES * (lc + 1))
        uh = ah_ref[:, sl].astype(F32) * _sig(bh_ref[:, sl].astype(F32))
        u_ext[lc, 0:CONV_PAD, :] = jnp.where(first_tile, 0.0, uh)
        u_ext[lc, CONV_PAD:CONV_PAD + CONV_TM, :] = a_ref[:, sl].astype(F32) * _sig(b_ref[:, sl].astype(F32))


def _conv_forward(proj, conv_w, dw_b, ln_g, ln_b, w_co):
    tokens = proj.shape[0]
    tm = CONV_TM
    halo_blocks = tm // CONV_PAD

    def body(a_ref, b_ref, ah_ref, bh_ref, cg_ref, cw_ref, dwb_ref, lng_ref, lnb_ref, wco_ref,
             cv_ref, yc_ref, u_ext, cv_scr):
        _fill_u_ext(u_ext, a_ref, b_ref, ah_ref, bh_ref, pl.program_id(0) == 0)

        def lane_chunk(lc, carry):
            for rc in range(tm // 64):
                acc = jnp.zeros((64, LANES), F32)
                for j in range(CONV_KERNEL):
                    acc = acc + cw_ref[lc, pl.ds(j, 1), :] * u_ext[lc, pl.ds(64 * rc + 2 + j, 64), :]
                cv_scr[lc, pl.ds(64 * rc, 64), :] = acc
            return carry
        lax.fori_loop(0, N_LANE_CHUNKS, lane_chunk, 0)

        cv = jnp.concatenate([cv_scr[lc] for lc in range(N_LANE_CHUNKS)], axis=1) + dwb_ref[...]
        cv_ref[...] = cv
        mu = jnp.mean(cv, axis=-1, keepdims=True)
        zc = cv - mu
        rstd = lax.rsqrt(jnp.mean(zc * zc, axis=-1, keepdims=True) + LN_EPS)
        ln = zc * rstd * lng_ref[...] + lnb_ref[...]
        cg = cg_ref[...].astype(F32)
        pc = (ln * _sig(ln)) * (cg * _sig(cg))
        yc_ref[...] = _dot(pc.astype(BF16), _square(wco_ref)).astype(BF16)

    def halo_map(i):
        return (jnp.maximum(i * halo_blocks - 1, 0), 0)

    tile = lambda col: pl.BlockSpec((tm, D_MODEL), lambda i: (i, col))
    return pl.pallas_call(
        body, name="conv_forward", grid=(tokens // tm,),
        in_specs=[tile(COL_A), tile(COL_B),
                  pl.BlockSpec((CONV_PAD, D_MODEL), lambda i: (halo_map(i)[0], COL_A)),
                  pl.BlockSpec((CONV_PAD, D_MODEL), lambda i: (halo_map(i)[0], COL_B)),
                  tile(COL_CG), _const_spec((N_DEV, CONV_PAD, LANES)),
                  _const_spec((1, D_MODEL)), _const_spec((1, D_MODEL)), _const_spec((1, D_MODEL)),
                  _pack_weight_spec(0)],
        out_specs=(pl.BlockSpec((tm, D_MODEL), lambda i: (i, 0)),
                   pl.BlockSpec((tm, D_MODEL), lambda i: (i, 0))),
        out_shape=(jax.ShapeDtypeStruct((tokens, D_MODEL), F32),
                   jax.ShapeDtypeStruct((tokens, D_MODEL), BF16)),
        scratch_shapes=[pltpu.VMEM((N_LANE_CHUNKS, CONV_PAD + tm, LANES), F32),
                        pltpu.VMEM((N_LANE_CHUNKS, tm, LANES), F32)],
        compiler_params=_cparams(("parallel",), VMEM_LIMIT),
    )(proj, proj, proj, proj, proj, conv_w, dw_b, ln_g, ln_b, w_co)


def _band_mask(n):
    row = lax.broadcasted_iota(jnp.int32, (4 * BLOCK, 2 * BLOCK), 0) & (BLOCK - 1)
    col = lax.broadcasted_iota(jnp.int32, (4 * BLOCK, 2 * BLOCK), 1)
    before = jnp.logical_and(jnp.logical_and(col < BLOCK, col > row), n > 0)
    return jnp.logical_or(before, jnp.logical_and(col >= BLOCK, col - BLOCK <= row))


def _stack_heads(tile_a, tile_b, lo):
    zero = jnp.zeros_like(tile_a)
    return jnp.concatenate([jnp.where(lo, tile_a, zero), jnp.where(lo, zero, tile_a),
                            jnp.where(lo, tile_b, zero), jnp.where(lo, zero, tile_b)], axis=0)


def _unstack_heads(stacked, lo):
    s = [stacked[BLOCK * g:BLOCK * (g + 1)] for g in range(4)]
    return (jnp.where(lo, s[0], 0.0) + jnp.where(lo, 0.0, s[1]),
            jnp.where(lo, s[2], 0.0) + jnp.where(lo, 0.0, s[3]))


def _band_scores(q_stack, k2, mask):
    return jnp.where(mask, _dot_nt(q_stack, k2), NEG)


def _sink_rows(sinks_ref, kvh):
    return jnp.concatenate([jnp.full((BLOCK, LANES), sinks_ref[0, 4 * kvh + g], F32) for g in range(4)], axis=0)


def _attention_forward(qr, kd, vd, proj, sinks, w_ao):
    tokens = qr.shape[0]
    tm = min(512, tokens)
    per_tile = tm // BLOCK

    def body(q_ref, kc_ref, kp_ref, vc_ref, vp_ref, ag0_ref, ag1_ref, sinks_ref, wao_ref, o_ref, ya_ref, lse_ref,
             k_ext, v_ext, o_scr):
        i = pl.program_id(0)
        lo = _lane_halves()
        head_lane = lax.broadcasted_iota(jnp.int32, (1, LANES), 1)
        k_ext[0:BLOCK, :], k_ext[BLOCK:BLOCK + tm, :] = kp_ref[...], kc_ref[...]
        v_ext[0:BLOCK, :], v_ext[BLOCK:BLOCK + tm, :] = vp_ref[...], vc_ref[...]

        def block(b, carry):
            r0 = pl.multiple_of(b * BLOCK, BLOCK)
            band = pl.ds(r0, 2 * BLOCK)
            mask = _band_mask(i * per_tile + b)
            lse_tile = jnp.zeros((BLOCK, LANES), F32)
            for kvh in range(N_KV_HEADS):
                ta, tb = slice(LANES * 2 * kvh, LANES * (2 * kvh + 1)), slice(LANES * (2 * kvh + 1), LANES * (2 * kvh + 2))
                ks = slice(LANES * kvh, LANES * (kvh + 1))
                q_stack = _stack_heads(q_ref[pl.ds(r0, BLOCK), ta], q_ref[pl.ds(r0, BLOCK), tb], lo)
                s = _band_scores(q_stack, k_ext[band, ks], mask)
                sink = _sink_rows(sinks_ref, kvh)
                m = jnp.max(jnp.maximum(s[:, :BLOCK], s[:, BLOCK:]), axis=1, keepdims=True)
                m = jnp.maximum(jnp.broadcast_to(m, (4 * BLOCK, LANES)), sink)
                e = jnp.concatenate([jnp.exp(s[:, :BLOCK] - m), jnp.exp(s[:, BLOCK:] - m)], axis=1).astype(BF16)
                den = _dot(e, jnp.ones((2 * BLOCK, LANES), BF16)) + jnp.exp(sink - m)
                o_stack = _dot(e, v_ext[band, ks]) / den
                o_scr[pl.ds(r0, BLOCK), ta], o_scr[pl.ds(r0, BLOCK), tb] = _unstack_heads(o_stack, lo)
                lse = m + jnp.log(den)
                for g in range(4):
                    lse_tile = lse_tile + jnp.where(head_lane == 4 * kvh + g, lse[BLOCK * g:BLOCK * (g + 1)], 0.0)
            lse_ref[pl.ds(r0, BLOCK), :] = lse_tile
            return carry
        lax.fori_loop(0, per_tile, block, 0)
        o = o_scr[...]
        o_ref[...] = o.astype(BF16)
        ag = jnp.concatenate([ag0_ref[...], ag1_ref[...]], axis=1).astype(F32)
        ya_ref[...] = _dot((o * (ag * _sig(ag))).astype(BF16), _square(wao_ref)).astype(BF16)

    cur = lambda w, col=0: pl.BlockSpec((tm, w), lambda i: (i, col))
    prev = lambda w: pl.BlockSpec((BLOCK, w), lambda i: (jnp.maximum(i * per_tile - 1, 0), 0))
    return pl.pallas_call(
        body, name="attention_forward", grid=(tokens // tm,),
        in_specs=[cur(D_MODEL), cur(512), prev(512), cur(512), prev(512),
                  cur(512, COL512_AG), cur(512, COL512_AG + 1),
                  pl.BlockSpec(memory_space=pltpu.SMEM), _pack_weight_spec(1)],
        out_specs=(cur(D_MODEL), cur(D_MODEL), cur(LANES)),
        out_shape=(jax.ShapeDtypeStruct((tokens, D_MODEL), BF16),
                   jax.ShapeDtypeStruct((tokens, D_MODEL), BF16),
                   jax.ShapeDtypeStruct((tokens, LANES), F32)),
        scratch_shapes=[pltpu.VMEM((BLOCK + tm, 512), BF16), pltpu.VMEM((BLOCK + tm, 512), BF16),
                        pltpu.VMEM((tm, D_MODEL), F32)],
        compiler_params=_cparams(("parallel",), VMEM_LIMIT),
    )(qr, kd, kd, vd, vd, proj, proj, sinks, w_ao)


def _merge_and_head(yc, ya, proj, x, target, w_out, final_g):
    tokens = x.shape[0]
    tm = min(512, tokens)
    last = tokens // tm - 1

    def body(yc_ref, ya_ref, mlc0_ref, mlc1_ref, mla0_ref, mla1_ref, x_ref, t_ref, wout_ref, fg_ref,
             dx2_ref, dyc_ref, dya_ref, dmlc_ref, dmla_ref, gwout_ref, part_ref, gacc):
        i = pl.program_id(0)

        @pl.when(i == 0)
        def _():
            gacc[...] = jnp.zeros_like(gacc)
            part_ref[...] = jnp.zeros_like(part_ref)

        yc, ya = yc_ref[...].astype(F32), ya_ref[...].astype(F32)
        gc = _sig(jnp.concatenate([mlc0_ref[...], mlc1_ref[...]], axis=1).astype(F32))
        ga = _sig(jnp.concatenate([mla0_ref[...], mla1_ref[...]], axis=1).astype(F32))
        merged = (gc * yc + ga * ya).astype(BF16)
        x2 = x_ref[...] + _dot(merged, _square(wout_ref))
        r2 = lax.rsqrt(jnp.mean(x2 * x2, axis=-1, keepdims=True) + RMS_EPS)
        x2n = x2 * r2
        fg = fg_ref[...]
        err = x2n * fg - t_ref[...]
        dy = err * (1.0 / D_MODEL)
        part_ref[0:1, :] += jnp.sum(dy * x2n, axis=0, keepdims=True)
        part_ref[1:2, :] += jnp.sum(err * err, axis=0, keepdims=True) * (0.5 / D_MODEL)
        dx2n = dy * fg
        dx2 = r2 * (dx2n - x2n * jnp.mean(dx2n * x2n, axis=-1, keepdims=True))
        dx2_ref[...] = dx2
        dx2b = dx2.astype(BF16)
        gacc[...] += _dot_tn(merged, dx2b)
        dm = _dot_nt(dx2b, _square(wout_ref))
        dyc_ref[...] = (dm * gc).astype(BF16)
        dya_ref[...] = (dm * ga).astype(BF16)
        dmlc_ref[...] = (dm * yc * (gc * (1.0 - gc))).astype(BF16)
        dmla_ref[...] = (dm * ya * (ga * (1.0 - ga))).astype(BF16)

        @pl.when(i == last)
        def _():
            gwout_ref[...] = gacc[...].astype(BF16)

    tile = lambda col=0: pl.BlockSpec((tm, D_MODEL), lambda i: (i, col))
    half = lambda col: pl.BlockSpec((tm, 512), lambda i: (i, col))
    return pl.pallas_call(
        body, name="merge_and_head", grid=(tokens // tm,),
        in_specs=[tile(), tile(), half(COL512_MLC), half(COL512_MLC + 1), half(COL512_MLA), half(COL512_MLA + 1),
                  tile(), tile(), _pack_weight_spec(2), _const_spec((1, D_MODEL))],
        out_specs=(tile(), tile(), tile(), tile(), tile(),
                   _const_spec((D_MODEL, D_MODEL)), _const_spec((8, D_MODEL))),
        out_shape=(jax.ShapeDtypeStruct((tokens, D_MODEL), F32),
                   jax.ShapeDtypeStruct((tokens, D_MODEL), BF16),
                   jax.ShapeDtypeStruct((tokens, D_MODEL), BF16),
                   jax.ShapeDtypeStruct((tokens, D_MODEL), BF16),
                   jax.ShapeDtypeStruct((tokens, D_MODEL), BF16),
                   jax.ShapeDtypeStruct((D_MODEL, D_MODEL), BF16),
                   jax.ShapeDtypeStruct((8, D_MODEL), F32)),
        scratch_shapes=[pltpu.VMEM((D_MODEL, D_MODEL), F32)],
        compiler_params=_cparams(("arbitrary",), VMEM_LIMIT),
    )(yc, ya, proj, proj, proj, proj, x, target, w_out, final_g)


def _conv_backward_pointwise(dyc, cv, proj, w_co, ln_g, ln_b):
    tokens = cv.shape[0]
    tm = min(512, tokens)
    last = tokens // tm - 1

    def body(dyc_ref, cv_ref, cg_ref, wco_ref, lng_ref, lnb_ref, dcv_ref, dcg_ref, gwco_ref, part_ref, gacc):
        i = pl.program_id(0)

        @pl.when(i == 0)
        def _():
            gacc[...] = jnp.zeros_like(gacc)
            part_ref[...] = jnp.zeros_like(part_ref)

        cv = cv_ref[...]
        mu = jnp.mean(cv, axis=-1, keepdims=True)
        zc = cv - mu
        rstd = lax.rsqrt(jnp.mean(zc * zc, axis=-1, keepdims=True) + LN_EPS)
        z = zc * rstd
        lng = lng_ref[...]
        ln = z * lng + lnb_ref[...]
        sl = _sig(ln)
        c = ln * sl
        cg = cg_ref[...].astype(F32)
        scg = _sig(cg)
        gate = cg * scg
        dyc = dyc_ref[...]
        gacc[...] += _dot_tn((c * gate).astype(BF16), dyc)
        dpc = _dot_nt(dyc, _square(wco_ref))
        dcg_ref[...] = (dpc * c * (scg * (1.0 + cg * (1.0 - scg)))).astype(BF16)
        dln = dpc * gate * (sl * (1.0 + ln * (1.0 - sl)))
        part_ref[0:1, :] += jnp.sum(dln * z, axis=0, keepdims=True)
        part_ref[1:2, :] += jnp.sum(dln, axis=0, keepdims=True)
        dz = dln * lng
        dcv = rstd * (dz - jnp.mean(dz, axis=-1, keepdims=True) - z * jnp.mean(dz * z, axis=-1, keepdims=True))
        part_ref[2:3, :] += jnp.sum(dcv, axis=0, keepdims=True)
        dcv_ref[...] = dcv

        @pl.when(i == last)
        def _():
            gwco_ref[...] = gacc[...].astype(BF16)

    tile = lambda col=0: pl.BlockSpec((tm, D_MODEL), lambda i: (i, col))
    return pl.pallas_call(
        body, name="conv_backward_pointwise", grid=(tokens // tm,),
        in_specs=[tile(), tile(), tile(COL_CG), _pack_weight_spec(0),
                  _const_spec((1, D_MODEL)), _const_spec((1, D_MODEL))],
        out_specs=(tile(), tile(), _const_spec((D_MODEL, D_MODEL)), _const_spec((8, D_MODEL))),
        out_shape=(jax.ShapeDtypeStruct((tokens, D_MODEL), F32),
                   jax.ShapeDtypeStruct((tokens, D_MODEL), BF16),
                   jax.ShapeDtypeStruct((D_MODEL, D_MODEL), BF16),
                   jax.ShapeDtypeStruct((8, D_MODEL), F32)),
        scratch_shapes=[pltpu.VMEM((D_MODEL, D_MODEL), F32)],
        compiler_params=_cparams(("arbitrary",), VMEM_LIMIT),
    )(dyc, cv, proj, w_co, ln_g, ln_b)


def _conv_backward_taps(dcv, proj, conv_w):
    tokens = dcv.shape[0]
    tm = CONV_TM
    nt = tokens // tm
    halo_blocks = tm // CONV_PAD

    def body(d_ref, dn_ref, a_ref, b_ref, ah_ref, bh_ref, cw_ref, da_ref, db_ref, gw_ref, u_ext, d_ext, du_scr, gw_acc):
        i = pl.program_id(0)

        @pl.when(i == 0)
        def _():
            gw_acc[...] = jnp.zeros_like(gw_acc)

        _fill_u_ext(u_ext, a_ref, b_ref, ah_ref, bh_ref, i == 0)
        for lc in range(N_LANE_CHUNKS):
            sl = slice(LANES * lc, LANES * (lc + 1))
            d_ext[lc, 0:tm, :] = d_ref[:, sl]
            d_ext[lc, tm:tm + CONV_PAD, :] = jnp.where(i == nt - 1, 0.0, dn_ref[:, sl])

        def lane_chunk(lc, carry):
            n_rc = tm // 64
            du = [jnp.zeros((64, LANES), F32) for _ in range(n_rc)]
            for j in range(CONV_KERNEL):
                w = cw_ref[lc, pl.ds(j, 1), :]
                gsum = jnp.zeros((8, LANES), F32)
                for rc in range(n_rc):
                    du[rc] = du[rc] + w * d_ext[lc, pl.ds(64 * rc + 30 - j, 64), :]
                    prod = d_ext[lc, pl.ds(64 * rc, 64), :] * u_ext[lc, pl.ds(64 * rc + 2 + j, 64), :]
                    gsum = gsum + jnp.sum(prod.reshape(8, 8, LANES), axis=0)
                gw_acc[lc, j] += gsum
            for rc in range(n_rc):
                du_scr[lc, pl.ds(64 * rc, 64), :] = du[rc]
            return carry
        lax.fori_loop(0, N_LANE_CHUNKS, lane_chunk, 0)

        du = jnp.concatenate([du_scr[lc] for lc in range(N_LANE_CHUNKS)], axis=1)
        a, b = a_ref[...].astype(F32), b_ref[...].astype(F32)
        sb = _sig(b)
        da_ref[...] = (du * sb).astype(BF16)
        db_ref[...] = (du * a * (sb * (1.0 - sb))).astype(BF16)

        @pl.when(i == nt - 1)
        def _():
            gw_ref[...] = jnp.sum(gw_acc[...], axis=2)

    def prev_halo(i):
        return jnp.maximum(i * halo_blocks - 1, 0)

    def next_halo(i):
        return jnp.minimum((i + 1) * halo_blocks, tokens // CONV_PAD - 1)

    tile = lambda col=0: pl.BlockSpec((tm, D_MODEL), lambda i: (i, col))
    return pl.pallas_call(
        body, name="conv_backward_taps", grid=(nt,),
        in_specs=[tile(), pl.BlockSpec((CONV_PAD, D_MODEL), lambda i: (next_halo(i), 0)),
                  tile(COL_A), tile(COL_B),
                  pl.BlockSpec((CONV_PAD, D_MODEL), lambda i: (prev_halo(i), COL_A)),
                  pl.BlockSpec((CONV_PAD, D_MODEL), lambda i: (prev_halo(i), COL_B)),
                  _const_spec((N_DEV, CONV_PAD, LANES))],
        out_specs=(tile(), tile(), _const_spec((N_DEV, CONV_PAD, LANES))),
        out_shape=(jax.ShapeDtypeStruct((tokens, D_MODEL), BF16),
                   jax.ShapeDtypeStruct((tokens, D_MODEL), BF16),
                   jax.ShapeDtypeStruct((N_DEV, CONV_PAD, LANES), F32)),
        scratch_shapes=[pltpu.VMEM((N_LANE_CHUNKS, CONV_PAD + tm, LANES), F32),
                        pltpu.VMEM((N_LANE_CHUNKS, tm + CONV_PAD, LANES), F32),
                        pltpu.VMEM((N_LANE_CHUNKS, tm, LANES), F32),
                        pltpu.VMEM((N_LANE_CHUNKS, CONV_PAD, 8, LANES), F32)],
        compiler_params=_cparams(("arbitrary",), VMEM_LIMIT),
    )(dcv, dcv, proj, proj, proj, proj, conv_w)


def _fold_kv_head(dup, lo, second_half):
    both = dup + pltpu.roll(dup, HEAD_DIM, 1)
    lo = lax.broadcasted_iota(jnp.int32, dup.shape, 1) < HEAD_DIM
    return jnp.where(lo, 0.0, both) if second_half else jnp.where(lo, both, 0.0)


def _attention_backward(dya, o, lse, qr, kd, vd, proj, sinks, w_ao, cos_t, sin_up, sin_dn):
    tokens = qr.shape[0]
    tm = min(512, tokens)
    per_tile = tm // BLOCK
    nt = tokens // tm
    scale = HEAD_DIM ** -0.5

    def body(dya_ref, o_ref, ag0_ref, ag1_ref, q_ref, kc_ref, kp_ref, vc_ref, vp_ref, sinks_ref, wao_ref, lse_ref,
             cos_c, up_c, dn_c, cos_p, up_p, dn_p,
             dq_ref, dkv_ref, dag_ref, gwao_ref, gsink_ref,
             gacc, k_ext, v_ext, dk_ext, dv_ext, dk_carry, dv_carry, do_scr, dq_scr):
        i = pl.program_id(0)
        lo = _lane_halves()

        @pl.when(i == 0)
        def _():
            gacc[...] = jnp.zeros_like(gacc)
            gsink_ref[...] = jnp.zeros_like(gsink_ref)
            dk_carry[...] = jnp.zeros_like(dk_carry)
            dv_carry[...] = jnp.zeros_like(dv_carry)
        dk_ext[...] = jnp.zeros_like(dk_ext)
        dv_ext[...] = jnp.zeros_like(dv_ext)

        @pl.when(i < nt)
        def _():
            dya = dya_ref[...]
            dpa = _dot_nt(dya, _square(wao_ref))
            o = o_ref[...].astype(F32)
            ag = jnp.concatenate([ag0_ref[...], ag1_ref[...]], axis=1).astype(F32)
            sg = _sig(ag)
            gate = ag * sg
            gacc[...] += _dot_tn((o * gate).astype(BF16), dya)
            dag_ref[...] = (dpa * o * (sg * (1.0 + ag * (1.0 - sg)))).astype(BF16)
            do_scr[...] = (dpa * gate).astype(BF16)
            k_ext[0:BLOCK, :], k_ext[BLOCK:BLOCK + tm, :] = kp_ref[...], kc_ref[...]
            v_ext[0:BLOCK, :], v_ext[BLOCK:BLOCK + tm, :] = vp_ref[...], vc_ref[...]

            def block(b, carry):
                r0 = pl.multiple_of(b * BLOCK, BLOCK)
                mine, band = pl.ds(r0, BLOCK), pl.ds(r0, 2 * BLOCK)
                lse_tile = lse_ref[mine, :]
                mask = _band_mask(i * per_tile + b)
                head_lane = lax.broadcasted_iota(jnp.int32, (1, LANES), 1)
                gsink = jnp.zeros((1, LANES), F32)
                zero_band = jnp.zeros((2 * BLOCK, LANES), F32)
                dk_band, dv_band = [zero_band, zero_band], [zero_band, zero_band]
                for kvh in range(N_KV_HEADS):
                    ta, tb = slice(LANES * 2 * kvh, LANES * (2 * kvh + 1)), slice(LANES * (2 * kvh + 1), LANES * (2 * kvh + 2))
                    ks = slice(LANES * kvh, LANES * (kvh + 1))
                    q_stack = _stack_heads(q_ref[mine, ta], q_ref[mine, tb], lo)
                    do_stack = _stack_heads(do_scr[mine, ta], do_scr[mine, tb], lo)
                    k2, v2 = k_ext[band, ks], v_ext[band, ks]
                    s = _band_scores(q_stack, k2, mask)
                    lse = jnp.concatenate([jnp.broadcast_to(lse_tile[:, 4 * kvh + g:4 * kvh + g + 1], (BLOCK, LANES))
                                           for g in range(4)], axis=0)
                    p_p, p_c = jnp.exp(s[:, :BLOCK] - lse), jnp.exp(s[:, BLOCK:] - lse)
                    dp = _dot_nt(do_stack, v2)
                    dp_p, dp_c = dp[:, :BLOCK], dp[:, BLOCK:]
                    delta = jnp.broadcast_to(jnp.sum(p_p * dp_p + p_c * dp_c, axis=1, keepdims=True), (4 * BLOCK, LANES))
                    ds = jnp.concatenate([p_p * (dp_p - delta), p_c * (dp_c - delta)], axis=1).astype(BF16)
                    sink_terms = jnp.exp(_sink_rows(sinks_ref, kvh) - lse) * delta
                    for g in range(4):
                        total = jnp.sum(sink_terms[BLOCK * g:BLOCK * (g + 1)], axis=0, keepdims=True)
                        gsink = gsink - jnp.where(head_lane == 4 * kvh + g, total, 0.0)
                    dq_scr[mine, ta], dq_scr[mine, tb] = _unstack_heads(_dot(ds, k2), lo)
                    tile, second = kvh // 2, kvh % 2 == 1
                    dk_band[tile] = dk_band[tile] + _fold_kv_head(_dot_tn(ds, q_stack), lo, second)
                    dv_band[tile] = dv_band[tile] + _fold_kv_head(
                        _dot_tn(jnp.concatenate([p_p, p_c], axis=1).astype(BF16), do_stack), lo, second)
                gsink_ref[0:1, :] += gsink
                cs, up, dn = cos_c[mine, :], up_c[mine, :], dn_c[mine, :]
                for p in range(D_MODEL // LANES):
                    sl = slice(LANES * p, LANES * (p + 1))
                    dq_ref[mine, sl] = (_rope_transposed(dq_scr[mine, sl], cs, up, dn) * scale).astype(BF16)
                for p in range(2):
                    sl = slice(LANES * p, LANES * (p + 1))
                    dk_ext[band, sl] += dk_band[p]
                    dv_ext[band, sl] += dv_band[p]
                return carry
            lax.fori_loop(0, per_tile, block, 0)

        last = slice(tm - BLOCK, tm)
        dk_carry[last, :] += dk_ext[0:BLOCK, :]
        dv_carry[last, :] += dv_ext[0:BLOCK, :]
        for p in range(2):
            sl = slice(LANES * p, LANES * (p + 1))
            dkv_ref[:, sl] = _rope_transposed(dk_carry[:, sl], cos_p[...], up_p[...], dn_p[...]).astype(BF16)
            dkv_ref[:, slice(256 + LANES * p, 256 + LANES * (p + 1))] = dv_carry[:, sl].astype(BF16)
        dk_carry[...] = dk_ext[BLOCK:BLOCK + tm, :]
        dv_carry[...] = dv_ext[BLOCK:BLOCK + tm, :]

        @pl.when(i == nt)
        def _():
            gwao_ref[...] = gacc[...].astype(BF16)

    def cur_idx(i):
        return jnp.minimum(i, nt - 1)

    def prev_idx(i):
        return jnp.clip(i - 1, 0, nt - 1)

    cur = lambda w, col=0: pl.BlockSpec((tm, w), lambda i: (cur_idx(i), col))
    prev = lambda w: pl.BlockSpec((tm, w), lambda i: (prev_idx(i), 0))
    before = lambda w: pl.BlockSpec((BLOCK, w), lambda i: (jnp.maximum(cur_idx(i) * per_tile - 1, 0), 0))
    return pl.pallas_call(
        body, name="attention_backward", grid=(nt + 1,),
        in_specs=[cur(D_MODEL), cur(D_MODEL), cur(512, COL512_AG), cur(512, COL512_AG + 1), cur(D_MODEL),
                  cur(512), before(512), cur(512), before(512),
                  pl.BlockSpec(memory_space=pltpu.SMEM), _pack_weight_spec(1), cur(LANES),
                  cur(LANES), cur(LANES), cur(LANES), prev(LANES), prev(LANES), prev(LANES)],
        out_specs=(cur(D_MODEL), prev(512), cur(D_MODEL),
                   _const_spec((D_MODEL, D_MODEL)), _const_spec((8, LANES))),
        out_shape=(jax.ShapeDtypeStruct((tokens, D_MODEL), BF16),
                   jax.ShapeDtypeStruct((tokens, 512), BF16),
                   jax.ShapeDtypeStruct((tokens, D_MODEL), BF16),
                   jax.ShapeDtypeStruct((D_MODEL, D_MODEL), BF16),
                   jax.ShapeDtypeStruct((8, LANES), F32)),
        scratch_shapes=[pltpu.VMEM((D_MODEL, D_MODEL), F32),
                        pltpu.VMEM((BLOCK + tm, 512), BF16), pltpu.VMEM((BLOCK + tm, 512), BF16),
                        pltpu.VMEM((BLOCK + tm, 256), F32), pltpu.VMEM((BLOCK + tm, 256), F32),
                        pltpu.VMEM((tm, 256), F32), pltpu.VMEM((tm, 256), F32),
                        pltpu.VMEM((tm, D_MODEL), BF16), pltpu.VMEM((tm, D_MODEL), F32)],
        compiler_params=_cparams(("arbitrary",), VMEM_LIMIT),
    )(dya, o, proj, proj, qr, kd, kd, vd, vd, sinks, w_ao, lse, cos_t, sin_up, sin_dn, cos_t, sin_up, sin_dn)


def _transpose_tokens(h):
    tokens = h.shape[0]
    tt = min(512, tokens)

    def body(h_ref, out_ref):
        out_ref[...] = h_ref[...].astype(F32).T.astype(BF16)

    return pl.pallas_call(
        body, name="transpose_tokens", grid=(tokens // tt,),
        in_specs=[pl.BlockSpec((tt, D_MODEL), lambda i: (i, 0))],
        out_specs=pl.BlockSpec((D_MODEL, tt), lambda i: (0, i)),
        out_shape=jax.ShapeDtypeStruct((D_MODEL, tokens), BF16),
        compiler_params=_cparams(("parallel",)),
    )(h)


def _input_backward(sections, w_in_t, x, dx2, norm_g):
    tokens = x.shape[0]
    tm = 256

    def body(*refs):
        sec = refs[:8]
        w_ref, x_ref, dx2_ref, g_ref, gx_ref, part_ref = refs[8:]

        @pl.when(pl.program_id(0) == 0)
        def _():
            part_ref[...] = jnp.zeros_like(part_ref)

        dh = jnp.zeros((tm, D_MODEL), F32)
        for s in range(8):
            dh = dh + _dot(sec[s][...], w_ref[_SECTION_ROWS[s]:_SECTION_ROWS[s] + _SECTION_WIDTH[s], :])
        xv = x_ref[...]
        r = lax.rsqrt(jnp.mean(xv * xv, axis=-1, keepdims=True) + RMS_EPS)
        xn = xv * r
        part_ref[0:1, :] += jnp.sum(dh * xn, axis=0, keepdims=True)
        dxn = dh * g_ref[...]
        gx_ref[...] = dx2_ref[...] + r * (dxn - xn * jnp.mean(dxn * xn, axis=-1, keepdims=True))

    tile = lambda w=D_MODEL: pl.BlockSpec((tm, w), lambda i: (i, 0))
    return pl.pallas_call(
        body, name="input_backward", grid=(tokens // tm,),
        in_specs=[tile(w) for w in _SECTION_WIDTH] + [
            pl.BlockSpec((IN_WIDTH, D_MODEL), lambda i: (0, 0), pipeline_mode=pl.Buffered(1)),
            tile(), tile(), _const_spec((1, D_MODEL))],
        out_specs=(tile(), _const_spec((8, D_MODEL))),
        out_shape=(jax.ShapeDtypeStruct((tokens, D_MODEL), F32),
                   jax.ShapeDtypeStruct((8, D_MODEL), F32)),
        compiler_params=_cparams(("arbitrary",), VMEM_LIMIT),
    )(*sections, w_in_t, x, dx2, norm_g)


def _adamw_math(w, g, m, v):
    m = ADAM_B1 * m + (1.0 - ADAM_B1) * g
    v = ADAM_B2 * v + (1.0 - ADAM_B2) * (g * g)
    m_hat = m / (1.0 - ADAM_B1 ** ADAM_STEP)
    v_hat = v / (1.0 - ADAM_B2 ** ADAM_STEP)
    delta = -ADAM_LR * (m_hat / (jnp.sqrt(v_hat) + ADAM_EPS) + ADAM_WD * w)
    return delta, m, v


def _sum_slots(recv_ref):
    total = recv_ref[0].astype(F32)
    for d in range(1, N_DEV):
        total = total + recv_ref[d].astype(F32)
    return total


def _adamw(name, w, g, m, v, tile_rows):
    rows, cols = w.shape

    def body(w_ref, g_ref, m_ref, v_ref, d_ref, nm_ref, nv_ref):
        d_ref[...], nm_ref[...], nv_ref[...] = _adamw_math(w_ref[...], g_ref[...], m_ref[...], v_ref[...])

    spec = pl.BlockSpec((tile_rows, cols), lambda i: (i, 0))
    shape = jax.ShapeDtypeStruct((rows, cols), F32)
    return pl.pallas_call(
        body, name=name, grid=(rows // tile_rows,),
        in_specs=[spec] * 4, out_specs=(spec,) * 3, out_shape=(shape,) * 3,
        compiler_params=_cparams(("parallel",)),
    )(w, g, m, v)


def _sum_adamw(name, recv, w, m, v):
    def body(recv_ref, w_ref, m_ref, v_ref, g_ref, d_ref, nm_ref, nv_ref):
        g = _sum_slots(recv_ref)
        g_ref[...] = g
        d_ref[...], nm_ref[...], nv_ref[...] = _adamw_math(w_ref[...], g, m_ref[...], v_ref[...])

    shape = jax.ShapeDtypeStruct(w.shape, F32)
    return pl.pallas_call(body, name=name, out_shape=(shape,) * 4)(recv, w, m, v)


def _pad_rows(a, rows):
    return jnp.concatenate([a, jnp.zeros((rows - a.shape[0],) + a.shape[1:], a.dtype)], axis=0)


def kernel(x, norm_g, w_in, conv_dw_w, conv_dw_b, conv_ln_g, conv_ln_b, w_conv_out, attn_sinks, w_attn_out, w_out, final_norm_g, loss_target, m_norm_g, m_w_in, m_conv_dw_w, m_conv_dw_b, m_conv_ln_g, m_conv_ln_b, m_w_conv_out, m_attn_sinks, m_w_attn_out, m_w_out, m_final_norm_g, v_norm_g, v_w_in, v_conv_dw_w, v_conv_dw_b, v_conv_ln_g, v_conv_ln_b, v_w_conv_out, v_attn_sinks, v_w_attn_out, v_w_out, v_final_norm_g):
    xs, target = x[0], loss_target[0]
    tokens = xs.shape[0]
    fg_row = final_norm_g.reshape(1, D_MODEL)

    taps_bits = lax.bitcast_convert_type(_pad_rows(conv_dw_w[0], CONV_PAD), BF16).reshape(8, D_MODEL)
    pack = jnp.concatenate([w_conv_out[0].astype(BF16), w_attn_out[0].astype(BF16), w_out[0].astype(BF16),
                            jnp.pad(taps_bits, ((0, PACK_ROWS - 3 * SHARD_SQ - 8), (0, 0)))], axis=0)
    w_in_t32 = w_in[0].T
    proj, h, w_in_t, pack_full = _gather_project(xs, norm_g, w_in_t32.astype(BF16), pack)
    w_co = w_ao = w_o = pack_full
    conv_w = lax.bitcast_convert_type(
        pack_full[:, 3 * SHARD_SQ:3 * SHARD_SQ + 8].reshape(N_DEV, CONV_PAD, LANES, 2), F32)

    cos_t, sin_up, sin_dn = _rope_tables(tokens)
    qr, kd, vd = _rope_qkv(proj, cos_t, sin_up, sin_dn)
    cv, yc = _conv_forward(proj, conv_w, conv_dw_b, conv_ln_g, conv_ln_b, w_co)
    o, ya, lse = _attention_forward(qr, kd, vd, proj, attn_sinks, w_ao)

    dx2, dyc, dya, dmlc, dmla, g_out, part_head = _merge_and_head(yc, ya, proj, xs, target, w_o, fg_row)
    dcv, dcg, g_co, part_conv = _conv_backward_pointwise(dyc, cv, proj, w_co, conv_ln_g, conv_ln_b)
    da, db, g_conv = _conv_backward_taps(dcv, proj, conv_w)
    dq, dkv, dag, g_ao, part_sink = _attention_backward(dya, o, lse, qr, kd, vd, proj, attn_sinks, w_ao, cos_t, sin_up, sin_dn)
    sections = (da, db, dcg, dq, dkv, dag, dmlc, dmla)
    grad_x, part_in = _input_backward(sections, w_in_t, xs, dx2, norm_g)

    small = jnp.concatenate([
        part_in[0:1], part_conv[2:3], part_conv[0:1], part_conv[1:2], part_head[0:1],
        jnp.pad(part_sink[0:1], ((0, 0), (0, D_MODEL - LANES))), part_head[1:2],
        jnp.zeros((1, D_MODEL), F32)], axis=0)

    g_mine, r_conv, r_small = _grad_exchange(sections, _transpose_tokens(h), g_co, g_ao, g_out, g_conv, small)

    g_in_t = g_mine[:SHARD_IN]
    w_in_res = _adamw("adamw_w_in", w_in_t32, g_in_t, m_w_in[0].T, v_w_in[0].T, 192)
    grad_w_in, d_w_in, nm_w_in, nv_w_in = (a.T for a in (g_in_t,) + tuple(w_in_res))
    sq = {}
    for j, (nm, w, m, v) in enumerate((("w_conv_out", w_conv_out, m_w_conv_out, v_w_conv_out),
                                       ("w_attn_out", w_attn_out, m_w_attn_out, v_w_attn_out),
                                       ("w_out", w_out, m_w_out, v_w_out))):
        g = g_mine[SHARD_IN + j * SHARD_SQ:SHARD_IN + (j + 1) * SHARD_SQ]
        sq[nm] = (g,) + tuple(_adamw("adamw_" + nm, w[0], g, m[0], v[0], SHARD_SQ))
    conv_res = _sum_adamw("sum_adamw_conv_dw_w", r_conv.reshape(N_DEV, CONV_PAD, LANES),
                          _pad_rows(conv_dw_w[0], CONV_PAD), _pad_rows(m_conv_dw_w[0], CONV_PAD),
                          _pad_rows(v_conv_dw_w[0], CONV_PAD))
    pad_sink = lambda a: jnp.pad(a, ((0, 0), (0, D_MODEL - N_Q_HEADS)))
    zero_rows = jnp.zeros((2, D_MODEL), F32)
    stack = lambda a, b, c, d, e, f: jnp.concatenate([a, b, c, d, e.reshape(1, D_MODEL), pad_sink(f), zero_rows], axis=0)
    small_res = _sum_adamw(
        "sum_adamw_small", r_small,
        stack(norm_g, conv_dw_b, conv_ln_g, conv_ln_b, final_norm_g, attn_sinks),
        stack(m_norm_g, m_conv_dw_b, m_conv_ln_g, m_conv_ln_b, m_final_norm_g, m_attn_sinks),
        stack(v_norm_g, v_conv_dw_b, v_conv_ln_g, v_conv_ln_b, v_final_norm_g, v_attn_sinks))
    loss = jnp.sum(small_res[0][6])

    def leaf(k):
        s = small_res[k]
        return (s[0:1], (grad_w_in, d_w_in, nm_w_in, nv_w_in)[k][None], conv_res[k][None, :CONV_KERNEL],
                s[1:2], s[2:3], s[3:4], sq["w_conv_out"][k][None], s[5:6, :N_Q_HEADS],
                sq["w_attn_out"][k][None], sq["w_out"][k][None], s[4])

    return (loss, grad_x[None], *leaf(0), *leaf(1), *leaf(2), *leaf(3))
```

```python
import jax
import jax.numpy as jnp
from jax import lax
from jax.experimental import pallas as pl
from jax.experimental.pallas import tpu as pltpu

F32 = jnp.float32
BF16 = jnp.bfloat16
MESH = pl.DeviceIdType.MESH

D_MODEL = 1024
IN_WIDTH = 7680
N_DEV = 8
SHARD_IN = IN_WIDTH // N_DEV
SHARD_SQ = D_MODEL // N_DEV
CONV_KERNEL = 31
CONV_PAD = 32
HEAD_DIM = 64
N_Q_HEADS = 16
N_KV_HEADS = 4
BLOCK = 128
LANES = 128
ROPE_THETA = 10000.0
RMS_EPS = 1e-5
LN_EPS = 1e-5
NEG = -1e30
ADAM_LR = 0.001
ADAM_B1 = 0.9
ADAM_B2 = 0.999
ADAM_EPS = 1e-08
ADAM_WD = 0.01
ADAM_STEP = 10

OFF_A, OFF_B, OFF_CG, OFF_Q, OFF_KV, OFF_AG, OFF_MLC, OFF_MLA = 0, 1024, 2048, 3072, 4096, 4608, 5632, 6656
COL_A, COL_B, COL_CG, COL_Q = 0, 1, 2, 3
COL512_KV, COL512_AG, COL512_MLC, COL512_MLA = 8, 9, 11, 13
UNIT = 2 * SHARD_IN
PACK_ROWS = 400

VMEM_LIMIT = 56 * 1024 * 1024


def _cparams(sem=None, vmem=None):
    return pltpu.CompilerParams(dimension_semantics=sem, vmem_limit_bytes=vmem)


def _sig(v):
    return 0.5 * jnp.tanh(0.5 * v) + 0.5


def _dot(a, b):
    return jnp.dot(a, b, preferred_element_type=F32)


def _dot_nt(a, b):
    return lax.dot_general(a, b, (((1,), (1,)), ((), ())), preferred_element_type=F32)


def _dot_tn(a, b):
    return lax.dot_general(a, b, (((0,), (0,)), ((), ())), preferred_element_type=F32)


def _const_spec(shape):
    nd = len(shape)
    return pl.BlockSpec(shape, lambda *_: (0,) * nd)


def _pack_weight_spec(j):
    return pl.BlockSpec((N_DEV, SHARD_SQ, D_MODEL), lambda *_: (0, j, 0))


def _square(w_ref):
    return w_ref[...].reshape(D_MODEL, D_MODEL)


def _mesh_pos():
    x, y, c = lax.axis_index("x"), lax.axis_index("y"), lax.axis_index("c")
    return x, y, c, 4 * x + 2 * y + c


def _peer(x, y, c, k):
    px = 1 - x if (k >> 2) & 1 else x
    py = 1 - y if (k >> 1) & 1 else y
    pc = 1 - c if k & 1 else c
    return (px, py, pc), 4 * px + 2 * py + pc


def _gather_project(x, norm_g, w_shard_t, pack):
    tokens = x.shape[0]
    tt = min(512, tokens // 2)
    n_tok = tokens // tt
    rc = min(128, tt)

    def body(x_hbm, g_ref, ws_hbm, pack_hbm, proj_hbm, ht_hbm, wfull_hbm, packfull_hbm,
             w_vmem, h_vmem, ht_vmem, x_buf, o_buf, send_sems, recv_sems, local_sems, x_sems, o_sems, ht_sems):
        x_, y_, c_, me = _mesh_pos()
        myself, sibling = (x_, y_, c_), (x_, y_, 1 - c_)
        chips = ((1 - x_, y_), (x_, 1 - y_), (1 - x_, 1 - y_))

        def shard(ref, idx):
            return ref.at[pl.ds(pl.multiple_of(idx * SHARD_IN, 64), SHARD_IN)]

        def copy(a, k, idx, to, own=False):
            if a == 0:
                src, dst = ws_hbm if own else shard(w_vmem, idx), shard(w_vmem, idx)
            else:
                src, dst = pack_hbm if own else packfull_hbm.at[idx], packfull_hbm.at[idx]
            return pltpu.make_async_remote_copy(src_ref=src, dst_ref=dst, send_sem=send_sems.at[a, k],
                                                recv_sem=recv_sems.at[a, k], device_id=to, device_id_type=MESH)

        own_w = pltpu.make_async_copy(ws_hbm, shard(w_vmem, me), local_sems.at[0])
        own_p = pltpu.make_async_copy(pack_hbm, packfull_hbm.at[me], local_sems.at[1])
        own_w.start()
        own_p.start()
        sent = []
        for a in range(2):
            sent.append(copy(a, 0, me, sibling, own=True))
            sent += [copy(a, 1 + r, me, (*chip, c_), own=True) for r, chip in enumerate(chips)]
        for cp in sent:
            cp.start()

        def x_copy(t, slot):
            return pltpu.make_async_copy(x_hbm.at[pl.ds(t * tt, tt)], x_buf.at[slot], x_sems.at[slot])

        x_copy(0, 0).start()
        for t in range(n_tok):
            slot = t % 2
            if t + 1 < n_tok:
                x_copy(t + 1, 1 - slot).start()
            x_copy(t, slot).wait()

            def chunk(r0, t=t, slot=slot):
                xv = x_buf[slot, pl.ds(r0, rc), :]
                r = lax.rsqrt(jnp.mean(xv * xv, axis=-1, keepdims=True) + RMS_EPS)
                h = xv * r * g_ref[...]
                h_vmem[pl.ds(t * tt + r0, rc), :] = h.astype(BF16)
                ht_vmem[t * (tt // rc) + r0 // rc] = h.T.astype(BF16)
            _row_chunks(tt, rc, chunk)
        local = [own_p]
        for n in range(tokens // rc):
            cp = pltpu.make_async_copy(ht_vmem.at[n], ht_hbm.at[pl.ds(0, D_MODEL), pl.ds(n * rc, rc)], ht_sems.at[n])
            cp.start()
            local.append(cp)

        def project_unit(q, u):
            rows = pl.ds(pl.multiple_of(q * UNIT, LANES), UNIT)
            w_out = pltpu.make_async_copy(w_vmem.at[rows], wfull_hbm.at[rows], local_sems.at[2 + u])
            w_out.start()
            local.append(w_out)

            def o_copy(slot, t):
                return pltpu.make_async_copy(
                    o_buf.at[slot], proj_hbm.at[pl.ds(pl.multiple_of(t * tt, tt), tt), rows], o_sems.at[slot])

            def tile(t, carry):
                slot = lax.rem(t, 2)

                @pl.when(t >= 2)
                def _():
                    o_copy(slot, t).wait()
                o_buf[slot] = _dot_nt(h_vmem[pl.ds(pl.multiple_of(t * tt, tt), tt), :], w_vmem[rows, :]).astype(BF16)
                o_copy(slot, t).start()
                return carry
            lax.fori_loop(0, n_tok, tile, 0)
            o_copy(0, 0).wait()
            o_copy(1, 0).wait()

        def dev(chip, core):
            return 4 * chip[0] + 2 * chip[1] + core

        def arrive_and_pass_on(a, r):
            copy(a, 1 + r, dev(chips[r], c_), myself).wait_recv()
            passed = copy(a, 4 + r, dev(chips[r], c_), sibling)
            passed.start()
            sent.append(passed)

        def passed_on_to_me(a, r):
            copy(a, 4 + r, dev(chips[r], 1 - c_), myself).wait_recv()

        own_w.wait()
        copy(0, 0, dev((x_, y_), 1 - c_), myself).wait_recv()
        project_unit(2 * x_ + y_, 0)
        arrive_and_pass_on(0, 0)
        arrive_and_pass_on(0, 1)
        passed_on_to_me(0, 0)
        project_unit(2 * chips[0][0] + chips[0][1], 1)
        arrive_and_pass_on(0, 2)
        passed_on_to_me(0, 1)
        project_unit(2 * chips[1][0] + chips[1][1], 2)
        passed_on_to_me(0, 2)
        project_unit(2 * chips[2][0] + chips[2][1], 3)
        for r in range(3):
            arrive_and_pass_on(1, r)
        copy(1, 0, dev((x_, y_), 1 - c_), myself).wait_recv()
        for r in range(3):
            passed_on_to_me(1, r)
        for cp in sent:
            cp.wait_send()
        for cp in local:
            cp.wait()

    hbm = pl.BlockSpec(memory_space=pltpu.HBM)
    return pl.pallas_call(
        body, name="gather_project",
        in_specs=[hbm, pl.BlockSpec(memory_space=pltpu.VMEM), hbm, hbm],
        out_specs=(hbm, hbm, hbm, hbm),
        out_shape=(jax.ShapeDtypeStruct((tokens, IN_WIDTH), BF16),
                   jax.ShapeDtypeStruct((D_MODEL, tokens), BF16),
                   jax.ShapeDtypeStruct((IN_WIDTH, D_MODEL), BF16),
                   jax.ShapeDtypeStruct((N_DEV, PACK_ROWS, D_MODEL), BF16)),
        scratch_shapes=[pltpu.VMEM((IN_WIDTH, D_MODEL), BF16),
                        pltpu.VMEM((tokens, D_MODEL), BF16),
                        pltpu.VMEM((tokens // rc, D_MODEL, rc), BF16),
                        pltpu.VMEM((2, tt, D_MODEL), F32),
                        pltpu.VMEM((2, tt, UNIT), BF16),
                        pltpu.SemaphoreType.DMA((2, N_DEV - 1)),
                        pltpu.SemaphoreType.DMA((2, N_DEV - 1)),
                        pltpu.SemaphoreType.DMA((6,)),
                        pltpu.SemaphoreType.DMA((2,)),
                        pltpu.SemaphoreType.DMA((2,)),
                        pltpu.SemaphoreType.DMA((tokens // rc,))],
        compiler_params=_cparams(None, VMEM_LIMIT),
    )(x, norm_g, w_shard_t, pack)


TAIL_ROWS = 3 * SHARD_SQ
HALF_ROWS = SHARD_IN + TAIL_ROWS

GRAD_CHUNK = 384
_SECTION_ROWS = (OFF_A, OFF_B, OFF_CG, OFF_Q, OFF_KV, OFF_AG, OFF_MLC, OFF_MLA)
_SECTION_WIDTH = (1024, 1024, 1024, 1024, 512, 1024, 1024, 1024)


def _dproj_pieces(first, width):
    out = []
    for s, (start, w) in enumerate(zip(_SECTION_ROWS, _SECTION_WIDTH)):
        lo, hi = max(first, start), min(first + width, start + w)
        if lo < hi:
            out.append((s, lo - start, hi - lo, lo - first))
    return out


def _grad_exchange(sections, h_t, g_co, g_ao, g_out, g_conv, small):
    tokens = h_t.shape[1]
    n_chunk = UNIT // GRAD_CHUNK
    rc = 192

    def body(*refs):
        sec = refs[:8]
        (ht_hbm, gco_hbm, gao_hbm, gout_hbm, gconv_hbm, small_hbm, gmine_hbm, rconv_hbm, rsmall_hbm,
         lhs_buf, ht_vmem, halves, tail_buf, out_buf, stage, final,
         lhs_sems, ht_sem, tail_in_sems, d2d_send, d2d_recv, ici_send, ici_recv,
         d2d_tail_send, d2d_tail_recv, ici_tail_send, ici_tail_recv,
         tiny_send, tiny_recv, local_sems) = refs[8:]
        head_rows, tail_rows = pl.ds(0, SHARD_IN), pl.ds(SHARD_IN, TAIL_ROWS)
        x_, y_, c_, me = _mesh_pos()
        myself, sibling = (x_, y_, c_), (x_, y_, 1 - c_)
        chips = ((1 - x_, 1 - y_), (1 - x_, y_), (x_, 1 - y_), (x_, y_))
        squares = (gco_hbm, gao_hbm, gout_hbm)

        def remote(src, dst, send_sem, recv_sem, to):
            return pltpu.make_async_remote_copy(src_ref=src, dst_ref=dst, send_sem=send_sem, recv_sem=recv_sem,
                                                device_id=to, device_id_type=MESH)

        own_tiny = [pltpu.make_async_copy(gconv_hbm.at[me], rconv_hbm.at[me], local_sems.at[0]),
                    pltpu.make_async_copy(small_hbm, rsmall_hbm.at[me], local_sems.at[1])]
        for cp in own_tiny:
            cp.start()
        tiny = []
        for k in range(1, N_DEV):
            peer, peer_idx = _peer(x_, y_, c_, k)
            tiny += [remote(gconv_hbm.at[peer_idx], rconv_hbm.at[me], tiny_send.at[0, k - 1], tiny_recv.at[0, k - 1], peer),
                     remote(small_hbm, rsmall_hbm.at[me], tiny_send.at[1, k - 1], tiny_recv.at[1, k - 1], peer)]
        for cp in tiny:
            cp.start()

        ht_in = pltpu.make_async_copy(ht_hbm, ht_vmem, ht_sem.at[0])
        ht_in.start()

        def fetch(q, j, slot, wait):
            for k in range(4):
                @pl.when(q == k)
                def _(k=k):
                    for n, (s, col, width, place) in enumerate(_dproj_pieces(k * UNIT + j * GRAD_CHUNK, GRAD_CHUNK)):
                        cp = pltpu.make_async_copy(sec[s].at[pl.ds(0, tokens), pl.ds(col, width)],
                                                   lhs_buf.at[slot, pl.ds(0, tokens), pl.ds(place, width)],
                                                   lhs_sems.at[slot, n])
                        cp.wait() if wait else cp.start()

        def chip_of(u):
            return 2 * chips[u][0] + chips[u][1]

        def d2d(u):
            return remote(halves.at[1 - c_], stage.at[u, head_rows], d2d_send.at[u], d2d_recv.at[u], sibling)

        def ici(u):
            return remote(stage.at[u, head_rows], final.at[u, head_rows], ici_send.at[u], ici_recv.at[u], (*chips[u], c_))

        def d2d_tail(u):
            return remote(tail_buf.at[u, 1 - c_], stage.at[u, tail_rows], d2d_tail_send.at[u], d2d_tail_recv.at[u], sibling)

        def ici_tail(u):
            return remote(stage.at[u, tail_rows], final.at[u, tail_rows], ici_tail_send.at[u], ici_tail_recv.at[u],
                          (*chips[u], c_))

        def tail_in(u):
            out = []
            for core in range(2):
                for n, g in enumerate(squares):
                    rows = pl.ds(pl.multiple_of((2 * chip_of(u) + core) * SHARD_SQ, SHARD_SQ), SHARD_SQ)
                    out.append(pltpu.make_async_copy(g.at[rows], tail_buf.at[u, core, pl.ds(n * SHARD_SQ, SHARD_SQ)],
                                                     tail_in_sems.at[u, 3 * core + n]))
            return out

        def add_mine(u, first, count, mine):
            def chunk(r0):
                rows = pl.ds(pl.multiple_of(first + r0, 64), rc)
                stage[u, rows, :] = (stage[u, rows, :].astype(F32) + mine(pl.ds(r0, rc)).astype(F32)).astype(BF16)
            _row_chunks(count, rc, chunk)

        def chip_sum(u):
            d2d(u).wait_recv()
            add_mine(u, 0, SHARD_IN, lambda rows: halves[c_, rows, :])
            if u < 3:
                ici(u).start()

        for u in range(4):
            for cp in tail_in(u):
                cp.start()
        for u in range(4):
            for cp in tail_in(u):
                cp.wait()
            d2d_tail(u).start()
        for u in range(4):
            d2d_tail(u).wait_recv()
            add_mine(u, SHARD_IN, TAIL_ROWS, lambda rows, u=u: tail_buf[u, c_, rows, :])
            if u < 3:
                ici_tail(u).start()

        def store_rows(block, first):
            n = block.shape[0]
            for core in range(2):
                lo, hi = max(first, core * SHARD_IN), min(first + n, (core + 1) * SHARD_IN)
                if lo < hi:
                    halves[core, lo - core * SHARD_IN:hi - core * SHARD_IN, :] = block[lo - first:hi - first].astype(BF16)

        fetch(chip_of(0), 0, 0, wait=False)
        ht_in.wait()
        for u in range(4):
            q = chip_of(u)
            for j in range(n_chunk):
                slot = (u * n_chunk + j) % 2
                if j + 1 < n_chunk:
                    fetch(q, j + 1, 1 - slot, wait=False)
                elif u + 1 < 4:
                    fetch(chip_of(u + 1), 0, 1 - slot, wait=False)
                fetch(q, j, slot, wait=True)
                grad_t = _dot(ht_vmem[...], lhs_buf[slot])
                if j == 0 and u > 0:
                    chip_sum(u - 1)
                    d2d(u - 1).wait_send()
                for r in range(GRAD_CHUNK // LANES):
                    store_rows(grad_t[:, LANES * r:LANES * (r + 1)].T, j * GRAD_CHUNK + LANES * r)
            d2d(u).start()

        chip_sum(3)
        for u in range(3):
            remote(stage.at[u, head_rows], final.at[u, head_rows], ici_send.at[u], ici_recv.at[u], myself).wait_recv()
            remote(stage.at[u, tail_rows], final.at[u, tail_rows], ici_tail_send.at[u], ici_tail_recv.at[u],
                   myself).wait_recv()

        def total(r0):
            rows = pl.ds(r0, rc)
            out_buf[rows, :] = ((stage[3, rows, :].astype(F32) + final[0, rows, :].astype(F32))
                                + final[1, rows, :].astype(F32)) + final[2, rows, :].astype(F32)
        _row_chunks(HALF_ROWS, rc, total)
        out = pltpu.make_async_copy(out_buf, gmine_hbm, local_sems.at[2])
        out.start()
        d2d(3).wait_send()
        for u in range(4):
            d2d_tail(u).wait_send()
        for u in range(3):
            ici(u).wait_send()
            ici_tail(u).wait_send()
        for k in range(1, N_DEV):
            peer, peer_idx = _peer(x_, y_, c_, k)
            remote(gconv_hbm.at[me], rconv_hbm.at[peer_idx], tiny_send.at[0, k - 1], tiny_recv.at[0, k - 1], myself).wait_recv()
            remote(small_hbm, rsmall_hbm.at[peer_idx], tiny_send.at[1, k - 1], tiny_recv.at[1, k - 1], myself).wait_recv()
        for cp in tiny:
            cp.wait_send()
        for cp in own_tiny:
            cp.wait()
        out.wait()

    hbm = pl.BlockSpec(memory_space=pltpu.HBM)
    return pl.pallas_call(
        body, name="grad_exchange",
        in_specs=[hbm] * 14, out_specs=(hbm, hbm, hbm),
        out_shape=(jax.ShapeDtypeStruct((HALF_ROWS, D_MODEL), F32),
                   jax.ShapeDtypeStruct((N_DEV, CONV_PAD, LANES), F32),
                   jax.ShapeDtypeStruct((N_DEV, 8, D_MODEL), F32)),
        scratch_shapes=[pltpu.VMEM((2, tokens, GRAD_CHUNK), BF16),
                        pltpu.VMEM((D_MODEL, tokens), BF16),
                        pltpu.VMEM((2, SHARD_IN, D_MODEL), BF16),
                        pltpu.VMEM((4, 2, TAIL_ROWS, D_MODEL), BF16),
                        pltpu.VMEM((HALF_ROWS, D_MODEL), F32),
                        pltpu.VMEM((4, HALF_ROWS, D_MODEL), BF16),
                        pltpu.VMEM((3, HALF_ROWS, D_MODEL), BF16),
                        pltpu.SemaphoreType.DMA((2, 3)),
                        pltpu.SemaphoreType.DMA((1,)),
                        pltpu.SemaphoreType.DMA((4, 6)),
                        pltpu.SemaphoreType.DMA((4,)),
                        pltpu.SemaphoreType.DMA((4,)),
                        pltpu.SemaphoreType.DMA((3,)),
                        pltpu.SemaphoreType.DMA((3,)),
                        pltpu.SemaphoreType.DMA((4,)),
                        pltpu.SemaphoreType.DMA((4,)),
                        pltpu.SemaphoreType.DMA((3,)),
                        pltpu.SemaphoreType.DMA((3,)),
                        pltpu.SemaphoreType.DMA((2, N_DEV - 1)),
                        pltpu.SemaphoreType.DMA((2, N_DEV - 1)),
                        pltpu.SemaphoreType.DMA((3,))],
        compiler_params=_cparams(None, 60 * 1024 * 1024),
    )(*sections, h_t, g_co, g_ao, g_out, g_conv, small)


def _row_chunks(total, size, fn):
    n = total // size
    if n == 1:
        fn(0)
        return

    def step(i, carry):
        fn(pl.multiple_of(i * size, size))
        return carry
    lax.fori_loop(0, n, step, 0)


def _rope_tables(tokens):
    inv_freq = ROPE_THETA ** (-jnp.arange(0, HEAD_DIM, 2, dtype=F32) / HEAD_DIM)
    ang = jnp.arange(tokens, dtype=jnp.int32).astype(F32)[:, None] * inv_freq[None, :]
    cos, sin = jnp.cos(ang), jnp.sin(ang)
    zero = jnp.zeros_like(sin)
    cos_t = jnp.tile(jnp.concatenate([cos, cos], axis=1), (1, LANES // HEAD_DIM))
    sin_up = jnp.tile(jnp.concatenate([-sin, zero], axis=1), (1, LANES // HEAD_DIM))
    sin_dn = jnp.tile(jnp.concatenate([zero, sin], axis=1), (1, LANES // HEAD_DIM))
    return cos_t, sin_up, sin_dn


def _rope(t, cos_t, sin_up, sin_dn):
    return t * cos_t + pltpu.roll(t, LANES - 32, 1) * sin_up + pltpu.roll(t, 32, 1) * sin_dn


def _rope_transposed(g, cos_t, sin_up, sin_dn):
    return g * cos_t + pltpu.roll(g * sin_up, 32, 1) + pltpu.roll(g * sin_dn, LANES - 32, 1)


def _lane_halves():
    lane = lax.broadcasted_iota(jnp.int32, (BLOCK, LANES), 1)
    return lane < HEAD_DIM


def _rope_qkv(proj, cos_t, sin_up, sin_dn):
    tokens = proj.shape[0]
    tm = min(512, tokens)
    scale = HEAD_DIM ** -0.5

    def body(q_ref, kv_ref, cos_ref, up_ref, dn_ref, qr_ref, kd_ref, vd_ref):
        lo = _lane_halves()

        def chunk(r0):
            rows = pl.ds(r0, BLOCK)
            cs, up, dn = cos_ref[rows, :], up_ref[rows, :], dn_ref[rows, :]
            for p in range(D_MODEL // LANES):
                sl = slice(LANES * p, LANES * (p + 1))
                qt = q_ref[rows, sl].astype(F32)
                qr_ref[rows, sl] = (_rope(qt, cs, up, dn) * scale).astype(BF16)
            for p in range(2):
                sl = slice(LANES * p, LANES * (p + 1))
                kt = _rope(kv_ref[rows, sl].astype(F32), cs, up, dn)
                vt = kv_ref[rows, slice(256 + LANES * p, 256 + LANES * (p + 1))].astype(F32)
                for src, dst in ((kt, kd_ref), (vt, vd_ref)):
                    first = jnp.where(lo, src, 0.0)
                    second = src - first
                    dst[rows, slice(LANES * 2 * p, LANES * (2 * p + 1))] = (first + pltpu.roll(first, HEAD_DIM, 1)).astype(BF16)
                    dst[rows, slice(LANES * (2 * p + 1), LANES * (2 * p + 2))] = (second + pltpu.roll(second, HEAD_DIM, 1)).astype(BF16)
        _row_chunks(tm, BLOCK, chunk)

    tab = pl.BlockSpec((tm, LANES), lambda i: (i, 0))
    return pl.pallas_call(
        body, name="rope_qkv", grid=(tokens // tm,),
        in_specs=[pl.BlockSpec((tm, D_MODEL), lambda i: (i, COL_Q)),
                  pl.BlockSpec((tm, 512), lambda i: (i, COL512_KV)), tab, tab, tab],
        out_specs=(pl.BlockSpec((tm, D_MODEL), lambda i: (i, 0)),
                   pl.BlockSpec((tm, 512), lambda i: (i, 0)),
                   pl.BlockSpec((tm, 512), lambda i: (i, 0))),
        out_shape=(jax.ShapeDtypeStruct((tokens, D_MODEL), BF16),
                   jax.ShapeDtypeStruct((tokens, 512), BF16),
                   jax.ShapeDtypeStruct((tokens, 512), BF16)),
        compiler_params=_cparams(("parallel",)),
    )(proj, proj, cos_t, sin_up, sin_dn)


CONV_TM = 256
N_LANE_CHUNKS = D_MODEL // LANES


def _fill_u_ext(u_ext, a_ref, b_ref, ah_ref, bh_ref, first_tile):
    for lc in range(N_LANE_CHUNKS):
        sl = slice(LANES * lc, LANES * (lc + 1))
        uh = ah_ref[:, sl].astype(F32) * _sig(bh_ref[:, sl].astype(F32))
        u_ext[lc, 0:CONV_PAD, :] = jnp.where(first_tile, 0.0, uh)
        u_ext[lc, CONV_PAD:CONV_PAD + CONV_TM, :] = a_ref[:, sl].astype(F32) * _sig(b_ref[:, sl].astype(F32))


def _conv_forward(proj, conv_w, dw_b, ln_g, ln_b, w_co):
    tokens = proj.shape[0]
    tm = CONV_TM
    halo_blocks = tm // CONV_PAD

    def body(a_ref, b_ref, ah_ref, bh_ref, cg_ref, cw_ref, dwb_ref, lng_ref, lnb_ref, wco_ref,
             cv_ref, yc_ref, u_ext, cv_scr):
        _fill_u_ext(u_ext, a_ref, b_ref, ah_ref, bh_ref, pl.program_id(0) == 0)

        def lane_chunk(lc, carry):
            for rc in range(tm // 64):
                acc = jnp.zeros((64, LANES), F32)
                for j in range(CONV_KERNEL):
                    acc = acc + cw_ref[lc, pl.ds(j, 1), :] * u_ext[lc, pl.ds(64 * rc + 2 + j, 64), :]
                cv_scr[lc, pl.ds(64 * rc, 64), :] = acc
            return carry
        lax.fori_loop(0, N_LANE_CHUNKS, lane_chunk, 0)

        cv = jnp.concatenate([cv_scr[lc] for lc in range(N_LANE_CHUNKS)], axis=1) + dwb_ref[...]
        cv_ref[...] = cv
        mu = jnp.mean(cv, axis=-1, keepdims=True)
        zc = cv - mu
        rstd = lax.rsqrt(jnp.mean(zc * zc, axis=-1, keepdims=True) + LN_EPS)
        ln = zc * rstd * lng_ref[...] + lnb_ref[...]
        cg = cg_ref[...].astype(F32)
        pc = (ln * _sig(ln)) * (cg * _sig(cg))
        yc_ref[...] = _dot(pc.astype(BF16), _square(wco_ref)).astype(BF16)

    def halo_map(i):
        return (jnp.maximum(i * halo_blocks - 1, 0), 0)

    tile = lambda col: pl.BlockSpec((tm, D_MODEL), lambda i: (i, col))
    return pl.pallas_call(
        body, name="conv_forward", grid=(tokens // tm,),
        in_specs=[tile(COL_A), tile(COL_B),
                  pl.BlockSpec((CONV_PAD, D_MODEL), lambda i: (halo_map(i)[0], COL_A)),
                  pl.BlockSpec((CONV_PAD, D_MODEL), lambda i: (halo_map(i)[0], COL_B)),
                  tile(COL_CG), _const_spec((N_DEV, CONV_PAD, LANES)),
                  _const_spec((1, D_MODEL)), _const_spec((1, D_MODEL)), _const_spec((1, D_MODEL)),
                  _pack_weight_spec(0)],
        out_specs=(pl.BlockSpec((tm, D_MODEL), lambda i: (i, 0)),
                   pl.BlockSpec((tm, D_MODEL), lambda i: (i, 0))),
        out_shape=(jax.ShapeDtypeStruct((tokens, D_MODEL), F32),
                   jax.ShapeDtypeStruct((tokens, D_MODEL), BF16)),
        scratch_shapes=[pltpu.VMEM((N_LANE_CHUNKS, CONV_PAD + tm, LANES), F32),
                        pltpu.VMEM((N_LANE_CHUNKS, tm, LANES), F32)],
        compiler_params=_cparams(("parallel",), VMEM_LIMIT),
    )(proj, proj, proj, proj, proj, conv_w, dw_b, ln_g, ln_b, w_co)


def _band_mask(n):
    row = lax.broadcasted_iota(jnp.int32, (4 * BLOCK, 2 * BLOCK), 0) & (BLOCK - 1)
    col = lax.broadcasted_iota(jnp.int32, (4 * BLOCK, 2 * BLOCK), 1)
    before = jnp.logical_and(jnp.logical_and(col < BLOCK, col > row), n > 0)
    return jnp.logical_or(before, jnp.logical_and(col >= BLOCK, col - BLOCK <= row))


def _stack_heads(tile_a, tile_b, lo):
    zero = jnp.zeros_like(tile_a)
    return jnp.concatenate([jnp.where(lo, tile_a, zero), jnp.where(lo, zero, tile_a),
                            jnp.where(lo, tile_b, zero), jnp.where(lo, zero, tile_b)], axis=0)


def _unstack_heads(stacked, lo):
    s = [stacked[BLOCK * g:BLOCK * (g + 1)] for g in range(4)]
    return (jnp.where(lo, s[0], 0.0) + jnp.where(lo, 0.0, s[1]),
            jnp.where(lo, s[2], 0.0) + jnp.where(lo, 0.0, s[3]))


def _band_scores(q_stack, k2, mask):
    return jnp.where(mask, _dot_nt(q_stack, k2), NEG)


def _sink_rows(sinks_ref, kvh):
    return jnp.concatenate([jnp.full((BLOCK, LANES), sinks_ref[0, 4 * kvh + g], F32) for g in range(4)], axis=0)


def _attention_forward(qr, kd, vd, proj, sinks, w_ao):
    tokens = qr.shape[0]
    tm = min(512, tokens)
    per_tile = tm // BLOCK

    def body(q_ref, kc_ref, kp_ref, vc_ref, vp_ref, ag0_ref, ag1_ref, sinks_ref, wao_ref, o_ref, ya_ref, lse_ref,
             k_ext, v_ext, o_scr):
        i = pl.program_id(0)
        lo = _lane_halves()
        head_lane = lax.broadcasted_iota(jnp.int32, (1, LANES), 1)
        k_ext[0:BLOCK, :], k_ext[BLOCK:BLOCK + tm, :] = kp_ref[...], kc_ref[...]
        v_ext[0:BLOCK, :], v_ext[BLOCK:BLOCK + tm, :] = vp_ref[...], vc_ref[...]

        def block(b, carry):
            r0 = pl.multiple_of(b * BLOCK, BLOCK)
            band = pl.ds(r0, 2 * BLOCK)
            mask = _band_mask(i * per_tile + b)
            lse_tile = jnp.zeros((BLOCK, LANES), F32)
            for kvh in range(N_KV_HEADS):
                ta, tb = slice(LANES * 2 * kvh, LANES * (2 * kvh + 1)), slice(LANES * (2 * kvh + 1), LANES * (2 * kvh + 2))
                ks = slice(LANES * kvh, LANES * (kvh + 1))
                q_stack = _stack_heads(q_ref[pl.ds(r0, BLOCK), ta], q_ref[pl.ds(r0, BLOCK), tb], lo)
                s = _band_scores(q_stack, k_ext[band, ks], mask)
                sink = _sink_rows(sinks_ref, kvh)
                m = jnp.max(jnp.maximum(s[:, :BLOCK], s[:, BLOCK:]), axis=1, keepdims=True)
                m = jnp.maximum(jnp.broadcast_to(m, (4 * BLOCK, LANES)), sink)
                e = jnp.concatenate([jnp.exp(s[:, :BLOCK] - m), jnp.exp(s[:, BLOCK:] - m)], axis=1).astype(BF16)
                den = _dot(e, jnp.ones((2 * BLOCK, LANES), BF16)) + jnp.exp(sink - m)
                o_stack = _dot(e, v_ext[band, ks]) / den
                o_scr[pl.ds(r0, BLOCK), ta], o_scr[pl.ds(r0, BLOCK), tb] = _unstack_heads(o_stack, lo)
                lse = m + jnp.log(den)
                for g in range(4):
                    lse_tile = lse_tile + jnp.where(head_lane == 4 * kvh + g, lse[BLOCK * g:BLOCK * (g + 1)], 0.0)
            lse_ref[pl.ds(r0, BLOCK), :] = lse_tile
            return carry
        lax.fori_loop(0, per_tile, block, 0)
        o = o_scr[...]
        o_ref[...] = o.astype(BF16)
        ag = jnp.concatenate([ag0_ref[...], ag1_ref[...]], axis=1).astype(F32)
        ya_ref[...] = _dot((o * (ag * _sig(ag))).astype(BF16), _square(wao_ref)).astype(BF16)

    cur = lambda w, col=0: pl.BlockSpec((tm, w), lambda i: (i, col))
    prev = lambda w: pl.BlockSpec((BLOCK, w), lambda i: (jnp.maximum(i * per_tile - 1, 0), 0))
    return pl.pallas_call(
        body, name="attention_forward", grid=(tokens // tm,),
        in_specs=[cur(D_MODEL), cur(512), prev(512), cur(512), prev(512),
                  cur(512, COL512_AG), cur(512, COL512_AG + 1),
                  pl.BlockSpec(memory_space=pltpu.SMEM), _pack_weight_spec(1)],
        out_specs=(cur(D_MODEL), cur(D_MODEL), cur(LANES)),
        out_shape=(jax.ShapeDtypeStruct((tokens, D_MODEL), BF16),
                   jax.ShapeDtypeStruct((tokens, D_MODEL), BF16),
                   jax.ShapeDtypeStruct((tokens, LANES), F32)),
        scratch_shapes=[pltpu.VMEM((BLOCK + tm, 512), BF16), pltpu.VMEM((BLOCK + tm, 512), BF16),
                        pltpu.VMEM((tm, D_MODEL), F32)],
        compiler_params=_cparams(("parallel",), VMEM_LIMIT),
    )(qr, kd, kd, vd, vd, proj, proj, sinks, w_ao)


def _merge_and_head(yc, ya, proj, x, target, w_out, final_g):
    tokens = x.shape[0]
    tm = min(512, tokens)
    last = tokens // tm - 1

    def body(yc_ref, ya_ref, mlc0_ref, mlc1_ref, mla0_ref, mla1_ref, x_ref, t_ref, wout_ref, fg_ref,
             dx2_ref, dyc_ref, dya_ref, dmlc_ref, dmla_ref, gwout_ref, part_ref, gacc):
        i = pl.program_id(0)

        @pl.when(i == 0)
        def _():
            gacc[...] = jnp.zeros_like(gacc)
            part_ref[...] = jnp.zeros_like(part_ref)

        yc, ya = yc_ref[...].astype(F32), ya_ref[...].astype(F32)
        gc = _sig(jnp.concatenate([mlc0_ref[...], mlc1_ref[...]], axis=1).astype(F32))
        ga = _sig(jnp.concatenate([mla0_ref[...], mla1_ref[...]], axis=1).astype(F32))
        merged = (gc * yc + ga * ya).astype(BF16)
        x2 = x_ref[...] + _dot(merged, _square(wout_ref))
        r2 = lax.rsqrt(jnp.mean(x2 * x2, axis=-1, keepdims=True) + RMS_EPS)
        x2n = x2 * r2
        fg = fg_ref[...]
        err = x2n * fg - t_ref[...]
        dy = err * (1.0 / D_MODEL)
        part_ref[0:1, :] += jnp.sum(dy * x2n, axis=0, keepdims=True)
        part_ref[1:2, :] += jnp.sum(err * err, axis=0, keepdims=True) * (0.5 / D_MODEL)
        dx2n = dy * fg
        dx2 = r2 * (dx2n - x2n * jnp.mean(dx2n * x2n, axis=-1, keepdims=True))
        dx2_ref[...] = dx2
        dx2b = dx2.astype(BF16)
        gacc[...] += _dot_tn(merged, dx2b)
        dm = _dot_nt(dx2b, _square(wout_ref))
        dyc_ref[...] = (dm * gc).astype(BF16)
        dya_ref[...] = (dm * ga).astype(BF16)
        dmlc_ref[...] = (dm * yc * (gc * (1.0 - gc))).astype(BF16)
        dmla_ref[...] = (dm * ya * (ga * (1.0 - ga))).astype(BF16)

        @pl.when(i == last)
        def _():
            gwout_ref[...] = gacc[...].astype(BF16)

    tile = lambda col=0: pl.BlockSpec((tm, D_MODEL), lambda i: (i, col))
    half = lambda col: pl.BlockSpec((tm, 512), lambda i: (i, col))
    return pl.pallas_call(
        body, name="merge_and_head", grid=(tokens // tm,),
        in_specs=[tile(), tile(), half(COL512_MLC), half(COL512_MLC + 1), half(COL512_MLA), half(COL512_MLA + 1),
                  tile(), tile(), _pack_weight_spec(2), _const_spec((1, D_MODEL))],
        out_specs=(tile(), tile(), tile(), tile(), tile(),
                   _const_spec((D_MODEL, D_MODEL)), _const_spec((8, D_MODEL))),
        out_shape=(jax.ShapeDtypeStruct((tokens, D_MODEL), F32),
                   jax.ShapeDtypeStruct((tokens, D_MODEL), BF16),
                   jax.ShapeDtypeStruct((tokens, D_MODEL), BF16),
                   jax.ShapeDtypeStruct((tokens, D_MODEL), BF16),
                   jax.ShapeDtypeStruct((tokens, D_MODEL), BF16),
                   jax.ShapeDtypeStruct((D_MODEL, D_MODEL), BF16),
                   jax.ShapeDtypeStruct((8, D_MODEL), F32)),
        scratch_shapes=[pltpu.VMEM((D_MODEL, D_MODEL), F32)],
        compiler_params=_cparams(("arbitrary",), VMEM_LIMIT),
    )(yc, ya, proj, proj, proj, proj, x, target, w_out, final_g)


def _conv_backward_pointwise(dyc, cv, proj, w_co, ln_g, ln_b):
    tokens = cv.shape[0]
    tm = min(512, tokens)
    last = tokens // tm - 1

    def body(dyc_ref, cv_ref, cg_ref, wco_ref, lng_ref, lnb_ref, dcv_ref, dcg_ref, gwco_ref, part_ref, gacc):
        i = pl.program_id(0)

        @pl.when(i == 0)
        def _():
            gacc[...] = jnp.zeros_like(gacc)
            part_ref[...] = jnp.zeros_like(part_ref)

        cv = cv_ref[...]
        mu = jnp.mean(cv, axis=-1, keepdims=True)
        zc = cv - mu
        rstd = lax.rsqrt(jnp.mean(zc * zc, axis=-1, keepdims=True) + LN_EPS)
        z = zc * rstd
        lng = lng_ref[...]
        ln = z * lng + lnb_ref[...]
        sl = _sig(ln)
        c = ln * sl
        cg = cg_ref[...].astype(F32)
        scg = _sig(cg)
        gate = cg * scg
        dyc = dyc_ref[...]
        gacc[...] += _dot_tn((c * gate).astype(BF16), dyc)
        dpc = _dot_nt(dyc, _square(wco_ref))
        dcg_ref[...] = (dpc * c * (scg * (1.0 + cg * (1.0 - scg)))).astype(BF16)
        dln = dpc * gate * (sl * (1.0 + ln * (1.0 - sl)))
        part_ref[0:1, :] += jnp.sum(dln * z, axis=0, keepdims=True)
        part_ref[1:2, :] += jnp.sum(dln, axis=0, keepdims=True)
        dz = dln * lng
        dcv = rstd * (dz - jnp.mean(dz, axis=-1, keepdims=True) - z * jnp.mean(dz * z, axis=-1, keepdims=True))
        part_ref[2:3, :] += jnp.sum(dcv, axis=0, keepdims=True)
        dcv_ref[...] = dcv

        @pl.when(i == last)
        def _():
            gwco_ref[...] = gacc[...].astype(BF16)

    tile = lambda col=0: pl.BlockSpec((tm, D_MODEL), lambda i: (i, col))
    return pl.pallas_call(
        body, name="conv_backward_pointwise", grid=(tokens // tm,),
        in_specs=[tile(), tile(), tile(COL_CG), _pack_weight_spec(0),
                  _const_spec((1, D_MODEL)), _const_spec((1, D_MODEL))],
        out_specs=(tile(), tile(), _const_spec((D_MODEL, D_MODEL)), _const_spec((8, D_MODEL))),
        out_shape=(jax.ShapeDtypeStruct((tokens, D_MODEL), F32),
                   jax.ShapeDtypeStruct((tokens, D_MODEL), BF16),
                   jax.ShapeDtypeStruct((D_MODEL, D_MODEL), BF16),
                   jax.ShapeDtypeStruct((8, D_MODEL), F32)),
        scratch_shapes=[pltpu.VMEM((D_MODEL, D_MODEL), F32)],
        compiler_params=_cparams(("arbitrary",), VMEM_LIMIT),
    )(dyc, cv, proj, w_co, ln_g, ln_b)


def _conv_backward_taps(dcv, proj, conv_w):
    tokens = dcv.shape[0]
    tm = CONV_TM
    nt = tokens // tm
    halo_blocks = tm // CONV_PAD

    def body(d_ref, dn_ref, a_ref, b_ref, ah_ref, bh_ref, cw_ref, da_ref, db_ref, gw_ref, u_ext, d_ext, du_scr, gw_acc):
        i = pl.program_id(0)

        @pl.when(i == 0)
        def _():
            gw_acc[...] = jnp.zeros_like(gw_acc)

        _fill_u_ext(u_ext, a_ref, b_ref, ah_ref, bh_ref, i == 0)
        for lc in range(N_LANE_CHUNKS):
            sl = slice(LANES * lc, LANES * (lc + 1))
            d_ext[lc, 0:tm, :] = d_ref[:, sl]
            d_ext[lc, tm:tm + CONV_PAD, :] = jnp.where(i == nt - 1, 0.0, dn_ref[:, sl])

        def lane_chunk(lc, carry):
            n_rc = tm // 64
            du = [jnp.zeros((64, LANES), F32) for _ in range(n_rc)]
            for j in range(CONV_KERNEL):
                w = cw_ref[lc, pl.ds(j, 1), :]
                gsum = jnp.zeros((8, LANES), F32)
                for rc in range(n_rc):
                    du[rc] = du[rc] + w * d_ext[lc, pl.ds(64 * rc + 30 - j, 64), :]
                    prod = d_ext[lc, pl.ds(64 * rc, 64), :] * u_ext[lc, pl.ds(64 * rc + 2 + j, 64), :]
                    gsum = gsum + jnp.sum(prod.reshape(8, 8, LANES), axis=0)
                gw_acc[lc, j] += gsum
            for rc in range(n_rc):
                du_scr[lc, pl.ds(64 * rc, 64), :] = du[rc]
            return carry
        lax.fori_loop(0, N_LANE_CHUNKS, lane_chunk, 0)

        du = jnp.concatenate([du_scr[lc] for lc in range(N_LANE_CHUNKS)], axis=1)
        a, b = a_ref[...].astype(F32), b_ref[...].astype(F32)
        sb = _sig(b)
        da_ref[...] = (du * sb).astype(BF16)
        db_ref[...] = (du * a * (sb * (1.0 - sb))).astype(BF16)

        @pl.when(i == nt - 1)
        def _():
            gw_ref[...] = jnp.sum(gw_acc[...], axis=2)

    def prev_halo(i):
        return jnp.maximum(i * halo_blocks - 1, 0)

    def next_halo(i):
        return jnp.minimum((i + 1) * halo_blocks, tokens // CONV_PAD - 1)

    tile = lambda col=0: pl.BlockSpec((tm, D_MODEL), lambda i: (i, col))
    return pl.pallas_call(
        body, name="conv_backward_taps", grid=(nt,),
        in_specs=[tile(), pl.BlockSpec((CONV_PAD, D_MODEL), lambda i: (next_halo(i), 0)),
                  tile(COL_A), tile(COL_B),
                  pl.BlockSpec((CONV_PAD, D_MODEL), lambda i: (prev_halo(i), COL_A)),
                  pl.BlockSpec((CONV_PAD, D_MODEL), lambda i: (prev_halo(i), COL_B)),
                  _const_spec((N_DEV, CONV_PAD, LANES))],
        out_specs=(tile(), tile(), _const_spec((N_DEV, CONV_PAD, LANES))),
        out_shape=(jax.ShapeDtypeStruct((tokens, D_MODEL), BF16),
                   jax.ShapeDtypeStruct((tokens, D_MODEL), BF16),
                   jax.ShapeDtypeStruct((N_DEV, CONV_PAD, LANES), F32)),
        scratch_shapes=[pltpu.VMEM((N_LANE_CHUNKS, CONV_PAD + tm, LANES), F32),
                        pltpu.VMEM((N_LANE_CHUNKS, tm + CONV_PAD, LANES), F32),
                        pltpu.VMEM((N_LANE_CHUNKS, tm, LANES), F32),
                        pltpu.VMEM((N_LANE_CHUNKS, CONV_PAD, 8, LANES), F32)],
        compiler_params=_cparams(("arbitrary",), VMEM_LIMIT),
    )(dcv, dcv, proj, proj, proj, proj, conv_w)


def _fold_kv_head(dup, lo, second_half):
    both = dup + pltpu.roll(dup, HEAD_DIM, 1)
    lo = lax.broadcasted_iota(jnp.int32, dup.shape, 1) < HEAD_DIM
    return jnp.where(lo, 0.0, both) if second_half else jnp.where(lo, both, 0.0)


def _attention_backward(dya, o, lse, qr, kd, vd, proj, sinks, w_ao, cos_t, sin_up, sin_dn):
    tokens = qr.shape[0]
    tm = min(512, tokens)
    per_tile = tm // BLOCK
    nt = tokens // tm
    scale = HEAD_DIM ** -0.5

    def body(dya_ref, o_ref, ag0_ref, ag1_ref, q_ref, kc_ref, kp_ref, vc_ref, vp_ref, sinks_ref, wao_ref, lse_ref,
             cos_c, up_c, dn_c, cos_p, up_p, dn_p,
             dq_ref, dkv_ref, dag_ref, gwao_ref, gsink_ref,
             gacc, k_ext, v_ext, dk_ext, dv_ext, dk_carry, dv_carry, do_scr, dq_scr, delta_scr):
        i = pl.program_id(0)
        lo = _lane_halves()

        @pl.when(i == 0)
        def _():
            gacc[...] = jnp.zeros_like(gacc)
            gsink_ref[...] = jnp.zeros_like(gsink_ref)
            dk_carry[...] = jnp.zeros_like(dk_carry)
            dv_carry[...] = jnp.zeros_like(dv_carry)
        dk_ext[...] = jnp.zeros_like(dk_ext)
        dv_ext[...] = jnp.zeros_like(dv_ext)

        @pl.when(i < nt)
        def _():
            dya = dya_ref[...]
            dpa = _dot_nt(dya, _square(wao_ref))
            o = o_ref[...].astype(F32)
            ag = jnp.concatenate([ag0_ref[...], ag1_ref[...]], axis=1).astype(F32)
            sg = _sig(ag)
            gate = ag * sg
            gacc[...] += _dot_tn((o * gate).astype(BF16), dya)
            dag_ref[...] = (dpa * o * (sg * (1.0 + ag * (1.0 - sg)))).astype(BF16)
            do = dpa * gate
            do_scr[...] = do.astype(BF16)
            member = jnp.where(lax.broadcasted_iota(jnp.int32, (D_MODEL, LANES), 0) >> (HEAD_DIM.bit_length() - 1)
                               == lax.broadcasted_iota(jnp.int32, (D_MODEL, LANES), 1), 1.0, 0.0).astype(BF16)
            delta_scr[...] = _dot((do * o).astype(BF16), member)
            k_ext[0:BLOCK, :], k_ext[BLOCK:BLOCK + tm, :] = kp_ref[...], kc_ref[...]
            v_ext[0:BLOCK, :], v_ext[BLOCK:BLOCK + tm, :] = vp_ref[...], vc_ref[...]

            def block(b, carry):
                r0 = pl.multiple_of(b * BLOCK, BLOCK)
                mine, band = pl.ds(r0, BLOCK), pl.ds(r0, 2 * BLOCK)
                lse_tile, delta_tile = lse_ref[mine, :], delta_scr[mine, :]
                mask = _band_mask(i * per_tile + b)
                head_lane = lax.broadcasted_iota(jnp.int32, (1, LANES), 1)
                gsink = jnp.zeros((1, LANES), F32)
                zero_band = jnp.zeros((2 * BLOCK, LANES), F32)
                dk_band, dv_band = [zero_band, zero_band], [zero_band, zero_band]
                for kvh in range(N_KV_HEADS):
                    ta, tb = slice(LANES * 2 * kvh, LANES * (2 * kvh + 1)), slice(LANES * (2 * kvh + 1), LANES * (2 * kvh + 2))
                    ks = slice(LANES * kvh, LANES * (kvh + 1))
                    q_stack = _stack_heads(q_ref[mine, ta], q_ref[mine, tb], lo)
                    do_stack = _stack_heads(do_scr[mine, ta], do_scr[mine, tb], lo)
                    k2, v2 = k_ext[band, ks], v_ext[band, ks]
                    s = _band_scores(q_stack, k2, mask)
                    lse = jnp.concatenate([jnp.broadcast_to(lse_tile[:, 4 * kvh + g:4 * kvh + g + 1], (BLOCK, LANES))
                                           for g in range(4)], axis=0)
                    p_p, p_c = jnp.exp(s[:, :BLOCK] - lse), jnp.exp(s[:, BLOCK:] - lse)
                    dp = _dot_nt(do_stack, v2)
                    dp_p, dp_c = dp[:, :BLOCK], dp[:, BLOCK:]
                    delta = jnp.concatenate([jnp.broadcast_to(delta_tile[:, 4 * kvh + g:4 * kvh + g + 1], (BLOCK, LANES))
                                             for g in range(4)], axis=0)
                    ds = jnp.concatenate([p_p * (dp_p - delta), p_c * (dp_c - delta)], axis=1).astype(BF16)
                    sink_terms = jnp.exp(_sink_rows(sinks_ref, kvh) - lse) * delta
                    for g in range(4):
                        total = jnp.sum(sink_terms[BLOCK * g:BLOCK * (g + 1)], axis=0, keepdims=True)
                        gsink = gsink - jnp.where(head_lane == 4 * kvh + g, total, 0.0)
                    dq_scr[mine, ta], dq_scr[mine, tb] = _unstack_heads(_dot(ds, k2), lo)
                    tile, second = kvh // 2, kvh % 2 == 1
                    dk_band[tile] = dk_band[tile] + _fold_kv_head(_dot_tn(ds, q_stack), lo, second)
                    dv_band[tile] = dv_band[tile] + _fold_kv_head(
                        _dot_tn(jnp.concatenate([p_p, p_c], axis=1).astype(BF16), do_stack), lo, second)
                gsink_ref[0:1, :] += gsink
                cs, up, dn = cos_c[mine, :], up_c[mine, :], dn_c[mine, :]
                for p in range(D_MODEL // LANES):
                    sl = slice(LANES * p, LANES * (p + 1))
                    dq_ref[mine, sl] = (_rope_transposed(dq_scr[mine, sl], cs, up, dn) * scale).astype(BF16)
                for p in range(2):
                    sl = slice(LANES * p, LANES * (p + 1))
                    dk_ext[band, sl] += dk_band[p]
                    dv_ext[band, sl] += dv_band[p]
                return carry
            lax.fori_loop(0, per_tile, block, 0)

        last = slice(tm - BLOCK, tm)
        dk_carry[last, :] += dk_ext[0:BLOCK, :]
        dv_carry[last, :] += dv_ext[0:BLOCK, :]
        for p in range(2):
            sl = slice(LANES * p, LANES * (p + 1))
            dkv_ref[:, sl] = _rope_transposed(dk_carry[:, sl], cos_p[...], up_p[...], dn_p[...]).astype(BF16)
            dkv_ref[:, slice(256 + LANES * p, 256 + LANES * (p + 1))] = dv_carry[:, sl].astype(BF16)
        dk_carry[...] = dk_ext[BLOCK:BLOCK + tm, :]
        dv_carry[...] = dv_ext[BLOCK:BLOCK + tm, :]

        @pl.when(i == nt)
        def _():
            gwao_ref[...] = gacc[...].astype(BF16)

    def cur_idx(i):
        return jnp.minimum(i, nt - 1)

    def prev_idx(i):
        return jnp.clip(i - 1, 0, nt - 1)

    cur = lambda w, col=0: pl.BlockSpec((tm, w), lambda i: (cur_idx(i), col))
    prev = lambda w: pl.BlockSpec((tm, w), lambda i: (prev_idx(i), 0))
    before = lambda w: pl.BlockSpec((BLOCK, w), lambda i: (jnp.maximum(cur_idx(i) * per_tile - 1, 0), 0))
    return pl.pallas_call(
        body, name="attention_backward", grid=(nt + 1,),
        in_specs=[cur(D_MODEL), cur(D_MODEL), cur(512, COL512_AG), cur(512, COL512_AG + 1), cur(D_MODEL),
                  cur(512), before(512), cur(512), before(512),
                  pl.BlockSpec(memory_space=pltpu.SMEM), _pack_weight_spec(1), cur(LANES),
                  cur(LANES), cur(LANES), cur(LANES), prev(LANES), prev(LANES), prev(LANES)],
        out_specs=(cur(D_MODEL), prev(512), cur(D_MODEL),
                   _const_spec((D_MODEL, D_MODEL)), _const_spec((8, LANES))),
        out_shape=(jax.ShapeDtypeStruct((tokens, D_MODEL), BF16),
                   jax.ShapeDtypeStruct((tokens, 512), BF16),
                   jax.ShapeDtypeStruct((tokens, D_MODEL), BF16),
                   jax.ShapeDtypeStruct((D_MODEL, D_MODEL), BF16),
                   jax.ShapeDtypeStruct((8, LANES), F32)),
        scratch_shapes=[pltpu.VMEM((D_MODEL, D_MODEL), F32),
                        pltpu.VMEM((BLOCK + tm, 512), BF16), pltpu.VMEM((BLOCK + tm, 512), BF16),
                        pltpu.VMEM((BLOCK + tm, 256), F32), pltpu.VMEM((BLOCK + tm, 256), F32),
                        pltpu.VMEM((tm, 256), F32), pltpu.VMEM((tm, 256), F32),
                        pltpu.VMEM((tm, D_MODEL), BF16), pltpu.VMEM((tm, D_MODEL), F32),
                        pltpu.VMEM((tm, LANES), F32)],
        compiler_params=_cparams(("arbitrary",), VMEM_LIMIT),
    )(dya, o, proj, proj, qr, kd, kd, vd, vd, sinks, w_ao, lse, cos_t, sin_up, sin_dn, cos_t, sin_up, sin_dn)


def _input_backward(sections, w_in_t, x, dx2, norm_g):
    tokens = x.shape[0]
    tm = 256

    def body(*refs):
        sec = refs[:8]
        w_ref, x_ref, dx2_ref, g_ref, gx_ref, part_ref = refs[8:]

        @pl.when(pl.program_id(0) == 0)
        def _():
            part_ref[...] = jnp.zeros_like(part_ref)

        dh = jnp.zeros((tm, D_MODEL), F32)
        for s in range(8):
            dh = dh + _dot(sec[s][...], w_ref[_SECTION_ROWS[s]:_SECTION_ROWS[s] + _SECTION_WIDTH[s], :])
        xv = x_ref[...]
        r = lax.rsqrt(jnp.mean(xv * xv, axis=-1, keepdims=True) + RMS_EPS)
        xn = xv * r
        part_ref[0:1, :] += jnp.sum(dh * xn, axis=0, keepdims=True)
        dxn = dh * g_ref[...]
        gx_ref[...] = dx2_ref[...] + r * (dxn - xn * jnp.mean(dxn * xn, axis=-1, keepdims=True))

    tile = lambda w=D_MODEL: pl.BlockSpec((tm, w), lambda i: (i, 0))
    return pl.pallas_call(
        body, name="input_backward", grid=(tokens // tm,),
        in_specs=[tile(w) for w in _SECTION_WIDTH] + [
            pl.BlockSpec((IN_WIDTH, D_MODEL), lambda i: (0, 0), pipeline_mode=pl.Buffered(1)),
            tile(), tile(), _const_spec((1, D_MODEL))],
        out_specs=(tile(), _const_spec((8, D_MODEL))),
        out_shape=(jax.ShapeDtypeStruct((tokens, D_MODEL), F32),
                   jax.ShapeDtypeStruct((8, D_MODEL), F32)),
        compiler_params=_cparams(("arbitrary",), VMEM_LIMIT),
    )(*sections, w_in_t, x, dx2, norm_g)


def _adamw_math(w, g, m, v):
    m = ADAM_B1 * m + (1.0 - ADAM_B1) * g
    v = ADAM_B2 * v + (1.0 - ADAM_B2) * (g * g)
    m_hat = m / (1.0 - ADAM_B1 ** ADAM_STEP)
    v_hat = v / (1.0 - ADAM_B2 ** ADAM_STEP)
    delta = -ADAM_LR * (m_hat / (jnp.sqrt(v_hat) + ADAM_EPS) + ADAM_WD * w)
    return delta, m, v


def _sum_slots(recv_ref):
    total = recv_ref[0].astype(F32)
    for d in range(1, N_DEV):
        total = total + recv_ref[d].astype(F32)
    return total


def _adamw(name, w, g, m, v, tile_rows):
    rows, cols = w.shape

    def body(w_ref, g_ref, m_ref, v_ref, d_ref, nm_ref, nv_ref):
        d_ref[...], nm_ref[...], nv_ref[...] = _adamw_math(w_ref[...], g_ref[...], m_ref[...], v_ref[...])

    spec = pl.BlockSpec((tile_rows, cols), lambda i: (i, 0))
    shape = jax.ShapeDtypeStruct((rows, cols), F32)
    return pl.pallas_call(
        body, name=name, grid=(rows // tile_rows,),
        in_specs=[spec] * 4, out_specs=(spec,) * 3, out_shape=(shape,) * 3,
        compiler_params=_cparams(("parallel",)),
    )(w, g, m, v)


def _sum_adamw(name, recv, w, m, v):
    def body(recv_ref, w_ref, m_ref, v_ref, g_ref, d_ref, nm_ref, nv_ref):
        g = _sum_slots(recv_ref)
        g_ref[...] = g
        d_ref[...], nm_ref[...], nv_ref[...] = _adamw_math(w_ref[...], g, m_ref[...], v_ref[...])

    shape = jax.ShapeDtypeStruct(w.shape, F32)
    return pl.pallas_call(body, name=name, out_shape=(shape,) * 4)(recv, w, m, v)


def _pad_rows(a, rows):
    return jnp.concatenate([a, jnp.zeros((rows - a.shape[0],) + a.shape[1:], a.dtype)], axis=0)


def kernel(x, norm_g, w_in, conv_dw_w, conv_dw_b, conv_ln_g, conv_ln_b, w_conv_out, attn_sinks, w_attn_out, w_out, final_norm_g, loss_target, m_norm_g, m_w_in, m_conv_dw_w, m_conv_dw_b, m_conv_ln_g, m_conv_ln_b, m_w_conv_out, m_attn_sinks, m_w_attn_out, m_w_out, m_final_norm_g, v_norm_g, v_w_in, v_conv_dw_w, v_conv_dw_b, v_conv_ln_g, v_conv_ln_b, v_w_conv_out, v_attn_sinks, v_w_attn_out, v_w_out, v_final_norm_g):
    xs, target = x[0], loss_target[0]
    tokens = xs.shape[0]
    fg_row = final_norm_g.reshape(1, D_MODEL)

    taps_bits = lax.bitcast_convert_type(_pad_rows(conv_dw_w[0], CONV_PAD), BF16).reshape(8, D_MODEL)
    pack = jnp.concatenate([w_conv_out[0].astype(BF16), w_attn_out[0].astype(BF16), w_out[0].astype(BF16),
                            jnp.pad(taps_bits, ((0, PACK_ROWS - 3 * SHARD_SQ - 8), (0, 0)))], axis=0)
    w_in_t32 = w_in[0].T
    proj, h_t, w_in_t, pack_full = _gather_project(xs, norm_g, w_in_t32.astype(BF16), pack)
    w_co = w_ao = w_o = pack_full
    conv_w = lax.bitcast_convert_type(
        pack_full[:, 3 * SHARD_SQ:3 * SHARD_SQ + 8].reshape(N_DEV, CONV_PAD, LANES, 2), F32)

    cos_t, sin_up, sin_dn = _rope_tables(tokens)
    qr, kd, vd = _rope_qkv(proj, cos_t, sin_up, sin_dn)
    cv, yc = _conv_forward(proj, conv_w, conv_dw_b, conv_ln_g, conv_ln_b, w_co)
    o, ya, lse = _attention_forward(qr, kd, vd, proj, attn_sinks, w_ao)

    dx2, dyc, dya, dmlc, dmla, g_out, part_head = _merge_and_head(yc, ya, proj, xs, target, w_o, fg_row)
    dcv, dcg, g_co, part_conv = _conv_backward_pointwise(dyc, cv, proj, w_co, conv_ln_g, conv_ln_b)
    da, db, g_conv = _conv_backward_taps(dcv, proj, conv_w)
    dq, dkv, dag, g_ao, part_sink = _attention_backward(dya, o, lse, qr, kd, vd, proj, attn_sinks, w_ao, cos_t, sin_up, sin_dn)
    sections = (da, db, dcg, dq, dkv, dag, dmlc, dmla)
    grad_x, part_in = _input_backward(sections, w_in_t, xs, dx2, norm_g)

    small = jnp.concatenate([
        part_in[0:1], part_conv[2:3], part_conv[0:1], part_conv[1:2], part_head[0:1],
        jnp.pad(part_sink[0:1], ((0, 0), (0, D_MODEL - LANES))), part_head[1:2],
        jnp.zeros((1, D_MODEL), F32)], axis=0)

    g_mine, r_conv, r_small = _grad_exchange(sections, h_t, g_co, g_ao, g_out, g_conv, small)

    g_in_t = g_mine[:SHARD_IN]
    w_in_res = _adamw("adamw_w_in", w_in_t32, g_in_t, m_w_in[0].T, v_w_in[0].T, 192)
    grad_w_in, d_w_in, nm_w_in, nv_w_in = (a.T for a in (g_in_t,) + tuple(w_in_res))
    sq = {}
    for j, (nm, w, m, v) in enumerate((("w_conv_out", w_conv_out, m_w_conv_out, v_w_conv_out),
                                       ("w_attn_out", w_attn_out, m_w_attn_out, v_w_attn_out),
                                       ("w_out", w_out, m_w_out, v_w_out))):
        g = g_mine[SHARD_IN + j * SHARD_SQ:SHARD_IN + (j + 1) * SHARD_SQ]
        sq[nm] = (g,) + tuple(_adamw("adamw_" + nm, w[0], g, m[0], v[0], SHARD_SQ))
    conv_res = _sum_adamw("sum_adamw_conv_dw_w", r_conv.reshape(N_DEV, CONV_PAD, LANES),
                          _pad_rows(conv_dw_w[0], CONV_PAD), _pad_rows(m_conv_dw_w[0], CONV_PAD),
                          _pad_rows(v_conv_dw_w[0], CONV_PAD))
    pad_sink = lambda a: jnp.pad(a, ((0, 0), (0, D_MODEL - N_Q_HEADS)))
    zero_rows = jnp.zeros((2, D_MODEL), F32)
    stack = lambda a, b, c, d, e, f: jnp.concatenate([a, b, c, d, e.reshape(1, D_MODEL), pad_sink(f), zero_rows], axis=0)
    small_res = _sum_adamw(
        "sum_adamw_small", r_small,
        stack(norm_g, conv_dw_b, conv_ln_g, conv_ln_b, final_norm_g, attn_sinks),
        stack(m_norm_g, m_conv_dw_b, m_conv_ln_g, m_conv_ln_b, m_final_norm_g, m_attn_sinks),
        stack(v_norm_g, v_conv_dw_b, v_conv_ln_g, v_conv_ln_b, v_final_norm_g, v_attn_sinks))
    loss = jnp.sum(small_res[0][6])

    def leaf(k):
        s = small_res[k]
        return (s[0:1], (grad_w_in, d_w_in, nm_w_in, nv_w_in)[k][None], conv_res[k][None, :CONV_KERNEL],
                s[1:2], s[2:3], s[3:4], sq["w_conv_out"][k][None], s[5:6, :N_Q_HEADS],
                sq["w_attn_out"][k][None], sq["w_out"][k][None], s[4])

    return (loss, grad_x[None], *leaf(0), *leaf(1), *leaf(2), *leaf(3))
```

```python
import jax
import jax.numpy as jnp
from jax import lax
from jax.experimental import pallas as pl
from jax.experimental.pallas import tpu as pltpu

F32 = jnp.float32
BF16 = jnp.bfloat16
MESH = pl.DeviceIdType.MESH

D_MODEL = 1024
IN_WIDTH = 7680
N_DEV = 8
SHARD_IN = IN_WIDTH // N_DEV
SHARD_SQ = D_MODEL // N_DEV
CONV_KERNEL = 31
CONV_PAD = 32
HEAD_DIM = 64
N_Q_HEADS = 16
N_KV_HEADS = 4
BLOCK = 128
LANES = 128
ROPE_THETA = 10000.0
RMS_EPS = 1e-5
LN_EPS = 1e-5
NEG = -1e30
ADAM_LR = 0.001
ADAM_B1 = 0.9
ADAM_B2 = 0.999
ADAM_EPS = 1e-08
ADAM_WD = 0.01
ADAM_STEP = 10

OFF_A, OFF_B, OFF_CG, OFF_Q, OFF_KV, OFF_AG, OFF_MLC, OFF_MLA = 0, 1024, 2048, 3072, 4096, 4608, 5632, 6656
COL_A, COL_B, COL_CG, COL_Q = 0, 1, 2, 3
COL512_KV, COL512_AG, COL512_MLC, COL512_MLA = 8, 9, 11, 13
UNIT = 2 * SHARD_IN
PACK_ROWS = 144

VMEM_LIMIT = 56 * 1024 * 1024


def _cparams(sem=None, vmem=None):
    return pltpu.CompilerParams(dimension_semantics=sem, vmem_limit_bytes=vmem)


def _sig(v):
    return 0.5 * jnp.tanh(0.5 * v) + 0.5


def _dot(a, b):
    return jnp.dot(a, b, preferred_element_type=F32)


def _dot_nt(a, b):
    return lax.dot_general(a, b, (((1,), (1,)), ((), ())), preferred_element_type=F32)


def _dot_tn(a, b):
    return lax.dot_general(a, b, (((0,), (0,)), ((), ())), preferred_element_type=F32)


def _const_spec(shape):
    nd = len(shape)
    return pl.BlockSpec(shape, lambda *_: (0,) * nd)


def _pack_weight_spec(j):
    return pl.BlockSpec((N_DEV, SHARD_SQ, D_MODEL), lambda *_: (0, j, 0))


def _square(w_ref):
    return w_ref[...].reshape(D_MODEL, D_MODEL)


def _mesh_pos():
    x, y, c = lax.axis_index("x"), lax.axis_index("y"), lax.axis_index("c")
    return x, y, c, 4 * x + 2 * y + c


def _peer(x, y, c, k):
    px = 1 - x if (k >> 2) & 1 else x
    py = 1 - y if (k >> 1) & 1 else y
    pc = 1 - c if k & 1 else c
    return (px, py, pc), 4 * px + 2 * py + pc


def _gather_project(x, norm_g, w_shard_t, pack):
    tokens = x.shape[0]
    tt = min(512, tokens // 2)
    n_tok = tokens // tt
    rc = min(128, tt)

    def body(x_hbm, g_ref, ws_hbm, pack_hbm, proj_hbm, ht_hbm, wfull_hbm, packfull_hbm,
             w_vmem, h_vmem, ht_vmem, x_buf, o_buf, send_sems, recv_sems, local_sems, x_sems, o_sems, ht_sems):
        x_, y_, c_, me = _mesh_pos()
        myself, sibling = (x_, y_, c_), (x_, y_, 1 - c_)
        chips = ((1 - x_, y_), (x_, 1 - y_), (1 - x_, 1 - y_))

        def shard(ref, idx):
            return ref.at[pl.ds(pl.multiple_of(idx * SHARD_IN, 64), SHARD_IN)]

        def copy(a, k, idx, to, own=False):
            if a == 0:
                src, dst = ws_hbm if own else shard(w_vmem, idx), shard(w_vmem, idx)
            else:
                src, dst = pack_hbm if own else packfull_hbm.at[idx], packfull_hbm.at[idx]
            return pltpu.make_async_remote_copy(src_ref=src, dst_ref=dst, send_sem=send_sems.at[a, k],
                                                recv_sem=recv_sems.at[a, k], device_id=to, device_id_type=MESH)

        own_w = pltpu.make_async_copy(ws_hbm, shard(w_vmem, me), local_sems.at[0])
        own_p = pltpu.make_async_copy(pack_hbm, packfull_hbm.at[me], local_sems.at[1])
        own_w.start()
        own_p.start()
        sent = []
        for a in range(2):
            sent.append(copy(a, 0, me, sibling, own=True))
            sent += [copy(a, 1 + r, me, (*chip, c_), own=True) for r, chip in enumerate(chips)]
        for cp in sent:
            cp.start()

        def x_copy(t, slot):
            return pltpu.make_async_copy(x_hbm.at[pl.ds(t * tt, tt)], x_buf.at[slot], x_sems.at[slot])

        x_copy(0, 0).start()
        for t in range(n_tok):
            slot = t % 2
            if t + 1 < n_tok:
                x_copy(t + 1, 1 - slot).start()
            x_copy(t, slot).wait()

            def chunk(r0, t=t, slot=slot):
                xv = x_buf[slot, pl.ds(r0, rc), :]
                r = lax.rsqrt(jnp.mean(xv * xv, axis=-1, keepdims=True) + RMS_EPS)
                h = xv * r * g_ref[...]
                h_vmem[pl.ds(t * tt + r0, rc), :] = h.astype(BF16)
                ht_vmem[t * (tt // rc) + r0 // rc] = h.T.astype(BF16)
            _row_chunks(tt, rc, chunk)
        local = [own_p]
        for n in range(tokens // rc):
            cp = pltpu.make_async_copy(ht_vmem.at[n], ht_hbm.at[pl.ds(0, D_MODEL), pl.ds(n * rc, rc)], ht_sems.at[n])
            cp.start()
            local.append(cp)

        def project_unit(q, u):
            rows = pl.ds(pl.multiple_of(q * UNIT, LANES), UNIT)
            w_out = pltpu.make_async_copy(w_vmem.at[rows], wfull_hbm.at[rows], local_sems.at[2 + u])
            w_out.start()
            local.append(w_out)

            def o_copy(slot, t):
                return pltpu.make_async_copy(
                    o_buf.at[slot], proj_hbm.at[pl.ds(pl.multiple_of(t * tt, tt), tt), rows], o_sems.at[slot])

            def tile(t, carry):
                slot = lax.rem(t, 2)

                @pl.when(t >= 2)
                def _():
                    o_copy(slot, t).wait()
                o_buf[slot] = _dot_nt(h_vmem[pl.ds(pl.multiple_of(t * tt, tt), tt), :], w_vmem[rows, :]).astype(BF16)
                o_copy(slot, t).start()
                return carry
            lax.fori_loop(0, n_tok, tile, 0)
            o_copy(0, 0).wait()
            o_copy(1, 0).wait()

        def dev(chip, core):
            return 4 * chip[0] + 2 * chip[1] + core

        def arrive_and_pass_on(a, r):
            copy(a, 1 + r, dev(chips[r], c_), myself).wait_recv()
            passed = copy(a, 4 + r, dev(chips[r], c_), sibling)
            passed.start()
            sent.append(passed)

        def passed_on_to_me(a, r):
            copy(a, 4 + r, dev(chips[r], 1 - c_), myself).wait_recv()

        own_w.wait()
        copy(0, 0, dev((x_, y_), 1 - c_), myself).wait_recv()
        project_unit(2 * x_ + y_, 0)
        arrive_and_pass_on(0, 0)
        arrive_and_pass_on(0, 1)
        passed_on_to_me(0, 0)
        project_unit(2 * chips[0][0] + chips[0][1], 1)
        arrive_and_pass_on(0, 2)
        passed_on_to_me(0, 1)
        project_unit(2 * chips[1][0] + chips[1][1], 2)
        passed_on_to_me(0, 2)
        project_unit(2 * chips[2][0] + chips[2][1], 3)
        for r in range(3):
            arrive_and_pass_on(1, r)
        copy(1, 0, dev((x_, y_), 1 - c_), myself).wait_recv()
        for r in range(3):
            passed_on_to_me(1, r)
        for cp in sent:
            cp.wait_send()
        for cp in local:
            cp.wait()

    hbm = pl.BlockSpec(memory_space=pltpu.HBM)
    return pl.pallas_call(
        body, name="gather_project",
        in_specs=[hbm, pl.BlockSpec(memory_space=pltpu.VMEM), hbm, hbm],
        out_specs=(hbm, hbm, hbm, hbm),
        out_shape=(jax.ShapeDtypeStruct((tokens, IN_WIDTH), BF16),
                   jax.ShapeDtypeStruct((D_MODEL, tokens), BF16),
                   jax.ShapeDtypeStruct((IN_WIDTH, D_MODEL), BF16),
                   jax.ShapeDtypeStruct((N_DEV, PACK_ROWS, D_MODEL), BF16)),
        scratch_shapes=[pltpu.VMEM((IN_WIDTH, D_MODEL), BF16),
                        pltpu.VMEM((tokens, D_MODEL), BF16),
                        pltpu.VMEM((tokens // rc, D_MODEL, rc), BF16),
                        pltpu.VMEM((2, tt, D_MODEL), F32),
                        pltpu.VMEM((2, tt, UNIT), BF16),
                        pltpu.SemaphoreType.DMA((2, N_DEV - 1)),
                        pltpu.SemaphoreType.DMA((2, N_DEV - 1)),
                        pltpu.SemaphoreType.DMA((6,)),
                        pltpu.SemaphoreType.DMA((2,)),
                        pltpu.SemaphoreType.DMA((2,)),
                        pltpu.SemaphoreType.DMA((tokens // rc,))],
        compiler_params=_cparams(None, VMEM_LIMIT),
    )(x, norm_g, w_shard_t, pack)


TAIL_ROWS = 3 * SHARD_SQ
HALF_ROWS = SHARD_IN + TAIL_ROWS

GRAD_CHUNK = 384
_SECTION_ROWS = (OFF_A, OFF_B, OFF_CG, OFF_Q, OFF_KV, OFF_AG, OFF_MLC, OFF_MLA)
_SECTION_WIDTH = (1024, 1024, 1024, 1024, 512, 1024, 1024, 1024)


def _dproj_pieces(first, width):
    out = []
    for s, (start, w) in enumerate(zip(_SECTION_ROWS, _SECTION_WIDTH)):
        lo, hi = max(first, start), min(first + width, start + w)
        if lo < hi:
            out.append((s, lo - start, hi - lo, lo - first))
    return out


def _grad_exchange(sections, h_t, g_co, g_ao, g_out, g_conv, small):
    tokens = h_t.shape[1]
    n_chunk = UNIT // GRAD_CHUNK
    rc = 192

    def body(*refs):
        sec = refs[:8]
        (ht_hbm, gco_hbm, gao_hbm, gout_hbm, gconv_hbm, small_hbm, gmine_hbm, rconv_hbm, rsmall_hbm,
         lhs_buf, ht_vmem, halves, tail_buf, out_buf, stage, final,
         lhs_sems, ht_sem, tail_in_sems, d2d_send, d2d_recv, ici_send, ici_recv,
         d2d_tail_send, d2d_tail_recv, ici_tail_send, ici_tail_recv,
         tiny_send, tiny_recv, local_sems) = refs[8:]
        head_rows, tail_rows = pl.ds(0, SHARD_IN), pl.ds(SHARD_IN, TAIL_ROWS)
        x_, y_, c_, me = _mesh_pos()
        myself, sibling = (x_, y_, c_), (x_, y_, 1 - c_)
        chips = ((1 - x_, 1 - y_), (1 - x_, y_), (x_, 1 - y_), (x_, y_))
        squares = (gco_hbm, gao_hbm, gout_hbm)

        def remote(src, dst, send_sem, recv_sem, to):
            return pltpu.make_async_remote_copy(src_ref=src, dst_ref=dst, send_sem=send_sem, recv_sem=recv_sem,
                                                device_id=to, device_id_type=MESH)

        own_tiny = [pltpu.make_async_copy(gconv_hbm.at[me], rconv_hbm.at[me], local_sems.at[0]),
                    pltpu.make_async_copy(small_hbm, rsmall_hbm.at[me], local_sems.at[1])]
        for cp in own_tiny:
            cp.start()
        tiny = []
        for k in range(1, N_DEV):
            peer, peer_idx = _peer(x_, y_, c_, k)
            tiny += [remote(gconv_hbm.at[peer_idx], rconv_hbm.at[me], tiny_send.at[0, k - 1], tiny_recv.at[0, k - 1], peer),
                     remote(small_hbm, rsmall_hbm.at[me], tiny_send.at[1, k - 1], tiny_recv.at[1, k - 1], peer)]
        for cp in tiny:
            cp.start()

        ht_in = pltpu.make_async_copy(ht_hbm, ht_vmem, ht_sem.at[0])
        ht_in.start()

        def fetch(q, j, slot, wait):
            for k in range(4):
                @pl.when(q == k)
                def _(k=k):
                    for n, (s, col, width, place) in enumerate(_dproj_pieces(k * UNIT + j * GRAD_CHUNK, GRAD_CHUNK)):
                        cp = pltpu.make_async_copy(sec[s].at[pl.ds(0, tokens), pl.ds(col, width)],
                                                   lhs_buf.at[slot, pl.ds(0, tokens), pl.ds(place, width)],
                                                   lhs_sems.at[slot, n])
                        cp.wait() if wait else cp.start()

        def chip_of(u):
            return 2 * chips[u][0] + chips[u][1]

        def d2d(u):
            return remote(halves.at[1 - c_], stage.at[u, head_rows], d2d_send.at[u], d2d_recv.at[u], sibling)

        def ici(u):
            return remote(stage.at[u, head_rows], final.at[u, head_rows], ici_send.at[u], ici_recv.at[u], (*chips[u], c_))

        def d2d_tail(u):
            return remote(tail_buf.at[u, 1 - c_], stage.at[u, tail_rows], d2d_tail_send.at[u], d2d_tail_recv.at[u], sibling)

        def ici_tail(u):
            return remote(stage.at[u, tail_rows], final.at[u, tail_rows], ici_tail_send.at[u], ici_tail_recv.at[u],
                          (*chips[u], c_))

        def tail_in(u):
            out = []
            for core in range(2):
                for n, g in enumerate(squares):
                    rows = pl.ds(pl.multiple_of((2 * chip_of(u) + core) * SHARD_SQ, SHARD_SQ), SHARD_SQ)
                    out.append(pltpu.make_async_copy(g.at[rows], tail_buf.at[u, core, pl.ds(n * SHARD_SQ, SHARD_SQ)],
                                                     tail_in_sems.at[u, 3 * core + n]))
            return out

        def add_mine(u, first, count, mine):
            def chunk(r0):
                rows = pl.ds(pl.multiple_of(first + r0, 64), rc)
                stage[u, rows, :] = (stage[u, rows, :].astype(F32) + mine(pl.ds(r0, rc)).astype(F32)).astype(BF16)
            _row_chunks(count, rc, chunk)

        def chip_sum(u):
            d2d(u).wait_recv()
            add_mine(u, 0, SHARD_IN, lambda rows: halves[c_, rows, :])
            if u < 3:
                ici(u).start()

        for u in range(4):
            for cp in tail_in(u):
                cp.start()
        for u in range(4):
            for cp in tail_in(u):
                cp.wait()
            d2d_tail(u).start()
        for u in range(4):
            d2d_tail(u).wait_recv()
            add_mine(u, SHARD_IN, TAIL_ROWS, lambda rows, u=u: tail_buf[u, c_, rows, :])
            if u < 3:
                ici_tail(u).start()

        def store_rows(block, first):
            n = block.shape[0]
            for core in range(2):
                lo, hi = max(first, core * SHARD_IN), min(first + n, (core + 1) * SHARD_IN)
                if lo < hi:
                    halves[core, lo - core * SHARD_IN:hi - core * SHARD_IN, :] = block[lo - first:hi - first].astype(BF16)

        fetch(chip_of(0), 0, 0, wait=False)
        ht_in.wait()
        for u in range(4):
            q = chip_of(u)
            for j in range(n_chunk):
                slot = (u * n_chunk + j) % 2
                if j + 1 < n_chunk:
                    fetch(q, j + 1, 1 - slot, wait=False)
                elif u + 1 < 4:
                    fetch(chip_of(u + 1), 0, 1 - slot, wait=False)
                fetch(q, j, slot, wait=True)
                grad_t = _dot(ht_vmem[...], lhs_buf[slot])
                if j == 0 and u > 0:
                    chip_sum(u - 1)
                    d2d(u - 1).wait_send()
                for r in range(GRAD_CHUNK // LANES):
                    store_rows(grad_t[:, LANES * r:LANES * (r + 1)].T, j * GRAD_CHUNK + LANES * r)
            d2d(u).start()

        chip_sum(3)
        for u in range(3):
            remote(stage.at[u, head_rows], final.at[u, head_rows], ici_send.at[u], ici_recv.at[u], myself).wait_recv()
            remote(stage.at[u, tail_rows], final.at[u, tail_rows], ici_tail_send.at[u], ici_tail_recv.at[u],
                   myself).wait_recv()

        def total(r0):
            rows = pl.ds(r0, rc)
            out_buf[rows, :] = ((stage[3, rows, :].astype(F32) + final[0, rows, :].astype(F32))
                                + final[1, rows, :].astype(F32)) + final[2, rows, :].astype(F32)
        _row_chunks(HALF_ROWS, rc, total)
        out = pltpu.make_async_copy(out_buf, gmine_hbm, local_sems.at[2])
        out.start()
        d2d(3).wait_send()
        for u in range(4):
            d2d_tail(u).wait_send()
        for u in range(3):
            ici(u).wait_send()
            ici_tail(u).wait_send()
        for k in range(1, N_DEV):
            peer, peer_idx = _peer(x_, y_, c_, k)
            remote(gconv_hbm.at[me], rconv_hbm.at[peer_idx], tiny_send.at[0, k - 1], tiny_recv.at[0, k - 1], myself).wait_recv()
            remote(small_hbm, rsmall_hbm.at[peer_idx], tiny_send.at[1, k - 1], tiny_recv.at[1, k - 1], myself).wait_recv()
        for cp in tiny:
            cp.wait_send()
        for cp in own_tiny:
            cp.wait()
        out.wait()

    hbm = pl.BlockSpec(memory_space=pltpu.HBM)
    return pl.pallas_call(
        body, name="grad_exchange",
        in_specs=[hbm] * 14, out_specs=(hbm, hbm, hbm),
        out_shape=(jax.ShapeDtypeStruct((HALF_ROWS, D_MODEL), F32),
                   jax.ShapeDtypeStruct((N_DEV, CONV_PAD, LANES), F32),
                   jax.ShapeDtypeStruct((N_DEV, 8, D_MODEL), F32)),
        scratch_shapes=[pltpu.VMEM((2, tokens, GRAD_CHUNK), BF16),
                        pltpu.VMEM((D_MODEL, tokens), BF16),
                        pltpu.VMEM((2, SHARD_IN, D_MODEL), BF16),
                        pltpu.VMEM((4, 2, TAIL_ROWS, D_MODEL), BF16),
                        pltpu.VMEM((HALF_ROWS, D_MODEL), F32),
                        pltpu.VMEM((4, HALF_ROWS, D_MODEL), BF16),
                        pltpu.VMEM((3, HALF_ROWS, D_MODEL), BF16),
                        pltpu.SemaphoreType.DMA((2, 3)),
                        pltpu.SemaphoreType.DMA((1,)),
                        pltpu.SemaphoreType.DMA((4, 6)),
                        pltpu.SemaphoreType.DMA((4,)),
                        pltpu.SemaphoreType.DMA((4,)),
                        pltpu.SemaphoreType.DMA((3,)),
                        pltpu.SemaphoreType.DMA((3,)),
                        pltpu.SemaphoreType.DMA((4,)),
                        pltpu.SemaphoreType.DMA((4,)),
                        pltpu.SemaphoreType.DMA((3,)),
                        pltpu.SemaphoreType.DMA((3,)),
                        pltpu.SemaphoreType.DMA((2, N_DEV - 1)),
                        pltpu.SemaphoreType.DMA((2, N_DEV - 1)),
                        pltpu.SemaphoreType.DMA((3,))],
        compiler_params=_cparams(None, 60 * 1024 * 1024),
    )(*sections, h_t, g_co, g_ao, g_out, g_conv, small)


def _row_chunks(total, size, fn):
    n = total // size
    if n == 1:
        fn(0)
        return

    def step(i, carry):
        fn(pl.multiple_of(i * size, size))
        return carry
    lax.fori_loop(0, n, step, 0)


def _rope_tables(tokens):
    inv_freq = ROPE_THETA ** (-jnp.arange(0, HEAD_DIM, 2, dtype=F32) / HEAD_DIM)
    ang = jnp.arange(tokens, dtype=jnp.int32).astype(F32)[:, None] * inv_freq[None, :]
    cos, sin = jnp.cos(ang), jnp.sin(ang)
    zero = jnp.zeros_like(sin)
    cos_t = jnp.tile(jnp.concatenate([cos, cos], axis=1), (1, LANES // HEAD_DIM))
    sin_up = jnp.tile(jnp.concatenate([-sin, zero], axis=1), (1, LANES // HEAD_DIM))
    sin_dn = jnp.tile(jnp.concatenate([zero, sin], axis=1), (1, LANES // HEAD_DIM))
    return cos_t, sin_up, sin_dn


def _rope(t, cos_t, sin_up, sin_dn):
    return t * cos_t + pltpu.roll(t, LANES - 32, 1) * sin_up + pltpu.roll(t, 32, 1) * sin_dn


def _rope_transposed(g, cos_t, sin_up, sin_dn):
    return g * cos_t + pltpu.roll(g * sin_up, 32, 1) + pltpu.roll(g * sin_dn, LANES - 32, 1)


def _lane_halves():
    lane = lax.broadcasted_iota(jnp.int32, (BLOCK, LANES), 1)
    return lane < HEAD_DIM


def _rope_qkv(proj, cos_t, sin_up, sin_dn):
    tokens = proj.shape[0]
    tm = min(512, tokens)
    scale = HEAD_DIM ** -0.5

    def body(q_ref, kv_ref, cos_ref, up_ref, dn_ref, qr_ref, kd_ref, vd_ref):
        lo = _lane_halves()

        def chunk(r0):
            rows = pl.ds(r0, BLOCK)
            cs, up, dn = cos_ref[rows, :], up_ref[rows, :], dn_ref[rows, :]
            for p in range(D_MODEL // LANES):
                sl = slice(LANES * p, LANES * (p + 1))
                qt = q_ref[rows, sl].astype(F32)
                qr_ref[rows, sl] = (_rope(qt, cs, up, dn) * scale).astype(BF16)
            for p in range(2):
                sl = slice(LANES * p, LANES * (p + 1))
                kt = _rope(kv_ref[rows, sl].astype(F32), cs, up, dn)
                vt = kv_ref[rows, slice(256 + LANES * p, 256 + LANES * (p + 1))].astype(F32)
                for src, dst in ((kt, kd_ref), (vt, vd_ref)):
                    first = jnp.where(lo, src, 0.0)
                    second = src - first
                    dst[rows, slice(LANES * 2 * p, LANES * (2 * p + 1))] = (first + pltpu.roll(first, HEAD_DIM, 1)).astype(BF16)
                    dst[rows, slice(LANES * (2 * p + 1), LANES * (2 * p + 2))] = (second + pltpu.roll(second, HEAD_DIM, 1)).astype(BF16)
        _row_chunks(tm, BLOCK, chunk)

    tab = pl.BlockSpec((tm, LANES), lambda i: (i, 0))
    return pl.pallas_call(
        body, name="rope_qkv", grid=(tokens // tm,),
        in_specs=[pl.BlockSpec((tm, D_MODEL), lambda i: (i, COL_Q)),
                  pl.BlockSpec((tm, 512), lambda i: (i, COL512_KV)), tab, tab, tab],
        out_specs=(pl.BlockSpec((tm, D_MODEL), lambda i: (i, 0)),
                   pl.BlockSpec((tm, 512), lambda i: (i, 0)),
                   pl.BlockSpec((tm, 512), lambda i: (i, 0))),
        out_shape=(jax.ShapeDtypeStruct((tokens, D_MODEL), BF16),
                   jax.ShapeDtypeStruct((tokens, 512), BF16),
                   jax.ShapeDtypeStruct((tokens, 512), BF16)),
        compiler_params=_cparams(("parallel",)),
    )(proj, proj, cos_t, sin_up, sin_dn)


CONV_TM = 256
N_LANE_CHUNKS = D_MODEL // LANES


def _fill_u_ext(u_ext, a_ref, b_ref, ah_ref, bh_ref, first_tile):
    for lc in range(N_LANE_CHUNKS):
        sl = slice(LANES * lc, LANES * (lc + 1))
        uh = ah_ref[:, sl].astype(F32) * _sig(bh_ref[:, sl].astype(F32))
        u_ext[lc, 0:CONV_PAD, :] = jnp.where(first_tile, 0.0, uh)
        u_ext[lc, CONV_PAD:CONV_PAD + CONV_TM, :] = a_ref[:, sl].astype(F32) * _sig(b_ref[:, sl].astype(F32))


def _gather_in_steps(step, n_steps, src_hbm, dst_hbm, send_sems, recv_sems, local_sem):
    x_, y_, c_, me = _mesh_pos()
    myself, sibling = (x_, y_, c_), (x_, y_, 1 - c_)
    chips = ((1 - x_, y_), (x_, 1 - y_), (1 - x_, 1 - y_))

    def dev(chip, core):
        return 4 * chip[0] + 2 * chip[1] + core

    def copy(k, idx, to, own=False):
        return pltpu.make_async_remote_copy(src_ref=src_hbm if own else dst_hbm.at[idx], dst_ref=dst_hbm.at[idx],
                                            send_sem=send_sems.at[k], recv_sem=recv_sems.at[k],
                                            device_id=to, device_id_type=MESH)

    mine = pltpu.make_async_copy(src_hbm, dst_hbm.at[me], local_sem.at[0])

    @pl.when(step == 0)
    def _():
        mine.start()
        copy(0, me, sibling, own=True).start()
        for r, chip in enumerate(chips):
            copy(1 + r, me, (*chip, c_), own=True).start()

    @pl.when(step == n_steps // 2)
    def _():
        for r in range(3):
            copy(1 + r, dev(chips[r], c_), myself).wait_recv()
            copy(4 + r, dev(chips[r], c_), sibling).start()

    @pl.when(step == n_steps - 1)
    def _():
        copy(0, dev((x_, y_), 1 - c_), myself).wait_recv()
        for r in range(3):
            copy(4 + r, dev(chips[r], 1 - c_), myself).wait_recv()
        copy(0, me, sibling, own=True).wait_send()
        for r, chip in enumerate(chips):
            copy(1 + r, me, (*chip, c_), own=True).wait_send()
            copy(4 + r, dev(chip, c_), sibling).wait_send()
        mine.wait()


def _conv_forward(proj, conv_w, dw_b, ln_g, ln_b, w_co, pack_late):
    tokens = proj.shape[0]
    tm = CONV_TM
    halo_blocks = tm // CONV_PAD
    n_steps = tokens // tm

    def body(a_ref, b_ref, ah_ref, bh_ref, cg_ref, cw_ref, dwb_ref, lng_ref, lnb_ref, wco_ref, late_hbm,
             cv_ref, yc_ref, late_full_hbm, u_ext, cv_scr, send_sems, recv_sems, local_sem):
        _gather_in_steps(pl.program_id(0), n_steps, late_hbm, late_full_hbm, send_sems, recv_sems, local_sem)
        _fill_u_ext(u_ext, a_ref, b_ref, ah_ref, bh_ref, pl.program_id(0) == 0)

        def lane_chunk(lc, carry):
            for rc in range(tm // 64):
                acc = jnp.zeros((64, LANES), F32)
                for j in range(CONV_KERNEL):
                    acc = acc + cw_ref[lc, pl.ds(j, 1), :] * u_ext[lc, pl.ds(64 * rc + 2 + j, 64), :]
                cv_scr[lc, pl.ds(64 * rc, 64), :] = acc
            return carry
        lax.fori_loop(0, N_LANE_CHUNKS, lane_chunk, 0)

        cv = jnp.concatenate([cv_scr[lc] for lc in range(N_LANE_CHUNKS)], axis=1) + dwb_ref[...]
        cv_ref[...] = cv
        mu = jnp.mean(cv, axis=-1, keepdims=True)
        zc = cv - mu
        rstd = lax.rsqrt(jnp.mean(zc * zc, axis=-1, keepdims=True) + LN_EPS)
        ln = zc * rstd * lng_ref[...] + lnb_ref[...]
        cg = cg_ref[...].astype(F32)
        pc = (ln * _sig(ln)) * (cg * _sig(cg))
        yc_ref[...] = _dot(pc.astype(BF16), _square(wco_ref)).astype(BF16)

    def halo_map(i):
        return (jnp.maximum(i * halo_blocks - 1, 0), 0)

    tile = lambda col: pl.BlockSpec((tm, D_MODEL), lambda i: (i, col))
    return pl.pallas_call(
        body, name="conv_forward", grid=(tokens // tm,),
        in_specs=[tile(COL_A), tile(COL_B),
                  pl.BlockSpec((CONV_PAD, D_MODEL), lambda i: (halo_map(i)[0], COL_A)),
                  pl.BlockSpec((CONV_PAD, D_MODEL), lambda i: (halo_map(i)[0], COL_B)),
                  tile(COL_CG), _const_spec((N_DEV, CONV_PAD, LANES)),
                  _const_spec((1, D_MODEL)), _const_spec((1, D_MODEL)), _const_spec((1, D_MODEL)),
                  _pack_weight_spec(0), pl.BlockSpec(memory_space=pltpu.HBM)],
        out_specs=(pl.BlockSpec((tm, D_MODEL), lambda i: (i, 0)),
                   pl.BlockSpec((tm, D_MODEL), lambda i: (i, 0)),
                   pl.BlockSpec(memory_space=pltpu.HBM)),
        out_shape=(jax.ShapeDtypeStruct((tokens, D_MODEL), F32),
                   jax.ShapeDtypeStruct((tokens, D_MODEL), BF16),
                   jax.ShapeDtypeStruct((N_DEV,) + pack_late.shape, BF16)),
        scratch_shapes=[pltpu.VMEM((N_LANE_CHUNKS, CONV_PAD + tm, LANES), F32),
                        pltpu.VMEM((N_LANE_CHUNKS, tm, LANES), F32),
                        pltpu.SemaphoreType.DMA((N_DEV - 1,)),
                        pltpu.SemaphoreType.DMA((N_DEV - 1,)),
                        pltpu.SemaphoreType.DMA((1,))],
        compiler_params=_cparams(("arbitrary",), VMEM_LIMIT),
    )(proj, proj, proj, proj, proj, conv_w, dw_b, ln_g, ln_b, w_co, pack_late)


def _band_mask(n):
    row = lax.broadcasted_iota(jnp.int32, (4 * BLOCK, 2 * BLOCK), 0) & (BLOCK - 1)
    col = lax.broadcasted_iota(jnp.int32, (4 * BLOCK, 2 * BLOCK), 1)
    before = jnp.logical_and(jnp.logical_and(col < BLOCK, col > row), n > 0)
    return jnp.logical_or(before, jnp.logical_and(col >= BLOCK, col - BLOCK <= row))


def _stack_heads(tile_a, tile_b, lo):
    zero = jnp.zeros_like(tile_a)
    return jnp.concatenate([jnp.where(lo, tile_a, zero), jnp.where(lo, zero, tile_a),
                            jnp.where(lo, tile_b, zero), jnp.where(lo, zero, tile_b)], axis=0)


def _unstack_heads(stacked, lo):
    s = [stacked[BLOCK * g:BLOCK * (g + 1)] for g in range(4)]
    return (jnp.where(lo, s[0], 0.0) + jnp.where(lo, 0.0, s[1]),
            jnp.where(lo, s[2], 0.0) + jnp.where(lo, 0.0, s[3]))


def _band_scores(q_stack, k2, mask):
    return jnp.where(mask, _dot_nt(q_stack, k2), NEG)


def _sink_rows(sinks_ref, kvh):
    return jnp.concatenate([jnp.full((BLOCK, LANES), sinks_ref[0, 4 * kvh + g], F32) for g in range(4)], axis=0)


def _attention_forward(qr, kd, vd, proj, sinks, w_ao):
    tokens = qr.shape[0]
    tm = min(512, tokens)
    per_tile = tm // BLOCK

    def body(q_ref, kc_ref, kp_ref, vc_ref, vp_ref, ag0_ref, ag1_ref, sinks_ref, wao_ref, o_ref, ya_ref, lse_ref,
             k_ext, v_ext, o_scr):
        i = pl.program_id(0)
        lo = _lane_halves()
        head_lane = lax.broadcasted_iota(jnp.int32, (1, LANES), 1)
        k_ext[0:BLOCK, :], k_ext[BLOCK:BLOCK + tm, :] = kp_ref[...], kc_ref[...]
        v_ext[0:BLOCK, :], v_ext[BLOCK:BLOCK + tm, :] = vp_ref[...], vc_ref[...]

        def block(b, carry):
            r0 = pl.multiple_of(b * BLOCK, BLOCK)
            band = pl.ds(r0, 2 * BLOCK)
            mask = _band_mask(i * per_tile + b)
            lse_tile = jnp.zeros((BLOCK, LANES), F32)
            for kvh in range(N_KV_HEADS):
                ta, tb = slice(LANES * 2 * kvh, LANES * (2 * kvh + 1)), slice(LANES * (2 * kvh + 1), LANES * (2 * kvh + 2))
                ks = slice(LANES * kvh, LANES * (kvh + 1))
                q_stack = _stack_heads(q_ref[pl.ds(r0, BLOCK), ta], q_ref[pl.ds(r0, BLOCK), tb], lo)
                s = _band_scores(q_stack, k_ext[band, ks], mask)
                sink = _sink_rows(sinks_ref, kvh)
                m = jnp.max(jnp.maximum(s[:, :BLOCK], s[:, BLOCK:]), axis=1, keepdims=True)
                m = jnp.maximum(jnp.broadcast_to(m, (4 * BLOCK, LANES)), sink)
                e = jnp.concatenate([jnp.exp(s[:, :BLOCK] - m), jnp.exp(s[:, BLOCK:] - m)], axis=1).astype(BF16)
                den = _dot(e, jnp.ones((2 * BLOCK, LANES), BF16)) + jnp.exp(sink - m)
                o_stack = _dot(e, v_ext[band, ks]) / den
                o_scr[pl.ds(r0, BLOCK), ta], o_scr[pl.ds(r0, BLOCK), tb] = _unstack_heads(o_stack, lo)
                lse = m + jnp.log(den)
                for g in range(4):
                    lse_tile = lse_tile + jnp.where(head_lane == 4 * kvh + g, lse[BLOCK * g:BLOCK * (g + 1)], 0.0)
            lse_ref[pl.ds(r0, BLOCK), :] = lse_tile
            return carry
        lax.fori_loop(0, per_tile, block, 0)
        o = o_scr[...]
        o_ref[...] = o.astype(BF16)
        ag = jnp.concatenate([ag0_ref[...], ag1_ref[...]], axis=1).astype(F32)
        ya_ref[...] = _dot((o * (ag * _sig(ag))).astype(BF16), _square(wao_ref)).astype(BF16)

    cur = lambda w, col=0: pl.BlockSpec((tm, w), lambda i: (i, col))
    prev = lambda w: pl.BlockSpec((BLOCK, w), lambda i: (jnp.maximum(i * per_tile - 1, 0), 0))
    return pl.pallas_call(
        body, name="attention_forward", grid=(tokens // tm,),
        in_specs=[cur(D_MODEL), cur(512), prev(512), cur(512), prev(512),
                  cur(512, COL512_AG), cur(512, COL512_AG + 1),
                  pl.BlockSpec(memory_space=pltpu.SMEM), _pack_weight_spec(0)],
        out_specs=(cur(D_MODEL), cur(D_MODEL), cur(LANES)),
        out_shape=(jax.ShapeDtypeStruct((tokens, D_MODEL), BF16),
                   jax.ShapeDtypeStruct((tokens, D_MODEL), BF16),
                   jax.ShapeDtypeStruct((tokens, LANES), F32)),
        scratch_shapes=[pltpu.VMEM((BLOCK + tm, 512), BF16), pltpu.VMEM((BLOCK + tm, 512), BF16),
                        pltpu.VMEM((tm, D_MODEL), F32)],
        compiler_params=_cparams(("parallel",), VMEM_LIMIT),
    )(qr, kd, kd, vd, vd, proj, proj, sinks, w_ao)


def _merge_and_head(yc, ya, proj, x, target, w_out, final_g):
    tokens = x.shape[0]
    tm = min(512, tokens)
    last = tokens // tm - 1

    def body(yc_ref, ya_ref, mlc0_ref, mlc1_ref, mla0_ref, mla1_ref, x_ref, t_ref, wout_ref, fg_ref,
             dx2_ref, dyc_ref, dya_ref, dmlc_ref, dmla_ref, gwout_ref, part_ref, gacc):
        i = pl.program_id(0)

        @pl.when(i == 0)
        def _():
            gacc[...] = jnp.zeros_like(gacc)
            part_ref[...] = jnp.zeros_like(part_ref)

        yc, ya = yc_ref[...].astype(F32), ya_ref[...].astype(F32)
        gc = _sig(jnp.concatenate([mlc0_ref[...], mlc1_ref[...]], axis=1).astype(F32))
        ga = _sig(jnp.concatenate([mla0_ref[...], mla1_ref[...]], axis=1).astype(F32))
        merged = (gc * yc + ga * ya).astype(BF16)
        x2 = x_ref[...] + _dot(merged, _square(wout_ref))
        r2 = lax.rsqrt(jnp.mean(x2 * x2, axis=-1, keepdims=True) + RMS_EPS)
        x2n = x2 * r2
        fg = fg_ref[...]
        err = x2n * fg - t_ref[...]
        dy = err * (1.0 / D_MODEL)
        part_ref[0:1, :] += jnp.sum(dy * x2n, axis=0, keepdims=True)
        part_ref[1:2, :] += jnp.sum(err * err, axis=0, keepdims=True) * (0.5 / D_MODEL)
        dx2n = dy * fg
        dx2 = r2 * (dx2n - x2n * jnp.mean(dx2n * x2n, axis=-1, keepdims=True))
        dx2_ref[...] = dx2
        dx2b = dx2.astype(BF16)
        gacc[...] += _dot_tn(merged, dx2b)
        dm = _dot_nt(dx2b, _square(wout_ref))
        dyc_ref[...] = (dm * gc).astype(BF16)
        dya_ref[...] = (dm * ga).astype(BF16)
        dmlc_ref[...] = (dm * yc * (gc * (1.0 - gc))).astype(BF16)
        dmla_ref[...] = (dm * ya * (ga * (1.0 - ga))).astype(BF16)

        @pl.when(i == last)
        def _():
            gwout_ref[...] = gacc[...].astype(BF16)

    tile = lambda col=0: pl.BlockSpec((tm, D_MODEL), lambda i: (i, col))
    half = lambda col: pl.BlockSpec((tm, 512), lambda i: (i, col))
    return pl.pallas_call(
        body, name="merge_and_head", grid=(tokens // tm,),
        in_specs=[tile(), tile(), half(COL512_MLC), half(COL512_MLC + 1), half(COL512_MLA), half(COL512_MLA + 1),
                  tile(), tile(), _pack_weight_spec(1), _const_spec((1, D_MODEL))],
        out_specs=(tile(), tile(), tile(), tile(), tile(),
                   _const_spec((D_MODEL, D_MODEL)), _const_spec((8, D_MODEL))),
        out_shape=(jax.ShapeDtypeStruct((tokens, D_MODEL), F32),
                   jax.ShapeDtypeStruct((tokens, D_MODEL), BF16),
                   jax.ShapeDtypeStruct((tokens, D_MODEL), BF16),
                   jax.ShapeDtypeStruct((tokens, D_MODEL), BF16),
                   jax.ShapeDtypeStruct((tokens, D_MODEL), BF16),
                   jax.ShapeDtypeStruct((D_MODEL, D_MODEL), BF16),
                   jax.ShapeDtypeStruct((8, D_MODEL), F32)),
        scratch_shapes=[pltpu.VMEM((D_MODEL, D_MODEL), F32)],
        compiler_params=_cparams(("arbitrary",), VMEM_LIMIT),
    )(yc, ya, proj, proj, proj, proj, x, target, w_out, final_g)


def _conv_backward_pointwise(dyc, cv, proj, w_co, ln_g, ln_b):
    tokens = cv.shape[0]
    tm = min(512, tokens)
    last = tokens // tm - 1

    def body(dyc_ref, cv_ref, cg_ref, wco_ref, lng_ref, lnb_ref, dcv_ref, dcg_ref, gwco_ref, part_ref, gacc):
        i = pl.program_id(0)

        @pl.when(i == 0)
        def _():
            gacc[...] = jnp.zeros_like(gacc)
            part_ref[...] = jnp.zeros_like(part_ref)

        cv = cv_ref[...]
        mu = jnp.mean(cv, axis=-1, keepdims=True)
        zc = cv - mu
        rstd = lax.rsqrt(jnp.mean(zc * zc, axis=-1, keepdims=True) + LN_EPS)
        z = zc * rstd
        lng = lng_ref[...]
        ln = z * lng + lnb_ref[...]
        sl = _sig(ln)
        c = ln * sl
        cg = cg_ref[...].astype(F32)
        scg = _sig(cg)
        gate = cg * scg
        dyc = dyc_ref[...]
        gacc[...] += _dot_tn((c * gate).astype(BF16), dyc)
        dpc = _dot_nt(dyc, _square(wco_ref))
        dcg_ref[...] = (dpc * c * (scg * (1.0 + cg * (1.0 - scg)))).astype(BF16)
        dln = dpc * gate * (sl * (1.0 + ln * (1.0 - sl)))
        part_ref[0:1, :] += jnp.sum(dln * z, axis=0, keepdims=True)
        part_ref[1:2, :] += jnp.sum(dln, axis=0, keepdims=True)
        dz = dln * lng
        dcv = rstd * (dz - jnp.mean(dz, axis=-1, keepdims=True) - z * jnp.mean(dz * z, axis=-1, keepdims=True))
        part_ref[2:3, :] += jnp.sum(dcv, axis=0, keepdims=True)
        dcv_ref[...] = dcv

        @pl.when(i == last)
        def _():
            gwco_ref[...] = gacc[...].astype(BF16)

    tile = lambda col=0: pl.BlockSpec((tm, D_MODEL), lambda i: (i, col))
    return pl.pallas_call(
        body, name="conv_backward_pointwise", grid=(tokens // tm,),
        in_specs=[tile(), tile(), tile(COL_CG), _pack_weight_spec(0),
                  _const_spec((1, D_MODEL)), _const_spec((1, D_MODEL))],
        out_specs=(tile(), tile(), _const_spec((D_MODEL, D_MODEL)), _const_spec((8, D_MODEL))),
        out_shape=(jax.ShapeDtypeStruct((tokens, D_MODEL), F32),
                   jax.ShapeDtypeStruct((tokens, D_MODEL), BF16),
                   jax.ShapeDtypeStruct((D_MODEL, D_MODEL), BF16),
                   jax.ShapeDtypeStruct((8, D_MODEL), F32)),
        scratch_shapes=[pltpu.VMEM((D_MODEL, D_MODEL), F32)],
        compiler_params=_cparams(("arbitrary",), VMEM_LIMIT),
    )(dyc, cv, proj, w_co, ln_g, ln_b)


def _conv_backward_taps(dcv, proj, conv_w):
    tokens = dcv.shape[0]
    tm = CONV_TM
    nt = tokens // tm
    halo_blocks = tm // CONV_PAD

    def body(d_ref, dn_ref, a_ref, b_ref, ah_ref, bh_ref, cw_ref, da_ref, db_ref, gw_ref, u_ext, d_ext, du_scr, gw_acc):
        i = pl.program_id(0)

        @pl.when(i == 0)
        def _():
            gw_acc[...] = jnp.zeros_like(gw_acc)

        _fill_u_ext(u_ext, a_ref, b_ref, ah_ref, bh_ref, i == 0)
        for lc in range(N_LANE_CHUNKS):
            sl = slice(LANES * lc, LANES * (lc + 1))
            d_ext[lc, 0:tm, :] = d_ref[:, sl]
            d_ext[lc, tm:tm + CONV_PAD, :] = jnp.where(i == nt - 1, 0.0, dn_ref[:, sl])

        def lane_chunk(lc, carry):
            n_rc = tm // 64
            du = [jnp.zeros((64, LANES), F32) for _ in range(n_rc)]
            for j in range(CONV_KERNEL):
                w = cw_ref[lc, pl.ds(j, 1), :]
                gsum = jnp.zeros((8, LANES), F32)
                for rc in range(n_rc):
                    du[rc] = du[rc] + w * d_ext[lc, pl.ds(64 * rc + 30 - j, 64), :]
                    prod = d_ext[lc, pl.ds(64 * rc, 64), :] * u_ext[lc, pl.ds(64 * rc + 2 + j, 64), :]
                    gsum = gsum + jnp.sum(prod.reshape(8, 8, LANES), axis=0)
                gw_acc[lc, j] += gsum
            for rc in range(n_rc):
                du_scr[lc, pl.ds(64 * rc, 64), :] = du[rc]
            return carry
        lax.fori_loop(0, N_LANE_CHUNKS, lane_chunk, 0)

        du = jnp.concatenate([du_scr[lc] for lc in range(N_LANE_CHUNKS)], axis=1)
        a, b = a_ref[...].astype(F32), b_ref[...].astype(F32)
        sb = _sig(b)
        da_ref[...] = (du * sb).astype(BF16)
        db_ref[...] = (du * a * (sb * (1.0 - sb))).astype(BF16)

        @pl.when(i == nt - 1)
        def _():
            gw_ref[...] = jnp.sum(gw_acc[...], axis=2)

    def prev_halo(i):
        return jnp.maximum(i * halo_blocks - 1, 0)

    def next_halo(i):
        return jnp.minimum((i + 1) * halo_blocks, tokens // CONV_PAD - 1)

    tile = lambda col=0: pl.BlockSpec((tm, D_MODEL), lambda i: (i, col))
    return pl.pallas_call(
        body, name="conv_backward_taps", grid=(nt,),
        in_specs=[tile(), pl.BlockSpec((CONV_PAD, D_MODEL), lambda i: (next_halo(i), 0)),
                  tile(COL_A), tile(COL_B),
                  pl.BlockSpec((CONV_PAD, D_MODEL), lambda i: (prev_halo(i), COL_A)),
                  pl.BlockSpec((CONV_PAD, D_MODEL), lambda i: (prev_halo(i), COL_B)),
                  _const_spec((N_DEV, CONV_PAD, LANES))],
        out_specs=(tile(), tile(), _const_spec((N_DEV, CONV_PAD, LANES))),
        out_shape=(jax.ShapeDtypeStruct((tokens, D_MODEL), BF16),
                   jax.ShapeDtypeStruct((tokens, D_MODEL), BF16),
                   jax.ShapeDtypeStruct((N_DEV, CONV_PAD, LANES), F32)),
        scratch_shapes=[pltpu.VMEM((N_LANE_CHUNKS, CONV_PAD + tm, LANES), F32),
                        pltpu.VMEM((N_LANE_CHUNKS, tm + CONV_PAD, LANES), F32),
                        pltpu.VMEM((N_LANE_CHUNKS, tm, LANES), F32),
                        pltpu.VMEM((N_LANE_CHUNKS, CONV_PAD, 8, LANES), F32)],
        compiler_params=_cparams(("arbitrary",), VMEM_LIMIT),
    )(dcv, dcv, proj, proj, proj, proj, conv_w)


def _fold_kv_head(dup, lo, second_half):
    both = dup + pltpu.roll(dup, HEAD_DIM, 1)
    lo = lax.broadcasted_iota(jnp.int32, dup.shape, 1) < HEAD_DIM
    return jnp.where(lo, 0.0, both) if second_half else jnp.where(lo, both, 0.0)


def _attention_backward(dya, o, lse, qr, kd, vd, proj, sinks, w_ao, cos_t, sin_up, sin_dn):
    tokens = qr.shape[0]
    tm = min(512, tokens)
    per_tile = tm // BLOCK
    nt = tokens // tm
    scale = HEAD_DIM ** -0.5

    def body(dya_ref, o_ref, ag0_ref, ag1_ref, q_ref, kc_ref, kp_ref, vc_ref, vp_ref, sinks_ref, wao_ref, lse_ref,
             cos_c, up_c, dn_c, cos_p, up_p, dn_p,
             dq_ref, dkv_ref, dag_ref, gwao_ref, gsink_ref,
             gacc, k_ext, v_ext, dk_ext, dv_ext, dk_carry, dv_carry, do_scr, dq_scr, delta_scr):
        i = pl.program_id(0)
        lo = _lane_halves()

        @pl.when(i == 0)
        def _():
            gacc[...] = jnp.zeros_like(gacc)
            gsink_ref[...] = jnp.zeros_like(gsink_ref)
            dk_carry[...] = jnp.zeros_like(dk_carry)
            dv_carry[...] = jnp.zeros_like(dv_carry)
        dk_ext[...] = jnp.zeros_like(dk_ext)
        dv_ext[...] = jnp.zeros_like(dv_ext)

        @pl.when(i < nt)
        def _():
            dya = dya_ref[...]
            dpa = _dot_nt(dya, _square(wao_ref))
            o = o_ref[...].astype(F32)
            ag = jnp.concatenate([ag0_ref[...], ag1_ref[...]], axis=1).astype(F32)
            sg = _sig(ag)
            gate = ag * sg
            gacc[...] += _dot_tn((o * gate).astype(BF16), dya)
            dag_ref[...] = (dpa * o * (sg * (1.0 + ag * (1.0 - sg)))).astype(BF16)
            do = dpa * gate
            do_scr[...] = do.astype(BF16)
            member = jnp.where(lax.broadcasted_iota(jnp.int32, (D_MODEL, LANES), 0) >> (HEAD_DIM.bit_length() - 1)
                               == lax.broadcasted_iota(jnp.int32, (D_MODEL, LANES), 1), 1.0, 0.0).astype(BF16)
            prod = do * o
            prod_hi = prod.astype(BF16)
            prod_lo = (prod - prod_hi.astype(F32)).astype(BF16)
            delta_scr[...] = _dot(prod_hi, member) + _dot(prod_lo, member)
            k_ext[0:BLOCK, :], k_ext[BLOCK:BLOCK + tm, :] = kp_ref[...], kc_ref[...]
            v_ext[0:BLOCK, :], v_ext[BLOCK:BLOCK + tm, :] = vp_ref[...], vc_ref[...]

            def block(b, carry):
                r0 = pl.multiple_of(b * BLOCK, BLOCK)
                mine, band = pl.ds(r0, BLOCK), pl.ds(r0, 2 * BLOCK)
                lse_tile, delta_tile = lse_ref[mine, :], delta_scr[mine, :]
                mask = _band_mask(i * per_tile + b)
                head_lane = lax.broadcasted_iota(jnp.int32, (1, LANES), 1)
                gsink = jnp.zeros((1, LANES), F32)
                zero_band = jnp.zeros((2 * BLOCK, LANES), F32)
                dk_band, dv_band = [zero_band, zero_band], [zero_band, zero_band]
                for kvh in range(N_KV_HEADS):
                    ta, tb = slice(LANES * 2 * kvh, LANES * (2 * kvh + 1)), slice(LANES * (2 * kvh + 1), LANES * (2 * kvh + 2))
                    ks = slice(LANES * kvh, LANES * (kvh + 1))
                    q_stack = _stack_heads(q_ref[mine, ta], q_ref[mine, tb], lo)
                    do_stack = _stack_heads(do_scr[mine, ta], do_scr[mine, tb], lo)
                    k2, v2 = k_ext[band, ks], v_ext[band, ks]
                    s = _band_scores(q_stack, k2, mask)
                    lse = jnp.concatenate([jnp.broadcast_to(lse_tile[:, 4 * kvh + g:4 * kvh + g + 1], (BLOCK, LANES))
                                           for g in range(4)], axis=0)
                    p_p, p_c = jnp.exp(s[:, :BLOCK] - lse), jnp.exp(s[:, BLOCK:] - lse)
                    dp = _dot_nt(do_stack, v2)
                    dp_p, dp_c = dp[:, :BLOCK], dp[:, BLOCK:]
                    delta = jnp.concatenate([jnp.broadcast_to(delta_tile[:, 4 * kvh + g:4 * kvh + g + 1], (BLOCK, LANES))
                                             for g in range(4)], axis=0)
                    ds = jnp.concatenate([p_p * (dp_p - delta), p_c * (dp_c - delta)], axis=1).astype(BF16)
                    sink_terms = jnp.exp(_sink_rows(sinks_ref, kvh) - lse) * delta
                    for g in range(4):
                        total = jnp.sum(sink_terms[BLOCK * g:BLOCK * (g + 1)], axis=0, keepdims=True)
                        gsink = gsink - jnp.where(head_lane == 4 * kvh + g, total, 0.0)
                    dq_scr[mine, ta], dq_scr[mine, tb] = _unstack_heads(_dot(ds, k2), lo)
                    tile, second = kvh // 2, kvh % 2 == 1
                    dk_band[tile] = dk_band[tile] + _fold_kv_head(_dot_tn(ds, q_stack), lo, second)
                    dv_band[tile] = dv_band[tile] + _fold_kv_head(
                        _dot_tn(jnp.concatenate([p_p, p_c], axis=1).astype(BF16), do_stack), lo, second)
                gsink_ref[0:1, :] += gsink
                cs, up, dn = cos_c[mine, :], up_c[mine, :], dn_c[mine, :]
                for p in range(D_MODEL // LANES):
                    sl = slice(LANES * p, LANES * (p + 1))
                    dq_ref[mine, sl] = (_rope_transposed(dq_scr[mine, sl], cs, up, dn) * scale).astype(BF16)
                for p in range(2):
                    sl = slice(LANES * p, LANES * (p + 1))
                    dk_ext[band, sl] += dk_band[p]
                    dv_ext[band, sl] += dv_band[p]
                return carry
            lax.fori_loop(0, per_tile, block, 0)

        last = slice(tm - BLOCK, tm)
        dk_carry[last, :] += dk_ext[0:BLOCK, :]
        dv_carry[last, :] += dv_ext[0:BLOCK, :]
        for p in range(2):
            sl = slice(LANES * p, LANES * (p + 1))
            dkv_ref[:, sl] = _rope_transposed(dk_carry[:, sl], cos_p[...], up_p[...], dn_p[...]).astype(BF16)
            dkv_ref[:, slice(256 + LANES * p, 256 + LANES * (p + 1))] = dv_carry[:, sl].astype(BF16)
        dk_carry[...] = dk_ext[BLOCK:BLOCK + tm, :]
        dv_carry[...] = dv_ext[BLOCK:BLOCK + tm, :]

        @pl.when(i == nt)
        def _():
            gwao_ref[...] = gacc[...].astype(BF16)

    def cur_idx(i):
        return jnp.minimum(i, nt - 1)

    def prev_idx(i):
        return jnp.clip(i - 1, 0, nt - 1)

    cur = lambda w, col=0: pl.BlockSpec((tm, w), lambda i: (cur_idx(i), col))
    prev = lambda w: pl.BlockSpec((tm, w), lambda i: (prev_idx(i), 0))
    before = lambda w: pl.BlockSpec((BLOCK, w), lambda i: (jnp.maximum(cur_idx(i) * per_tile - 1, 0), 0))
    return pl.pallas_call(
        body, name="attention_backward", grid=(nt + 1,),
        in_specs=[cur(D_MODEL), cur(D_MODEL), cur(512, COL512_AG), cur(512, COL512_AG + 1), cur(D_MODEL),
                  cur(512), before(512), cur(512), before(512),
                  pl.BlockSpec(memory_space=pltpu.SMEM), _pack_weight_spec(0), cur(LANES),
                  cur(LANES), cur(LANES), cur(LANES), prev(LANES), prev(LANES), prev(LANES)],
        out_specs=(cur(D_MODEL), prev(512), cur(D_MODEL),
                   _const_spec((D_MODEL, D_MODEL)), _const_spec((8, LANES))),
        out_shape=(jax.ShapeDtypeStruct((tokens, D_MODEL), BF16),
                   jax.ShapeDtypeStruct((tokens, 512), BF16),
                   jax.ShapeDtypeStruct((tokens, D_MODEL), BF16),
                   jax.ShapeDtypeStruct((D_MODEL, D_MODEL), BF16),
                   jax.ShapeDtypeStruct((8, LANES), F32)),
        scratch_shapes=[pltpu.VMEM((D_MODEL, D_MODEL), F32),
                        pltpu.VMEM((BLOCK + tm, 512), BF16), pltpu.VMEM((BLOCK + tm, 512), BF16),
                        pltpu.VMEM((BLOCK + tm, 256), F32), pltpu.VMEM((BLOCK + tm, 256), F32),
                        pltpu.VMEM((tm, 256), F32), pltpu.VMEM((tm, 256), F32),
                        pltpu.VMEM((tm, D_MODEL), BF16), pltpu.VMEM((tm, D_MODEL), F32),
                        pltpu.VMEM((tm, LANES), F32)],
        compiler_params=_cparams(("arbitrary",), VMEM_LIMIT),
    )(dya, o, proj, proj, qr, kd, kd, vd, vd, sinks, w_ao, lse, cos_t, sin_up, sin_dn, cos_t, sin_up, sin_dn)


def _input_backward(sections, w_in_t, x, dx2, norm_g):
    tokens = x.shape[0]
    tm = 256

    def body(*refs):
        sec = refs[:8]
        w_ref, x_ref, dx2_ref, g_ref, gx_ref, part_ref = refs[8:]

        @pl.when(pl.program_id(0) == 0)
        def _():
            part_ref[...] = jnp.zeros_like(part_ref)

        dh = jnp.zeros((tm, D_MODEL), F32)
        for s in range(8):
            dh = dh + _dot(sec[s][...], w_ref[_SECTION_ROWS[s]:_SECTION_ROWS[s] + _SECTION_WIDTH[s], :])
        xv = x_ref[...]
        r = lax.rsqrt(jnp.mean(xv * xv, axis=-1, keepdims=True) + RMS_EPS)
        xn = xv * r
        part_ref[0:1, :] += jnp.sum(dh * xn, axis=0, keepdims=True)
        dxn = dh * g_ref[...]
        gx_ref[...] = dx2_ref[...] + r * (dxn - xn * jnp.mean(dxn * xn, axis=-1, keepdims=True))

    tile = lambda w=D_MODEL: pl.BlockSpec((tm, w), lambda i: (i, 0))
    return pl.pallas_call(
        body, name="input_backward", grid=(tokens // tm,),
        in_specs=[tile(w) for w in _SECTION_WIDTH] + [
            pl.BlockSpec((IN_WIDTH, D_MODEL), lambda i: (0, 0), pipeline_mode=pl.Buffered(1)),
            tile(), tile(), _const_spec((1, D_MODEL))],
        out_specs=(tile(), _const_spec((8, D_MODEL))),
        out_shape=(jax.ShapeDtypeStruct((tokens, D_MODEL), F32),
                   jax.ShapeDtypeStruct((8, D_MODEL), F32)),
        compiler_params=_cparams(("arbitrary",), VMEM_LIMIT),
    )(*sections, w_in_t, x, dx2, norm_g)


def _adamw_math(w, g, m, v):
    m = ADAM_B1 * m + (1.0 - ADAM_B1) * g
    v = ADAM_B2 * v + (1.0 - ADAM_B2) * (g * g)
    m_hat = m / (1.0 - ADAM_B1 ** ADAM_STEP)
    v_hat = v / (1.0 - ADAM_B2 ** ADAM_STEP)
    delta = -ADAM_LR * (m_hat / (jnp.sqrt(v_hat) + ADAM_EPS) + ADAM_WD * w)
    return delta, m, v


def _sum_slots(recv_ref):
    total = recv_ref[0].astype(F32)
    for d in range(1, N_DEV):
        total = total + recv_ref[d].astype(F32)
    return total


def _adamw(name, w, g, m, v, tile_rows):
    rows, cols = w.shape

    def body(w_ref, g_ref, m_ref, v_ref, d_ref, nm_ref, nv_ref):
        d_ref[...], nm_ref[...], nv_ref[...] = _adamw_math(w_ref[...], g_ref[...], m_ref[...], v_ref[...])

    spec = pl.BlockSpec((tile_rows, cols), lambda i: (i, 0))
    shape = jax.ShapeDtypeStruct((rows, cols), F32)
    return pl.pallas_call(
        body, name=name, grid=(rows // tile_rows,),
        in_specs=[spec] * 4, out_specs=(spec,) * 3, out_shape=(shape,) * 3,
        compiler_params=_cparams(("parallel",)),
    )(w, g, m, v)


def _sum_adamw(name, recv, w, m, v):
    def body(recv_ref, w_ref, m_ref, v_ref, g_ref, d_ref, nm_ref, nv_ref):
        g = _sum_slots(recv_ref)
        g_ref[...] = g
        d_ref[...], nm_ref[...], nv_ref[...] = _adamw_math(w_ref[...], g, m_ref[...], v_ref[...])

    shape = jax.ShapeDtypeStruct(w.shape, F32)
    return pl.pallas_call(body, name=name, out_shape=(shape,) * 4)(recv, w, m, v)


def _pad_rows(a, rows):
    return jnp.concatenate([a, jnp.zeros((rows - a.shape[0],) + a.shape[1:], a.dtype)], axis=0)


def kernel(x, norm_g, w_in, conv_dw_w, conv_dw_b, conv_ln_g, conv_ln_b, w_conv_out, attn_sinks, w_attn_out, w_out, final_norm_g, loss_target, m_norm_g, m_w_in, m_conv_dw_w, m_conv_dw_b, m_conv_ln_g, m_conv_ln_b, m_w_conv_out, m_attn_sinks, m_w_attn_out, m_w_out, m_final_norm_g, v_norm_g, v_w_in, v_conv_dw_w, v_conv_dw_b, v_conv_ln_g, v_conv_ln_b, v_w_conv_out, v_attn_sinks, v_w_attn_out, v_w_out, v_final_norm_g):
    xs, target = x[0], loss_target[0]
    tokens = xs.shape[0]
    fg_row = final_norm_g.reshape(1, D_MODEL)

    taps_bits = lax.bitcast_convert_type(_pad_rows(conv_dw_w[0], CONV_PAD), BF16).reshape(8, D_MODEL)
    pack = jnp.concatenate([w_conv_out[0].astype(BF16),
                            jnp.pad(taps_bits, ((0, PACK_ROWS - SHARD_SQ - 8), (0, 0)))], axis=0)
    pack_late = jnp.concatenate([w_attn_out[0].astype(BF16), w_out[0].astype(BF16)], axis=0)
    w_in_t32 = w_in[0].T
    proj, h_t, w_in_t, pack_full = _gather_project(xs, norm_g, w_in_t32.astype(BF16), pack)
    w_co = pack_full
    conv_w = lax.bitcast_convert_type(
        pack_full[:, SHARD_SQ:SHARD_SQ + 8].reshape(N_DEV, CONV_PAD, LANES, 2), F32)

    cos_t, sin_up, sin_dn = _rope_tables(tokens)
    qr, kd, vd = _rope_qkv(proj, cos_t, sin_up, sin_dn)
    cv, yc, late_full = _conv_forward(proj, conv_w, conv_dw_b, conv_ln_g, conv_ln_b, w_co, pack_late)
    w_ao = w_o = late_full
    o, ya, lse = _attention_forward(qr, kd, vd, proj, attn_sinks, w_ao)

    dx2, dyc, dya, dmlc, dmla, g_out, part_head = _merge_and_head(yc, ya, proj, xs, target, w_o, fg_row)
    dcv, dcg, g_co, part_conv = _conv_backward_pointwise(dyc, cv, proj, w_co, conv_ln_g, conv_ln_b)
    da, db, g_conv = _conv_backward_taps(dcv, proj, conv_w)
    dq, dkv, dag, g_ao, part_sink = _attention_backward(dya, o, lse, qr, kd, vd, proj, attn_sinks, w_ao, cos_t, sin_up, sin_dn)
    sections = (da, db, dcg, dq, dkv, dag, dmlc, dmla)
    grad_x, part_in = _input_backward(sections, w_in_t, xs, dx2, norm_g)

    small = jnp.concatenate([
        part_in[0:1], part_conv[2:3], part_conv[0:1], part_conv[1:2], part_head[0:1],
        jnp.pad(part_sink[0:1], ((0, 0), (0, D_MODEL - LANES))), part_head[1:2],
        jnp.zeros((1, D_MODEL), F32)], axis=0)

    g_mine, r_conv, r_small = _grad_exchange(sections, h_t, g_co, g_ao, g_out, g_conv, small)

    g_in_t = g_mine[:SHARD_IN]
    w_in_res = _adamw("adamw_w_in", w_in_t32, g_in_t, m_w_in[0].T, v_w_in[0].T, 192)
    grad_w_in, d_w_in, nm_w_in, nv_w_in = (a.T for a in (g_in_t,) + tuple(w_in_res))
    sq = {}
    for j, (nm, w, m, v) in enumerate((("w_conv_out", w_conv_out, m_w_conv_out, v_w_conv_out),
                                       ("w_attn_out", w_attn_out, m_w_attn_out, v_w_attn_out),
                                       ("w_out", w_out, m_w_out, v_w_out))):
        g = g_mine[SHARD_IN + j * SHARD_SQ:SHARD_IN + (j + 1) * SHARD_SQ]
        sq[nm] = (g,) + tuple(_adamw("adamw_" + nm, w[0], g, m[0], v[0], SHARD_SQ))
    conv_res = _sum_adamw("sum_adamw_conv_dw_w", r_conv.reshape(N_DEV, CONV_PAD, LANES),
                          _pad_rows(conv_dw_w[0], CONV_PAD), _pad_rows(m_conv_dw_w[0], CONV_PAD),
                          _pad_rows(v_conv_dw_w[0], CONV_PAD))
    pad_sink = lambda a: jnp.pad(a, ((0, 0), (0, D_MODEL - N_Q_HEADS)))
    zero_rows = jnp.zeros((2, D_MODEL), F32)
    stack = lambda a, b, c, d, e, f: jnp.concatenate([a, b, c, d, e.reshape(1, D_MODEL), pad_sink(f), zero_rows], axis=0)
    small_res = _sum_adamw(
        "sum_adamw_small", r_small,
        stack(norm_g, conv_dw_b, conv_ln_g, conv_ln_b, final_norm_g, attn_sinks),
        stack(m_norm_g, m_conv_dw_b, m_conv_ln_g, m_conv_ln_b, m_final_norm_g, m_attn_sinks),
        stack(v_norm_g, v_conv_dw_b, v_conv_ln_g, v_conv_ln_b, v_final_norm_g, v_attn_sinks))
    loss = jnp.sum(small_res[0][6])

    def leaf(k):
        s = small_res[k]
        return (s[0:1], (grad_w_in, d_w_in, nm_w_in, nv_w_in)[k][None], conv_res[k][None, :CONV_KERNEL],
                s[1:2], s[2:3], s[3:4], sq["w_conv_out"][k][None], s[5:6, :N_Q_HEADS],
                sq["w_attn_out"][k][None], sq["w_out"][k][None], s[4])

    return (loss, grad_x[None], *leaf(0), *leaf(1), *leaf(2), *leaf(3))
```

```python
import jax
import jax.numpy as jnp
from jax import lax
from jax.experimental import pallas as pl
from jax.experimental.pallas import tpu as pltpu

F32 = jnp.float32
BF16 = jnp.bfloat16
MESH = pl.DeviceIdType.MESH

D_MODEL = 1024
IN_WIDTH = 7680
N_DEV = 8
SHARD_IN = IN_WIDTH // N_DEV
SHARD_SQ = D_MODEL // N_DEV
CONV_KERNEL = 31
CONV_PAD = 32
HEAD_DIM = 64
N_Q_HEADS = 16
N_KV_HEADS = 4
BLOCK = 128
LANES = 128
ROPE_THETA = 10000.0
RMS_EPS = 1e-5
LN_EPS = 1e-5
NEG = -1e30
ADAM_LR = 0.001
ADAM_B1 = 0.9
ADAM_B2 = 0.999
ADAM_EPS = 1e-08
ADAM_WD = 0.01
ADAM_STEP = 10

OFF_A, OFF_B, OFF_CG, OFF_Q, OFF_KV, OFF_AG, OFF_MLC, OFF_MLA = 0, 1024, 2048, 3072, 4096, 4608, 5632, 6656
COL_A, COL_B, COL_CG, COL_Q = 0, 1, 2, 3
COL512_KV, COL512_AG, COL512_MLC, COL512_MLA = 8, 9, 11, 13
UNIT = 2 * SHARD_IN
PACK_ROWS = 400

VMEM_LIMIT = 56 * 1024 * 1024


def _cparams(sem=None, vmem=None):
    return pltpu.CompilerParams(dimension_semantics=sem, vmem_limit_bytes=vmem)


def _sig(v):
    return 0.5 * jnp.tanh(0.5 * v) + 0.5


def _dot(a, b):
    return jnp.dot(a, b, preferred_element_type=F32)


def _dot_nt(a, b):
    return lax.dot_general(a, b, (((1,), (1,)), ((), ())), preferred_element_type=F32)


def _dot_tn(a, b):
    return lax.dot_general(a, b, (((0,), (0,)), ((), ())), preferred_element_type=F32)


def _const_spec(shape):
    nd = len(shape)
    return pl.BlockSpec(shape, lambda *_: (0,) * nd)


def _pack_weight_spec(j):
    return pl.BlockSpec((N_DEV, SHARD_SQ, D_MODEL), lambda *_: (0, j, 0))


def _square(w_ref):
    return w_ref[...].reshape(D_MODEL, D_MODEL)


def _mesh_pos():
    x, y, c = lax.axis_index("x"), lax.axis_index("y"), lax.axis_index("c")
    return x, y, c, 4 * x + 2 * y + c


def _peer(x, y, c, k):
    px = 1 - x if (k >> 2) & 1 else x
    py = 1 - y if (k >> 1) & 1 else y
    pc = 1 - c if k & 1 else c
    return (px, py, pc), 4 * px + 2 * py + pc


def _gather_project(x, norm_g, w_shard_t, pack):
    tokens = x.shape[0]
    tt = min(512, tokens // 2)
    n_tok = tokens // tt
    rc = min(128, tt)

    def body(x_hbm, g_ref, ws_hbm, pack_hbm, proj_hbm, ht_hbm, wfull_hbm, packfull_hbm,
             w_vmem, h_vmem, ht_vmem, x_buf, o_buf, send_sems, recv_sems, local_sems, x_sems, o_sems, ht_sems):
        x_, y_, c_, me = _mesh_pos()
        myself, sibling = (x_, y_, c_), (x_, y_, 1 - c_)
        chips = ((1 - x_, y_), (x_, 1 - y_), (1 - x_, 1 - y_))

        def shard(ref, idx):
            return ref.at[pl.ds(pl.multiple_of(idx * SHARD_IN, 64), SHARD_IN)]

        def copy(a, k, idx, to, own=False):
            if a == 0:
                src, dst = ws_hbm if own else shard(w_vmem, idx), shard(w_vmem, idx)
            else:
                src, dst = pack_hbm if own else packfull_hbm.at[idx], packfull_hbm.at[idx]
            return pltpu.make_async_remote_copy(src_ref=src, dst_ref=dst, send_sem=send_sems.at[a, k],
                                                recv_sem=recv_sems.at[a, k], device_id=to, device_id_type=MESH)

        own_w = pltpu.make_async_copy(ws_hbm, shard(w_vmem, me), local_sems.at[0])
        own_p = pltpu.make_async_copy(pack_hbm, packfull_hbm.at[me], local_sems.at[1])
        own_w.start()
        own_p.start()
        sent = []
        for a in range(2):
            sent.append(copy(a, 0, me, sibling, own=True))
            sent += [copy(a, 1 + r, me, (*chip, c_), own=True) for r, chip in enumerate(chips)]
        for cp in sent:
            cp.start()

        def x_copy(t, slot):
            return pltpu.make_async_copy(x_hbm.at[pl.ds(t * tt, tt)], x_buf.at[slot], x_sems.at[slot])

        x_copy(0, 0).start()
        for t in range(n_tok):
            slot = t % 2
            if t + 1 < n_tok:
                x_copy(t + 1, 1 - slot).start()
            x_copy(t, slot).wait()

            def chunk(r0, t=t, slot=slot):
                xv = x_buf[slot, pl.ds(r0, rc), :]
                r = lax.rsqrt(jnp.mean(xv * xv, axis=-1, keepdims=True) + RMS_EPS)
                h = xv * r * g_ref[...]
                h_vmem[pl.ds(t * tt + r0, rc), :] = h.astype(BF16)
                ht_vmem[t * (tt // rc) + r0 // rc] = h.T.astype(BF16)
            _row_chunks(tt, rc, chunk)
        local = [own_p]
        for n in range(tokens // rc):
            cp = pltpu.make_async_copy(ht_vmem.at[n], ht_hbm.at[pl.ds(0, D_MODEL), pl.ds(n * rc, rc)], ht_sems.at[n])
            cp.start()
            local.append(cp)

        def project_unit(q, u):
            rows = pl.ds(pl.multiple_of(q * UNIT, LANES), UNIT)
            w_out = pltpu.make_async_copy(w_vmem.at[rows], wfull_hbm.at[rows], local_sems.at[2 + u])
            w_out.start()
            local.append(w_out)

            def o_copy(slot, t):
                return pltpu.make_async_copy(
                    o_buf.at[slot], proj_hbm.at[pl.ds(pl.multiple_of(t * tt, tt), tt), rows], o_sems.at[slot])

            def tile(t, carry):
                slot = lax.rem(t, 2)

                @pl.when(t >= 2)
                def _():
                    o_copy(slot, t).wait()
                o_buf[slot] = _dot_nt(h_vmem[pl.ds(pl.multiple_of(t * tt, tt), tt), :], w_vmem[rows, :]).astype(BF16)
                o_copy(slot, t).start()
                return carry
            lax.fori_loop(0, n_tok, tile, 0)
            o_copy(0, 0).wait()
            o_copy(1, 0).wait()

        def dev(chip, core):
            return 4 * chip[0] + 2 * chip[1] + core

        def arrive_and_pass_on(a, r):
            copy(a, 1 + r, dev(chips[r], c_), myself).wait_recv()
            passed = copy(a, 4 + r, dev(chips[r], c_), sibling)
            passed.start()
            sent.append(passed)

        def passed_on_to_me(a, r):
            copy(a, 4 + r, dev(chips[r], 1 - c_), myself).wait_recv()

        own_w.wait()
        copy(0, 0, dev((x_, y_), 1 - c_), myself).wait_recv()
        project_unit(2 * x_ + y_, 0)
        arrive_and_pass_on(0, 0)
        arrive_and_pass_on(0, 1)
        passed_on_to_me(0, 0)
        project_unit(2 * chips[0][0] + chips[0][1], 1)
        arrive_and_pass_on(0, 2)
        passed_on_to_me(0, 1)
        project_unit(2 * chips[1][0] + chips[1][1], 2)
        passed_on_to_me(0, 2)
        project_unit(2 * chips[2][0] + chips[2][1], 3)
        for r in range(3):
            arrive_and_pass_on(1, r)
        copy(1, 0, dev((x_, y_), 1 - c_), myself).wait_recv()
        for r in range(3):
            passed_on_to_me(1, r)
        for cp in sent:
            cp.wait_send()
        for cp in local:
            cp.wait()

    hbm = pl.BlockSpec(memory_space=pltpu.HBM)
    return pl.pallas_call(
        body, name="gather_project",
        in_specs=[hbm, pl.BlockSpec(memory_space=pltpu.VMEM), hbm, hbm],
        out_specs=(hbm, hbm, hbm, hbm),
        out_shape=(jax.ShapeDtypeStruct((tokens, IN_WIDTH), BF16),
                   jax.ShapeDtypeStruct((D_MODEL, tokens), BF16),
                   jax.ShapeDtypeStruct((IN_WIDTH, D_MODEL), BF16),
                   jax.ShapeDtypeStruct((N_DEV, PACK_ROWS, D_MODEL), BF16)),
        scratch_shapes=[pltpu.VMEM((IN_WIDTH, D_MODEL), BF16),
                        pltpu.VMEM((tokens, D_MODEL), BF16),
                        pltpu.VMEM((tokens // rc, D_MODEL, rc), BF16),
                        pltpu.VMEM((2, tt, D_MODEL), F32),
                        pltpu.VMEM((2, tt, UNIT), BF16),
                        pltpu.SemaphoreType.DMA((2, N_DEV - 1)),
                        pltpu.SemaphoreType.DMA((2, N_DEV - 1)),
                        pltpu.SemaphoreType.DMA((6,)),
                        pltpu.SemaphoreType.DMA((2,)),
                        pltpu.SemaphoreType.DMA((2,)),
                        pltpu.SemaphoreType.DMA((tokens // rc,))],
        compiler_params=_cparams(None, VMEM_LIMIT),
    )(x, norm_g, w_shard_t, pack)


TAIL_ROWS = 3 * SHARD_SQ
HALF_ROWS = SHARD_IN + TAIL_ROWS

GRAD_CHUNK = 384
_SECTION_ROWS = (OFF_A, OFF_B, OFF_CG, OFF_Q, OFF_KV, OFF_AG, OFF_MLC, OFF_MLA)
_SECTION_WIDTH = (1024, 1024, 1024, 1024, 512, 1024, 1024, 1024)


def _dproj_pieces(first, width):
    out = []
    for s, (start, w) in enumerate(zip(_SECTION_ROWS, _SECTION_WIDTH)):
        lo, hi = max(first, start), min(first + width, start + w)
        if lo < hi:
            out.append((s, lo - start, hi - lo, lo - first))
    return out


def _grad_exchange(sections, h_t, g_co, g_ao, g_out, g_conv, small):
    tokens = h_t.shape[1]
    n_chunk = UNIT // GRAD_CHUNK
    rc = 192

    def body(*refs):
        sec = refs[:8]
        (ht_hbm, gco_hbm, gao_hbm, gout_hbm, gconv_hbm, small_hbm, gmine_hbm, rconv_hbm, rsmall_hbm,
         lhs_buf, ht_vmem, halves, tail_buf, out_buf, stage, final,
         lhs_sems, ht_sem, tail_in_sems, d2d_send, d2d_recv, ici_send, ici_recv,
         d2d_tail_send, d2d_tail_recv, ici_tail_send, ici_tail_recv,
         tiny_send, tiny_recv, local_sems) = refs[8:]
        head_rows, tail_rows = pl.ds(0, SHARD_IN), pl.ds(SHARD_IN, TAIL_ROWS)
        x_, y_, c_, me = _mesh_pos()
        myself, sibling = (x_, y_, c_), (x_, y_, 1 - c_)
        chips = ((1 - x_, 1 - y_), (1 - x_, y_), (x_, 1 - y_), (x_, y_))
        squares = (gco_hbm, gao_hbm, gout_hbm)

        def remote(src, dst, send_sem, recv_sem, to):
            return pltpu.make_async_remote_copy(src_ref=src, dst_ref=dst, send_sem=send_sem, recv_sem=recv_sem,
                                                device_id=to, device_id_type=MESH)

        own_tiny = [pltpu.make_async_copy(gconv_hbm.at[me], rconv_hbm.at[me], local_sems.at[0]),
                    pltpu.make_async_copy(small_hbm, rsmall_hbm.at[me], local_sems.at[1])]
        for cp in own_tiny:
            cp.start()
        tiny = []
        for k in range(1, N_DEV):
            peer, peer_idx = _peer(x_, y_, c_, k)
            tiny += [remote(gconv_hbm.at[peer_idx], rconv_hbm.at[me], tiny_send.at[0, k - 1], tiny_recv.at[0, k - 1], peer),
                     remote(small_hbm, rsmall_hbm.at[me], tiny_send.at[1, k - 1], tiny_recv.at[1, k - 1], peer)]
        for cp in tiny:
            cp.start()

        ht_in = pltpu.make_async_copy(ht_hbm, ht_vmem, ht_sem.at[0])
        ht_in.start()

        def fetch(q, j, slot, wait):
            for k in range(4):
                @pl.when(q == k)
                def _(k=k):
                    for n, (s, col, width, place) in enumerate(_dproj_pieces(k * UNIT + j * GRAD_CHUNK, GRAD_CHUNK)):
                        cp = pltpu.make_async_copy(sec[s].at[pl.ds(0, tokens), pl.ds(col, width)],
                                                   lhs_buf.at[slot, pl.ds(0, tokens), pl.ds(place, width)],
                                                   lhs_sems.at[slot, n])
                        cp.wait() if wait else cp.start()

        def chip_of(u):
            return 2 * chips[u][0] + chips[u][1]

        def d2d(u):
            return remote(halves.at[1 - c_], stage.at[u, head_rows], d2d_send.at[u], d2d_recv.at[u], sibling)

        def ici(u):
            return remote(stage.at[u, head_rows], final.at[u, head_rows], ici_send.at[u], ici_recv.at[u], (*chips[u], c_))

        def d2d_tail(u):
            return remote(tail_buf.at[u, 1 - c_], stage.at[u, tail_rows], d2d_tail_send.at[u], d2d_tail_recv.at[u], sibling)

        def ici_tail(u):
            return remote(stage.at[u, tail_rows], final.at[u, tail_rows], ici_tail_send.at[u], ici_tail_recv.at[u],
                          (*chips[u], c_))

        def tail_in(u):
            out = []
            for core in range(2):
                for n, g in enumerate(squares):
                    rows = pl.ds(pl.multiple_of((2 * chip_of(u) + core) * SHARD_SQ, SHARD_SQ), SHARD_SQ)
                    out.append(pltpu.make_async_copy(g.at[rows], tail_buf.at[u, core, pl.ds(n * SHARD_SQ, SHARD_SQ)],
                                                     tail_in_sems.at[u, 3 * core + n]))
            return out

        def add_mine(u, first, count, mine):
            def chunk(r0):
                rows = pl.ds(pl.multiple_of(first + r0, 64), rc)
                stage[u, rows, :] = (stage[u, rows, :].astype(F32) + mine(pl.ds(r0, rc)).astype(F32)).astype(BF16)
            _row_chunks(count, rc, chunk)

        def chip_sum(u):
            d2d(u).wait_recv()
            add_mine(u, 0, SHARD_IN, lambda rows: halves[c_, rows, :])
            if u < 3:
                ici(u).start()

        for u in range(4):
            for cp in tail_in(u):
                cp.start()
        for u in range(4):
            for cp in tail_in(u):
                cp.wait()
            d2d_tail(u).start()
        for u in range(4):
            d2d_tail(u).wait_recv()
            add_mine(u, SHARD_IN, TAIL_ROWS, lambda rows, u=u: tail_buf[u, c_, rows, :])
            if u < 3:
                ici_tail(u).start()

        def store_rows(block, first):
            n = block.shape[0]
            for core in range(2):
                lo, hi = max(first, core * SHARD_IN), min(first + n, (core + 1) * SHARD_IN)
                if lo < hi:
                    halves[core, lo - core * SHARD_IN:hi - core * SHARD_IN, :] = block[lo - first:hi - first].astype(BF16)

        fetch(chip_of(0), 0, 0, wait=False)
        ht_in.wait()
        for u in range(4):
            q = chip_of(u)
            for j in range(n_chunk):
                slot = (u * n_chunk + j) % 2
                if j + 1 < n_chunk:
                    fetch(q, j + 1, 1 - slot, wait=False)
                elif u + 1 < 4:
                    fetch(chip_of(u + 1), 0, 1 - slot, wait=False)
                fetch(q, j, slot, wait=True)
                grad_t = _dot(ht_vmem[...], lhs_buf[slot])
                if j == 0 and u > 0:
                    chip_sum(u - 1)
                    d2d(u - 1).wait_send()
                for r in range(GRAD_CHUNK // LANES):
                    store_rows(grad_t[:, LANES * r:LANES * (r + 1)].T, j * GRAD_CHUNK + LANES * r)
            d2d(u).start()

        chip_sum(3)
        for u in range(3):
            remote(stage.at[u, head_rows], final.at[u, head_rows], ici_send.at[u], ici_recv.at[u], myself).wait_recv()
            remote(stage.at[u, tail_rows], final.at[u, tail_rows], ici_tail_send.at[u], ici_tail_recv.at[u],
                   myself).wait_recv()

        def total(r0):
            rows = pl.ds(r0, rc)
            out_buf[rows, :] = ((stage[3, rows, :].astype(F32) + final[0, rows, :].astype(F32))
                                + final[1, rows, :].astype(F32)) + final[2, rows, :].astype(F32)
        _row_chunks(HALF_ROWS, rc, total)
        out = pltpu.make_async_copy(out_buf, gmine_hbm, local_sems.at[2])
        out.start()
        d2d(3).wait_send()
        for u in range(4):
            d2d_tail(u).wait_send()
        for u in range(3):
            ici(u).wait_send()
            ici_tail(u).wait_send()
        for k in range(1, N_DEV):
            peer, peer_idx = _peer(x_, y_, c_, k)
            remote(gconv_hbm.at[me], rconv_hbm.at[peer_idx], tiny_send.at[0, k - 1], tiny_recv.at[0, k - 1], myself).wait_recv()
            remote(small_hbm, rsmall_hbm.at[peer_idx], tiny_send.at[1, k - 1], tiny_recv.at[1, k - 1], myself).wait_recv()
        for cp in tiny:
            cp.wait_send()
        for cp in own_tiny:
            cp.wait()
        out.wait()

    hbm = pl.BlockSpec(memory_space=pltpu.HBM)
    return pl.pallas_call(
        body, name="grad_exchange",
        in_specs=[hbm] * 14, out_specs=(hbm, hbm, hbm),
        out_shape=(jax.ShapeDtypeStruct((HALF_ROWS, D_MODEL), F32),
                   jax.ShapeDtypeStruct((N_DEV, CONV_PAD, LANES), F32),
                   jax.ShapeDtypeStruct((N_DEV, 8, D_MODEL), F32)),
        scratch_shapes=[pltpu.VMEM((2, tokens, GRAD_CHUNK), BF16),
                        pltpu.VMEM((D_MODEL, tokens), BF16),
                        pltpu.VMEM((2, SHARD_IN, D_MODEL), BF16),
                        pltpu.VMEM((4, 2, TAIL_ROWS, D_MODEL), BF16),
                        pltpu.VMEM((HALF_ROWS, D_MODEL), F32),
                        pltpu.VMEM((4, HALF_ROWS, D_MODEL), BF16),
                        pltpu.VMEM((3, HALF_ROWS, D_MODEL), BF16),
                        pltpu.SemaphoreType.DMA((2, 3)),
                        pltpu.SemaphoreType.DMA((1,)),
                        pltpu.SemaphoreType.DMA((4, 6)),
                        pltpu.SemaphoreType.DMA((4,)),
                        pltpu.SemaphoreType.DMA((4,)),
                        pltpu.SemaphoreType.DMA((3,)),
                        pltpu.SemaphoreType.DMA((3,)),
                        pltpu.SemaphoreType.DMA((4,)),
                        pltpu.SemaphoreType.DMA((4,)),
                        pltpu.SemaphoreType.DMA((3,)),
                        pltpu.SemaphoreType.DMA((3,)),
                        pltpu.SemaphoreType.DMA((2, N_DEV - 1)),
                        pltpu.SemaphoreType.DMA((2, N_DEV - 1)),
                        pltpu.SemaphoreType.DMA((3,))],
        compiler_params=_cparams(None, 60 * 1024 * 1024),
    )(*sections, h_t, g_co, g_ao, g_out, g_conv, small)


def _row_chunks(total, size, fn):
    n = total // size
    if n == 1:
        fn(0)
        return

    def step(i, carry):
        fn(pl.multiple_of(i * size, size))
        return carry
    lax.fori_loop(0, n, step, 0)


def _rope_tables(tokens):
    inv_freq = ROPE_THETA ** (-jnp.arange(0, HEAD_DIM, 2, dtype=F32) / HEAD_DIM)
    ang = jnp.arange(tokens, dtype=jnp.int32).astype(F32)[:, None] * inv_freq[None, :]
    cos, sin = jnp.cos(ang), jnp.sin(ang)
    zero = jnp.zeros_like(sin)
    cos_t = jnp.tile(jnp.concatenate([cos, cos], axis=1), (1, LANES // HEAD_DIM))
    sin_up = jnp.tile(jnp.concatenate([-sin, zero], axis=1), (1, LANES // HEAD_DIM))
    sin_dn = jnp.tile(jnp.concatenate([zero, sin], axis=1), (1, LANES // HEAD_DIM))
    return cos_t, sin_up, sin_dn


def _rope(t, cos_t, sin_up, sin_dn):
    return t * cos_t + pltpu.roll(t, LANES - 32, 1) * sin_up + pltpu.roll(t, 32, 1) * sin_dn


def _rope_transposed(g, cos_t, sin_up, sin_dn):
    return g * cos_t + pltpu.roll(g * sin_up, 32, 1) + pltpu.roll(g * sin_dn, LANES - 32, 1)


def _lane_halves():
    lane = lax.broadcasted_iota(jnp.int32, (BLOCK, LANES), 1)
    return lane < HEAD_DIM


def _rope_qkv(proj, cos_t, sin_up, sin_dn):
    tokens = proj.shape[0]
    tm = min(512, tokens)
    scale = HEAD_DIM ** -0.5

    def body(q_ref, kv_ref, cos_ref, up_ref, dn_ref, qr_ref, kd_ref, vd_ref):
        lo = _lane_halves()

        def chunk(r0):
            rows = pl.ds(r0, BLOCK)
            cs, up, dn = cos_ref[rows, :], up_ref[rows, :], dn_ref[rows, :]
            for p in range(D_MODEL // LANES):
                sl = slice(LANES * p, LANES * (p + 1))
                qt = q_ref[rows, sl].astype(F32)
                qr_ref[rows, sl] = (_rope(qt, cs, up, dn) * scale).astype(BF16)
            for p in range(2):
                sl = slice(LANES * p, LANES * (p + 1))
                kt = _rope(kv_ref[rows, sl].astype(F32), cs, up, dn)
                vt = kv_ref[rows, slice(256 + LANES * p, 256 + LANES * (p + 1))].astype(F32)
                for src, dst in ((kt, kd_ref), (vt, vd_ref)):
                    first = jnp.where(lo, src, 0.0)
                    second = src - first
                    dst[rows, slice(LANES * 2 * p, LANES * (2 * p + 1))] = (first + pltpu.roll(first, HEAD_DIM, 1)).astype(BF16)
                    dst[rows, slice(LANES * (2 * p + 1), LANES * (2 * p + 2))] = (second + pltpu.roll(second, HEAD_DIM, 1)).astype(BF16)
        _row_chunks(tm, BLOCK, chunk)

    tab = pl.BlockSpec((tm, LANES), lambda i: (i, 0))
    return pl.pallas_call(
        body, name="rope_qkv", grid=(tokens // tm,),
        in_specs=[pl.BlockSpec((tm, D_MODEL), lambda i: (i, COL_Q)),
                  pl.BlockSpec((tm, 512), lambda i: (i, COL512_KV)), tab, tab, tab],
        out_specs=(pl.BlockSpec((tm, D_MODEL), lambda i: (i, 0)),
                   pl.BlockSpec((tm, 512), lambda i: (i, 0)),
                   pl.BlockSpec((tm, 512), lambda i: (i, 0))),
        out_shape=(jax.ShapeDtypeStruct((tokens, D_MODEL), BF16),
                   jax.ShapeDtypeStruct((tokens, 512), BF16),
                   jax.ShapeDtypeStruct((tokens, 512), BF16)),
        compiler_params=_cparams(("parallel",)),
    )(proj, proj, cos_t, sin_up, sin_dn)


CONV_TM = 256
N_LANE_CHUNKS = D_MODEL // LANES


def _fill_u_ext(u_ext, a_ref, b_ref, ah_ref, bh_ref, first_tile):
    for lc in range(N_LANE_CHUNKS):
        sl = slice(LANES * lc, LANES * (lc + 1))
        uh = ah_ref[:, sl].astype(F32) * _sig(bh_ref[:, sl].astype(F32))
        u_ext[lc, 0:CONV_PAD, :] = jnp.where(first_tile, 0.0, uh)
        u_ext[lc, CONV_PAD:CONV_PAD + CONV_TM, :] = a_ref[:, sl].astype(F32) * _sig(b_ref[:, sl].astype(F32))


def _conv_forward(proj, conv_w, dw_b, ln_g, ln_b, w_co):
    tokens = proj.shape[0]
    tm = CONV_TM
    halo_blocks = tm // CONV_PAD

    def body(a_ref, b_ref, ah_ref, bh_ref, cg_ref, cw_ref, dwb_ref, lng_ref, lnb_ref, wco_ref,
             cv_ref, yc_ref, u_ext, cv_scr):
        _fill_u_ext(u_ext, a_ref, b_ref, ah_ref, bh_ref, pl.program_id(0) == 0)

        def lane_chunk(lc, carry):
            for rc in range(tm // 64):
                acc = jnp.zeros((64, LANES), F32)
                for j in range(CONV_KERNEL):
                    acc = acc + cw_ref[lc, pl.ds(j, 1), :] * u_ext[lc, pl.ds(64 * rc + 2 + j, 64), :]
                cv_scr[lc, pl.ds(64 * rc, 64), :] = acc
            return carry
        lax.fori_loop(0, N_LANE_CHUNKS, lane_chunk, 0)

        cv = jnp.concatenate([cv_scr[lc] for lc in range(N_LANE_CHUNKS)], axis=1) + dwb_ref[...]
        cv_ref[...] = cv
        mu = jnp.mean(cv, axis=-1, keepdims=True)
        zc = cv - mu
        rstd = lax.rsqrt(jnp.mean(zc * zc, axis=-1, keepdims=True) + LN_EPS)
        ln = zc * rstd * lng_ref[...] + lnb_ref[...]
        cg = cg_ref[...].astype(F32)
        pc = (ln * _sig(ln)) * (cg * _sig(cg))
        yc_ref[...] = _dot(pc.astype(BF16), _square(wco_ref)).astype(BF16)

    def halo_map(i):
        return (jnp.maximum(i * halo_blocks - 1, 0), 0)

    tile = lambda col: pl.BlockSpec((tm, D_MODEL), lambda i: (i, col))
    return pl.pallas_call(
        body, name="conv_forward", grid=(tokens // tm,),
        in_specs=[tile(COL_A), tile(COL_B),
                  pl.BlockSpec((CONV_PAD, D_MODEL), lambda i: (halo_map(i)[0], COL_A)),
                  pl.BlockSpec((CONV_PAD, D_MODEL), lambda i: (halo_map(i)[0], COL_B)),
                  tile(COL_CG), _const_spec((N_DEV, CONV_PAD, LANES)),
                  _const_spec((1, D_MODEL)), _const_spec((1, D_MODEL)), _const_spec((1, D_MODEL)),
                  _pack_weight_spec(0)],
        out_specs=(pl.BlockSpec((tm, D_MODEL), lambda i: (i, 0)),
                   pl.BlockSpec((tm, D_MODEL), lambda i: (i, 0))),
        out_shape=(jax.ShapeDtypeStruct((tokens, D_MODEL), F32),
                   jax.ShapeDtypeStruct((tokens, D_MODEL), BF16)),
        scratch_shapes=[pltpu.VMEM((N_LANE_CHUNKS, CONV_PAD + tm, LANES), F32),
                        pltpu.VMEM((N_LANE_CHUNKS, tm, LANES), F32)],
        compiler_params=_cparams(("parallel",), VMEM_LIMIT),
    )(proj, proj, proj, proj, proj, conv_w, dw_b, ln_g, ln_b, w_co)


def _band_mask(n):
    row = lax.broadcasted_iota(jnp.int32, (4 * BLOCK, 2 * BLOCK), 0) & (BLOCK - 1)
    col = lax.broadcasted_iota(jnp.int32, (4 * BLOCK, 2 * BLOCK), 1)
    before = jnp.logical_and(jnp.logical_and(col < BLOCK, col > row), n > 0)
    return jnp.logical_or(before, jnp.logical_and(col >= BLOCK, col - BLOCK <= row))


def _stack_heads(tile_a, tile_b, lo):
    zero = jnp.zeros_like(tile_a)
    return jnp.concatenate([jnp.where(lo, tile_a, zero), jnp.where(lo, zero, tile_a),
                            jnp.where(lo, tile_b, zero), jnp.where(lo, zero, tile_b)], axis=0)


def _unstack_heads(stacked, lo):
    s = [stacked[BLOCK * g:BLOCK * (g + 1)] for g in range(4)]
    return (jnp.where(lo, s[0], 0.0) + jnp.where(lo, 0.0, s[1]),
            jnp.where(lo, s[2], 0.0) + jnp.where(lo, 0.0, s[3]))


def _band_scores(q_stack, k2, mask):
    return jnp.where(mask, _dot_nt(q_stack, k2), NEG)


def _sink_rows(sinks_ref, kvh):
    return jnp.concatenate([jnp.full((BLOCK, LANES), sinks_ref[0, 4 * kvh + g], F32) for g in range(4)], axis=0)


def _attention_forward(qr, kd, vd, proj, sinks, w_ao):
    tokens = qr.shape[0]
    tm = min(512, tokens)
    per_tile = tm // BLOCK

    def body(q_ref, kc_ref, kp_ref, vc_ref, vp_ref, ag0_ref, ag1_ref, sinks_ref, wao_ref, o_ref, ya_ref, lse_ref,
             k_ext, v_ext, o_scr):
        i = pl.program_id(0)
        lo = _lane_halves()
        head_lane = lax.broadcasted_iota(jnp.int32, (1, LANES), 1)
        k_ext[0:BLOCK, :], k_ext[BLOCK:BLOCK + tm, :] = kp_ref[...], kc_ref[...]
        v_ext[0:BLOCK, :], v_ext[BLOCK:BLOCK + tm, :] = vp_ref[...], vc_ref[...]

        def block(b, carry):
            r0 = pl.multiple_of(b * BLOCK, BLOCK)
            band = pl.ds(r0, 2 * BLOCK)
            mask = _band_mask(i * per_tile + b)
            lse_tile = jnp.zeros((BLOCK, LANES), F32)
            for kvh in range(N_KV_HEADS):
                ta, tb = slice(LANES * 2 * kvh, LANES * (2 * kvh + 1)), slice(LANES * (2 * kvh + 1), LANES * (2 * kvh + 2))
                ks = slice(LANES * kvh, LANES * (kvh + 1))
                q_stack = _stack_heads(q_ref[pl.ds(r0, BLOCK), ta], q_ref[pl.ds(r0, BLOCK), tb], lo)
                s = _band_scores(q_stack, k_ext[band, ks], mask)
                sink = _sink_rows(sinks_ref, kvh)
                m = jnp.max(jnp.maximum(s[:, :BLOCK], s[:, BLOCK:]), axis=1, keepdims=True)
                m = jnp.maximum(jnp.broadcast_to(m, (4 * BLOCK, LANES)), sink)
                e = jnp.concatenate([jnp.exp(s[:, :BLOCK] - m), jnp.exp(s[:, BLOCK:] - m)], axis=1).astype(BF16)
                den = _dot(e, jnp.ones((2 * BLOCK, LANES), BF16)) + jnp.exp(sink - m)
                o_stack = _dot(e, v_ext[band, ks]) / den
                o_scr[pl.ds(r0, BLOCK), ta], o_scr[pl.ds(r0, BLOCK), tb] = _unstack_heads(o_stack, lo)
                lse = m + jnp.log(den)
                for g in range(4):
                    lse_tile = lse_tile + jnp.where(head_lane == 4 * kvh + g, lse[BLOCK * g:BLOCK * (g + 1)], 0.0)
            lse_ref[pl.ds(r0, BLOCK), :] = lse_tile
            return carry
        lax.fori_loop(0, per_tile, block, 0)
        o = o_scr[...]
        o_ref[...] = o.astype(BF16)
        ag = jnp.concatenate([ag0_ref[...], ag1_ref[...]], axis=1).astype(F32)
        ya_ref[...] = _dot((o * (ag * _sig(ag))).astype(BF16), _square(wao_ref)).astype(BF16)

    cur = lambda w, col=0: pl.BlockSpec((tm, w), lambda i: (i, col))
    prev = lambda w: pl.BlockSpec((BLOCK, w), lambda i: (jnp.maximum(i * per_tile - 1, 0), 0))
    return pl.pallas_call(
        body, name="attention_forward", grid=(tokens // tm,),
        in_specs=[cur(D_MODEL), cur(512), prev(512), cur(512), prev(512),
                  cur(512, COL512_AG), cur(512, COL512_AG + 1),
                  pl.BlockSpec(memory_space=pltpu.SMEM), _pack_weight_spec(1)],
        out_specs=(cur(D_MODEL), cur(D_MODEL), cur(LANES)),
        out_shape=(jax.ShapeDtypeStruct((tokens, D_MODEL), BF16),
                   jax.ShapeDtypeStruct((tokens, D_MODEL), BF16),
                   jax.ShapeDtypeStruct((tokens, LANES), F32)),
        scratch_shapes=[pltpu.VMEM((BLOCK + tm, 512), BF16), pltpu.VMEM((BLOCK + tm, 512), BF16),
                        pltpu.VMEM((tm, D_MODEL), F32)],
        compiler_params=_cparams(("parallel",), VMEM_LIMIT),
    )(qr, kd, kd, vd, vd, proj, proj, sinks, w_ao)


def _merge_and_head(yc, ya, proj, x, target, w_out, final_g):
    tokens = x.shape[0]
    tm = min(512, tokens)
    last = tokens // tm - 1

    def body(yc_ref, ya_ref, mlc0_ref, mlc1_ref, mla0_ref, mla1_ref, x_ref, t_ref, wout_ref, fg_ref,
             dx2_ref, dyc_ref, dya_ref, dmlc_ref, dmla_ref, gwout_ref, part_ref, gacc):
        i = pl.program_id(0)

        @pl.when(i == 0)
        def _():
            gacc[...] = jnp.zeros_like(gacc)
            part_ref[...] = jnp.zeros_like(part_ref)

        yc, ya = yc_ref[...].astype(F32), ya_ref[...].astype(F32)
        gc = _sig(jnp.concatenate([mlc0_ref[...], mlc1_ref[...]], axis=1).astype(F32))
        ga = _sig(jnp.concatenate([mla0_ref[...], mla1_ref[...]], axis=1).astype(F32))
        merged = (gc * yc + ga * ya).astype(BF16)
        x2 = x_ref[...] + _dot(merged, _square(wout_ref))
        r2 = lax.rsqrt(jnp.mean(x2 * x2, axis=-1, keepdims=True) + RMS_EPS)
        x2n = x2 * r2
        fg = fg_ref[...]
        err = x2n * fg - t_ref[...]
        dy = err * (1.0 / D_MODEL)
        part_ref[0:1, :] += jnp.sum(dy * x2n, axis=0, keepdims=True)
        part_ref[1:2, :] += jnp.sum(err * err, axis=0, keepdims=True) * (0.5 / D_MODEL)
        dx2n = dy * fg
        dx2 = r2 * (dx2n - x2n * jnp.mean(dx2n * x2n, axis=-1, keepdims=True))
        dx2_ref[...] = dx2
        dx2b = dx2.astype(BF16)
        gacc[...] += _dot_tn(merged, dx2b)
        dm = _dot_nt(dx2b, _square(wout_ref))
        dyc_ref[...] = (dm * gc).astype(BF16)
        dya_ref[...] = (dm * ga).astype(BF16)
        dmlc_ref[...] = (dm * yc * (gc * (1.0 - gc))).astype(BF16)
        dmla_ref[...] = (dm * ya * (ga * (1.0 - ga))).astype(BF16)

        @pl.when(i == last)
        def _():
            gwout_ref[...] = gacc[...].astype(BF16)

    tile = lambda col=0: pl.BlockSpec((tm, D_MODEL), lambda i: (i, col))
    half = lambda col: pl.BlockSpec((tm, 512), lambda i: (i, col))
    return pl.pallas_call(
        body, name="merge_and_head", grid=(tokens // tm,),
        in_specs=[tile(), tile(), half(COL512_MLC), half(COL512_MLC + 1), half(COL512_MLA), half(COL512_MLA + 1),
                  tile(), tile(), _pack_weight_spec(2), _const_spec((1, D_MODEL))],
        out_specs=(tile(), tile(), tile(), tile(), tile(),
                   _const_spec((D_MODEL, D_MODEL)), _const_spec((8, D_MODEL))),
        out_shape=(jax.ShapeDtypeStruct((tokens, D_MODEL), F32),
                   jax.ShapeDtypeStruct((tokens, D_MODEL), BF16),
                   jax.ShapeDtypeStruct((tokens, D_MODEL), BF16),
                   jax.ShapeDtypeStruct((tokens, D_MODEL), BF16),
                   jax.ShapeDtypeStruct((tokens, D_MODEL), BF16),
                   jax.ShapeDtypeStruct((D_MODEL, D_MODEL), BF16),
                   jax.ShapeDtypeStruct((8, D_MODEL), F32)),
        scratch_shapes=[pltpu.VMEM((D_MODEL, D_MODEL), F32)],
        compiler_params=_cparams(("arbitrary",), VMEM_LIMIT),
    )(yc, ya, proj, proj, proj, proj, x, target, w_out, final_g)


def _conv_backward_pointwise(dyc, cv, proj, w_co, ln_g, ln_b):
    tokens = cv.shape[0]
    tm = min(512, tokens)
    last = tokens // tm - 1

    def body(dyc_ref, cv_ref, cg_ref, wco_ref, lng_ref, lnb_ref, dcv_ref, dcg_ref, gwco_ref, part_ref, gacc):
        i = pl.program_id(0)

        @pl.when(i == 0)
        def _():
            gacc[...] = jnp.zeros_like(gacc)
            part_ref[...] = jnp.zeros_like(part_ref)

        cv = cv_ref[...]
        mu = jnp.mean(cv, axis=-1, keepdims=True)
        zc = cv - mu
        rstd = lax.rsqrt(jnp.mean(zc * zc, axis=-1, keepdims=True) + LN_EPS)
        z = zc * rstd
        lng = lng_ref[...]
        ln = z * lng + lnb_ref[...]
        sl = _sig(ln)
        c = ln * sl
        cg = cg_ref[...].astype(F32)
        scg = _sig(cg)
        gate = cg * scg
        dyc = dyc_ref[...]
        gacc[...] += _dot_tn((c * gate).astype(BF16), dyc)
        dpc = _dot_nt(dyc, _square(wco_ref))
        dcg_ref[...] = (dpc * c * (scg * (1.0 + cg * (1.0 - scg)))).astype(BF16)
        dln = dpc * gate * (sl * (1.0 + ln * (1.0 - sl)))
        part_ref[0:1, :] += jnp.sum(dln * z, axis=0, keepdims=True)
        part_ref[1:2, :] += jnp.sum(dln, axis=0, keepdims=True)
        dz = dln * lng
        dcv = rstd * (dz - jnp.mean(dz, axis=-1, keepdims=True) - z * jnp.mean(dz * z, axis=-1, keepdims=True))
        part_ref[2:3, :] += jnp.sum(dcv, axis=0, keepdims=True)
        dcv_ref[...] = dcv

        @pl.when(i == last)
        def _():
            gwco_ref[...] = gacc[...].astype(BF16)

    tile = lambda col=0: pl.BlockSpec((tm, D_MODEL), lambda i: (i, col))
    return pl.pallas_call(
        body, name="conv_backward_pointwise", grid=(tokens // tm,),
        in_specs=[tile(), tile(), tile(COL_CG), _pack_weight_spec(0),
                  _const_spec((1, D_MODEL)), _const_spec((1, D_MODEL))],
        out_specs=(tile(), tile(), _const_spec((D_MODEL, D_MODEL)), _const_spec((8, D_MODEL))),
        out_shape=(jax.ShapeDtypeStruct((tokens, D_MODEL), F32),
                   jax.ShapeDtypeStruct((tokens, D_MODEL), BF16),
                   jax.ShapeDtypeStruct((D_MODEL, D_MODEL), BF16),
                   jax.ShapeDtypeStruct((8, D_MODEL), F32)),
        scratch_shapes=[pltpu.VMEM((D_MODEL, D_MODEL), F32)],
        compiler_params=_cparams(("arbitrary",), VMEM_LIMIT),
    )(dyc, cv, proj, w_co, ln_g, ln_b)


def _conv_backward_taps(dcv, proj, conv_w):
    tokens = dcv.shape[0]
    tm = CONV_TM
    nt = tokens // tm
    halo_blocks = tm // CONV_PAD

    def body(d_ref, dn_ref, a_ref, b_ref, ah_ref, bh_ref, cw_ref, da_ref, db_ref, gw_ref, u_ext, d_ext, du_scr, gw_acc):
        i = pl.program_id(0)

        @pl.when(i == 0)
        def _():
            gw_acc[...] = jnp.zeros_like(gw_acc)

        _fill_u_ext(u_ext, a_ref, b_ref, ah_ref, bh_ref, i == 0)
        for lc in range(N_LANE_CHUNKS):
            sl = slice(LANES * lc, LANES * (lc + 1))
            d_ext[lc, 0:tm, :] = d_ref[:, sl]
            d_ext[lc, tm:tm + CONV_PAD, :] = jnp.where(i == nt - 1, 0.0, dn_ref[:, sl])

        def lane_chunk(lc, carry):
            n_rc = tm // 64
            du = [jnp.zeros((64, LANES), F32) for _ in range(n_rc)]
            for j in range(CONV_KERNEL):
                w = cw_ref[lc, pl.ds(j, 1), :]
                gsum = jnp.zeros((8, LANES), F32)
                for rc in range(n_rc):
                    du[rc] = du[rc] + w * d_ext[lc, pl.ds(64 * rc + 30 - j, 64), :]
                    prod = d_ext[lc, pl.ds(64 * rc, 64), :] * u_ext[lc, pl.ds(64 * rc + 2 + j, 64), :]
                    gsum = gsum + jnp.sum(prod.reshape(8, 8, LANES), axis=0)
                gw_acc[lc, j] += gsum
            for rc in range(n_rc):
                du_scr[lc, pl.ds(64 * rc, 64), :] = du[rc]
            return carry
        lax.fori_loop(0, N_LANE_CHUNKS, lane_chunk, 0)

        du = jnp.concatenate([du_scr[lc] for lc in range(N_LANE_CHUNKS)], axis=1)
        a, b = a_ref[...].astype(F32), b_ref[...].astype(F32)
        sb = _sig(b)
        da_ref[...] = (du * sb).astype(BF16)
        db_ref[...] = (du * a * (sb * (1.0 - sb))).astype(BF16)

        @pl.when(i == nt - 1)
        def _():
            gw_ref[...] = jnp.sum(gw_acc[...], axis=2)

    def prev_halo(i):
        return jnp.maximum(i * halo_blocks - 1, 0)

    def next_halo(i):
        return jnp.minimum((i + 1) * halo_blocks, tokens // CONV_PAD - 1)

    tile = lambda col=0: pl.BlockSpec((tm, D_MODEL), lambda i: (i, col))
    return pl.pallas_call(
        body, name="conv_backward_taps", grid=(nt,),
        in_specs=[tile(), pl.BlockSpec((CONV_PAD, D_MODEL), lambda i: (next_halo(i), 0)),
                  tile(COL_A), tile(COL_B),
                  pl.BlockSpec((CONV_PAD, D_MODEL), lambda i: (prev_halo(i), COL_A)),
                  pl.BlockSpec((CONV_PAD, D_MODEL), lambda i: (prev_halo(i), COL_B)),
                  _const_spec((N_DEV, CONV_PAD, LANES))],
        out_specs=(tile(), tile(), _const_spec((N_DEV, CONV_PAD, LANES))),
        out_shape=(jax.ShapeDtypeStruct((tokens, D_MODEL), BF16),
                   jax.ShapeDtypeStruct((tokens, D_MODEL), BF16),
                   jax.ShapeDtypeStruct((N_DEV, CONV_PAD, LANES), F32)),
        scratch_shapes=[pltpu.VMEM((N_LANE_CHUNKS, CONV_PAD + tm, LANES), F32),
                        pltpu.VMEM((N_LANE_CHUNKS, tm + CONV_PAD, LANES), F32),
                        pltpu.VMEM((N_LANE_CHUNKS, tm, LANES), F32),
                        pltpu.VMEM((N_LANE_CHUNKS, CONV_PAD, 8, LANES), F32)],
        compiler_params=_cparams(("arbitrary",), VMEM_LIMIT),
    )(dcv, dcv, proj, proj, proj, proj, conv_w)


def _fold_kv_head(dup, lo, second_half):
    both = dup + pltpu.roll(dup, HEAD_DIM, 1)
    lo = lax.broadcasted_iota(jnp.int32, dup.shape, 1) < HEAD_DIM
    return jnp.where(lo, 0.0, both) if second_half else jnp.where(lo, both, 0.0)


def _attention_backward(dya, o, lse, qr, kd, vd, proj, sinks, w_ao, cos_t, sin_up, sin_dn):
    tokens = qr.shape[0]
    tm = min(512, tokens)
    per_tile = tm // BLOCK
    nt = tokens // tm
    scale = HEAD_DIM ** -0.5

    def body(dya_ref, o_ref, ag0_ref, ag1_ref, q_ref, kc_ref, kp_ref, vc_ref, vp_ref, sinks_ref, wao_ref, lse_ref,
             cos_c, up_c, dn_c, cos_p, up_p, dn_p,
             dq_ref, dkv_ref, dag_ref, gwao_ref, gsink_ref,
             gacc, k_ext, v_ext, dk_ext, dv_ext, dk_carry, dv_carry, do_scr, dq_scr, delta_scr):
        i = pl.program_id(0)
        lo = _lane_halves()

        @pl.when(i == 0)
        def _():
            gacc[...] = jnp.zeros_like(gacc)
            gsink_ref[...] = jnp.zeros_like(gsink_ref)
            dk_carry[...] = jnp.zeros_like(dk_carry)
            dv_carry[...] = jnp.zeros_like(dv_carry)
        dk_ext[...] = jnp.zeros_like(dk_ext)
        dv_ext[...] = jnp.zeros_like(dv_ext)

        @pl.when(i < nt)
        def _():
            dya = dya_ref[...]
            dpa = _dot_nt(dya, _square(wao_ref))
            o = o_ref[...].astype(F32)
            ag = jnp.concatenate([ag0_ref[...], ag1_ref[...]], axis=1).astype(F32)
            sg = _sig(ag)
            gate = ag * sg
            gacc[...] += _dot_tn((o * gate).astype(BF16), dya)
            dag_ref[...] = (dpa * o * (sg * (1.0 + ag * (1.0 - sg)))).astype(BF16)
            do = dpa * gate
            do_scr[...] = do.astype(BF16)
            member = jnp.where(lax.broadcasted_iota(jnp.int32, (D_MODEL, LANES), 0) >> (HEAD_DIM.bit_length() - 1)
                               == lax.broadcasted_iota(jnp.int32, (D_MODEL, LANES), 1), 1.0, 0.0).astype(BF16)
            prod = do * o
            prod_hi = prod.astype(BF16)
            prod_lo = (prod - prod_hi.astype(F32)).astype(BF16)
            delta_scr[...] = _dot(prod_hi, member) + _dot(prod_lo, member)
            k_ext[0:BLOCK, :], k_ext[BLOCK:BLOCK + tm, :] = kp_ref[...], kc_ref[...]
            v_ext[0:BLOCK, :], v_ext[BLOCK:BLOCK + tm, :] = vp_ref[...], vc_ref[...]

            def block(b, carry):
                r0 = pl.multiple_of(b * BLOCK, BLOCK)
                mine, band = pl.ds(r0, BLOCK), pl.ds(r0, 2 * BLOCK)
                lse_tile, delta_tile = lse_ref[mine, :], delta_scr[mine, :]
                mask = _band_mask(i * per_tile + b)
                head_lane = lax.broadcasted_iota(jnp.int32, (1, LANES), 1)
                gsink = jnp.zeros((1, LANES), F32)
                zero_band = jnp.zeros((2 * BLOCK, LANES), F32)
                dk_band, dv_band = [zero_band, zero_band], [zero_band, zero_band]
                for kvh in range(N_KV_HEADS):
                    ta, tb = slice(LANES * 2 * kvh, LANES * (2 * kvh + 1)), slice(LANES * (2 * kvh + 1), LANES * (2 * kvh + 2))
                    ks = slice(LANES * kvh, LANES * (kvh + 1))
                    q_stack = _stack_heads(q_ref[mine, ta], q_ref[mine, tb], lo)
                    do_stack = _stack_heads(do_scr[mine, ta], do_scr[mine, tb], lo)
                    k2, v2 = k_ext[band, ks], v_ext[band, ks]
                    s = _band_scores(q_stack, k2, mask)
                    lse = jnp.concatenate([jnp.broadcast_to(lse_tile[:, 4 * kvh + g:4 * kvh + g + 1], (BLOCK, LANES))
                                           for g in range(4)], axis=0)
                    p_p, p_c = jnp.exp(s[:, :BLOCK] - lse), jnp.exp(s[:, BLOCK:] - lse)
                    dp = _dot_nt(do_stack, v2)
                    dp_p, dp_c = dp[:, :BLOCK], dp[:, BLOCK:]
                    delta = jnp.concatenate([jnp.broadcast_to(delta_tile[:, 4 * kvh + g:4 * kvh + g + 1], (BLOCK, LANES))
                                             for g in range(4)], axis=0)
                    ds = jnp.concatenate([p_p * (dp_p - delta), p_c * (dp_c - delta)], axis=1).astype(BF16)
                    sink_terms = jnp.exp(_sink_rows(sinks_ref, kvh) - lse) * delta
                    for g in range(4):
                        total = jnp.sum(sink_terms[BLOCK * g:BLOCK * (g + 1)], axis=0, keepdims=True)
                        gsink = gsink - jnp.where(head_lane == 4 * kvh + g, total, 0.0)
                    dq_scr[mine, ta], dq_scr[mine, tb] = _unstack_heads(_dot(ds, k2), lo)
                    tile, second = kvh // 2, kvh % 2 == 1
                    dk_band[tile] = dk_band[tile] + _fold_kv_head(_dot_tn(ds, q_stack), lo, second)
                    dv_band[tile] = dv_band[tile] + _fold_kv_head(
                        _dot_tn(jnp.concatenate([p_p, p_c], axis=1).astype(BF16), do_stack), lo, second)
                gsink_ref[0:1, :] += gsink
                cs, up, dn = cos_c[mine, :], up_c[mine, :], dn_c[mine, :]
                for p in range(D_MODEL // LANES):
                    sl = slice(LANES * p, LANES * (p + 1))
                    dq_ref[mine, sl] = (_rope_transposed(dq_scr[mine, sl], cs, up, dn) * scale).astype(BF16)
                for p in range(2):
                    sl = slice(LANES * p, LANES * (p + 1))
                    dk_ext[band, sl] += dk_band[p]
                    dv_ext[band, sl] += dv_band[p]
                return carry
            lax.fori_loop(0, per_tile, block, 0)

        last = slice(tm - BLOCK, tm)
        dk_carry[last, :] += dk_ext[0:BLOCK, :]
        dv_carry[last, :] += dv_ext[0:BLOCK, :]
        for p in range(2):
            sl = slice(LANES * p, LANES * (p + 1))
            dkv_ref[:, sl] = _rope_transposed(dk_carry[:, sl], cos_p[...], up_p[...], dn_p[...]).astype(BF16)
            dkv_ref[:, slice(256 + LANES * p, 256 + LANES * (p + 1))] = dv_carry[:, sl].astype(BF16)
        dk_carry[...] = dk_ext[BLOCK:BLOCK + tm, :]
        dv_carry[...] = dv_ext[BLOCK:BLOCK + tm, :]

        @pl.when(i == nt)
        def _():
            gwao_ref[...] = gacc[...].astype(BF16)

    def cur_idx(i):
        return jnp.minimum(i, nt - 1)

    def prev_idx(i):
        return jnp.clip(i - 1, 0, nt - 1)

    cur = lambda w, col=0: pl.BlockSpec((tm, w), lambda i: (cur_idx(i), col))
    prev = lambda w: pl.BlockSpec((tm, w), lambda i: (prev_idx(i), 0))
    before = lambda w: pl.BlockSpec((BLOCK, w), lambda i: (jnp.maximum(cur_idx(i) * per_tile - 1, 0), 0))
    return pl.pallas_call(
        body, name="attention_backward", grid=(nt + 1,),
        in_specs=[cur(D_MODEL), cur(D_MODEL), cur(512, COL512_AG), cur(512, COL512_AG + 1), cur(D_MODEL),
                  cur(512), before(512), cur(512), before(512),
                  pl.BlockSpec(memory_space=pltpu.SMEM), _pack_weight_spec(1), cur(LANES),
                  cur(LANES), cur(LANES), cur(LANES), prev(LANES), prev(LANES), prev(LANES)],
        out_specs=(cur(D_MODEL), prev(512), cur(D_MODEL),
                   _const_spec((D_MODEL, D_MODEL)), _const_spec((8, LANES))),
        out_shape=(jax.ShapeDtypeStruct((tokens, D_MODEL), BF16),
                   jax.ShapeDtypeStruct((tokens, 512), BF16),
                   jax.ShapeDtypeStruct((tokens, D_MODEL), BF16),
                   jax.ShapeDtypeStruct((D_MODEL, D_MODEL), BF16),
                   jax.ShapeDtypeStruct((8, LANES), F32)),
        scratch_shapes=[pltpu.VMEM((D_MODEL, D_MODEL), F32),
                        pltpu.VMEM((BLOCK + tm, 512), BF16), pltpu.VMEM((BLOCK + tm, 512), BF16),
                        pltpu.VMEM((BLOCK + tm, 256), F32), pltpu.VMEM((BLOCK + tm, 256), F32),
                        pltpu.VMEM((tm, 256), F32), pltpu.VMEM((tm, 256), F32),
                        pltpu.VMEM((tm, D_MODEL), BF16), pltpu.VMEM((tm, D_MODEL), F32),
                        pltpu.VMEM((tm, LANES), F32)],
        compiler_params=_cparams(("arbitrary",), VMEM_LIMIT),
    )(dya, o, proj, proj, qr, kd, kd, vd, vd, sinks, w_ao, lse, cos_t, sin_up, sin_dn, cos_t, sin_up, sin_dn)


def _input_backward(sections, w_in_t, x, dx2, norm_g):
    tokens = x.shape[0]
    tm = 256

    def body(*refs):
        sec = refs[:8]
        w_ref, x_ref, dx2_ref, g_ref, gx_ref, part_ref = refs[8:]

        @pl.when(pl.program_id(0) == 0)
        def _():
            part_ref[...] = jnp.zeros_like(part_ref)

        dh = jnp.zeros((tm, D_MODEL), F32)
        for s in range(8):
            dh = dh + _dot(sec[s][...], w_ref[_SECTION_ROWS[s]:_SECTION_ROWS[s] + _SECTION_WIDTH[s], :])
        xv = x_ref[...]
        r = lax.rsqrt(jnp.mean(xv * xv, axis=-1, keepdims=True) + RMS_EPS)
        xn = xv * r
        part_ref[0:1, :] += jnp.sum(dh * xn, axis=0, keepdims=True)
        dxn = dh * g_ref[...]
        gx_ref[...] = dx2_ref[...] + r * (dxn - xn * jnp.mean(dxn * xn, axis=-1, keepdims=True))

    tile = lambda w=D_MODEL: pl.BlockSpec((tm, w), lambda i: (i, 0))
    return pl.pallas_call(
        body, name="input_backward", grid=(tokens // tm,),
        in_specs=[tile(w) for w in _SECTION_WIDTH] + [
            pl.BlockSpec((IN_WIDTH, D_MODEL), lambda i: (0, 0), pipeline_mode=pl.Buffered(1)),
            tile(), tile(), _const_spec((1, D_MODEL))],
        out_specs=(tile(), _const_spec((8, D_MODEL))),
        out_shape=(jax.ShapeDtypeStruct((tokens, D_MODEL), F32),
                   jax.ShapeDtypeStruct((8, D_MODEL), F32)),
        compiler_params=_cparams(("arbitrary",), VMEM_LIMIT),
    )(*sections, w_in_t, x, dx2, norm_g)


def _adamw_math(w, g, m, v):
    m = ADAM_B1 * m + (1.0 - ADAM_B1) * g
    v = ADAM_B2 * v + (1.0 - ADAM_B2) * (g * g)
    m_hat = m / (1.0 - ADAM_B1 ** ADAM_STEP)
    v_hat = v / (1.0 - ADAM_B2 ** ADAM_STEP)
    delta = -ADAM_LR * (m_hat / (jnp.sqrt(v_hat) + ADAM_EPS) + ADAM_WD * w)
    return delta, m, v


def _sum_slots(recv_ref):
    total = recv_ref[0].astype(F32)
    for d in range(1, N_DEV):
        total = total + recv_ref[d].astype(F32)
    return total


def _adamw(name, w, g, m, v, tile_rows):
    rows, cols = w.shape

    def body(w_ref, g_ref, m_ref, v_ref, d_ref, nm_ref, nv_ref):
        d_ref[...], nm_ref[...], nv_ref[...] = _adamw_math(w_ref[...], g_ref[...], m_ref[...], v_ref[...])

    spec = pl.BlockSpec((tile_rows, cols), lambda i: (i, 0))
    shape = jax.ShapeDtypeStruct((rows, cols), F32)
    return pl.pallas_call(
        body, name=name, grid=(rows // tile_rows,),
        in_specs=[spec] * 4, out_specs=(spec,) * 3, out_shape=(shape,) * 3,
        compiler_params=_cparams(("parallel",)),
    )(w, g, m, v)


def _sum_adamw(name, recv, w, m, v):
    def body(recv_ref, w_ref, m_ref, v_ref, g_ref, d_ref, nm_ref, nv_ref):
        g = _sum_slots(recv_ref)
        g_ref[...] = g
        d_ref[...], nm_ref[...], nv_ref[...] = _adamw_math(w_ref[...], g, m_ref[...], v_ref[...])

    shape = jax.ShapeDtypeStruct(w.shape, F32)
    return pl.pallas_call(body, name=name, out_shape=(shape,) * 4)(recv, w, m, v)


def _pad_rows(a, rows):
    return jnp.concatenate([a, jnp.zeros((rows - a.shape[0],) + a.shape[1:], a.dtype)], axis=0)


def kernel(x, norm_g, w_in, conv_dw_w, conv_dw_b, conv_ln_g, conv_ln_b, w_conv_out, attn_sinks, w_attn_out, w_out, final_norm_g, loss_target, m_norm_g, m_w_in, m_conv_dw_w, m_conv_dw_b, m_conv_ln_g, m_conv_ln_b, m_w_conv_out, m_attn_sinks, m_w_attn_out, m_w_out, m_final_norm_g, v_norm_g, v_w_in, v_conv_dw_w, v_conv_dw_b, v_conv_ln_g, v_conv_ln_b, v_w_conv_out, v_attn_sinks, v_w_attn_out, v_w_out, v_final_norm_g):
    xs, target = x[0], loss_target[0]
    tokens = xs.shape[0]
    fg_row = final_norm_g.reshape(1, D_MODEL)

    taps_bits = lax.bitcast_convert_type(_pad_rows(conv_dw_w[0], CONV_PAD), BF16).reshape(8, D_MODEL)
    pack = jnp.concatenate([w_conv_out[0].astype(BF16), w_attn_out[0].astype(BF16), w_out[0].astype(BF16),
                            jnp.pad(taps_bits, ((0, PACK_ROWS - 3 * SHARD_SQ - 8), (0, 0)))], axis=0)
    w_in_t32 = w_in[0].T
    proj, h_t, w_in_t, pack_full = _gather_project(xs, norm_g, w_in_t32.astype(BF16), pack)
    w_co = w_ao = w_o = pack_full
    conv_w = lax.bitcast_convert_type(
        pack_full[:, 3 * SHARD_SQ:3 * SHARD_SQ + 8].reshape(N_DEV, CONV_PAD, LANES, 2), F32)

    cos_t, sin_up, sin_dn = _rope_tables(tokens)
    qr, kd, vd = _rope_qkv(proj, cos_t, sin_up, sin_dn)
    cv, yc = _conv_forward(proj, conv_w, conv_dw_b, conv_ln_g, conv_ln_b, w_co)
    o, ya, lse = _attention_forward(qr, kd, vd, proj, attn_sinks, w_ao)

    dx2, dyc, dya, dmlc, dmla, g_out, part_head = _merge_and_head(yc, ya, proj, xs, target, w_o, fg_row)
    dcv, dcg, g_co, part_conv = _conv_backward_pointwise(dyc, cv, proj, w_co, conv_ln_g, conv_ln_b)
    da, db, g_conv = _conv_backward_taps(dcv, proj, conv_w)
    dq, dkv, dag, g_ao, part_sink = _attention_backward(dya, o, lse, qr, kd, vd, proj, attn_sinks, w_ao, cos_t, sin_up, sin_dn)
    sections = (da, db, dcg, dq, dkv, dag, dmlc, dmla)
    grad_x, part_in = _input_backward(sections, w_in_t, xs, dx2, norm_g)

    small = jnp.concatenate([
        part_in[0:1], part_conv[2:3], part_conv[0:1], part_conv[1:2], part_head[0:1],
        jnp.pad(part_sink[0:1], ((0, 0), (0, D_MODEL - LANES))), part_head[1:2],
        jnp.zeros((1, D_MODEL), F32)], axis=0)

    g_mine, r_conv, r_small = _grad_exchange(sections, h_t, g_co, g_ao, g_out, g_conv, small)

    g_in_t = g_mine[:SHARD_IN]
    w_in_res = _adamw("adamw_w_in", w_in_t32, g_in_t, m_w_in[0].T, v_w_in[0].T, 192)
    grad_w_in, d_w_in, nm_w_in, nv_w_in = (a.T for a in (g_in_t,) + tuple(w_in_res))
    sq = {}
    for j, (nm, w, m, v) in enumerate((("w_conv_out", w_conv_out, m_w_conv_out, v_w_conv_out),
                                       ("w_attn_out", w_attn_out, m_w_attn_out, v_w_attn_out),
                                       ("w_out", w_out, m_w_out, v_w_out))):
        g = g_mine[SHARD_IN + j * SHARD_SQ:SHARD_IN + (j + 1) * SHARD_SQ]
        sq[nm] = (g,) + tuple(_adamw("adamw_" + nm, w[0], g, m[0], v[0], SHARD_SQ))
    conv_res = _sum_adamw("sum_adamw_conv_dw_w", r_conv.reshape(N_DEV, CONV_PAD, LANES),
                          _pad_rows(conv_dw_w[0], CONV_PAD), _pad_rows(m_conv_dw_w[0], CONV_PAD),
                          _pad_rows(v_conv_dw_w[0], CONV_PAD))
    pad_sink = lambda a: jnp.pad(a, ((0, 0), (0, D_MODEL - N_Q_HEADS)))
    zero_rows = jnp.zeros((2, D_MODEL), F32)
    stack = lambda a, b, c, d, e, f: jnp.concatenate([a, b, c, d, e.reshape(1, D_MODEL), pad_sink(f), zero_rows], axis=0)
    small_res = _sum_adamw(
        "sum_adamw_small", r_small,
        stack(norm_g, conv_dw_b, conv_ln_g, conv_ln_b, final_norm_g, attn_sinks),
        stack(m_norm_g, m_conv_dw_b, m_conv_ln_g, m_conv_ln_b, m_final_norm_g, m_attn_sinks),
        stack(v_norm_g, v_conv_dw_b, v_conv_ln_g, v_conv_ln_b, v_final_norm_g, v_attn_sinks))
    loss = jnp.sum(small_res[0][6])

    def leaf(k):
        s = small_res[k]
        return (s[0:1], (grad_w_in, d_w_in, nm_w_in, nv_w_in)[k][None], conv_res[k][None, :CONV_KERNEL],
                s[1:2], s[2:3], s[3:4], sq["w_conv_out"][k][None], s[5:6, :N_Q_HEADS],
                sq["w_attn_out"][k][None], sq["w_out"][k][None], s[4])

    return (loss, grad_x[None], *leaf(0), *leaf(1), *leaf(2), *leaf(3))
```
